```python
import math, functools
import jax, jax.numpy as jnp
from jax import lax
import numpy as np

D_MODEL = 2048
BATCH = 2
SEQ = 4096
DEPTH = 2
DEC_BATCH = 8
DEC_SEQ = 4
PAST_LEN = 16384
PAGE_SIZE = 128

HG_HEADS = 8
HG_DK = 128
HG_DV = 128
HG_WIDTH = HG_HEADS * HG_DV
HG_CHUNK = 32
NSA_HEADS = 16
NSA_KV_HEADS = 4
NSA_GROUP = NSA_HEADS // NSA_KV_HEADS
NSA_HEAD_DIM = 64
NSA_WIDTH = NSA_HEADS * NSA_HEAD_DIM
NSA_KV_WIDTH = NSA_KV_HEADS * NSA_HEAD_DIM
CMP_BLOCK = 32
CMP_STRIDE = 16
SEL_BLOCK = 64
NSA_TOPK = 16
WINDOW = 512
WIN_QBLK = 128
NSA_QBLK = 64
RET_HEADS = 8
RET_DK = 128
RET_DV = 128
RET_WIDTH = RET_HEADS * RET_DV
RET_CHUNK = 64
FFN_HIDDEN = ((8 * D_MODEL // 3 + 255) // 256) * 256
EPS = 1e-6
NEG_BIG = -1e30
POS_BIG = 1e30
MIN_F = 1e-20

IN_SPLITS = (
    HG_HEADS * HG_DK, HG_HEADS * HG_DK, HG_WIDTH, HG_WIDTH,
    NSA_WIDTH, 4 * NSA_KV_WIDTH, 2 * NSA_KV_WIDTH, 3 * NSA_HEADS,
    RET_HEADS * RET_DK, RET_HEADS * RET_DK, RET_WIDTH, RET_WIDTH,
    3 * D_MODEL,
)
W_IN = sum(IN_SPLITS)

kernel_name = 'hybrid_hgrn2_nsa_retention_step'


def rmsnorm(x, gain):
    xf = x.astype(jnp.float32)
    xf = xf * lax.rsqrt(jnp.mean(xf * xf, axis=-1, keepdims=True) + EPS)
    return (xf * gain.astype(jnp.float32)).astype(x.dtype)


def masked_softmax(s, valid):
    s = jnp.where(valid, s.astype(jnp.float32), NEG_BIG)
    m = jnp.max(s, axis=-1, keepdims=True)
    p = jnp.where(valid, jnp.exp(s - m), 0.0)
    return p / jnp.maximum(jnp.sum(p, axis=-1, keepdims=True), 1e-30)


def alibi_slopes():
    h = jnp.arange(1, NSA_HEADS + 1, dtype=jnp.float32)
    return jnp.exp2(-8.0 * h / NSA_HEADS).reshape(NSA_KV_HEADS, NSA_GROUP)


def split_in(h):
    offs = [int(o) for o in np.cumsum(IN_SPLITS)[:-1]]
    return jnp.split(h, offs, axis=-1)


def to_chunks(a, c):
    B, T = a.shape[:2]
    return a.reshape((B, T // c, c) + a.shape[2:]).swapaxes(0, 1)


def from_chunks(a):
    n, B, c = a.shape[:3]
    return a.swapaxes(0, 1).reshape((B, n * c) + a.shape[3:])


def hgrn2_mixer(hq, hf, hi, hg, lb, out_gain, S0):
    B, T = hq.shape[:2]
    q = jax.nn.silu(hq.astype(jnp.float32)).reshape(B, T, HG_HEADS, HG_DK)
    z = hf.astype(jnp.float32).reshape(B, T, HG_HEADS, HG_DK)
    lb = lb.reshape(HG_HEADS, HG_DK)
    sig = jax.nn.sigmoid(z)
    f = lb + (1.0 - lb) * sig
    log_f = jnp.log(jnp.maximum(f, MIN_F))
    k = (1.0 - lb) * (1.0 - sig)
    v = hi.astype(jnp.float32).reshape(B, T, HG_HEADS, HG_DV)
    C = math.gcd(T, HG_CHUNK)
    causal = jnp.tril(jnp.ones((C, C), dtype=bool))[None, :, :, None, None]

    def step(S, inp):
        qc, kc, vc, gc = inp
        b = jnp.cumsum(gc, axis=1)
        diff = b[:, :, None] - b[:, None]
        decay = jnp.where(causal, jnp.exp(jnp.where(causal, diff, 0.0)), 0.0)
        att = jnp.einsum('bthk,btshk,bshk->bths', qc, decay, kc)
        o = jnp.einsum('bths,bshv->bthv', att, vc) + jnp.einsum('bthk,bhkv->bthv', qc * jnp.exp(b), S)
        b_last = b[:, -1]
        S = jnp.exp(b_last)[..., None] * S + jnp.einsum('bshk,bshv->bhkv', kc * jnp.exp(b_last[:, None] - b), vc)
        return S, o

    S, o = lax.scan(step, S0.astype(jnp.float32),
                    (to_chunks(q, C), to_chunks(k, C), to_chunks(v, C), to_chunks(log_f, C)))
    o = rmsnorm(from_chunks(o), out_gain).reshape(B, T, HG_WIDTH) * jax.nn.silu(hg.astype(jnp.float32))
    return o.astype(hq.dtype), S.astype(S0.dtype)


def retention_mixer(rq, rk, rv, rg, S0):
    B, T = rq.shape[:2]
    q = rq.astype(jnp.float32).reshape(B, T, RET_HEADS, RET_DK) * RET_DK ** -0.5
    k = rk.astype(jnp.float32).reshape(B, T, RET_HEADS, RET_DK)
    v = rv.astype(jnp.float32).reshape(B, T, RET_HEADS, RET_DV)
    lg = jnp.log1p(-jnp.exp2(-5.0 - jnp.arange(RET_HEADS, dtype=jnp.float32)))
    C = math.gcd(T, RET_CHUNK)
    t = jnp.arange(C, dtype=jnp.float32)
    rel = t[:, None] - t[None, :]
    D = jnp.where((rel >= 0)[:, :, None], jnp.exp(jnp.maximum(rel, 0.0)[:, :, None] * lg), 0.0)
    D = D.transpose(0, 2, 1)
    q_dec = jnp.exp((t[:, None] + 1.0) * lg)
    k_dec = jnp.exp((C - 1.0 - t)[:, None] * lg)
    chunk_dec = jnp.exp(C * lg)

    def step(S, inp):
        qc, kc, vc = inp
        att = jnp.einsum('bthk,bshk->bths', qc, kc) * D[None]
        o = jnp.einsum('bths,bshv->bthv', att, vc) + jnp.einsum('bthk,bhkv->bthv', qc * q_dec[None, :, :, None], S)
        S = chunk_dec[None, :, None, None] * S + jnp.einsum('bshk,bshv->bhkv', kc * k_dec[None, :, :, None], vc)
        return S, o

    S, o = lax.scan(step, S0.astype(jnp.float32), (to_chunks(q, C), to_chunks(k, C), to_chunks(v, C)))
    o = from_chunks(o)
    mu = jnp.mean(o, axis=-1, keepdims=True)
    var = jnp.mean(jnp.square(o - mu), axis=-1, keepdims=True)
    o = ((o - mu) * lax.rsqrt(var + EPS)).reshape(B, T, RET_WIDTH) * jax.nn.silu(rg.astype(jnp.float32))
    return o.astype(rq.dtype), S.astype(S0.dtype)


def compress_blocks(rows, w_pos):
    B, L = rows.shape[:2]
    R = CMP_BLOCK // CMP_STRIDE
    n_chunk = -(-L // CMP_STRIDE)
    rows = jnp.pad(rows, ((0, 0), (0, n_chunk * CMP_STRIDE - L), (0, 0), (0, 0), (0, 0)))
    chunks = rows.reshape(B, n_chunk, CMP_STRIDE, 2, NSA_KV_HEADS, NSA_HEAD_DIM)
    w = w_pos.reshape(2, R, CMP_STRIDE).astype(rows.dtype)
    n_blk = n_chunk - R + 1
    comp = sum(jnp.einsum('bnsche,cs->bnche', chunks[:, r:r + n_blk], w[:, r]) for r in range(R))
    blk_end = jnp.arange(n_blk) * CMP_STRIDE + CMP_BLOCK - 1
    return comp, blk_end


def nsa_cmp_sel(q, q_pos, kc, vc, blk_end, gather_sel, n_sel, k_gain_sel):
    B, T = q.shape[:2]
    slopes = alibi_slopes()
    qblk = NSA_QBLK if T % NSA_QBLK == 0 else T
    nqb = T // qblk
    topk = min(NSA_TOPK, n_sel)
    R = CMP_BLOCK // CMP_STRIDE
    M = SEL_BLOCK // CMP_STRIDE
    NB = kc.shape[1]
    blk_ids = jnp.arange(n_sel)

    def body(args):
        qb, tb = args
        s = jnp.einsum('bqhgd,bnhd->bqhgn', qb, kc).astype(jnp.float32)
        dist = (tb[:, None] - blk_end[None, :]).astype(jnp.float32)
        s = s - slopes[None, None, :, :, None] * dist[None, :, None, None, :]
        p = masked_softmax(s, (dist >= 0)[None, :, None, None, :])
        o_cmp = jnp.einsum('bqhgn,bnhd->bqhgd', p.astype(vc.dtype), vc)
        imp = jnp.sum(p, axis=3)
        imp = jnp.pad(imp, ((0, 0), (0, 0), (0, 0), (0, M * n_sel + R - 1 - NB)))
        acc = sum(imp[..., r:r + M * n_sel] for r in range(R))
        imp_sel = acc.reshape(B, qblk, NSA_KV_HEADS, n_sel, M).sum(-1)
        cur = tb // SEL_BLOCK
        forced = (blk_ids[None] == 0) | (blk_ids[None] == cur[:, None])
        allowed = blk_ids[None] * SEL_BLOCK <= tb[:, None]
        score = jnp.where(forced[None, :, None], POS_BIG,
                          jnp.where(allowed[None, :, None], imp_sel, NEG_BIG))
        _, idx = lax.top_k(score, topk)
        rows = gather_sel(idx)
        shp = (B, qblk, NSA_KV_HEADS, topk * SEL_BLOCK, NSA_HEAD_DIM)
        ks = rmsnorm(rows[..., 0, :], k_gain_sel).reshape(shp)
        vs = rows[..., 1, :].reshape(shp)
        kpos = (idx[..., None] * SEL_BLOCK + jnp.arange(SEL_BLOCK)).reshape(shp[:4])
        d2 = (tb[None, :, None, None] - kpos).astype(jnp.float32)
        s2 = jnp.einsum('bqhgd,bqhkd->bqhgk', qb, ks).astype(jnp.float32)
        s2 = s2 - slopes[None, None, :, :, None] * d2[:, :, :, None, :]
        p2 = masked_softmax(s2, (d2 >= 0)[:, :, :, None, :])
        o_sel = jnp.einsum('bqhgk,bqhkd->bqhgd', p2.astype(vs.dtype), vs)
        return o_cmp, o_sel

    qs = q.reshape(B, nqb, qblk, NSA_KV_HEADS, NSA_GROUP, NSA_HEAD_DIM).swapaxes(0, 1)
    ts = q_pos.reshape(nqb, qblk)
    o_cmp, o_sel = lax.map(body, (qs, ts))
    o_cmp = o_cmp.swapaxes(0, 1).reshape(B, T, NSA_KV_HEADS, NSA_GROUP, NSA_HEAD_DIM)
    o_sel = o_sel.swapaxes(0, 1).reshape(B, T, NSA_KV_HEADS, NSA_GROUP, NSA_HEAD_DIM)
    return o_cmp, o_sel


def window_attend(q, kv, q_pos, k_pos, k_gain):
    slopes = alibi_slopes()
    k = rmsnorm(kv[..., 0, :, :], k_gain)
    s = jnp.einsum('bnqhgd,bnkhd->bnqhgk', q, k).astype(jnp.float32)
    dist = q_pos[:, :, None] - k_pos[:, None, :]
    valid = (dist >= 0) & (dist < WINDOW) & (k_pos[:, None, :] >= 0)
    s = s - slopes[None, None, None, :, :, None] * dist.astype(jnp.float32)[None, :, :, None, None, :]
    p = masked_softmax(s, valid[None, :, :, None, None, :])
    return jnp.einsum('bnqhgk,bnkhd->bnqhgd', p.astype(kv.dtype), kv[..., 1, :, :])


def nsa_prompt(q, kv4, kvw, lw):
    B, T = q.shape[:2]
    pos = jnp.arange(T)
    comp, blk_end = compress_blocks(kv4[:, :, 0:2], lw['nsa_cmp_w'])
    kc = rmsnorm(comp[:, :, 0], lw['nsa_k_norm'][0])
    vc = comp[:, :, 1]
    n_sel = -(-T // SEL_BLOCK)
    sel = jnp.pad(kv4[:, :, 2:4], ((0, 0), (0, n_sel * SEL_BLOCK - T), (0, 0), (0, 0), (0, 0)))
    sel = sel.reshape(B, n_sel, SEL_BLOCK, 2, NSA_KV_HEADS, NSA_HEAD_DIM)
    b_ix = jnp.arange(B)[:, None, None, None]
    h_ix = jnp.arange(NSA_KV_HEADS)[None, None, :, None]

    def gather(idx):
        return sel[b_ix, idx, :, :, h_ix]

    o_cmp, o_sel = nsa_cmp_sel(q, pos, kc, vc, blk_end, gather, n_sel, lw['nsa_k_norm'][1])
    nq = T // WIN_QBLK
    nw = WINDOW // WIN_QBLK
    padded = jnp.pad(kvw, ((0, 0), (WINDOW, 0), (0, 0), (0, 0), (0, 0)))
    padded = padded.reshape(B, nq + nw, WIN_QBLK, 2, NSA_KV_HEADS, NSA_HEAD_DIM)
    kvb = jnp.concatenate([padded[:, j:j + nq] for j in range(nw + 1)], axis=2)
    qb = q.reshape(B, nq, WIN_QBLK, NSA_KV_HEADS, NSA_GROUP, NSA_HEAD_DIM)
    q_pos = pos.reshape(nq, WIN_QBLK)
    k_pos = (jnp.arange(nq)[:, None] - nw) * WIN_QBLK + jnp.arange((nw + 1) * WIN_QBLK)[None]
    o_win = window_attend(qb, kvb, q_pos, k_pos, lw['nsa_k_norm'][2])
    o_win = o_win.reshape(B, T, NSA_KV_HEADS, NSA_GROUP, NSA_HEAD_DIM)
    win_state = kvw[:, T - min(WINDOW, T):]
    return o_cmp, o_sel, o_win, win_state


def nsa_sample(q, kv4, kvw, lw, pool, win_buf, page_table):
    B, T = q.shape[:2]
    pos = PAST_LEN + jnp.arange(T)
    n_pages = PAST_LEN // PAGE_SIZE
    past_cmp = pool[page_table, :, 0:2].reshape(B, n_pages * PAGE_SIZE, 2, NSA_KV_HEADS, NSA_HEAD_DIM)
    comp, blk_end = compress_blocks(jnp.concatenate([past_cmp, kv4[:, :, 0:2]], axis=1), lw['nsa_cmp_w'])
    kc = rmsnorm(comp[:, :, 0], lw['nsa_k_norm'][0])
    vc = comp[:, :, 1]
    n_past_blk = PAST_LEN // SEL_BLOCK
    n_new_blk = -(-T // SEL_BLOCK)
    n_sel = n_past_blk + n_new_blk
    bpp = PAGE_SIZE // SEL_BLOCK
    pool_blk = pool.reshape(pool.shape[0], bpp, SEL_BLOCK, 4, NSA_KV_HEADS, NSA_HEAD_DIM)
    new_blk = jnp.pad(kv4[:, :, 2:4], ((0, 0), (0, n_new_blk * SEL_BLOCK - T), (0, 0), (0, 0), (0, 0)))
    new_blk = new_blk.reshape(B, n_new_blk, SEL_BLOCK, 2, NSA_KV_HEADS, NSA_HEAD_DIM)
    b_ix = jnp.arange(B)[:, None, None, None]
    h_ix = jnp.arange(NSA_KV_HEADS)[None, None, :, None]

    def gather(idx):
        is_past = idx < n_past_blk
        ip = jnp.minimum(idx, n_past_blk - 1)
        page = page_table[b_ix, ip // bpp]
        past = pool_blk[page, ip % bpp, :, 2:4, h_ix]
        new = new_blk[b_ix, jnp.clip(idx - n_past_blk, 0, n_new_blk - 1), :, :, h_ix]
        return jnp.where(is_past[..., None, None, None], past, new)

    o_cmp, o_sel = nsa_cmp_sel(q, pos, kc, vc, blk_end, gather, n_sel, lw['nsa_k_norm'][1])
    wb = win_buf.shape[1]
    kv_ctx = jnp.concatenate([win_buf, kvw], axis=1)
    k_pos = (PAST_LEN - wb + jnp.arange(wb + T))[None]
    o_win = window_attend(q[:, None], kv_ctx[:, None], pos[None], k_pos, lw['nsa_k_norm'][2])[:, 0]
    win_state = kv_ctx[:, T:]
    return o_cmp, o_sel, o_win, win_state


def trunk_layer(x, lw, lb, hg_state, ret_state, nsa_attend):
    B, T, _ = x.shape
    h = rmsnorm(x, lw['norm_attn']) @ lw['w_in']
    (hq, hf, hi, hgate, nq, nkv, nwkv, nbg, rq, rk, rv, rgate, mg) = split_in(h)
    oA, hg_new = hgrn2_mixer(hq, hf, hi, hgate, lb, lw['hg_out_norm'], hg_state)
    q = rmsnorm(nq.reshape(B, T, NSA_KV_HEADS, NSA_GROUP, NSA_HEAD_DIM), lw['nsa_q_norm']) * NSA_HEAD_DIM ** -0.5
    kv4 = nkv.reshape(B, T, 4, NSA_KV_HEADS, NSA_HEAD_DIM)
    kvw = nwkv.reshape(B, T, 2, NSA_KV_HEADS, NSA_HEAD_DIM)
    o_cmp, o_sel, o_win, win_state = nsa_attend(q, kv4, kvw, lw)
    bg = jax.nn.sigmoid(nbg.reshape(B, T, NSA_KV_HEADS, NSA_GROUP, 3))
    oB = (bg[..., 0:1] * o_cmp + bg[..., 1:2] * o_sel + bg[..., 2:3] * o_win).reshape(B, T, NSA_WIDTH)
    oB = oB.astype(x.dtype)
    oC, ret_new = retention_mixer(rq, rk, rv, rgate, ret_state)
    gA, gB, gC = jnp.split(jax.nn.sigmoid(mg), 3, axis=-1)
    merged = gA * (oA @ lw['w_branch_hg']) + gB * (oB @ lw['w_branch_nsa']) + gC * (oC @ lw['w_branch_ret'])
    x = x + merged @ lw['w_out']
    hf2 = rmsnorm(x, lw['norm_ffn'])
    x = x + (jax.nn.silu(hf2 @ lw['w_gate']) * (hf2 @ lw['w_up'])) @ lw['w_down']
    return x, kv4, win_state, hg_new, ret_new


def setup_inputs(seed: int = 0) -> dict:
    key = jax.random.key(seed)
    ks = jax.random.split(key, 24)
    f32 = jnp.float32
    n_pages = PAST_LEN // PAGE_SIZE
    n_used = DEC_BATCH * n_pages
    n_pool = n_used + max(1, n_used // 4)
    win_buf = min(WINDOW, PAST_LEN)

    def nrm(k, shape, scale):
        return scale * jax.random.normal(k, shape, f32)

    def gain(k, shape):
        return 1.0 + nrm(k, shape, 0.05)

    return {
        'x_prompt': nrm(ks[0], (BATCH, SEQ, D_MODEL), 1.0),
        'x_sample': nrm(ks[1], (DEC_BATCH, DEC_SEQ, D_MODEL), 1.0),
        'cache_nsa': nrm(ks[2], (DEPTH, n_pool, PAGE_SIZE, 4, NSA_KV_HEADS, NSA_HEAD_DIM), 1.0),
        'cache_win': nrm(ks[3], (DEPTH, DEC_BATCH, win_buf, 2, NSA_KV_HEADS, NSA_HEAD_DIM), 1.0),
        'state_hgrn': nrm(ks[4], (DEPTH, DEC_BATCH, HG_HEADS, HG_DK, HG_DV), 0.5),
        'state_ret': nrm(ks[5], (DEPTH, DEC_BATCH, RET_HEADS, RET_DK, RET_DV), 2.0),
        'page_table': jax.random.permutation(ks[6], n_pool)[:n_used].reshape(DEC_BATCH, n_pages).astype(jnp.int32),
        'norm_attn': gain(ks[7], (DEPTH, D_MODEL)),
        'w_in': nrm(ks[8], (DEPTH, D_MODEL, W_IN), D_MODEL ** -0.5),
        'hgrn_lb_logits': nrm(ks[9], (DEPTH, HG_HEADS * HG_DK), 0.5),
        'hgrn_out_norm': gain(ks[10], (DEPTH, HG_DV)),
        'nsa_q_norm': gain(ks[11], (DEPTH, NSA_HEAD_DIM)),
        'nsa_k_norm': gain(ks[12], (DEPTH, 3, NSA_HEAD_DIM)),
        'nsa_cmp_w': (1.0 + nrm(ks[13], (DEPTH, 2, CMP_BLOCK), 0.1)) / CMP_BLOCK,
        'w_branch_hg': nrm(ks[14], (DEPTH, HG_WIDTH, D_MODEL), HG_WIDTH ** -0.5),
        'w_branch_nsa': nrm(ks[15], (DEPTH, NSA_WIDTH, D_MODEL), NSA_WIDTH ** -0.5),
        'w_branch_ret': nrm(ks[16], (DEPTH, RET_WIDTH, D_MODEL), RET_WIDTH ** -0.5),
        'w_out': nrm(ks[17], (DEPTH, D_MODEL, D_MODEL), D_MODEL ** -0.5),
        'norm_ffn': gain(ks[18], (DEPTH, D_MODEL)),
        'w_gate': nrm(ks[19], (DEPTH, D_MODEL, FFN_HIDDEN), D_MODEL ** -0.5),
        'w_up': nrm(ks[20], (DEPTH, D_MODEL, FFN_HIDDEN), D_MODEL ** -0.5),
        'w_down': nrm(ks[21], (DEPTH, FFN_HIDDEN, D_MODEL), FFN_HIDDEN ** -0.5),
    }


def reference(x_prompt, x_sample, cache_nsa, cache_win, state_hgrn, state_ret, page_table,
              norm_attn, w_in, hgrn_lb_logits, hgrn_out_norm, nsa_q_norm, nsa_k_norm, nsa_cmp_w,
              w_branch_hg, w_branch_nsa, w_branch_ret, w_out, norm_ffn, w_gate, w_up, w_down):
    sm = jax.nn.softmax(hgrn_lb_logits.astype(jnp.float32), axis=0)
    lower_bounds = jnp.clip(jnp.cumsum(sm, axis=0) - sm[0:1], 0.0, 1.0 - 1e-6)
    xp, xs = x_prompt, x_sample
    kv_p, kv_s, win_p, win_s, hg_p, hg_s, ret_p, ret_s = [], [], [], [], [], [], [], []
    for l in range(DEPTH):
        lw = {
            'norm_attn': norm_attn[l], 'w_in': w_in[l], 'hg_out_norm': hgrn_out_norm[l],
            'nsa_q_norm': nsa_q_norm[l], 'nsa_k_norm': nsa_k_norm[l], 'nsa_cmp_w': nsa_cmp_w[l],
            'w_branch_hg': w_branch_hg[l], 'w_branch_nsa': w_branch_nsa[l], 'w_branch_ret': w_branch_ret[l],
            'w_out': w_out[l], 'norm_ffn': norm_ffn[l], 'w_gate': w_gate[l], 'w_up': w_up[l], 'w_down': w_down[l],
        }
        hg0 = jnp.zeros((xp.shape[0], HG_HEADS, HG_DK, HG_DV), jnp.float32)
        ret0 = jnp.zeros((xp.shape[0], RET_HEADS, RET_DK, RET_DV), jnp.float32)
        xp, kvp, wp, hp, rp = trunk_layer(xp, lw, lower_bounds[l], hg0, ret0, nsa_prompt)
        sample_attend = functools.partial(nsa_sample, pool=cache_nsa[l], win_buf=cache_win[l], page_table=page_table)
        xs, kvs, wsmp, hs, rs = trunk_layer(xs, lw, lower_bounds[l], state_hgrn[l], state_ret[l], sample_attend)
        kv_p.append(kvp); kv_s.append(kvs); win_p.append(wp); win_s.append(wsmp)
        hg_p.append(hp); hg_s.append(hs); ret_p.append(rp); ret_s.append(rs)
    return (xp, xs, jnp.stack(kv_p), jnp.stack(kv_s), jnp.stack(win_p), jnp.stack(win_s),
            jnp.stack(hg_p), jnp.stack(hg_s), jnp.stack(ret_p), jnp.stack(ret_s))
```

```python
import functools

import jax
import jax.numpy as jnp
from jax import lax
from jax.experimental import pallas as pl
from jax.experimental.pallas import tpu as pltpu

F32 = jnp.float32
MXU_DT = jnp.bfloat16

HG_HEADS, HG_DK, HG_DV = 8, 128, 128
NSA_HEADS, NSA_KV_HEADS, NSA_GROUP, NSA_HEAD_DIM = 16, 4, 4, 64
NSA_WIDTH = NSA_HEADS * NSA_HEAD_DIM
NSA_KV_WIDTH = NSA_KV_HEADS * NSA_HEAD_DIM
CMP_BLOCK, CMP_STRIDE, SEL_BLOCK, NSA_TOPK, WINDOW = 32, 16, 64, 16, 512
RET_HEADS, RET_DK, RET_DV = 8, 128, 128
PAGE_SIZE = 128
EPS = 1e-6
NEG_BIG = -1e30
POS_BIG = 1e30
MIN_F = 1e-20
LOWEST = -3.0e38

LANES = 128
VMEM_LIMIT_BYTES = 56 * 1024 * 1024

HG_SUB_SHIFT = 4
HG_SUB = 1 << HG_SUB_SHIFT
PAGES_PER_STEP = 8
SEL_TK = 256
WIN_TK = 128


def _cparams(*sem):
    return pltpu.CompilerParams(dimension_semantics=sem, vmem_limit_bytes=VMEM_LIMIT_BYTES)


def _dot(a, b):
    return jnp.dot(a, b, preferred_element_type=F32)


def _dot_nt(a, b):
    return lax.dot_general(a, b, (((1,), (1,)), ((), ())), preferred_element_type=F32)


def _split(x, n):
    out = []
    r = x
    for _ in range(n):
        p = r.astype(MXU_DT)
        out.append(p)
        r = r - p.astype(F32)
    return out


def _sigmoid(x):
    return 1.0 / (1.0 + jnp.exp(-x))


def _silu(x):
    return x * _sigmoid(x)


def _iota(shape, dim):
    return lax.broadcasted_iota(jnp.int32, shape, dim)


def _eye(n):
    return (_iota((n, n), 0) == _iota((n, n), 1)).astype(MXU_DT)


def _transpose_exact(x):
    eye = _eye(x.shape[1])
    return sum(_dot_nt(eye, p) for p in _split(x, 3))


def _head_rms(x, bsum, gain):
    ssq = sum(_dot(p, bsum) for p in _split(x * x, 2))
    return x * lax.rsqrt(ssq * (1.0 / NSA_HEAD_DIM) + EPS) * gain


def _rms_matmul_kernel(x_ref, g_ref, w_ref, o_ref, xn_ref):
    @pl.when(pl.program_id(1) == 0)
    def _():
        x = x_ref[...]
        ms = jnp.mean(x * x, axis=-1, keepdims=True)
        xn_ref[...] = (x * lax.rsqrt(ms + EPS) * g_ref[...]).astype(xn_ref.dtype)

    o_ref[...] = _dot(xn_ref[...], w_ref[...])


def rms_matmul(x, gain, w, *, tm, tn):
    m, d = x.shape
    n = w.shape[1]
    return pl.pallas_call(
        _rms_matmul_kernel,
        grid=(m // tm, n // tn),
        in_specs=[pl.BlockSpec((tm, d), lambda i, j: (i, 0)),
                  pl.BlockSpec((1, d), lambda i, j: (0, 0)),
                  pl.BlockSpec((d, tn), lambda i, j: (0, j))],
        out_specs=pl.BlockSpec((tm, tn), lambda i, j: (i, j)),
        out_shape=jax.ShapeDtypeStruct((m, n), F32),
        scratch_shapes=[pltpu.VMEM((tm, d), MXU_DT)],
        compiler_params=_cparams("parallel", "arbitrary"),
        name="in_proj",
    )(x, gain.reshape(1, d), w)


def _ffn_up_kernel(x_ref, g_ref, wg_ref, wu_ref, o_ref, xn_ref):
    @pl.when(pl.program_id(1) == 0)
    def _():
        x = x_ref[...]
        ms = jnp.mean(x * x, axis=-1, keepdims=True)
        xn_ref[...] = (x * lax.rsqrt(ms + EPS) * g_ref[...]).astype(xn_ref.dtype)

    xn = xn_ref[...]
    o_ref[...] = (_silu(_dot(xn, wg_ref[...])) * _dot(xn, wu_ref[...])).astype(o_ref.dtype)


def ffn_up(x, gain, wg, wu, *, tm, tn):
    m, d = x.shape
    f = wg.shape[1]
    return pl.pallas_call(
        _ffn_up_kernel,
        grid=(m // tm, f // tn),
        in_specs=[pl.BlockSpec((tm, d), lambda i, j: (i, 0)),
                  pl.BlockSpec((1, d), lambda i, j: (0, 0)),
                  pl.BlockSpec((d, tn), lambda i, j: (0, j)),
                  pl.BlockSpec((d, tn), lambda i, j: (0, j))],
        out_specs=pl.BlockSpec((tm, tn), lambda i, j: (i, j)),
        out_shape=jax.ShapeDtypeStruct((m, f), MXU_DT),
        scratch_shapes=[pltpu.VMEM((tm, d), MXU_DT)],
        compiler_params=_cparams("parallel", "arbitrary"),
        name="ffn_up",
    )(x, gain.reshape(1, d), wg, wu)


def _matmul_res_kernel(a_ref, w_ref, r_ref, o_ref):
    o_ref[...] = r_ref[...] + _dot(a_ref[...], w_ref[...])


def matmul_res(a, w, res, *, tm, tn):
    m, k = a.shape
    n = w.shape[1]
    return pl.pallas_call(
        _matmul_res_kernel,
        grid=(m // tm, n // tn),
        in_specs=[pl.BlockSpec((tm, k), lambda i, j: (i, 0)),
                  pl.BlockSpec((k, tn), lambda i, j: (0, j)),
                  pl.BlockSpec((tm, tn), lambda i, j: (i, j))],
        out_specs=pl.BlockSpec((tm, tn), lambda i, j: (i, j)),
        out_shape=jax.ShapeDtypeStruct((m, n), F32),
        compiler_params=_cparams("parallel", "arbitrary"),
        name="matmul_res",
    )(a, w, res)


def _merge_kernel(oa_ref, ocmp_ref, osel_ref, owin_ref, oc_ref, nbg_ref, ga_ref, gb_ref, gc_ref,
                  gx_ref, wa_ref, wb_ref, wc_ref, o_ref, ob_ref):
    @pl.when(pl.program_id(1) == 0)
    def _():
        pieces = _split(_sigmoid(nbg_ref[...]), 2)
        acc = None
        for br, o_br in enumerate((ocmp_ref, osel_ref, owin_ref)):
            e = sum(_dot(p, gx_ref[br]) for p in pieces)
            term = e * o_br[...]
            acc = term if acc is None else acc + term
        ob_ref[...] = acc.astype(ob_ref.dtype)

    o_ref[...] = (_sigmoid(ga_ref[...]) * _dot(oa_ref[...], wa_ref[...])
                  + _sigmoid(gb_ref[...]) * _dot(ob_ref[...], wb_ref[...])
                  + _sigmoid(gc_ref[...]) * _dot(oc_ref[...], wc_ref[...])).astype(o_ref.dtype)


def merge_branches(h, lay, oa, o_cmp, o_sel, o_win, oc, gate_expand, wa, wb, wc, *, tm, tn):
    m = h.shape[0]
    d = wa.shape[1]
    wdt = oa.shape[1]
    ca, cb, cc = ((lay["mg"] + i * d) // tn for i in range(3))
    cn = lay["nbg"] // LANES
    row = lambda i, j: (i, 0)
    return pl.pallas_call(
        _merge_kernel,
        grid=(m // tm, d // tn),
        in_specs=[pl.BlockSpec((tm, wdt), row), pl.BlockSpec((tm, wdt), row),
                  pl.BlockSpec((tm, wdt), row), pl.BlockSpec((tm, wdt), row),
                  pl.BlockSpec((tm, wdt), row),
                  pl.BlockSpec((tm, LANES), lambda i, j: (i, cn)),
                  pl.BlockSpec((tm, tn), lambda i, j: (i, ca + j)),
                  pl.BlockSpec((tm, tn), lambda i, j: (i, cb + j)),
                  pl.BlockSpec((tm, tn), lambda i, j: (i, cc + j)),
                  pl.BlockSpec((3, LANES, wdt), lambda i, j: (0, 0, 0)),
                  pl.BlockSpec((wdt, tn), lambda i, j: (0, j)),
                  pl.BlockSpec((wdt, tn), lambda i, j: (0, j)),
                  pl.BlockSpec((wdt, tn), lambda i, j: (0, j))],
        out_specs=pl.BlockSpec((tm, tn), lambda i, j: (i, j)),
        out_shape=jax.ShapeDtypeStruct((m, d), MXU_DT),
        scratch_shapes=[pltpu.VMEM((tm, wdt), MXU_DT)],
        compiler_params=_cparams("parallel", "arbitrary"),
        name="merge",
    )(oa, o_cmp, o_sel, o_win, oc, h, h, h, h, gate_expand, wa, wb, wc)


def _hgrn_kernel(hq_ref, hf_ref, hi_ref, hg_ref, lb_ref, gain_ref, s0_ref, o_ref, sout_ref,
                 st_ref, ut_ref, *, tt, t_valid):
    c = HG_SUB
    nj = tt // c
    tb = pl.program_id(2)

    @pl.when(tb == 0)
    def _():
        st_ref[...] = _transpose_exact(s0_ref[0, 0])

    hq = hq_ref[0]
    lb = lb_ref[...]
    q = _silu(hq)
    sig = _sigmoid(hf_ref[0])
    g = jnp.log(jnp.maximum(lb + (1.0 - lb) * sig, MIN_F))
    kk = (1.0 - lb) * (1.0 - sig)
    v = hi_ref[0]
    row = _iota((tt, HG_DK), 0)
    if t_valid is not None:
        live = (tb * tt + row) < t_valid
        g = jnp.where(live, g, 0.0)
        kk = jnp.where(live, kk, 0.0)

    r2 = _iota((tt, tt), 0)
    c2 = _iota((tt, tt), 1)
    same = (r2 >> HG_SUB_SHIFT) == (c2 >> HG_SUB_SHIFT)
    gp = _split(g, 3)
    tri = (same & (c2 <= r2)).astype(MXU_DT)
    blk = same.astype(MXU_DT)
    b = sum(_dot(tri, p) for p in gp)
    dtot = sum(_dot(blk, p) for p in gp)
    qe = q * jnp.exp(b)
    ke = kk * jnp.exp(dtot - b)
    edec = jnp.exp(dtot)

    q3 = q.reshape(nj, c, HG_DK)
    kk3 = kk.reshape(nj, c, HG_DK)
    b3 = b.reshape(nj, c, HG_DK)
    v3 = v.reshape(nj, c, HG_DV)
    tpos = _iota((nj, c, HG_DK), 1)
    ones = jnp.ones((HG_DK, HG_DV), MXU_DT)
    o3 = jnp.zeros((nj, c, HG_DV), F32)
    for s in range(c):
        m = tpos >= s
        diff = jnp.where(m, b3 - b3[:, s:s + 1, :], 0.0)
        y = jnp.where(m, q3 * kk3[:, s:s + 1, :] * jnp.exp(diff), 0.0)
        z = _dot(y.reshape(tt, HG_DK).astype(MXU_DT), ones).reshape(nj, c, HG_DV)
        o3 = o3 + z * v3[:, s:s + 1, :]
    o = o3.reshape(tt, HG_DV)

    vt = _transpose_exact(v).astype(MXU_DT)
    for j in range(nj):
        kej = jnp.where((row >> HG_SUB_SHIFT) == j, ke, 0.0).astype(MXU_DT)
        ut_ref[j] = _dot(vt, kej)

    qe_b = qe.astype(MXU_DT)
    outs = []
    for j in range(nj):
        st = st_ref[...]
        outs.append(_dot_nt(qe_b[j * c:(j + 1) * c], st.astype(MXU_DT)))
        st_ref[...] = st * edec[j * c:j * c + 1, :] + ut_ref[j]
    o = o + jnp.concatenate(outs, axis=0)

    on = o * lax.rsqrt(jnp.mean(o * o, axis=-1, keepdims=True) + EPS) * gain_ref[...]
    o_ref[0] = (on * _silu(hg_ref[0])).astype(o_ref.dtype)

    @pl.when(tb == pl.num_programs(2) - 1)
    def _():
        sout_ref[0, 0] = _transpose_exact(st_ref[...])


def hgrn_mixer(h3, lay, lb, out_gain, s0, *, tt, t_valid):
    bsz, t, _ = h3.shape
    cq, cf, ci, cg = (lay[k] // LANES for k in ("hq", "hf", "hi", "hg"))
    col = lambda c0: (lambda b, hd, tb: (b, tb, c0 + hd))
    kern = functools.partial(_hgrn_kernel, tt=tt, t_valid=t_valid)
    return pl.pallas_call(
        kern,
        grid=(bsz, HG_HEADS, t // tt),
        in_specs=[pl.BlockSpec((1, tt, LANES), col(cq)), pl.BlockSpec((1, tt, LANES), col(cf)),
                  pl.BlockSpec((1, tt, LANES), col(ci)), pl.BlockSpec((1, tt, LANES), col(cg)),
                  pl.BlockSpec((1, HG_DK), lambda b, hd, tb: (0, hd)),
                  pl.BlockSpec((1, HG_DV), lambda b, hd, tb: (0, 0)),
                  pl.BlockSpec((1, 1, HG_DK, HG_DV), lambda b, hd, tb: (b, hd, 0, 0))],
        out_specs=[pl.BlockSpec((1, tt, HG_DV), lambda b, hd, tb: (b, tb, hd)),
                   pl.BlockSpec((1, 1, HG_DK, HG_DV), lambda b, hd, tb: (b, hd, 0, 0))],
        out_shape=[jax.ShapeDtypeStruct((bsz, t, HG_HEADS * HG_DV), MXU_DT),
                   jax.ShapeDtypeStruct((bsz, HG_HEADS, HG_DK, HG_DV), F32)],
        scratch_shapes=[pltpu.VMEM((HG_DV, HG_DK), F32),
                        pltpu.VMEM((tt // HG_SUB, HG_DV, HG_DK), F32)],
        compiler_params=_cparams("parallel", "parallel", "arbitrary"),
        name="hgrn",
    )(h3, h3, h3, h3, lb, out_gain.reshape(1, HG_DV), s0)


def _ret_kernel(q_ref, k_ref, v_ref, g_ref, lg_ref, s0_ref, o_ref, sout_ref, s_ref, *, tt, n_valid):
    tb = pl.program_id(2)

    @pl.when(tb == 0)
    def _():
        s_ref[...] = s0_ref[0, 0]

    lgl = lg_ref[0][:, :LANES]
    lgt = lg_ref[0][:, :tt]
    q = q_ref[0] * (RET_DK ** -0.5)
    k = k_ref[0]
    vb = v_ref[0].astype(MXU_DT)
    rel = (_iota((tt, tt), 0) - _iota((tt, tt), 1)).astype(F32)
    dmat = jnp.where(rel >= 0, jnp.exp(jnp.maximum(rel, 0.0) * lgt), 0.0)
    att = _dot_nt(q.astype(MXU_DT), k.astype(MXU_DT)) * dmat
    q_dec = jnp.exp((_iota((tt, RET_DK), 0) + 1).astype(F32) * lgl)
    s = s_ref[...]
    o = _dot(att.astype(MXU_DT), vb) + _dot((q * q_dec).astype(MXU_DT), s.astype(MXU_DT))

    spos = _iota((RET_DK, tt), 1)
    k_dec = jnp.where(spos < n_valid, jnp.exp(jnp.maximum(n_valid - 1 - spos, 0).astype(F32) * lgt), 0.0)
    kt = _transpose_exact(k)
    s_ref[...] = jnp.exp(float(n_valid) * lgl) * s + _dot((kt * k_dec).astype(MXU_DT), vb)

    mu = jnp.mean(o, axis=-1, keepdims=True)
    var = jnp.mean(jnp.square(o - mu), axis=-1, keepdims=True)
    o_ref[0] = ((o - mu) * lax.rsqrt(var + EPS) * _silu(g_ref[0])).astype(o_ref.dtype)

    @pl.when(tb == pl.num_programs(2) - 1)
    def _():
        sout_ref[0, 0] = s_ref[...]


def retention_mixer(h3, lay, lg_tab, s0, *, tt, n_valid):
    bsz, t, _ = h3.shape
    assert n_valid == tt or t == tt
    cq, ck, cv, cg = (lay[k] // LANES for k in ("rq", "rk", "rv", "rg"))
    col = lambda c0: (lambda b, hd, tb: (b, tb, c0 + hd))
    kern = functools.partial(_ret_kernel, tt=tt, n_valid=n_valid)
    return pl.pallas_call(
        kern,
        grid=(bsz, RET_HEADS, t // tt),
        in_specs=[pl.BlockSpec((1, tt, LANES), col(cq)), pl.BlockSpec((1, tt, LANES), col(ck)),
                  pl.BlockSpec((1, tt, LANES), col(cv)), pl.BlockSpec((1, tt, LANES), col(cg)),
                  pl.BlockSpec((1, 1, lg_tab.shape[2]), lambda b, hd, tb: (hd, 0, 0)),
                  pl.BlockSpec((1, 1, RET_DK, RET_DV), lambda b, hd, tb: (b, hd, 0, 0))],
        out_specs=[pl.BlockSpec((1, tt, RET_DV), lambda b, hd, tb: (b, tb, hd)),
                   pl.BlockSpec((1, 1, RET_DK, RET_DV), lambda b, hd, tb: (b, hd, 0, 0))],
        out_shape=[jax.ShapeDtypeStruct((bsz, t, RET_HEADS * RET_DV), MXU_DT),
                   jax.ShapeDtypeStruct((bsz, RET_HEADS, RET_DK, RET_DV), F32)],
        scratch_shapes=[pltpu.VMEM((RET_DK, RET_DV), F32)],
        compiler_params=_cparams("parallel", "parallel", "arbitrary"),
        name="retention",
    )(h3, h3, h3, h3, lg_tab, s0)


def _store_heads(ref, x, lead=()):
    for hh in range(NSA_KV_HEADS):
        ref[lead + (hh,)] = x[:, hh * NSA_HEAD_DIM:(hh + 1) * NSA_HEAD_DIM].astype(ref.dtype)


def _q_prep_kernel(q_ref, gain_ref, bsum_ref, o_ref):
    x = q_ref[0]
    for kvh in range(NSA_KV_HEADS):
        lo = kvh * NSA_KV_WIDTH
        qn = _head_rms(x[:, lo:lo + NSA_KV_WIDTH], bsum_ref[...], gain_ref[...]) * (NSA_HEAD_DIM ** -0.5)
        _store_heads(o_ref, qn, (0, kvh))


def q_prep(h3, lay, gain_q, bsum, *, tr):
    bsz, t, _ = h3.shape
    cq = lay["nq"] // NSA_WIDTH
    return pl.pallas_call(
        _q_prep_kernel,
        grid=(bsz, t // tr),
        in_specs=[pl.BlockSpec((1, tr, NSA_WIDTH), lambda b, i: (b, i, cq)),
                  pl.BlockSpec((1, NSA_KV_WIDTH), lambda b, i: (0, 0)),
                  pl.BlockSpec((NSA_KV_WIDTH, NSA_KV_WIDTH), lambda b, i: (0, 0))],
        out_specs=pl.BlockSpec((1, NSA_KV_HEADS, NSA_GROUP, tr, NSA_HEAD_DIM), lambda b, i: (b, 0, 0, i, 0)),
        out_shape=jax.ShapeDtypeStruct((bsz, NSA_KV_HEADS, NSA_GROUP, t, NSA_HEAD_DIM), MXU_DT),
        compiler_params=_cparams("parallel", "parallel"),
        name="nsa_q_prep",
    )(h3, jnp.tile(gain_q, NSA_KV_HEADS).reshape(1, NSA_KV_WIDTH), bsum)


def _kv_prep_kernel(*refs, n_in, rows):
    refs = refs[-(n_in + 4):]
    gain_ref, bsum_ref, ko_ref, vo_ref = refs[n_in:]
    for u in range(n_in):
        x = refs[u][0]
        kn = _head_rms(x[:, :NSA_KV_WIDTH], bsum_ref[...], gain_ref[...])
        for hh in range(NSA_KV_HEADS):
            sl = slice(hh * NSA_HEAD_DIM, (hh + 1) * NSA_HEAD_DIM)
            ko_ref[0, hh, u * rows:(u + 1) * rows, :] = kn[:, sl].astype(ko_ref.dtype)
            vo_ref[0, hh, u * rows:(u + 1) * rows, :] = x[:, NSA_KV_WIDTH + sl.start:NSA_KV_WIDTH + sl.stop].astype(vo_ref.dtype)


def kv_prep(x3, col, gain, bsum, *, tr):
    bsz, t, _ = x3.shape
    w2 = 2 * NSA_KV_WIDTH
    kern = functools.partial(_kv_prep_kernel, n_in=1, rows=tr)
    oshape = jax.ShapeDtypeStruct((bsz, NSA_KV_HEADS, t, NSA_HEAD_DIM), MXU_DT)
    ospec = pl.BlockSpec((1, NSA_KV_HEADS, tr, NSA_HEAD_DIM), lambda b, i: (b, 0, i, 0))
    return pl.pallas_call(
        kern,
        grid=(bsz, t // tr),
        in_specs=[pl.BlockSpec((1, tr, w2), lambda b, i: (b, i, col)),
                  pl.BlockSpec((1, NSA_KV_WIDTH), lambda b, i: (0, 0)),
                  pl.BlockSpec((NSA_KV_WIDTH, NSA_KV_WIDTH), lambda b, i: (0, 0))],
        out_specs=[ospec, ospec],
        out_shape=[oshape, oshape],
        compiler_params=_cparams("parallel", "parallel"),
        name="nsa_kv_prep",
    )(x3, jnp.tile(gain, NSA_KV_HEADS).reshape(1, NSA_KV_WIDTH), bsum)


def kv_prep_paged(pool3, pages, gain, bsum):
    bsz, n_pages = pages.shape
    pp = PAGES_PER_STEP
    w2 = 2 * NSA_KV_WIDTH
    kern = functools.partial(_kv_prep_kernel, n_in=pp, rows=PAGE_SIZE)
    oshape = jax.ShapeDtypeStruct((bsz, NSA_KV_HEADS, n_pages * PAGE_SIZE, NSA_HEAD_DIM), MXU_DT)
    ospec = pl.BlockSpec((1, NSA_KV_HEADS, pp * PAGE_SIZE, NSA_HEAD_DIM), lambda b, i, pt: (b, 0, i, 0))
    page_spec = lambda u: pl.BlockSpec((1, PAGE_SIZE, w2), lambda b, i, pt: (pt[b, i * pp + u], 0, 1))
    return pl.pallas_call(
        kern,
        grid_spec=pltpu.PrefetchScalarGridSpec(
            num_scalar_prefetch=1,
            grid=(bsz, n_pages // pp),
            in_specs=[page_spec(u) for u in range(pp)]
                     + [pl.BlockSpec((1, NSA_KV_WIDTH), lambda b, i, pt: (0, 0)),
                        pl.BlockSpec((NSA_KV_WIDTH, NSA_KV_WIDTH), lambda b, i, pt: (0, 0))],
            out_specs=[ospec, ospec]),
        out_shape=[oshape, oshape],
        compiler_params=_cparams("parallel", "arbitrary"),
        name="nsa_kv_prep_paged",
    )(pages, *([pool3] * pp), jnp.tile(gain, NSA_KV_HEADS).reshape(1, NSA_KV_WIDTH), bsum)


def _cmp_prep_kernel(*refs, n_in, rows):
    refs = refs[-(n_in + 7):]
    nxt_ref, tail_ref, w_ref, gain_ref, bsum_ref, kc_ref, vc_ref = refs[n_in:]
    cs = CMP_STRIDE
    nb = rows // cs
    w0 = w_ref[0]
    w1 = w_ref[1]
    a0 = []
    a1 = []
    for u in range(n_in):
        x3 = refs[u][0].reshape(nb, cs, 2 * NSA_KV_WIDTH)
        a0.append(jnp.sum(x3 * w0[None], axis=1))
        a1.append(jnp.sum(x3 * w1[None], axis=1))
    a0 = jnp.concatenate(a0, axis=0) if n_in > 1 else a0[0]
    a1 = jnp.concatenate(a1, axis=0) if n_in > 1 else a1[0]
    last = pl.program_id(1) == pl.num_programs(1) - 1
    nx = jnp.where(last, tail_ref[0], nxt_ref[0])
    a1_next = jnp.sum(nx * w1, axis=0, keepdims=True)
    tot = nb * n_in
    a1s = pltpu.roll(a1, tot - 1, 0)
    a1s = jnp.where(_iota(a1s.shape, 0) == tot - 1, a1_next, a1s)
    comp = a0 + a1s
    kc = _head_rms(comp[:, :NSA_KV_WIDTH], bsum_ref[...], gain_ref[...])
    _store_heads(kc_ref, kc, (0,))
    _store_heads(vc_ref, comp[:, NSA_KV_WIDTH:], (0,))


def _cmp_weight_table(cmp_w):
    w = cmp_w.reshape(2, CMP_BLOCK // CMP_STRIDE, CMP_STRIDE)
    w = jnp.transpose(w, (1, 2, 0))
    return jnp.repeat(w, NSA_KV_WIDTH, axis=2)


def cmp_prep(x3, col, tail16, cmp_w, gain, bsum, *, tr):
    bsz, t, _ = x3.shape
    w2 = 2 * NSA_KV_WIDTH
    nbt = tr // CMP_STRIDE
    n_steps = t // tr
    kern = functools.partial(_cmp_prep_kernel, n_in=1, rows=tr)
    oshape = jax.ShapeDtypeStruct((bsz, NSA_KV_HEADS, t // CMP_STRIDE, NSA_HEAD_DIM), MXU_DT)
    ospec = pl.BlockSpec((1, NSA_KV_HEADS, nbt, NSA_HEAD_DIM), lambda b, i: (b, 0, i, 0))
    chunks_per_step = tr // CMP_STRIDE
    return pl.pallas_call(
        kern,
        grid=(bsz, n_steps),
        in_specs=[pl.BlockSpec((1, tr, w2), lambda b, i: (b, i, col)),
                  pl.BlockSpec((1, CMP_STRIDE, w2),
                               lambda b, i: (b, jnp.minimum(i + 1, n_steps - 1) * chunks_per_step, col)),
                  pl.BlockSpec((1, CMP_STRIDE, w2), lambda b, i: (b, 0, 0)),
                  pl.BlockSpec((2, CMP_STRIDE, w2), lambda b, i: (0, 0, 0)),
                  pl.BlockSpec((1, NSA_KV_WIDTH), lambda b, i: (0, 0)),
                  pl.BlockSpec((NSA_KV_WIDTH, NSA_KV_WIDTH), lambda b, i: (0, 0))],
        out_specs=[ospec, ospec],
        out_shape=[oshape, oshape],
        compiler_params=_cparams("parallel", "arbitrary"),
        name="nsa_cmp_prep",
    )(x3, x3, tail16, _cmp_weight_table(cmp_w), jnp.tile(gain, NSA_KV_HEADS).reshape(1, NSA_KV_WIDTH), bsum)


def cmp_prep_paged(pool3, pages, tail16, cmp_w, gain, bsum):
    bsz, n_pages = pages.shape
    pp = PAGES_PER_STEP
    w2 = 2 * NSA_KV_WIDTH
    n_steps = n_pages // pp
    nbt = pp * PAGE_SIZE // CMP_STRIDE
    kern = functools.partial(_cmp_prep_kernel, n_in=pp, rows=PAGE_SIZE)
    oshape = jax.ShapeDtypeStruct((bsz, NSA_KV_HEADS, n_pages * PAGE_SIZE // CMP_STRIDE, NSA_HEAD_DIM), MXU_DT)
    ospec = pl.BlockSpec((1, NSA_KV_HEADS, nbt, NSA_HEAD_DIM), lambda b, i, pt: (b, 0, i, 0))
    page_spec = lambda u: pl.BlockSpec((1, PAGE_SIZE, w2), lambda b, i, pt: (pt[b, i * pp + u], 0, 0))
    const = lambda *shape: pl.BlockSpec(shape, lambda b, i, pt: (0,) * len(shape))
    return pl.pallas_call(
        kern,
        grid_spec=pltpu.PrefetchScalarGridSpec(
            num_scalar_prefetch=1,
            grid=(bsz, n_steps),
            in_specs=[page_spec(u) for u in range(pp)]
                     + [pl.BlockSpec((1, CMP_STRIDE, w2),
                                     lambda b, i, pt: (pt[b, jnp.minimum(i + 1, n_steps - 1) * pp], 0, 0)),
                        pl.BlockSpec((1, CMP_STRIDE, w2), lambda b, i, pt: (b, 0, 0)),
                        const(2, CMP_STRIDE, w2), const(1, NSA_KV_WIDTH), const(NSA_KV_WIDTH, NSA_KV_WIDTH)],
            out_specs=[ospec, ospec]),
        out_shape=[oshape, oshape],
        compiler_params=_cparams("parallel", "arbitrary"),
        name="nsa_cmp_prep_paged",
    )(pages, *([pool3] * pp), pool3, tail16, _cmp_weight_table(cmp_w),
      jnp.tile(gain, NSA_KV_HEADS).reshape(1, NSA_KV_WIDTH), bsum)


def _nsa_kernel(q_ref, sl_ref, kc_ref, vc_ref, ks_ref, vs_ref, kw_ref, vw_ref, at_ref,
                ocmp_ref, osel_ref, owin_ref, sc_ref, m_ref, l_ref, acc_ref,
                *, tq, qpos0, wpos0, n_sel, topk):
    rows = NSA_GROUP * tq
    hd = NSA_HEAD_DIM
    t0 = qpos0 + pl.program_id(2) * tq
    q = q_ref[0, 0].reshape(rows, hd)
    slope = sl_ref[0][:, :1]
    nbp = kc_ref.shape[2]
    n_sel_pad, tl = sc_ref.shape

    def tok(shape):
        return t0 + (_iota(shape, 0) & (tq - 1))

    s = _dot_nt(q, kc_ref[0, 0])
    dist = tok((rows, nbp)) - (_iota((rows, nbp), 1) * CMP_STRIDE + (CMP_BLOCK - 1))
    valid = dist >= 0
    s = jnp.where(valid, s - slope * dist.astype(F32), NEG_BIG)
    p = jnp.where(valid, jnp.exp(s - jnp.max(s, axis=-1, keepdims=True)), 0.0)
    p = p / jnp.maximum(jnp.sum(p, axis=-1, keepdims=True), 1e-30)
    o_cmp = _dot(p.astype(MXU_DT), vc_ref[0, 0])

    imp = p[0:tq] + p[tq:2 * tq] + p[2 * tq:3 * tq] + p[3 * tq:4 * tq]
    if tl > tq:
        imp = jnp.concatenate([imp, jnp.zeros((tl - tq, nbp), F32)], axis=0)
    imp_sel = sum(_dot_nt(at_ref[...], piece) for piece in _split(imp, 3))
    j = _iota((n_sel_pad, tl), 0)
    tt = t0 + _iota((n_sel_pad, tl), 1)
    forced = (j == 0) | (j == (tt >> 6))
    allowed = (j << 6) <= tt
    score = jnp.where(forced, POS_BIG, jnp.where(allowed, imp_sel, NEG_BIG))
    score = jnp.where(j >= n_sel, LOWEST, score)
    jf = j.astype(F32)
    sel = jnp.zeros((n_sel_pad, tl), F32)
    for _ in range(topk):
        mx = jnp.max(score, axis=0, keepdims=True)
        first = jnp.min(jnp.where(score == mx, jf, 1e9), axis=0, keepdims=True)
        pick = jf == first
        sel = jnp.where(pick, 1.0, sel)
        score = jnp.where(pick, -jnp.inf, score)
    sel = jnp.where(allowed, sel, 0.0).astype(MXU_DT)
    spread = ((_iota((rows, tl), 0) & (tq - 1)) == _iota((rows, tl), 1)).astype(MXU_DT)
    selb = _dot_nt(spread, sel).astype(MXU_DT)

    def softmax_pass(k_ref, v_ref, tk, lo, hi, kpos0, mask_fn):
        m_ref[...] = jnp.full(m_ref.shape, NEG_BIG, F32)
        l_ref[...] = jnp.zeros(l_ref.shape, F32)
        acc_ref[...] = jnp.zeros(acc_ref.shape, F32)
        rel = (_iota((rows, tk), 0) & (tq - 1)) - _iota((rows, tk), 1)
        reps = tk // LANES

        def body(kt, carry):
            k0 = pl.multiple_of(kt * tk, tk)
            s = _dot_nt(q, k_ref[0, 0, pl.ds(k0, tk), :])
            d = rel + (t0 - kpos0 - k0)
            valid = mask_fn(d, k0)
            s = jnp.where(valid, s - slope * d.astype(F32), NEG_BIG)
            m_prev = m_ref[...]
            m_new = jnp.maximum(m_prev, jnp.max(s, axis=-1, keepdims=True))
            alpha = jnp.exp(m_prev - m_new)
            mrep = m_new if reps == 1 else jnp.concatenate([m_new] * reps, axis=1)
            p = jnp.where(valid, jnp.exp(s - mrep), 0.0)
            l_ref[...] = alpha * l_ref[...] + jnp.sum(p, axis=-1, keepdims=True)
            acc_ref[...] = alpha[:, :hd] * acc_ref[...] + _dot(p.astype(MXU_DT), v_ref[0, 0, pl.ds(k0, tk), :])
            m_ref[...] = m_new
            return carry

        lax.fori_loop(lo, hi, body, 0)
        return acc_ref[...] / jnp.maximum(l_ref[...][:, :hd], 1e-30)

    n_kt_all = ks_ref.shape[2] // SEL_TK
    hi_sel = jnp.minimum(n_kt_all, (t0 + tq - 1) // SEL_TK + 1)

    def sel_mask(d, k0):
        blk_of_key = (k0 + _iota((n_sel_pad, SEL_TK), 1)) >> 6
        expand = (_iota((n_sel_pad, SEL_TK), 0) == blk_of_key).astype(MXU_DT)
        return (_dot(selb, expand) > 0.5) & (d >= 0)

    o_sel = softmax_pass(ks_ref, vs_ref, SEL_TK, 0, hi_sel, 0, sel_mask)

    n_wt_all = kw_ref.shape[2] // WIN_TK
    lo_win = jnp.maximum(t0 - (WINDOW - 1) - wpos0, 0) // WIN_TK
    hi_win = jnp.minimum(n_wt_all, (t0 + tq - 1 - wpos0) // WIN_TK + 1)
    o_win = softmax_pass(kw_ref, vw_ref, WIN_TK, lo_win, hi_win, wpos0,
                         lambda d, k0: (d >= 0) & (d < WINDOW))

    for g in range(NSA_GROUP):
        sl = slice(g * hd, (g + 1) * hd)
        ocmp_ref[0, :, sl] = o_cmp[g * tq:(g + 1) * tq]
        osel_ref[0, :, sl] = o_sel[g * tq:(g + 1) * tq]
        owin_ref[0, :, sl] = o_win[g * tq:(g + 1) * tq]


def _imp_to_sel_matrix(n_sel_pad, nbp):
    m = SEL_BLOCK // CMP_STRIDE
    r = CMP_BLOCK // CMP_STRIDE
    jj = jnp.arange(n_sel_pad)[:, None]
    ii = jnp.arange(nbp)[None, :]
    cnt = sum(((ii - rr) >= m * jj) & ((ii - rr) < m * (jj + 1)) for rr in range(r))
    return cnt.astype(MXU_DT)


def nsa_attention(qn, slopes, kc, vc, ks, vs, kw, vw, *, tq, qpos0, wpos0, n_sel):
    bsz, _, _, tql, hd = qn.shape
    nbp, tk_all, tw_all = kc.shape[2], ks.shape[2], kw.shape[2]
    assert tq & (tq - 1) == 0 and tk_all % SEL_TK == 0 and tw_all % WIN_TK == 0
    n_sel_pad = -(-n_sel // 64) * 64
    tl = max(tq, LANES)
    rows = NSA_GROUP * tq
    at = _imp_to_sel_matrix(n_sel_pad, nbp)
    kern = functools.partial(_nsa_kernel, tq=tq, qpos0=qpos0, wpos0=wpos0, n_sel=n_sel,
                             topk=min(NSA_TOPK, n_sel))
    full = lambda n: pl.BlockSpec((1, 1, n, hd), lambda b, kh, i: (b, kh, 0, 0))
    ospec = pl.BlockSpec((1, tq, NSA_GROUP * hd), lambda b, kh, i: (b, i, kh))
    oshape = jax.ShapeDtypeStruct((bsz, tql, NSA_WIDTH), F32)
    return pl.pallas_call(
        kern,
        grid=(bsz, NSA_KV_HEADS, tql // tq),
        in_specs=[pl.BlockSpec((1, 1, NSA_GROUP, tq, hd), lambda b, kh, i: (b, kh, 0, i, 0)),
                  pl.BlockSpec((1, rows, LANES), lambda b, kh, i: (kh, 0, 0)),
                  full(nbp), full(nbp), full(tk_all), full(tk_all), full(tw_all), full(tw_all),
                  pl.BlockSpec((n_sel_pad, nbp), lambda b, kh, i: (0, 0))],
        out_specs=[ospec, ospec, ospec],
        out_shape=[oshape, oshape, oshape],
        scratch_shapes=[pltpu.VMEM((n_sel_pad, tl), F32),
                        pltpu.VMEM((rows, LANES), F32), pltpu.VMEM((rows, LANES), F32),
                        pltpu.VMEM((rows, hd), F32)],
        compiler_params=_cparams("parallel", "parallel", "arbitrary"),
        name="nsa_attention",
    )(qn, slopes, kc, vc, ks, vs, kw, vw, at)


def _layout(d_model):
    hw, rw = HG_HEADS * HG_DK, RET_HEADS * RET_DK
    names = [("hq", hw), ("hf", hw), ("hi", hw), ("hg", hw), ("nq", NSA_WIDTH), ("nkv", 4 * NSA_KV_WIDTH),
             ("nwkv", 2 * NSA_KV_WIDTH), ("rq", rw), ("rk", rw), ("rv", rw), ("rg", rw), ("mg", 3 * d_model),
             ("nbg", 3 * NSA_HEADS)]
    lay, off = {}, 0
    for name, width in names:
        lay[name] = off
        off += width
    lay["used"] = off
    lay["total"] = -(-off // 1024) * 1024
    return lay


def _reorder_w_in(w_in, lay):
    a = lay["rq"]
    nb = 3 * NSA_HEADS
    d = w_in.shape[0]
    pad = jnp.zeros((d, lay["total"] - lay["used"]), w_in.dtype)
    return jnp.concatenate([w_in[:, :a], w_in[:, a + nb:], w_in[:, a:a + nb], pad], axis=1).astype(MXU_DT)


def _block_sum_matrix(width):
    i = jnp.arange(width)
    return (i[:, None] // NSA_HEAD_DIM == i[None, :] // NSA_HEAD_DIM).astype(MXU_DT)


def _gate_expand_matrix():
    c = jnp.arange(LANES)[:, None]
    col = jnp.arange(NSA_WIDTH)[None, :]
    return jnp.stack([(c == (col // NSA_HEAD_DIM) * 3 + br) for br in range(3)]).astype(MXU_DT)


def _slope_rows(tq):
    hh = jnp.arange(1, NSA_HEADS + 1, dtype=F32)
    slopes = jnp.exp2(-8.0 * hh / NSA_HEADS).reshape(NSA_KV_HEADS, NSA_GROUP)
    rows = jnp.repeat(slopes, tq, axis=1)
    return jnp.broadcast_to(rows[:, :, None], (NSA_KV_HEADS, NSA_GROUP * tq, LANES))


def _pad_rows(x, n):
    return jnp.pad(x, ((0, 0), (0, n - x.shape[1])) + ((0, 0),) * (x.ndim - 2))


def _trunk_layer(x, lw, lay, consts, nsa_fn, hg_state, ret_state, *, seq_tiles):
    bsz, t, d = x.shape
    tm, tt_h, tt_r, t_valid = seq_tiles
    x2 = x.reshape(bsz * t, d)
    h = rms_matmul(x2, lw["norm_attn"], lw["w_in"], tm=tm, tn=1024)
    np_ = h.shape[1]
    h3 = h.reshape(bsz, t, np_)
    tp = -(-t // tt_h) * tt_h
    h3p = _pad_rows(h3, tp) if tp != t else h3
    oa, hg_new = hgrn_mixer(h3p, lay, lw["lb"], lw["hg_out_norm"], hg_state, tt=tt_h,
                            t_valid=None if tp == t else t_valid)
    tpr = -(-t // tt_r) * tt_r
    h3r = _pad_rows(h3, tpr) if tpr != t else h3
    oc, ret_new = retention_mixer(h3r, lay, consts["lg_tab"], ret_state, tt=tt_r,
                                  n_valid=tt_r if tpr == t else t_valid)
    o_cmp, o_sel, o_win = nsa_fn(h3)
    flat = lambda a: a[:, :t].reshape(bsz * t, a.shape[-1])
    merged = merge_branches(h, lay, flat(oa), flat(o_cmp), flat(o_sel), flat(o_win), flat(oc),
                            consts["gate_expand"], lw["w_branch_hg"], lw["w_branch_nsa"], lw["w_branch_ret"],
                            tm=tm, tn=512)
    x2 = matmul_res(merged, lw["w_out"], x2, tm=tm, tn=512)
    hid = ffn_up(x2, lw["norm_ffn"], lw["w_gate"], lw["w_up"], tm=tm, tn=512)
    x2 = matmul_res(hid, lw["w_down"], x2, tm=tm, tn=512)
    return x2.reshape(bsz, t, d), h3, hg_new, ret_new


def kernel(x_prompt, x_sample, cache_nsa, cache_win, state_hgrn, state_ret, page_table,
           norm_attn, w_in, hgrn_lb_logits, hgrn_out_norm, nsa_q_norm, nsa_k_norm, nsa_cmp_w,
           w_branch_hg, w_branch_nsa, w_branch_ret, w_out, norm_ffn, w_gate, w_up, w_down):
    depth = w_in.shape[0]
    bp, tp, d = x_prompt.shape
    bs, ts, _ = x_sample.shape
    n_pool = cache_nsa.shape[1]
    n_pages = page_table.shape[1]
    past = n_pages * PAGE_SIZE
    wbuf = cache_win.shape[2]
    lay = _layout(d)

    sm = jax.nn.softmax(hgrn_lb_logits.astype(F32), axis=0)
    lower_bounds = jnp.clip(jnp.cumsum(sm, axis=0) - sm[0:1], 0.0, 1.0 - 1e-6)

    tt_p = 256 if tp % 256 == 0 else tp
    lg = jnp.log1p(-jnp.exp2(-5.0 - jnp.arange(RET_HEADS, dtype=F32)))
    consts = {
        "gate_expand": _gate_expand_matrix(),
        "lg_tab": jnp.broadcast_to(lg[:, None, None], (RET_HEADS, 1, max(tt_p, LANES))),
    }
    bsum = _block_sum_matrix(NSA_KV_WIDTH)
    pool3 = cache_nsa.reshape(depth * n_pool, PAGE_SIZE, 4 * NSA_KV_WIDTH)
    win3 = cache_win.reshape(depth, bs, wbuf, 2 * NSA_KV_WIDTH)

    ts_pad = 16
    tq_s = 16
    n_sel_p = -(-tp // SEL_BLOCK)
    n_sel_s = past // SEL_BLOCK + -(-ts // SEL_BLOCK)
    tk_tail = SEL_TK
    tq_p = 128 if tp % 128 == 0 else tp

    xp, xs = x_prompt, x_sample
    outs = {k: [] for k in ("kv_p", "kv_s", "win_p", "win_s", "hg_p", "hg_s", "ret_p", "ret_s")}
    for l in range(depth):
        lw = {
            "norm_attn": norm_attn[l], "w_in": _reorder_w_in(w_in[l], lay), "lb": lower_bounds[l].reshape(1, -1),
            "hg_out_norm": hgrn_out_norm[l],
            "w_branch_hg": w_branch_hg[l].astype(MXU_DT), "w_branch_nsa": w_branch_nsa[l].astype(MXU_DT),
            "w_branch_ret": w_branch_ret[l].astype(MXU_DT), "w_out": w_out[l].astype(MXU_DT),
            "norm_ffn": norm_ffn[l], "w_gate": w_gate[l].astype(MXU_DT), "w_up": w_up[l].astype(MXU_DT),
            "w_down": w_down[l].astype(MXU_DT),
        }
        gq, gk, cw = nsa_q_norm[l], nsa_k_norm[l], nsa_cmp_w[l]
        col_cmp = lay["nkv"] // (2 * NSA_KV_WIDTH)
        col_sel = col_cmp + 1
        col_win = lay["nwkv"] // (2 * NSA_KV_WIDTH)

        def nsa_prompt(h3):
            tr = 512 if tp % 512 == 0 else tp
            qn = q_prep(h3, lay, gq, bsum, tr=tr)
            zeros16 = jnp.zeros((bp, CMP_STRIDE, 2 * NSA_KV_WIDTH), F32)
            kc, vc = cmp_prep(h3, col_cmp, zeros16, cw, gk[0], bsum, tr=tr)
            ks, vs = kv_prep(h3, col_sel, gk[1], bsum, tr=tr)
            kw, vw = kv_prep(h3, col_win, gk[2], bsum, tr=tr)
            return nsa_attention(qn, _slope_rows(tq_p), kc, vc, ks, vs, kw, vw,
                                 tq=tq_p, qpos0=0, wpos0=0, n_sel=n_sel_p)

        def nsa_sample(h3):
            pages = page_table + l * n_pool
            h16 = _pad_rows(h3, ts_pad)
            qn = q_prep(h16, lay, gq, bsum, tr=ts_pad)
            tail16 = h16[:, :, lay["nkv"]:lay["nkv"] + 2 * NSA_KV_WIDTH]
            kc, vc = cmp_prep_paged(pool3, pages, tail16, cw, gk[0], bsum)
            ks_main, vs_main = kv_prep_paged(pool3, pages, gk[1], bsum)
            w2 = 2 * NSA_KV_WIDTH
            htail = _pad_rows(h3[:, :, col_sel * w2:(col_sel + 1) * w2], tk_tail)
            ks_tail, vs_tail = kv_prep(htail, 0, gk[1], bsum, tr=tk_tail)
            ks = jnp.concatenate([ks_main, ks_tail], axis=2)
            vs = jnp.concatenate([vs_main, vs_tail], axis=2)
            kw_old, vw_old = kv_prep(win3[l], 0, gk[2], bsum, tr=wbuf)
            hw_new = _pad_rows(h3[:, :, col_win * w2:(col_win + 1) * w2], WIN_TK)
            kw_new, vw_new = kv_prep(hw_new, 0, gk[2], bsum, tr=WIN_TK)
            kw = jnp.concatenate([kw_old, kw_new], axis=2)
            vw = jnp.concatenate([vw_old, vw_new], axis=2)
            return nsa_attention(qn, _slope_rows(tq_s), kc, vc, ks, vs, kw, vw,
                                 tq=tq_s, qpos0=past, wpos0=past - wbuf, n_sel=n_sel_s)

        zeros_state = jnp.zeros((bp, HG_HEADS, HG_DK, HG_DV), F32)
        tm_p = 512 if (bp * tp) % 512 == 0 else bp * tp
        xp, h3p, hgp, rtp = _trunk_layer(xp, lw, lay, consts, nsa_prompt, zeros_state, zeros_state,
                                         seq_tiles=(tm_p, tt_p, tt_p, tp))
        xs, h3s, hgs, rts = _trunk_layer(xs, lw, lay, consts, nsa_sample, state_hgrn[l], state_ret[l],
                                         seq_tiles=(bs * ts, ts_pad, ts_pad, ts))

        kv_cols = slice(lay["nkv"], lay["nkv"] + 4 * NSA_KV_WIDTH)
        win_cols = slice(lay["nwkv"], lay["nwkv"] + 2 * NSA_KV_WIDTH)
        outs["kv_p"].append(h3p[:, :, kv_cols].reshape(bp, tp, 4, NSA_KV_HEADS, NSA_HEAD_DIM))
        outs["kv_s"].append(h3s[:, :, kv_cols].reshape(bs, ts, 4, NSA_KV_HEADS, NSA_HEAD_DIM))
        wlen = min(WINDOW, tp)
        outs["win_p"].append(h3p[:, tp - wlen:, win_cols].reshape(bp, wlen, 2, NSA_KV_HEADS, NSA_HEAD_DIM))
        ctx = jnp.concatenate([win3[l], h3s[:, :, win_cols]], axis=1)[:, ts:]
        outs["win_s"].append(ctx.reshape(bs, wbuf, 2, NSA_KV_HEADS, NSA_HEAD_DIM))
        outs["hg_p"].append(hgp); outs["hg_s"].append(hgs)
        outs["ret_p"].append(rtp); outs["ret_s"].append(rts)

    st = lambda k: jnp.stack(outs[k])
    return (xp, xs, st("kv_p"), st("kv_s"), st("win_p"), st("win_s"),
            st("hg_p"), st("hg_s"), st("ret_p"), st("ret_s"))
```

```python
import functools

import jax
import jax.numpy as jnp
from jax import lax
from jax.experimental import pallas as pl
from jax.experimental.pallas import tpu as pltpu

F32 = jnp.float32
MXU_DT = jnp.bfloat16

HG_HEADS, HG_DK, HG_DV = 8, 128, 128
NSA_HEADS, NSA_KV_HEADS, NSA_GROUP, NSA_HEAD_DIM = 16, 4, 4, 64
NSA_WIDTH = NSA_HEADS * NSA_HEAD_DIM
NSA_KV_WIDTH = NSA_KV_HEADS * NSA_HEAD_DIM
CMP_BLOCK, CMP_STRIDE, SEL_BLOCK, NSA_TOPK, WINDOW = 32, 16, 64, 16, 512
RET_HEADS, RET_DK, RET_DV = 8, 128, 128
PAGE_SIZE = 128
EPS = 1e-6
NEG_BIG = -1e30
POS_BIG = 1e30
MIN_F = 1e-20
LOWEST = -3.0e38

LANES = 128
VMEM_LIMIT_BYTES = 56 * 1024 * 1024

HG_SUB_SHIFT = 4
HG_SUB = 1 << HG_SUB_SHIFT
PAGES_PER_STEP = 8
SEL_TK = 256
PROMPT_SEL_TK = 512
WIN_TK = 128


def _cparams(*sem):
    return pltpu.CompilerParams(dimension_semantics=sem, vmem_limit_bytes=VMEM_LIMIT_BYTES)


def _dot(a, b):
    return jnp.dot(a, b, preferred_element_type=F32)


def _dot_nt(a, b):
    return lax.dot_general(a, b, (((1,), (1,)), ((), ())), preferred_element_type=F32)


def _split(x, n):
    out = []
    r = x
    for _ in range(n):
        p = r.astype(MXU_DT)
        out.append(p)
        r = r - p.astype(F32)
    return out


def _sigmoid(x):
    return 1.0 / (1.0 + jnp.exp(-x))


def _silu(x):
    return x * _sigmoid(x)


def _iota(shape, dim):
    return lax.broadcasted_iota(jnp.int32, shape, dim)


def _eye(n):
    return (_iota((n, n), 0) == _iota((n, n), 1)).astype(MXU_DT)


def _transpose_exact(x):
    eye = _eye(x.shape[1])
    return sum(_dot_nt(eye, p) for p in _split(x, 3))


def _head_rms(x, bsum, gain):
    ssq = sum(_dot(p, bsum) for p in _split(x * x, 2))
    return x * lax.rsqrt(ssq * (1.0 / NSA_HEAD_DIM) + EPS) * gain


def _rms_matmul_kernel(x_ref, g_ref, w_ref, o_ref, xn_ref):
    @pl.when(pl.program_id(1) == 0)
    def _():
        x = x_ref[...]
        ms = jnp.mean(x * x, axis=-1, keepdims=True)
        xn_ref[...] = (x * lax.rsqrt(ms + EPS) * g_ref[...]).astype(xn_ref.dtype)

    o_ref[...] = _dot(xn_ref[...], w_ref[...])


def rms_matmul(x, gain, w, *, tm, tn):
    m, d = x.shape
    n = w.shape[1]
    return pl.pallas_call(
        _rms_matmul_kernel,
        grid=(m // tm, n // tn),
        in_specs=[pl.BlockSpec((tm, d), lambda i, j: (i, 0)),
                  pl.BlockSpec((1, d), lambda i, j: (0, 0)),
                  pl.BlockSpec((d, tn), lambda i, j: (0, j))],
        out_specs=pl.BlockSpec((tm, tn), lambda i, j: (i, j)),
        out_shape=jax.ShapeDtypeStruct((m, n), F32),
        scratch_shapes=[pltpu.VMEM((tm, d), MXU_DT)],
        compiler_params=_cparams("parallel", "arbitrary"),
        name="in_proj",
    )(x, gain.reshape(1, d), w)


def _ffn_up_kernel(x_ref, g_ref, wg_ref, wu_ref, o_ref, xn_ref):
    @pl.when(pl.program_id(1) == 0)
    def _():
        x = x_ref[...]
        ms = jnp.mean(x * x, axis=-1, keepdims=True)
        xn_ref[...] = (x * lax.rsqrt(ms + EPS) * g_ref[...]).astype(xn_ref.dtype)

    xn = xn_ref[...]
    o_ref[...] = (_silu(_dot(xn, wg_ref[...])) * _dot(xn, wu_ref[...])).astype(o_ref.dtype)


def ffn_up(x, gain, wg, wu, *, tm, tn):
    m, d = x.shape
    f = wg.shape[1]
    return pl.pallas_call(
        _ffn_up_kernel,
        grid=(m // tm, f // tn),
        in_specs=[pl.BlockSpec((tm, d), lambda i, j: (i, 0)),
                  pl.BlockSpec((1, d), lambda i, j: (0, 0)),
                  pl.BlockSpec((d, tn), lambda i, j: (0, j)),
                  pl.BlockSpec((d, tn), lambda i, j: (0, j))],
        out_specs=pl.BlockSpec((tm, tn), lambda i, j: (i, j)),
        out_shape=jax.ShapeDtypeStruct((m, f), MXU_DT),
        scratch_shapes=[pltpu.VMEM((tm, d), MXU_DT)],
        compiler_params=_cparams("parallel", "arbitrary"),
        name="ffn_up",
    )(x, gain.reshape(1, d), wg, wu)


def _matmul_res_kernel(a_ref, w_ref, r_ref, o_ref):
    o_ref[...] = r_ref[...] + _dot(a_ref[...], w_ref[...])


def matmul_res(a, w, res, *, tm, tn):
    m, k = a.shape
    n = w.shape[1]
    return pl.pallas_call(
        _matmul_res_kernel,
        grid=(m // tm, n // tn),
        in_specs=[pl.BlockSpec((tm, k), lambda i, j: (i, 0)),
                  pl.BlockSpec((k, tn), lambda i, j: (0, j)),
                  pl.BlockSpec((tm, tn), lambda i, j: (i, j))],
        out_specs=pl.BlockSpec((tm, tn), lambda i, j: (i, j)),
        out_shape=jax.ShapeDtypeStruct((m, n), F32),
        compiler_params=_cparams("parallel", "arbitrary"),
        name="matmul_res",
    )(a, w, res)


def _merge_kernel(oa_ref, ocmp_ref, osel_ref, owin_ref, oc_ref, nbg_ref, ga_ref, gb_ref, gc_ref,
                  gx_ref, wa_ref, wb_ref, wc_ref, o_ref, ob_ref):
    @pl.when(pl.program_id(1) == 0)
    def _():
        pieces = _split(_sigmoid(nbg_ref[...]), 2)
        acc = None
        for br, o_br in enumerate((ocmp_ref, osel_ref, owin_ref)):
            e = sum(_dot(p, gx_ref[br]) for p in pieces)
            term = e * o_br[...]
            acc = term if acc is None else acc + term
        ob_ref[...] = acc.astype(ob_ref.dtype)

    o_ref[...] = (_sigmoid(ga_ref[...]) * _dot(oa_ref[...], wa_ref[...])
                  + _sigmoid(gb_ref[...]) * _dot(ob_ref[...], wb_ref[...])
                  + _sigmoid(gc_ref[...]) * _dot(oc_ref[...], wc_ref[...])).astype(o_ref.dtype)


def merge_branches(h, lay, oa, o_cmp, o_sel, o_win, oc, gate_expand, wa, wb, wc, *, tm, tn):
    m = h.shape[0]
    d = wa.shape[1]
    wdt = oa.shape[1]
    ca, cb, cc = ((lay["mg"] + i * d) // tn for i in range(3))
    cn = lay["nbg"] // LANES
    row = lambda i, j: (i, 0)
    return pl.pallas_call(
        _merge_kernel,
        grid=(m // tm, d // tn),
        in_specs=[pl.BlockSpec((tm, wdt), row), pl.BlockSpec((tm, wdt), row),
                  pl.BlockSpec((tm, wdt), row), pl.BlockSpec((tm, wdt), row),
                  pl.BlockSpec((tm, wdt), row),
                  pl.BlockSpec((tm, LANES), lambda i, j: (i, cn)),
                  pl.BlockSpec((tm, tn), lambda i, j: (i, ca + j)),
                  pl.BlockSpec((tm, tn), lambda i, j: (i, cb + j)),
                  pl.BlockSpec((tm, tn), lambda i, j: (i, cc + j)),
                  pl.BlockSpec((3, LANES, wdt), lambda i, j: (0, 0, 0)),
                  pl.BlockSpec((wdt, tn), lambda i, j: (0, j)),
                  pl.BlockSpec((wdt, tn), lambda i, j: (0, j)),
                  pl.BlockSpec((wdt, tn), lambda i, j: (0, j))],
        out_specs=pl.BlockSpec((tm, tn), lambda i, j: (i, j)),
        out_shape=jax.ShapeDtypeStruct((m, d), MXU_DT),
        scratch_shapes=[pltpu.VMEM((tm, wdt), MXU_DT)],
        compiler_params=_cparams("parallel", "arbitrary"),
        name="merge",
    )(oa, o_cmp, o_sel, o_win, oc, h, h, h, h, gate_expand, wa, wb, wc)


def _hgrn_kernel(hq_ref, hf_ref, hi_ref, hg_ref, lb_ref, gain_ref, s0_ref, o_ref, sout_ref,
                 st_ref, ut_ref, *, tt, t_valid):
    c = HG_SUB
    nj = tt // c
    tb = pl.program_id(2)

    @pl.when(tb == 0)
    def _():
        st_ref[...] = _transpose_exact(s0_ref[0, 0])

    hq = hq_ref[0]
    lb = lb_ref[...]
    q = _silu(hq)
    sig = _sigmoid(hf_ref[0])
    g = jnp.log(jnp.maximum(lb + (1.0 - lb) * sig, MIN_F))
    kk = (1.0 - lb) * (1.0 - sig)
    v = hi_ref[0]
    row = _iota((tt, HG_DK), 0)
    if t_valid is not None:
        live = (tb * tt + row) < t_valid
        g = jnp.where(live, g, 0.0)
        kk = jnp.where(live, kk, 0.0)

    r2 = _iota((tt, tt), 0)
    c2 = _iota((tt, tt), 1)
    same = (r2 >> HG_SUB_SHIFT) == (c2 >> HG_SUB_SHIFT)
    gp = _split(g, 3)
    tri = (same & (c2 <= r2)).astype(MXU_DT)
    blk = same.astype(MXU_DT)
    b = sum(_dot(tri, p) for p in gp)
    dtot = sum(_dot(blk, p) for p in gp)
    qe = q * jnp.exp(b)
    ke = kk * jnp.exp(dtot - b)
    edec = jnp.exp(dtot)

    q3 = q.reshape(nj, c, HG_DK)
    kk3 = kk.reshape(nj, c, HG_DK)
    b3 = b.reshape(nj, c, HG_DK)
    v3 = v.reshape(nj, c, HG_DV)
    tpos = _iota((nj, c, HG_DK), 1)
    ones = jnp.ones((HG_DK, HG_DV), MXU_DT)
    o3 = jnp.zeros((nj, c, HG_DV), F32)
    for s in range(c):
        m = tpos >= s
        diff = jnp.where(m, b3 - b3[:, s:s + 1, :], 0.0)
        y = jnp.where(m, q3 * kk3[:, s:s + 1, :] * jnp.exp(diff), 0.0)
        z = _dot(y.reshape(tt, HG_DK).astype(MXU_DT), ones).reshape(nj, c, HG_DV)
        o3 = o3 + z * v3[:, s:s + 1, :]
    o = o3.reshape(tt, HG_DV)

    vt = _transpose_exact(v).astype(MXU_DT)
    for j in range(nj):
        kej = jnp.where((row >> HG_SUB_SHIFT) == j, ke, 0.0).astype(MXU_DT)
        ut_ref[j] = _dot(vt, kej)

    qe_b = qe.astype(MXU_DT)
    outs = []
    for j in range(nj):
        st = st_ref[...]
        outs.append(_dot_nt(qe_b[j * c:(j + 1) * c], st.astype(MXU_DT)))
        st_ref[...] = st * edec[j * c:j * c + 1, :] + ut_ref[j]
    o = o + jnp.concatenate(outs, axis=0)

    on = o * lax.rsqrt(jnp.mean(o * o, axis=-1, keepdims=True) + EPS) * gain_ref[...]
    o_ref[0] = (on * _silu(hg_ref[0])).astype(o_ref.dtype)

    @pl.when(tb == pl.num_programs(2) - 1)
    def _():
        sout_ref[0, 0] = _transpose_exact(st_ref[...])


def hgrn_mixer(h3, lay, lb, out_gain, s0, *, tt, t_valid):
    bsz, t, _ = h3.shape
    cq, cf, ci, cg = (lay[k] // LANES for k in ("hq", "hf", "hi", "hg"))
    col = lambda c0: (lambda b, hd, tb: (b, tb, c0 + hd))
    kern = functools.partial(_hgrn_kernel, tt=tt, t_valid=t_valid)
    return pl.pallas_call(
        kern,
        grid=(bsz, HG_HEADS, t // tt),
        in_specs=[pl.BlockSpec((1, tt, LANES), col(cq)), pl.BlockSpec((1, tt, LANES), col(cf)),
                  pl.BlockSpec((1, tt, LANES), col(ci)), pl.BlockSpec((1, tt, LANES), col(cg)),
                  pl.BlockSpec((1, HG_DK), lambda b, hd, tb: (0, hd)),
                  pl.BlockSpec((1, HG_DV), lambda b, hd, tb: (0, 0)),
                  pl.BlockSpec((1, 1, HG_DK, HG_DV), lambda b, hd, tb: (b, hd, 0, 0))],
        out_specs=[pl.BlockSpec((1, tt, HG_DV), lambda b, hd, tb: (b, tb, hd)),
                   pl.BlockSpec((1, 1, HG_DK, HG_DV), lambda b, hd, tb: (b, hd, 0, 0))],
        out_shape=[jax.ShapeDtypeStruct((bsz, t, HG_HEADS * HG_DV), MXU_DT),
                   jax.ShapeDtypeStruct((bsz, HG_HEADS, HG_DK, HG_DV), F32)],
        scratch_shapes=[pltpu.VMEM((HG_DV, HG_DK), F32),
                        pltpu.VMEM((tt // HG_SUB, HG_DV, HG_DK), F32)],
        compiler_params=_cparams("parallel", "parallel", "arbitrary"),
        name="hgrn",
    )(h3, h3, h3, h3, lb, out_gain.reshape(1, HG_DV), s0)


def _ret_kernel(q_ref, k_ref, v_ref, g_ref, lg_ref, s0_ref, o_ref, sout_ref, s_ref, *, tt, n_valid):
    tb = pl.program_id(2)

    @pl.when(tb == 0)
    def _():
        s_ref[...] = s0_ref[0, 0]

    lgl = lg_ref[0][:, :LANES]
    lgt = lg_ref[0][:, :tt]
    q = q_ref[0] * (RET_DK ** -0.5)
    k = k_ref[0]
    vb = v_ref[0].astype(MXU_DT)
    rel = (_iota((tt, tt), 0) - _iota((tt, tt), 1)).astype(F32)
    dmat = jnp.where(rel >= 0, jnp.exp(jnp.maximum(rel, 0.0) * lgt), 0.0)
    att = _dot_nt(q.astype(MXU_DT), k.astype(MXU_DT)) * dmat
    q_dec = jnp.exp((_iota((tt, RET_DK), 0) + 1).astype(F32) * lgl)
    s = s_ref[...]
    o = _dot(att.astype(MXU_DT), vb) + _dot((q * q_dec).astype(MXU_DT), s.astype(MXU_DT))

    spos = _iota((RET_DK, tt), 1)
    k_dec = jnp.where(spos < n_valid, jnp.exp(jnp.maximum(n_valid - 1 - spos, 0).astype(F32) * lgt), 0.0)
    kt = _transpose_exact(k)
    s_ref[...] = jnp.exp(float(n_valid) * lgl) * s + _dot((kt * k_dec).astype(MXU_DT), vb)

    mu = jnp.mean(o, axis=-1, keepdims=True)
    var = jnp.mean(jnp.square(o - mu), axis=-1, keepdims=True)
    o_ref[0] = ((o - mu) * lax.rsqrt(var + EPS) * _silu(g_ref[0])).astype(o_ref.dtype)

    @pl.when(tb == pl.num_programs(2) - 1)
    def _():
        sout_ref[0, 0] = s_ref[...]


def retention_mixer(h3, lay, lg_tab, s0, *, tt, n_valid):
    bsz, t, _ = h3.shape
    assert n_valid == tt or t == tt
    cq, ck, cv, cg = (lay[k] // LANES for k in ("rq", "rk", "rv", "rg"))
    col = lambda c0: (lambda b, hd, tb: (b, tb, c0 + hd))
    kern = functools.partial(_ret_kernel, tt=tt, n_valid=n_valid)
    return pl.pallas_call(
        kern,
        grid=(bsz, RET_HEADS, t // tt),
        in_specs=[pl.BlockSpec((1, tt, LANES), col(cq)), pl.BlockSpec((1, tt, LANES), col(ck)),
                  pl.BlockSpec((1, tt, LANES), col(cv)), pl.BlockSpec((1, tt, LANES), col(cg)),
                  pl.BlockSpec((1, 1, lg_tab.shape[2]), lambda b, hd, tb: (hd, 0, 0)),
                  pl.BlockSpec((1, 1, RET_DK, RET_DV), lambda b, hd, tb: (b, hd, 0, 0))],
        out_specs=[pl.BlockSpec((1, tt, RET_DV), lambda b, hd, tb: (b, tb, hd)),
                   pl.BlockSpec((1, 1, RET_DK, RET_DV), lambda b, hd, tb: (b, hd, 0, 0))],
        out_shape=[jax.ShapeDtypeStruct((bsz, t, RET_HEADS * RET_DV), MXU_DT),
                   jax.ShapeDtypeStruct((bsz, RET_HEADS, RET_DK, RET_DV), F32)],
        scratch_shapes=[pltpu.VMEM((RET_DK, RET_DV), F32)],
        compiler_params=_cparams("parallel", "parallel", "arbitrary"),
        name="retention",
    )(h3, h3, h3, h3, lg_tab, s0)


def _store_heads(ref, x, lead=()):
    for hh in range(NSA_KV_HEADS):
        ref[lead + (hh,)] = x[:, hh * NSA_HEAD_DIM:(hh + 1) * NSA_HEAD_DIM].astype(ref.dtype)


def _q_prep_kernel(q_ref, gain_ref, bsum_ref, o_ref):
    x = q_ref[0]
    hd = NSA_HEAD_DIM
    for kvh in range(NSA_KV_HEADS):
        lo = kvh * NSA_KV_WIDTH
        qn = _head_rms(x[:, lo:lo + NSA_KV_WIDTH], bsum_ref[...], gain_ref[...]) * (hd ** -0.5)
        for g in range(NSA_GROUP):
            o_ref[0, kvh, g, :, 0:hd] = qn[:, g * hd:(g + 1) * hd].astype(o_ref.dtype)
            if o_ref.shape[-1] > hd:
                o_ref[0, kvh, g, :, hd:] = jnp.zeros((x.shape[0], o_ref.shape[-1] - hd), o_ref.dtype)


def q_prep(h3, lay, gain_q, bsum, *, tr, width=NSA_HEAD_DIM):
    bsz, t, _ = h3.shape
    cq = lay["nq"] // NSA_WIDTH
    return pl.pallas_call(
        _q_prep_kernel,
        grid=(bsz, t // tr),
        in_specs=[pl.BlockSpec((1, tr, NSA_WIDTH), lambda b, i: (b, i, cq)),
                  pl.BlockSpec((1, NSA_KV_WIDTH), lambda b, i: (0, 0)),
                  pl.BlockSpec((NSA_KV_WIDTH, NSA_KV_WIDTH), lambda b, i: (0, 0))],
        out_specs=pl.BlockSpec((1, NSA_KV_HEADS, NSA_GROUP, tr, width), lambda b, i: (b, 0, 0, i, 0)),
        out_shape=jax.ShapeDtypeStruct((bsz, NSA_KV_HEADS, NSA_GROUP, t, width), MXU_DT),
        compiler_params=_cparams("parallel", "parallel"),
        name="nsa_q_prep",
    )(h3, jnp.tile(gain_q, NSA_KV_HEADS).reshape(1, NSA_KV_WIDTH), bsum)


def _kv_prep_kernel(x_ref, gain_ref, bsum_ref, *rest):
    ko_ref, vo_ref = rest[-2:]
    hd = NSA_HEAD_DIM
    x = x_ref[0]
    kn = _head_rms(x[:, :NSA_KV_WIDTH], bsum_ref[...], gain_ref[...])
    for hh in range(NSA_KV_HEADS):
        ko_ref[0, hh, :, 0:hd] = kn[:, hh * hd:(hh + 1) * hd].astype(ko_ref.dtype)
        if len(rest) == 3:
            ko_ref[0, hh, :, hd:] = rest[0][:, hd:]
        lo = NSA_KV_WIDTH + hh * hd
        vo_ref[0, hh, :, 0:hd] = x[:, lo:lo + hd].astype(vo_ref.dtype)
        if vo_ref.shape[-1] > hd:
            vo_ref[0, hh, :, hd:] = jnp.ones((x.shape[0], vo_ref.shape[-1] - hd), vo_ref.dtype)


def kv_prep(x3, col, gain, bsum, *, tr, aug=None):
    bsz, t, _ = x3.shape
    w2 = 2 * NSA_KV_WIDTH
    hd = NSA_HEAD_DIM
    kw = hd if aug is None else aug.shape[1]
    vw = hd if aug is None else LANES
    vshape = jax.ShapeDtypeStruct((bsz, NSA_KV_HEADS, t, vw), MXU_DT)
    kshape = jax.ShapeDtypeStruct((bsz, NSA_KV_HEADS, t, kw), MXU_DT)
    spec = lambda w: pl.BlockSpec((1, NSA_KV_HEADS, tr, w), lambda b, i: (b, 0, i, 0))
    in_specs = [pl.BlockSpec((1, tr, w2), lambda b, i: (b, i, col)),
                pl.BlockSpec((1, NSA_KV_WIDTH), lambda b, i: (0, 0)),
                pl.BlockSpec((NSA_KV_WIDTH, NSA_KV_WIDTH), lambda b, i: (0, 0))]
    args = [x3, jnp.tile(gain, NSA_KV_HEADS).reshape(1, NSA_KV_WIDTH), bsum]
    if aug is not None:
        in_specs.append(pl.BlockSpec((tr, kw), lambda b, i: (i, 0)))
        args.append(aug)
    return pl.pallas_call(
        _kv_prep_kernel,
        grid=(bsz, t // tr),
        in_specs=in_specs,
        out_specs=[spec(kw), spec(vw)],
        out_shape=[kshape, vshape],
        compiler_params=_cparams("parallel", "parallel"),
        name="nsa_kv_prep",
    )(*args)


def _head_rms_t(xt, gain_t):
    x3 = xt.reshape(NSA_KV_HEADS, NSA_HEAD_DIM, xt.shape[1])
    ms = jnp.mean(x3 * x3, axis=1, keepdims=True)
    return (x3 * lax.rsqrt(ms + EPS)).reshape(xt.shape) * gain_t


def _kv_prep_t_kernel(*refs, n_in):
    refs = refs[-(n_in + 3):]
    gain_ref, ko_ref, vo_ref = refs[n_in:]
    eye = _eye(PAGE_SIZE)
    for u in range(n_in):
        kt = _head_rms_t(refs[u][0, 0], gain_ref[...]).astype(MXU_DT)
        vt = refs[u][0, 1].astype(MXU_DT)
        k = _dot_nt(eye, kt)
        v = _dot_nt(eye, vt)
        for hh in range(NSA_KV_HEADS):
            sl = slice(hh * NSA_HEAD_DIM, (hh + 1) * NSA_HEAD_DIM)
            ko_ref[0, hh, u * PAGE_SIZE:(u + 1) * PAGE_SIZE, :] = k[:, sl].astype(ko_ref.dtype)
            vo_ref[0, hh, u * PAGE_SIZE:(u + 1) * PAGE_SIZE, :] = v[:, sl].astype(vo_ref.dtype)


def _gain_t(gain):
    return jnp.broadcast_to(jnp.tile(gain, NSA_KV_HEADS)[:, None], (NSA_KV_WIDTH, LANES))


def kv_prep_paged(pool_t, pages, gain, *, extra_rows):
    bsz, n_pages = pages.shape
    pp = PAGES_PER_STEP
    kern = functools.partial(_kv_prep_t_kernel, n_in=pp)
    oshape = jax.ShapeDtypeStruct((bsz, NSA_KV_HEADS, n_pages * PAGE_SIZE + extra_rows, NSA_HEAD_DIM), MXU_DT)
    ospec = pl.BlockSpec((1, NSA_KV_HEADS, pp * PAGE_SIZE, NSA_HEAD_DIM), lambda b, i, pt: (b, 0, i, 0))
    page_spec = lambda u: pl.BlockSpec((1, 2, NSA_KV_WIDTH, PAGE_SIZE), lambda b, i, pt: (pt[b, i * pp + u], 1, 0, 0))
    return pl.pallas_call(
        kern,
        grid_spec=pltpu.PrefetchScalarGridSpec(
            num_scalar_prefetch=1,
            grid=(bsz, n_pages // pp),
            in_specs=[page_spec(u) for u in range(pp)]
                     + [pl.BlockSpec((NSA_KV_WIDTH, LANES), lambda b, i, pt: (0, 0))],
            out_specs=[ospec, ospec]),
        out_shape=[oshape, oshape],
        compiler_params=_cparams("parallel", "arbitrary"),
        name="nsa_kv_prep_paged",
    )(pages, *([pool_t] * pp), _gain_t(gain))


def _cmp_prep_kernel(*refs, n_in, rows):
    refs = refs[-(n_in + 7):]
    nxt_ref, tail_ref, w_ref, gain_ref, bsum_ref, kc_ref, vc_ref = refs[n_in:]
    cs = CMP_STRIDE
    nb = rows // cs
    w0 = w_ref[0]
    w1 = w_ref[1]
    a0 = []
    a1 = []
    for u in range(n_in):
        x3 = refs[u][0].reshape(nb, cs, 2 * NSA_KV_WIDTH)
        a0.append(jnp.sum(x3 * w0[None], axis=1))
        a1.append(jnp.sum(x3 * w1[None], axis=1))
    a0 = jnp.concatenate(a0, axis=0) if n_in > 1 else a0[0]
    a1 = jnp.concatenate(a1, axis=0) if n_in > 1 else a1[0]
    last = pl.program_id(1) == pl.num_programs(1) - 1
    nx = jnp.where(last, tail_ref[0], nxt_ref[0])
    a1_next = jnp.sum(nx * w1, axis=0, keepdims=True)
    tot = nb * n_in
    a1s = pltpu.roll(a1, tot - 1, 0)
    a1s = jnp.where(_iota(a1s.shape, 0) == tot - 1, a1_next, a1s)
    comp = a0 + a1s
    kc = _head_rms(comp[:, :NSA_KV_WIDTH], bsum_ref[...], gain_ref[...])
    _store_heads(kc_ref, kc, (0,))
    _store_heads(vc_ref, comp[:, NSA_KV_WIDTH:], (0,))


def _cmp_weight_table(cmp_w):
    w = cmp_w.reshape(2, CMP_BLOCK // CMP_STRIDE, CMP_STRIDE)
    w = jnp.transpose(w, (1, 2, 0))
    return jnp.repeat(w, NSA_KV_WIDTH, axis=2)


def cmp_prep(x3, col, tail16, cmp_w, gain, bsum, *, tr):
    bsz, t, _ = x3.shape
    w2 = 2 * NSA_KV_WIDTH
    nbt = tr // CMP_STRIDE
    n_steps = t // tr
    kern = functools.partial(_cmp_prep_kernel, n_in=1, rows=tr)
    oshape = jax.ShapeDtypeStruct((bsz, NSA_KV_HEADS, t // CMP_STRIDE, NSA_HEAD_DIM), MXU_DT)
    ospec = pl.BlockSpec((1, NSA_KV_HEADS, nbt, NSA_HEAD_DIM), lambda b, i: (b, 0, i, 0))
    chunks_per_step = tr // CMP_STRIDE
    return pl.pallas_call(
        kern,
        grid=(bsz, n_steps),
        in_specs=[pl.BlockSpec((1, tr, w2), lambda b, i: (b, i, col)),
                  pl.BlockSpec((1, CMP_STRIDE, w2),
                               lambda b, i: (b, jnp.minimum(i + 1, n_steps - 1) * chunks_per_step, col)),
                  pl.BlockSpec((1, CMP_STRIDE, w2), lambda b, i: (b, 0, 0)),
                  pl.BlockSpec((2, CMP_STRIDE, w2), lambda b, i: (0, 0, 0)),
                  pl.BlockSpec((1, NSA_KV_WIDTH), lambda b, i: (0, 0)),
                  pl.BlockSpec((NSA_KV_WIDTH, NSA_KV_WIDTH), lambda b, i: (0, 0))],
        out_specs=[ospec, ospec],
        out_shape=[oshape, oshape],
        compiler_params=_cparams("parallel", "arbitrary"),
        name="nsa_cmp_prep",
    )(x3, x3, tail16, _cmp_weight_table(cmp_w), jnp.tile(gain, NSA_KV_HEADS).reshape(1, NSA_KV_WIDTH), bsum)


def _cmp_prep_t_kernel(*refs, n_in):
    refs = refs[-(n_in + 9):]
    nxt_ref, tail_ref, wk_ref, wv_ref, wnk_ref, wnv_ref, gain_ref, kc_ref, vc_ref = refs[n_in:]
    nb = wk_ref.shape[2]

    def wdot(xt, w):
        xh, xl = _split(xt, 2)
        wh, wl = _split(w, 2)
        return _dot(xh, wh) + _dot(xh, wl) + _dot(xl, wh)

    last = pl.program_id(1) == pl.num_programs(1) - 1
    comp_k = wdot(jnp.where(last, tail_ref[0, 0], nxt_ref[0, 0]), wnk_ref[...])
    comp_v = wdot(jnp.where(last, tail_ref[0, 1], nxt_ref[0, 1]), wnv_ref[...])
    for u in range(n_in):
        comp_k = comp_k + wdot(refs[u][0, 0], wk_ref[u])
        comp_v = comp_v + wdot(refs[u][0, 1], wv_ref[u])
    kct = _head_rms_t(comp_k, gain_ref[...][:, :nb]).astype(MXU_DT)
    eye = _eye(nb)
    _store_heads(kc_ref, _dot_nt(eye, kct), (0,))
    _store_heads(vc_ref, _dot_nt(eye, comp_v.astype(MXU_DT)), (0,))


def _cmp_band_tables(cmp_w, n_pages_step):
    cs = CMP_STRIDE
    w = cmp_w.reshape(2, CMP_BLOCK // cs, cs)
    nb = n_pages_step * PAGE_SIZE // cs
    rho = jnp.arange(PAGE_SIZE)
    ch = (jnp.arange(n_pages_step)[:, None] * (PAGE_SIZE // cs) + rho[None, :] // cs)[..., None]
    n = jnp.arange(nb)[None, None, :]
    s = rho % cs
    tabs = []
    for c in range(2):
        full = (w[c, 0][s][None, :, None] * (ch == n) + w[c, 1][s][None, :, None] * (ch == n + 1)).astype(F32)
        nxt = jnp.where((rho[:, None] < cs) & (n[0] == nb - 1), w[c, 1][s][:, None], 0.0).astype(F32)
        tabs.append((full, nxt))
    return tabs[0][0], tabs[1][0], tabs[0][1], tabs[1][1]


def cmp_prep_paged(pool_t, pages, tail_t, cmp_w, gain):
    bsz, n_pages = pages.shape
    pp = PAGES_PER_STEP
    n_steps = n_pages // pp
    nbt = pp * PAGE_SIZE // CMP_STRIDE
    kern = functools.partial(_cmp_prep_t_kernel, n_in=pp)
    oshape = jax.ShapeDtypeStruct((bsz, NSA_KV_HEADS, n_pages * PAGE_SIZE // CMP_STRIDE, NSA_HEAD_DIM), MXU_DT)
    ospec = pl.BlockSpec((1, NSA_KV_HEADS, nbt, NSA_HEAD_DIM), lambda b, i, pt: (b, 0, i, 0))
    pblock = (1, 2, NSA_KV_WIDTH, PAGE_SIZE)
    page_spec = lambda u: pl.BlockSpec(pblock, lambda b, i, pt: (pt[b, i * pp + u], 0, 0, 0))
    const = lambda *shape: pl.BlockSpec(shape, lambda b, i, pt: (0,) * len(shape))
    wk, wv, wnk, wnv = _cmp_band_tables(cmp_w, pp)
    return pl.pallas_call(
        kern,
        grid_spec=pltpu.PrefetchScalarGridSpec(
            num_scalar_prefetch=1,
            grid=(bsz, n_steps),
            in_specs=[page_spec(u) for u in range(pp)]
                     + [pl.BlockSpec(pblock, lambda b, i, pt: (pt[b, jnp.minimum(i + 1, n_steps - 1) * pp], 0, 0, 0)),
                        pl.BlockSpec(pblock, lambda b, i, pt: (b, 0, 0, 0)),
                        const(pp, PAGE_SIZE, nbt), const(pp, PAGE_SIZE, nbt), const(PAGE_SIZE, nbt),
                        const(PAGE_SIZE, nbt), const(NSA_KV_WIDTH, LANES)],
            out_specs=[ospec, ospec]),
        out_shape=[oshape, oshape],
        compiler_params=_cparams("parallel", "arbitrary"),
        name="nsa_cmp_prep_paged",
    )(pages, *([pool_t] * pp), pool_t, tail_t, wk, wv, wnk, wnv, _gain_t(gain))


def _nsa_kernel(q_ref, sl_ref, kc_ref, vc_ref, ks_ref, vs_ref, kw_ref, vw_ref, at_ref,
                ocmp_ref, osel_ref, owin_ref, sc_ref, m_ref, l_ref, acc_ref,
                *, tq, sel_tk, qpos0, wpos0, n_sel, topk):
    rows = NSA_GROUP * tq
    hd = NSA_HEAD_DIM
    t0 = qpos0 + pl.program_id(2) * tq
    q = q_ref[0, 0].reshape(rows, hd)
    slope = sl_ref[0][:, :1]
    nbp = kc_ref.shape[2]
    n_sel_pad, tl = sc_ref.shape

    def tok(shape):
        return t0 + (_iota(shape, 0) & (tq - 1))

    s = _dot_nt(q, kc_ref[0, 0])
    dist = tok((rows, nbp)) - (_iota((rows, nbp), 1) * CMP_STRIDE + (CMP_BLOCK - 1))
    valid = dist >= 0
    s = jnp.where(valid, s - slope * dist.astype(F32), NEG_BIG)
    p = jnp.where(valid, jnp.exp(s - jnp.max(s, axis=-1, keepdims=True)), 0.0)
    p = p / jnp.maximum(jnp.sum(p, axis=-1, keepdims=True), 1e-30)
    o_cmp = _dot(p.astype(MXU_DT), vc_ref[0, 0])

    sel = _topk_block_mask(p, at_ref, t0, tq=tq, tl=tl, n_sel=n_sel, topk=topk).astype(MXU_DT)
    spread =((_iota((rows, tl), 0) & (tq - 1)) == _iota((rows, tl), 1)).astype(MXU_DT)
    selb = _dot_nt(spread, sel).astype(MXU_DT)

    def softmax_pass(k_ref, v_ref, tk, lo, hi, kpos0, mask_fn):
        m_ref[...] = jnp.full(m_ref.shape, NEG_BIG, F32)
        l_ref[...] = jnp.zeros(l_ref.shape, F32)
        acc_ref[...] = jnp.zeros(acc_ref.shape, F32)
        rel = (_iota((rows, tk), 0) & (tq - 1)) - _iota((rows, tk), 1)
        reps = tk // LANES

        def body(kt, carry):
            k0 = pl.multiple_of(kt * tk, tk)
            s = _dot_nt(q, k_ref[0, 0, pl.ds(k0, tk), :])
            d = rel + (t0 - kpos0 - k0)
            valid = mask_fn(d, k0)
            s = jnp.where(valid, s - slope * d.astype(F32), NEG_BIG)
            m_prev = m_ref[...]
            m_new = jnp.maximum(m_prev, jnp.max(s, axis=-1, keepdims=True))
            alpha = jnp.exp(m_prev - m_new)
            mrep = m_new if reps == 1 else jnp.concatenate([m_new] * reps, axis=1)
            p = jnp.where(valid, jnp.exp(s - mrep), 0.0)
            l_ref[...] = alpha * l_ref[...] + jnp.sum(p, axis=-1, keepdims=True)
            acc_ref[...] = alpha[:, :hd] * acc_ref[...] + _dot(p.astype(MXU_DT), v_ref[0, 0, pl.ds(k0, tk), :])
            m_ref[...] = m_new
            return carry

        lax.fori_loop(lo, hi, body, 0)
        return acc_ref[...] / jnp.maximum(l_ref[...][:, :hd], 1e-30)

    n_kt_all = ks_ref.shape[2] // sel_tk
    hi_sel = jnp.minimum(n_kt_all, (t0 + tq - 1) // sel_tk + 1)

    def sel_mask(d, k0):
        blk_of_key = (k0 + _iota((n_sel_pad, sel_tk), 1)) >> 6
        expand = (_iota((n_sel_pad, sel_tk), 0) == blk_of_key).astype(MXU_DT)
        return (_dot(selb, expand) > 0.5) & (d >= 0)

    o_sel = softmax_pass(ks_ref, vs_ref, sel_tk, 0, hi_sel, 0, sel_mask)

    n_wt_all = kw_ref.shape[2] // WIN_TK
    lo_win = jnp.maximum(t0 - (WINDOW - 1) - wpos0, 0) // WIN_TK
    hi_win = jnp.minimum(n_wt_all, (t0 + tq - 1 - wpos0) // WIN_TK + 1)
    o_win = softmax_pass(kw_ref, vw_ref, WIN_TK, lo_win, hi_win, wpos0,
                         lambda d, k0: (d >= 0) & (d < WINDOW))

    for g in range(NSA_GROUP):
        sl = slice(g * hd, (g + 1) * hd)
        ocmp_ref[0, :, sl] = o_cmp[g * tq:(g + 1) * tq]
        osel_ref[0, :, sl] = o_sel[g * tq:(g + 1) * tq]
        owin_ref[0, :, sl] = o_win[g * tq:(g + 1) * tq]


def _imp_to_sel_matrix(n_sel_pad, nbp):
    m = SEL_BLOCK // CMP_STRIDE
    r = CMP_BLOCK // CMP_STRIDE
    jj = jnp.arange(n_sel_pad)[:, None]
    ii = jnp.arange(nbp)[None, :]
    cnt = sum(((ii - rr) >= m * jj) & ((ii - rr) < m * (jj + 1)) for rr in range(r))
    return cnt.astype(MXU_DT)


def nsa_attention(qn, slopes, kc, vc, ks, vs, kw, vw, *, tq, sel_tk, qpos0, wpos0, n_sel):
    bsz, _, _, tql, hd = qn.shape
    nbp, tk_all, tw_all = kc.shape[2], ks.shape[2], kw.shape[2]
    assert tq & (tq - 1) == 0 and tk_all % sel_tk == 0 and tw_all % WIN_TK == 0
    n_sel_pad = -(-n_sel // 64) * 64
    tl = max(tq, LANES)
    rows = NSA_GROUP * tq
    at = _imp_to_sel_matrix(n_sel_pad, nbp)
    kern = functools.partial(_nsa_kernel, tq=tq, sel_tk=sel_tk, qpos0=qpos0, wpos0=wpos0, n_sel=n_sel,
                             topk=min(NSA_TOPK, n_sel))
    full = lambda n: pl.BlockSpec((1, 1, n, hd), lambda b, kh, i: (b, kh, 0, 0))
    ospec = pl.BlockSpec((1, tq, NSA_GROUP * hd), lambda b, kh, i: (b, i, kh))
    oshape = jax.ShapeDtypeStruct((bsz, tql, NSA_WIDTH), F32)
    return pl.pallas_call(
        kern,
        grid=(bsz, NSA_KV_HEADS, tql // tq),
        in_specs=[pl.BlockSpec((1, 1, NSA_GROUP, tq, hd), lambda b, kh, i: (b, kh, 0, i, 0)),
                  pl.BlockSpec((1, rows, LANES), lambda b, kh, i: (kh, 0, 0)),
                  full(nbp), full(nbp), full(tk_all), full(tk_all), full(tw_all), full(tw_all),
                  pl.BlockSpec((n_sel_pad, nbp), lambda b, kh, i: (0, 0))],
        out_specs=[ospec, ospec, ospec],
        out_shape=[oshape, oshape, oshape],
        scratch_shapes=[pltpu.VMEM((n_sel_pad, tl), F32),
                        pltpu.VMEM((rows, LANES), F32), pltpu.VMEM((rows, LANES), F32),
                        pltpu.VMEM((rows, hd), F32)],
        compiler_params=_cparams("parallel", "parallel", "arbitrary"),
        name="nsa_attention",
    )(qn, slopes, kc, vc, ks, vs, kw, vw, at)


def _topk_block_mask(p, at_ref, t0, *, tq, tl, n_sel, topk):
    n_sel_pad, nbp = at_ref.shape
    imp = p[0:tq] + p[tq:2 * tq] + p[2 * tq:3 * tq] + p[3 * tq:4 * tq]
    if tl > tq:
        imp = jnp.concatenate([imp, jnp.zeros((tl - tq, nbp), F32)], axis=0)
    imp_sel = sum(_dot_nt(at_ref[...], piece) for piece in _split(imp, 3))
    j = _iota((n_sel_pad, tl), 0)
    tt = t0 + _iota((n_sel_pad, tl), 1)
    forced = (j == 0) | (j == (tt >> 6))
    allowed = (j << 6) <= tt
    score = jnp.where(forced, POS_BIG, jnp.where(allowed, imp_sel, NEG_BIG))
    score = jnp.where(j >= n_sel, LOWEST, score)
    jf = j.astype(F32)
    sel = jnp.zeros((n_sel_pad, tl), F32)
    for _ in range(topk):
        mx = jnp.max(score, axis=0, keepdims=True)
        first = jnp.min(jnp.where(score == mx, jf, 1e9), axis=0, keepdims=True)
        pick = jf == first
        sel = jnp.where(pick, 1.0, sel)
        score = jnp.where(pick, -jnp.inf, score)
    return jnp.where(allowed, sel, 0.0)


def _nsa_prompt_kernel(q_ref, sl_ref, qts_ref, qtw_ref, kc_ref, vc_ref, ks_ref, vs_ref, kw_ref, vw_ref, at_ref,
                       ocmp_ref, osel_ref, owin_ref, m_ref, acc_ref, *, tq, n_sel, topk):
    rows = NSA_GROUP * tq
    hd = NSA_HEAD_DIM
    assert tq == LANES
    qi = pl.program_id(2)
    t0 = qi * tq
    q128 = q_ref[0, 0].reshape(rows, LANES)
    slope = sl_ref[0][:, :1]
    nbp = kc_ref.shape[2]

    s = _dot_nt(q128[:, :hd], kc_ref[0, 0])
    tok = t0 + (_iota((rows, nbp), 0) & (tq - 1))
    dist = tok - (_iota((rows, nbp), 1) * CMP_STRIDE + (CMP_BLOCK - 1))
    valid = dist >= 0
    s = jnp.where(valid, s - slope * dist.astype(F32), NEG_BIG)
    p = jnp.where(valid, jnp.exp(s - jnp.max(s, axis=-1, keepdims=True)), 0.0)
    p = p / jnp.maximum(jnp.sum(p, axis=-1, keepdims=True), 1e-30)
    o_cmp = _dot(p.astype(MXU_DT), vc_ref[0, 0])

    sel = _topk_block_mask(p, at_ref, t0, tq=tq, tl=tq, n_sel=n_sel, topk=topk)
    sel_t = jnp.concatenate([jnp.zeros_like(sel), sel], axis=0).astype(MXU_DT)
    spread = ((_iota((rows, tq), 0) & (tq - 1)) == _iota((rows, tq), 1)).astype(MXU_DT)
    selb = _dot_nt(spread, sel_t)
    lane = _iota((rows, LANES), 1)
    neg = jnp.where((lane >= hd) & (selb < 0.5), NEG_BIG, 0.0)
    qa = jnp.concatenate([(q128.astype(F32) + neg).astype(MXU_DT), qts_ref[0]], axis=1)
    qw = jnp.where(lane < hd, q128, qtw_ref[0])

    def normalised(acc):
        return acc[:, :hd] / jnp.maximum(acc[:, hd:], 1e-30)

    tk = PROMPT_SEL_TK
    kd = t0 // tk
    k0 = pl.multiple_of(kd * tk, tk)
    s = _dot_nt(qa, ks_ref[0, 0, pl.ds(k0, tk), :])
    causal = ((_iota((rows, tk), 0) & (tq - 1)) - _iota((rows, tk), 1) + (t0 - k0)) >= 0
    s = jnp.where(causal, s, NEG_BIG)
    m = jnp.max(s, axis=-1, keepdims=True)
    m_ref[...] = jnp.broadcast_to(m, m_ref.shape)
    acc_ref[...] = _dot(jnp.exp(s - m).astype(MXU_DT), vs_ref[0, 0, pl.ds(k0, tk), :])

    def sel_body(kt, carry):
        ka = pl.multiple_of(kt * tk, tk)
        s = _dot_nt(qa, ks_ref[0, 0, pl.ds(ka, tk), :])
        m_prev = m_ref[...]
        m_new = jnp.maximum(m_prev, jnp.max(s, axis=-1, keepdims=True))
        p = jnp.exp(s - jnp.concatenate([m_new] * (tk // LANES), axis=1))
        acc_ref[...] = jnp.exp(m_prev - m_new) * acc_ref[...] + _dot(p.astype(MXU_DT), vs_ref[0, 0, pl.ds(ka, tk), :])
        m_ref[...] = m_new
        return carry

    lax.fori_loop(0, kd, sel_body, 0)
    o_sel = normalised(acc_ref[...])

    span = WINDOW + tq
    ws = pl.multiple_of(jnp.maximum(t0 - WINDOW, 0), tq)
    s = _dot_nt(qw, kw_ref[0, 0, pl.ds(ws, span), :])
    d = (_iota((rows, span), 0) & (tq - 1)) - _iota((rows, span), 1) + (t0 - ws)
    s = jnp.where((d >= 0) & (d < WINDOW), s, NEG_BIG)
    p = jnp.exp(s - jnp.max(s, axis=-1, keepdims=True))
    o_win = normalised(_dot(p.astype(MXU_DT), vw_ref[0, 0, pl.ds(ws, span), :]))

    for g in range(NSA_GROUP):
        sl = slice(g * hd, (g + 1) * hd)
        ocmp_ref[0, :, sl] = o_cmp[g * tq:(g + 1) * tq]
        osel_ref[0, :, sl] = o_sel[g * tq:(g + 1) * tq]
        owin_ref[0, :, sl] = o_win[g * tq:(g + 1) * tq]


def _slope_digits(tq, tile, lane0):
    hh = jnp.arange(1, NSA_HEADS + 1, dtype=F32)
    slopes = jnp.exp2(-8.0 * hh / NSA_HEADS)
    pieces = jnp.stack(_split(slopes, 3), axis=1).astype(F32)
    six = jnp.concatenate([pieces * float(tile), pieces], axis=1)
    tab = jnp.zeros((NSA_HEADS, LANES), F32).at[:, lane0:lane0 + 6].set(six)
    tab = tab.reshape(NSA_KV_HEADS, NSA_GROUP, 1, LANES)
    return jnp.broadcast_to(tab, (NSA_KV_HEADS, NSA_GROUP, tq, LANES)).reshape(
        NSA_KV_HEADS, NSA_GROUP * tq, LANES).astype(MXU_DT)


def _key_digits(t, tile, lane0, width, onehot):
    r = jnp.arange(t)
    lane = jnp.arange(width)[None, :]
    tab = jnp.zeros((t, width), F32)
    if onehot:
        tab = jnp.where(lane - NSA_HEAD_DIM == (r // SEL_BLOCK)[:, None], 1.0, tab)
    hi = (r // tile).astype(F32)[:, None]
    lo = (r % tile).astype(F32)[:, None]
    tab = jnp.where((lane >= lane0) & (lane < lane0 + 3), hi, tab)
    tab = jnp.where((lane >= lane0 + 3) & (lane < lane0 + 6), lo, tab)
    return tab.astype(MXU_DT)


def nsa_attention_prompt(q128, slopes, kc, vc, ks_aug, vs, kw_aug, vw, *, n_sel):
    bsz, _, _, t, _ = q128.shape
    tq = LANES
    hd = NSA_HEAD_DIM
    nbp = kc.shape[2]
    assert n_sel <= 64 and t % PROMPT_SEL_TK == 0 and t >= WINDOW + tq
    rows = NSA_GROUP * tq
    at = _imp_to_sel_matrix(64, nbp)
    kern = functools.partial(_nsa_prompt_kernel, tq=tq, n_sel=n_sel, topk=min(NSA_TOPK, n_sel))
    full = lambda n, w: pl.BlockSpec((1, 1, n, w), lambda b, kh, i: (b, kh, 0, 0))
    per_head = pl.BlockSpec((1, rows, LANES), lambda b, kh, i: (kh, 0, 0))
    ospec = pl.BlockSpec((1, tq, NSA_GROUP * hd), lambda b, kh, i: (b, i, kh))
    oshape = jax.ShapeDtypeStruct((bsz, t, NSA_WIDTH), F32)
    return pl.pallas_call(
        kern,
        grid=(bsz, NSA_KV_HEADS, t // tq),
        in_specs=[pl.BlockSpec((1, 1, NSA_GROUP, tq, LANES), lambda b, kh, i: (b, kh, 0, i, 0)),
                  per_head, per_head, per_head,
                  full(nbp, hd), full(nbp, hd), full(t, 2 * LANES), full(t, LANES), full(t, LANES), full(t, LANES),
                  pl.BlockSpec((64, nbp), lambda b, kh, i: (0, 0))],
        out_specs=[ospec, ospec, ospec],
        out_shape=[oshape, oshape, oshape],
        scratch_shapes=[pltpu.VMEM((rows, LANES), F32), pltpu.VMEM((rows, LANES), F32)],
        compiler_params=_cparams("parallel", "parallel", "arbitrary"),
        name="nsa_attention_prompt",
    )(q128, slopes, _slope_digits(tq, SEL_TK, 0), _slope_digits(tq, WIN_TK, hd),
      kc, vc, ks_aug, vs, kw_aug, vw, at)


def _layout(d_model):
    hw, rw = HG_HEADS * HG_DK, RET_HEADS * RET_DK
    names = [("hq", hw), ("hf", hw), ("hi", hw), ("hg", hw), ("nq", NSA_WIDTH), ("nkv", 4 * NSA_KV_WIDTH),
             ("nwkv", 2 * NSA_KV_WIDTH), ("rq", rw), ("rk", rw), ("rv", rw), ("rg", rw), ("mg", 3 * d_model),
             ("nbg", 3 * NSA_HEADS)]
    lay, off = {}, 0
    for name, width in names:
        lay[name] = off
        off += width
    lay["used"] = off
    lay["total"] = -(-off // 1024) * 1024
    return lay


def _reorder_w_in(w_in, lay):
    a = lay["rq"]
    nb = 3 * NSA_HEADS
    d = w_in.shape[0]
    pad = jnp.zeros((d, lay["total"] - lay["used"]), w_in.dtype)
    return jnp.concatenate([w_in[:, :a], w_in[:, a + nb:], w_in[:, a:a + nb], pad], axis=1).astype(MXU_DT)


def _block_sum_matrix(width):
    i = jnp.arange(width)
    return (i[:, None] // NSA_HEAD_DIM == i[None, :] // NSA_HEAD_DIM).astype(MXU_DT)


def _gate_expand_matrix():
    c = jnp.arange(LANES)[:, None]
    col = jnp.arange(NSA_WIDTH)[None, :]
    return jnp.stack([(c == (col // NSA_HEAD_DIM) * 3 + br) for br in range(3)]).astype(MXU_DT)


def _slope_rows(tq):
    hh = jnp.arange(1, NSA_HEADS + 1, dtype=F32)
    slopes = jnp.exp2(-8.0 * hh / NSA_HEADS).reshape(NSA_KV_HEADS, NSA_GROUP)
    rows = jnp.repeat(slopes, tq, axis=1)
    return jnp.broadcast_to(rows[:, :, None], (NSA_KV_HEADS, NSA_GROUP * tq, LANES))


def _pad_rows(x, n):
    return jnp.pad(x, ((0, 0), (0, n - x.shape[1])) + ((0, 0),) * (x.ndim - 2))


def _trunk_layer(x, lw, lay, consts, nsa_fn, hg_state, ret_state, *, seq_tiles):
    bsz, t, d = x.shape
    tm, tt_h, tt_r, t_valid = seq_tiles
    x2 = x.reshape(bsz * t, d)
    h = rms_matmul(x2, lw["norm_attn"], lw["w_in"], tm=tm, tn=1024)
    np_ = h.shape[1]
    h3 = h.reshape(bsz, t, np_)
    tp = -(-t // tt_h) * tt_h
    h3p = _pad_rows(h3, tp) if tp != t else h3
    oa, hg_new = hgrn_mixer(h3p, lay, lw["lb"], lw["hg_out_norm"], hg_state, tt=tt_h,
                            t_valid=None if tp == t else t_valid)
    tpr = -(-t // tt_r) * tt_r
    h3r = _pad_rows(h3, tpr) if tpr != t else h3
    oc, ret_new = retention_mixer(h3r, lay, consts["lg_tab"], ret_state, tt=tt_r,
                                  n_valid=tt_r if tpr == t else t_valid)
    o_cmp, o_sel, o_win = nsa_fn(h3)
    flat = lambda a: a[:, :t].reshape(bsz * t, a.shape[-1])
    merged = merge_branches(h, lay, flat(oa), flat(o_cmp), flat(o_sel), flat(o_win), flat(oc),
                            consts["gate_expand"], lw["w_branch_hg"], lw["w_branch_nsa"], lw["w_branch_ret"],
                            tm=tm, tn=512)
    x2 = matmul_res(merged, lw["w_out"], x2, tm=tm, tn=512)
    hid = ffn_up(x2, lw["norm_ffn"], lw["w_gate"], lw["w_up"], tm=tm, tn=512)
    x2 = matmul_res(hid, lw["w_down"], x2, tm=tm, tn=512)
    return x2.reshape(bsz, t, d), h3, hg_new, ret_new


def kernel(x_prompt, x_sample, cache_nsa, cache_win, state_hgrn, state_ret, page_table,
           norm_attn, w_in, hgrn_lb_logits, hgrn_out_norm, nsa_q_norm, nsa_k_norm, nsa_cmp_w,
           w_branch_hg, w_branch_nsa, w_branch_ret, w_out, norm_ffn, w_gate, w_up, w_down):
    depth = w_in.shape[0]
    bp, tp, d = x_prompt.shape
    bs, ts, _ = x_sample.shape
    n_pool = cache_nsa.shape[1]
    n_pages = page_table.shape[1]
    past = n_pages * PAGE_SIZE
    wbuf = cache_win.shape[2]
    lay = _layout(d)

    sm = jax.nn.softmax(hgrn_lb_logits.astype(F32), axis=0)
    lower_bounds = jnp.clip(jnp.cumsum(sm, axis=0) - sm[0:1], 0.0, 1.0 - 1e-6)

    tt_p = 256 if tp % 256 == 0 else tp
    lg = jnp.log1p(-jnp.exp2(-5.0 - jnp.arange(RET_HEADS, dtype=F32)))
    consts = {
        "gate_expand": _gate_expand_matrix(),
        "lg_tab": jnp.broadcast_to(lg[:, None, None], (RET_HEADS, 1, max(tt_p, LANES))),
    }
    bsum = _block_sum_matrix(NSA_KV_WIDTH)
    pool_t = jnp.transpose(cache_nsa, (0, 1, 3, 4, 5, 2)).reshape(depth * n_pool, 4, NSA_KV_WIDTH, PAGE_SIZE)
    win3 = cache_win.reshape(depth, bs, wbuf, 2 * NSA_KV_WIDTH)

    ts_pad = 16
    tq_s = 16
    n_sel_p = -(-tp // SEL_BLOCK)
    n_sel_s = past // SEL_BLOCK + -(-ts // SEL_BLOCK)
    tk_tail = 1024 if past % 1024 == 0 else SEL_TK
    tq_p = 128 if tp % 128 == 0 else tp

    xp, xs = x_prompt, x_sample
    outs = {k: [] for k in ("kv_p", "kv_s", "win_p", "win_s", "hg_p", "hg_s", "ret_p", "ret_s")}
    for l in range(depth):
        lw = {
            "norm_attn": norm_attn[l], "w_in": _reorder_w_in(w_in[l], lay), "lb": lower_bounds[l].reshape(1, -1),
            "hg_out_norm": hgrn_out_norm[l],
            "w_branch_hg": w_branch_hg[l].astype(MXU_DT), "w_branch_nsa": w_branch_nsa[l].astype(MXU_DT),
            "w_branch_ret": w_branch_ret[l].astype(MXU_DT), "w_out": w_out[l].astype(MXU_DT),
            "norm_ffn": norm_ffn[l], "w_gate": w_gate[l].astype(MXU_DT), "w_up": w_up[l].astype(MXU_DT),
            "w_down": w_down[l].astype(MXU_DT),
        }
        gq, gk, cw = nsa_q_norm[l], nsa_k_norm[l], nsa_cmp_w[l]
        col_cmp = lay["nkv"] // (2 * NSA_KV_WIDTH)
        col_sel = col_cmp + 1
        col_win = lay["nwkv"] // (2 * NSA_KV_WIDTH)

        def nsa_prompt(h3):
            tr = 512 if tp % 512 == 0 else tp
            zeros16 = jnp.zeros((bp, CMP_STRIDE, 2 * NSA_KV_WIDTH), F32)
            kc, vc = cmp_prep(h3, col_cmp, zeros16, cw, gk[0], bsum, tr=tr)
            if tp % PROMPT_SEL_TK == 0 and n_sel_p <= 64 and tp >= WINDOW + LANES:
                qn = q_prep(h3, lay, gq, bsum, tr=tr, width=LANES)
                ks, vs = kv_prep(h3, col_sel, gk[1], bsum, tr=tr,
                                 aug=_key_digits(tp, SEL_TK, LANES, 2 * LANES, True))
                kw, vw = kv_prep(h3, col_win, gk[2], bsum, tr=tr,
                                 aug=_key_digits(tp, WIN_TK, NSA_HEAD_DIM, LANES, False))
                return nsa_attention_prompt(qn, _slope_rows(LANES), kc, vc, ks, vs, kw, vw, n_sel=n_sel_p)
            qn = q_prep(h3, lay, gq, bsum, tr=tr)
            ks, vs = kv_prep(h3, col_sel, gk[1], bsum, tr=tr)
            kw, vw = kv_prep(h3, col_win, gk[2], bsum, tr=tr)
            return nsa_attention(qn, _slope_rows(tq_p), kc, vc, ks, vs, kw, vw,
                                 tq=tq_p, sel_tk=SEL_TK, qpos0=0, wpos0=0, n_sel=n_sel_p)

        def nsa_sample(h3):
            pages = page_table + l * n_pool
            h16 = _pad_rows(h3, ts_pad)
            qn = q_prep(h16, lay, gq, bsum, tr=ts_pad)
            w2 = 2 * NSA_KV_WIDTH
            tail16 = h16[:, :, lay["nkv"]:lay["nkv"] + w2].reshape(bs, ts_pad, 2, NSA_KV_WIDTH)
            tail_t = jnp.pad(jnp.transpose(tail16, (0, 2, 3, 1)), ((0, 0), (0, 0), (0, 0), (0, PAGE_SIZE - ts_pad)))
            kc, vc = cmp_prep_paged(pool_t, pages, tail_t, cw, gk[0])
            ks, vs = kv_prep_paged(pool_t, pages, gk[1], extra_rows=tk_tail)
            htail = _pad_rows(h3[:, :, col_sel * w2:(col_sel + 1) * w2], tk_tail)
            ks_tail, vs_tail = kv_prep(htail, 0, gk[1], bsum, tr=tk_tail)
            ks = lax.dynamic_update_slice(ks, ks_tail, (0, 0, past, 0))
            vs = lax.dynamic_update_slice(vs, vs_tail, (0, 0, past, 0))
            kw_old, vw_old = kv_prep(win3[l], 0, gk[2], bsum, tr=wbuf)
            hw_new = _pad_rows(h3[:, :, col_win * w2:(col_win + 1) * w2], WIN_TK)
            kw_new, vw_new = kv_prep(hw_new, 0, gk[2], bsum, tr=WIN_TK)
            kw = jnp.concatenate([kw_old, kw_new], axis=2)
            vw = jnp.concatenate([vw_old, vw_new], axis=2)
            return nsa_attention(qn, _slope_rows(tq_s), kc, vc, ks, vs, kw, vw,
                                 tq=tq_s, sel_tk=tk_tail, qpos0=past, wpos0=past - wbuf, n_sel=n_sel_s)

        zeros_state = jnp.zeros((bp, HG_HEADS, HG_DK, HG_DV), F32)
        tm_p = 512 if (bp * tp) % 512 == 0 else bp * tp
        xp, h3p, hgp, rtp = _trunk_layer(xp, lw, lay, consts, nsa_prompt, zeros_state, zeros_state,
                                         seq_tiles=(tm_p, tt_p, tt_p, tp))
        xs, h3s, hgs, rts = _trunk_layer(xs, lw, lay, consts, nsa_sample, state_hgrn[l], state_ret[l],
                                         seq_tiles=(bs * ts, ts_pad, ts_pad, ts))

        kv_cols = slice(lay["nkv"], lay["nkv"] + 4 * NSA_KV_WIDTH)
        win_cols = slice(lay["nwkv"], lay["nwkv"] + 2 * NSA_KV_WIDTH)
        outs["kv_p"].append(h3p[:, :, kv_cols].reshape(bp, tp, 4, NSA_KV_HEADS, NSA_HEAD_DIM))
        outs["kv_s"].append(h3s[:, :, kv_cols].reshape(bs, ts, 4, NSA_KV_HEADS, NSA_HEAD_DIM))
        wlen = min(WINDOW, tp)
        outs["win_p"].append(h3p[:, tp - wlen:, win_cols].reshape(bp, wlen, 2, NSA_KV_HEADS, NSA_HEAD_DIM))
        ctx = jnp.concatenate([win3[l], h3s[:, :, win_cols]], axis=1)[:, ts:]
        outs["win_s"].append(ctx.reshape(bs, wbuf, 2, NSA_KV_HEADS, NSA_HEAD_DIM))
        outs["hg_p"].append(hgp); outs["hg_s"].append(hgs)
        outs["ret_p"].append(rtp); outs["ret_s"].append(rts)

    st = lambda k: jnp.stack(outs[k])
    return (xp, xs, st("kv_p"), st("kv_s"), st("win_p"), st("win_s"),
            st("hg_p"), st("hg_s"), st("ret_p"), st("ret_s"))
```

```python
import functools

import jax
import jax.numpy as jnp
from jax import lax
from jax.experimental import pallas as pl
from jax.experimental.pallas import tpu as pltpu

F32 = jnp.float32
MXU_DT = jnp.bfloat16

HG_HEADS, HG_DK, HG_DV = 8, 128, 128
NSA_HEADS, NSA_KV_HEADS, NSA_GROUP, NSA_HEAD_DIM = 16, 4, 4, 64
NSA_WIDTH = NSA_HEADS * NSA_HEAD_DIM
NSA_KV_WIDTH = NSA_KV_HEADS * NSA_HEAD_DIM
CMP_BLOCK, CMP_STRIDE, SEL_BLOCK, NSA_TOPK, WINDOW = 32, 16, 64, 16, 512
RET_HEADS, RET_DK, RET_DV = 8, 128, 128
PAGE_SIZE = 128
EPS = 1e-6
NEG_BIG = -1e30
POS_BIG = 1e30
MIN_F = 1e-20
LOWEST = -3.0e38

LANES = 128
VMEM_LIMIT_BYTES = 56 * 1024 * 1024

HG_SUB_SHIFT = 4
HG_SUB = 1 << HG_SUB_SHIFT
PAGES_PER_STEP = 8
SEL_TK = 256
PROMPT_SEL_TK = 512
WIN_TK = 128


def _cparams(*sem):
    return pltpu.CompilerParams(dimension_semantics=sem, vmem_limit_bytes=VMEM_LIMIT_BYTES)


def _dot(a, b):
    return jnp.dot(a, b, preferred_element_type=F32)


def _dot_nt(a, b):
    return lax.dot_general(a, b, (((1,), (1,)), ((), ())), preferred_element_type=F32)


def _split(x, n):
    out = []
    r = x
    for _ in range(n):
        p = r.astype(MXU_DT)
        out.append(p)
        r = r - p.astype(F32)
    return out


def _sigmoid(x):
    return 1.0 / (1.0 + jnp.exp(-x))


def _silu(x):
    return x * _sigmoid(x)


def _iota(shape, dim):
    return lax.broadcasted_iota(jnp.int32, shape, dim)


def _eye(n):
    return (_iota((n, n), 0) == _iota((n, n), 1)).astype(MXU_DT)


def _transpose_exact(x):
    eye = _eye(x.shape[1])
    return sum(_dot_nt(eye, p) for p in _split(x, 3))


def _head_rms(x, bsum, gain):
    ssq = sum(_dot(p, bsum) for p in _split(x * x, 2))
    return x * lax.rsqrt(ssq * (1.0 / NSA_HEAD_DIM) + EPS) * gain


def _rms_matmul_kernel(x_ref, g_ref, w_ref, o_ref, xn_ref):
    @pl.when(pl.program_id(1) == 0)
    def _():
        x = x_ref[...]
        ms = jnp.mean(x * x, axis=-1, keepdims=True)
        xn_ref[...] = (x * lax.rsqrt(ms + EPS) * g_ref[...]).astype(xn_ref.dtype)

    o_ref[...] = _dot(xn_ref[...], w_ref[...])


def rms_matmul(x, gain, w, *, tm, tn):
    m, d = x.shape
    n = w.shape[1]
    return pl.pallas_call(
        _rms_matmul_kernel,
        grid=(m // tm, n // tn),
        in_specs=[pl.BlockSpec((tm, d), lambda i, j: (i, 0)),
                  pl.BlockSpec((1, d), lambda i, j: (0, 0)),
                  pl.BlockSpec((d, tn), lambda i, j: (0, j))],
        out_specs=pl.BlockSpec((tm, tn), lambda i, j: (i, j)),
        out_shape=jax.ShapeDtypeStruct((m, n), F32),
        scratch_shapes=[pltpu.VMEM((tm, d), MXU_DT)],
        compiler_params=_cparams("parallel", "arbitrary"),
        name="in_proj",
    )(x, gain.reshape(1, d), w)


def _ffn_up_kernel(x_ref, g_ref, wg_ref, wu_ref, o_ref, xn_ref):
    @pl.when(pl.program_id(1) == 0)
    def _():
        x = x_ref[...]
        ms = jnp.mean(x * x, axis=-1, keepdims=True)
        xn_ref[...] = (x * lax.rsqrt(ms + EPS) * g_ref[...]).astype(xn_ref.dtype)

    xn = xn_ref[...]
    o_ref[...] = (_silu(_dot(xn, wg_ref[...])) * _dot(xn, wu_ref[...])).astype(o_ref.dtype)


def ffn_up(x, gain, wg, wu, *, tm, tn):
    m, d = x.shape
    f = wg.shape[1]
    return pl.pallas_call(
        _ffn_up_kernel,
        grid=(m // tm, f // tn),
        in_specs=[pl.BlockSpec((tm, d), lambda i, j: (i, 0)),
                  pl.BlockSpec((1, d), lambda i, j: (0, 0)),
                  pl.BlockSpec((d, tn), lambda i, j: (0, j)),
                  pl.BlockSpec((d, tn), lambda i, j: (0, j))],
        out_specs=pl.BlockSpec((tm, tn), lambda i, j: (i, j)),
        out_shape=jax.ShapeDtypeStruct((m, f), MXU_DT),
        scratch_shapes=[pltpu.VMEM((tm, d), MXU_DT)],
        compiler_params=_cparams("parallel", "arbitrary"),
        name="ffn_up",
    )(x, gain.reshape(1, d), wg, wu)


def _matmul_res_kernel(a_ref, w_ref, r_ref, o_ref):
    o_ref[...] = r_ref[...] + _dot(a_ref[...], w_ref[...])


def matmul_res(a, w, res, *, tm, tn):
    m, k = a.shape
    n = w.shape[1]
    return pl.pallas_call(
        _matmul_res_kernel,
        grid=(m // tm, n // tn),
        in_specs=[pl.BlockSpec((tm, k), lambda i, j: (i, 0)),
                  pl.BlockSpec((k, tn), lambda i, j: (0, j)),
                  pl.BlockSpec((tm, tn), lambda i, j: (i, j))],
        out_specs=pl.BlockSpec((tm, tn), lambda i, j: (i, j)),
        out_shape=jax.ShapeDtypeStruct((m, n), F32),
        compiler_params=_cparams("parallel", "arbitrary"),
        name="matmul_res",
    )(a, w, res)


def _merge_kernel(oa_ref, ocmp_ref, osel_ref, owin_ref, oc_ref, nbg_ref, ga_ref, gb_ref, gc_ref,
                  gx_ref, wa_ref, wb_ref, wc_ref, o_ref, ob_ref):
    @pl.when(pl.program_id(1) == 0)
    def _():
        pieces = _split(_sigmoid(nbg_ref[...]), 2)
        acc = None
        for br, o_br in enumerate((ocmp_ref, osel_ref, owin_ref)):
            e = sum(_dot(p, gx_ref[br]) for p in pieces)
            term = e * o_br[...]
            acc = term if acc is None else acc + term
        ob_ref[...] = acc.astype(ob_ref.dtype)

    o_ref[...] = (_sigmoid(ga_ref[...]) * _dot(oa_ref[...], wa_ref[...])
                  + _sigmoid(gb_ref[...]) * _dot(ob_ref[...], wb_ref[...])
                  + _sigmoid(gc_ref[...]) * _dot(oc_ref[...], wc_ref[...])).astype(o_ref.dtype)


def merge_branches(h, lay, oa, o_cmp, o_sel, o_win, oc, gate_expand, wa, wb, wc, *, tm, tn):
    m = h.shape[0]
    d = wa.shape[1]
    wdt = oa.shape[1]
    ca, cb, cc = ((lay["mg"] + i * d) // tn for i in range(3))
    cn = lay["nbg"] // LANES
    row = lambda i, j: (i, 0)
    return pl.pallas_call(
        _merge_kernel,
        grid=(m // tm, d // tn),
        in_specs=[pl.BlockSpec((tm, wdt), row), pl.BlockSpec((tm, wdt), row),
                  pl.BlockSpec((tm, wdt), row), pl.BlockSpec((tm, wdt), row),
                  pl.BlockSpec((tm, wdt), row),
                  pl.BlockSpec((tm, LANES), lambda i, j: (i, cn)),
                  pl.BlockSpec((tm, tn), lambda i, j: (i, ca + j)),
                  pl.BlockSpec((tm, tn), lambda i, j: (i, cb + j)),
                  pl.BlockSpec((tm, tn), lambda i, j: (i, cc + j)),
                  pl.BlockSpec((3, LANES, wdt), lambda i, j: (0, 0, 0)),
                  pl.BlockSpec((wdt, tn), lambda i, j: (0, j)),
                  pl.BlockSpec((wdt, tn), lambda i, j: (0, j)),
                  pl.BlockSpec((wdt, tn), lambda i, j: (0, j))],
        out_specs=pl.BlockSpec((tm, tn), lambda i, j: (i, j)),
        out_shape=jax.ShapeDtypeStruct((m, d), MXU_DT),
        scratch_shapes=[pltpu.VMEM((tm, wdt), MXU_DT)],
        compiler_params=_cparams("parallel", "arbitrary"),
        name="merge",
    )(oa, o_cmp, o_sel, o_win, oc, h, h, h, h, gate_expand, wa, wb, wc)


def _hgrn_kernel(hq_ref, hf_ref, hi_ref, hg_ref, lb_ref, gain_ref, s0_ref, o_ref, sout_ref,
                 st_ref, ut_ref, *, tt, t_valid):
    c = HG_SUB
    nj = tt // c
    tb = pl.program_id(2)

    @pl.when(tb == 0)
    def _():
        st_ref[...] = _transpose_exact(s0_ref[0, 0])

    hq = hq_ref[0]
    lb = lb_ref[...]
    q = _silu(hq)
    sig = _sigmoid(hf_ref[0])
    g = jnp.log(jnp.maximum(lb + (1.0 - lb) * sig, MIN_F))
    kk = (1.0 - lb) * (1.0 - sig)
    v = hi_ref[0]
    row = _iota((tt, HG_DK), 0)
    if t_valid is not None:
        live = (tb * tt + row) < t_valid
        g = jnp.where(live, g, 0.0)
        kk = jnp.where(live, kk, 0.0)

    r2 = _iota((tt, tt), 0)
    c2 = _iota((tt, tt), 1)
    same = (r2 >> HG_SUB_SHIFT) == (c2 >> HG_SUB_SHIFT)
    gp = _split(g, 3)
    tri = (same & (c2 <= r2)).astype(MXU_DT)
    blk = same.astype(MXU_DT)
    b = sum(_dot(tri, p) for p in gp)
    dtot = sum(_dot(blk, p) for p in gp)
    qe = q * jnp.exp(b)
    ke = kk * jnp.exp(dtot - b)
    edec = jnp.exp(dtot)

    q3 = q.reshape(nj, c, HG_DK)
    kk3 = kk.reshape(nj, c, HG_DK)
    b3 = b.reshape(nj, c, HG_DK)
    v3 = v.reshape(nj, c, HG_DV)
    tpos = _iota((nj, c, HG_DK), 1)
    ones = jnp.ones((HG_DK, HG_DV), MXU_DT)
    o3 = jnp.zeros((nj, c, HG_DV), F32)
    for s in range(c):
        m = tpos >= s
        diff = jnp.where(m, b3 - b3[:, s:s + 1, :], 0.0)
        y = jnp.where(m, q3 * kk3[:, s:s + 1, :] * jnp.exp(diff), 0.0)
        z = _dot(y.reshape(tt, HG_DK).astype(MXU_DT), ones).reshape(nj, c, HG_DV)
        o3 = o3 + z * v3[:, s:s + 1, :]
    o = o3.reshape(tt, HG_DV)

    v_b = v.astype(MXU_DT)
    ke_b = ke.astype(MXU_DT)
    for j in range(nj):
        ut_ref[j] = lax.dot_general(v_b[j * c:(j + 1) * c], ke_b[j * c:(j + 1) * c],
                                    (((0,), (0,)), ((), ())), preferred_element_type=F32)

    qe_b = qe.astype(MXU_DT)
    outs = []
    for j in range(nj):
        st = st_ref[...]
        outs.append(_dot_nt(qe_b[j * c:(j + 1) * c], st.astype(MXU_DT)))
        st_ref[...] = st * edec[j * c:j * c + 1, :] + ut_ref[j]
    o = o + jnp.concatenate(outs, axis=0)

    on = o * lax.rsqrt(jnp.mean(o * o, axis=-1, keepdims=True) + EPS) * gain_ref[...]
    o_ref[0] = (on * _silu(hg_ref[0])).astype(o_ref.dtype)

    @pl.when(tb == pl.num_programs(2) - 1)
    def _():
        sout_ref[0, 0] = _transpose_exact(st_ref[...])


def hgrn_mixer(h3, lay, lb, out_gain, s0, *, tt, t_valid):
    bsz, t, _ = h3.shape
    cq, cf, ci, cg = (lay[k] // LANES for k in ("hq", "hf", "hi", "hg"))
    col = lambda c0: (lambda b, hd, tb: (b, tb, c0 + hd))
    kern = functools.partial(_hgrn_kernel, tt=tt, t_valid=t_valid)
    return pl.pallas_call(
        kern,
        grid=(bsz, HG_HEADS, t // tt),
        in_specs=[pl.BlockSpec((1, tt, LANES), col(cq)), pl.BlockSpec((1, tt, LANES), col(cf)),
                  pl.BlockSpec((1, tt, LANES), col(ci)), pl.BlockSpec((1, tt, LANES), col(cg)),
                  pl.BlockSpec((1, HG_DK), lambda b, hd, tb: (0, hd)),
                  pl.BlockSpec((1, HG_DV), lambda b, hd, tb: (0, 0)),
                  pl.BlockSpec((1, 1, HG_DK, HG_DV), lambda b, hd, tb: (b, hd, 0, 0))],
        out_specs=[pl.BlockSpec((1, tt, HG_DV), lambda b, hd, tb: (b, tb, hd)),
                   pl.BlockSpec((1, 1, HG_DK, HG_DV), lambda b, hd, tb: (b, hd, 0, 0))],
        out_shape=[jax.ShapeDtypeStruct((bsz, t, HG_HEADS * HG_DV), MXU_DT),
                   jax.ShapeDtypeStruct((bsz, HG_HEADS, HG_DK, HG_DV), F32)],
        scratch_shapes=[pltpu.VMEM((HG_DV, HG_DK), F32),
                        pltpu.VMEM((tt // HG_SUB, HG_DV, HG_DK), F32)],
        compiler_params=_cparams("parallel", "parallel", "arbitrary"),
        name="hgrn",
    )(h3, h3, h3, h3, lb, out_gain.reshape(1, HG_DV), s0)


def _ret_kernel(q_ref, k_ref, v_ref, g_ref, lg_ref, s0_ref, o_ref, sout_ref, s_ref, *, tt, n_valid):
    tb = pl.program_id(2)

    @pl.when(tb == 0)
    def _():
        s_ref[...] = s0_ref[0, 0]

    lgl = lg_ref[0][:, :LANES]
    lgt = lg_ref[0][:, :tt]
    q = q_ref[0] * (RET_DK ** -0.5)
    k = k_ref[0]
    vb = v_ref[0].astype(MXU_DT)
    rel = (_iota((tt, tt), 0) - _iota((tt, tt), 1)).astype(F32)
    dmat = jnp.where(rel >= 0, jnp.exp(jnp.maximum(rel, 0.0) * lgt), 0.0)
    att = _dot_nt(q.astype(MXU_DT), k.astype(MXU_DT)) * dmat
    q_dec = jnp.exp((_iota((tt, RET_DK), 0) + 1).astype(F32) * lgl)
    s = s_ref[...]
    o = _dot(att.astype(MXU_DT), vb) + _dot((q * q_dec).astype(MXU_DT), s.astype(MXU_DT))

    spos = _iota((RET_DK, tt), 1)
    k_dec = jnp.where(spos < n_valid, jnp.exp(jnp.maximum(n_valid - 1 - spos, 0).astype(F32) * lgt), 0.0)
    kt = _transpose_exact(k)
    s_ref[...] = jnp.exp(float(n_valid) * lgl) * s + _dot((kt * k_dec).astype(MXU_DT), vb)

    mu = jnp.mean(o, axis=-1, keepdims=True)
    var = jnp.mean(jnp.square(o - mu), axis=-1, keepdims=True)
    o_ref[0] = ((o - mu) * lax.rsqrt(var + EPS) * _silu(g_ref[0])).astype(o_ref.dtype)

    @pl.when(tb == pl.num_programs(2) - 1)
    def _():
        sout_ref[0, 0] = s_ref[...]


def retention_mixer(h3, lay, lg_tab, s0, *, tt, n_valid):
    bsz, t, _ = h3.shape
    assert n_valid == tt or t == tt
    cq, ck, cv, cg = (lay[k] // LANES for k in ("rq", "rk", "rv", "rg"))
    col = lambda c0: (lambda b, hd, tb: (b, tb, c0 + hd))
    kern = functools.partial(_ret_kernel, tt=tt, n_valid=n_valid)
    return pl.pallas_call(
        kern,
        grid=(bsz, RET_HEADS, t // tt),
        in_specs=[pl.BlockSpec((1, tt, LANES), col(cq)), pl.BlockSpec((1, tt, LANES), col(ck)),
                  pl.BlockSpec((1, tt, LANES), col(cv)), pl.BlockSpec((1, tt, LANES), col(cg)),
                  pl.BlockSpec((1, 1, lg_tab.shape[2]), lambda b, hd, tb: (hd, 0, 0)),
                  pl.BlockSpec((1, 1, RET_DK, RET_DV), lambda b, hd, tb: (b, hd, 0, 0))],
        out_specs=[pl.BlockSpec((1, tt, RET_DV), lambda b, hd, tb: (b, tb, hd)),
                   pl.BlockSpec((1, 1, RET_DK, RET_DV), lambda b, hd, tb: (b, hd, 0, 0))],
        out_shape=[jax.ShapeDtypeStruct((bsz, t, RET_HEADS * RET_DV), MXU_DT),
                   jax.ShapeDtypeStruct((bsz, RET_HEADS, RET_DK, RET_DV), F32)],
        scratch_shapes=[pltpu.VMEM((RET_DK, RET_DV), F32)],
        compiler_params=_cparams("parallel", "parallel", "arbitrary"),
        name="retention",
    )(h3, h3, h3, h3, lg_tab, s0)


def _store_heads(ref, x, lead=()):
    for hh in range(NSA_KV_HEADS):
        ref[lead + (hh,)] = x[:, hh * NSA_HEAD_DIM:(hh + 1) * NSA_HEAD_DIM].astype(ref.dtype)


def _q_prep_kernel(q_ref, gain_ref, bsum_ref, o_ref):
    x = q_ref[0]
    hd = NSA_HEAD_DIM
    for kvh in range(NSA_KV_HEADS):
        lo = kvh * NSA_KV_WIDTH
        qn = _head_rms(x[:, lo:lo + NSA_KV_WIDTH], bsum_ref[...], gain_ref[...]) * (hd ** -0.5)
        for g in range(NSA_GROUP):
            o_ref[0, kvh, g, :, 0:hd] = qn[:, g * hd:(g + 1) * hd].astype(o_ref.dtype)
            if o_ref.shape[-1] > hd:
                o_ref[0, kvh, g, :, hd:] = jnp.zeros((x.shape[0], o_ref.shape[-1] - hd), o_ref.dtype)


def q_prep(h3, lay, gain_q, bsum, *, tr, width=NSA_HEAD_DIM):
    bsz, t, _ = h3.shape
    cq = lay["nq"] // NSA_WIDTH
    return pl.pallas_call(
        _q_prep_kernel,
        grid=(bsz, t // tr),
        in_specs=[pl.BlockSpec((1, tr, NSA_WIDTH), lambda b, i: (b, i, cq)),
                  pl.BlockSpec((1, NSA_KV_WIDTH), lambda b, i: (0, 0)),
                  pl.BlockSpec((NSA_KV_WIDTH, NSA_KV_WIDTH), lambda b, i: (0, 0))],
        out_specs=pl.BlockSpec((1, NSA_KV_HEADS, NSA_GROUP, tr, width), lambda b, i: (b, 0, 0, i, 0)),
        out_shape=jax.ShapeDtypeStruct((bsz, NSA_KV_HEADS, NSA_GROUP, t, width), MXU_DT),
        compiler_params=_cparams("parallel", "parallel"),
        name="nsa_q_prep",
    )(h3, jnp.tile(gain_q, NSA_KV_HEADS).reshape(1, NSA_KV_WIDTH), bsum)


def _kv_prep_kernel(x_ref, gain_ref, bsum_ref, *rest):
    ko_ref, vo_ref = rest[-2:]
    hd = NSA_HEAD_DIM
    x = x_ref[0]
    kn = _head_rms(x[:, :NSA_KV_WIDTH], bsum_ref[...], gain_ref[...])
    for hh in range(NSA_KV_HEADS):
        ko_ref[0, hh, :, 0:hd] = kn[:, hh * hd:(hh + 1) * hd].astype(ko_ref.dtype)
        if len(rest) == 3:
            ko_ref[0, hh, :, hd:] = rest[0][:, hd:]
        lo = NSA_KV_WIDTH + hh * hd
        vo_ref[0, hh, :, 0:hd] = x[:, lo:lo + hd].astype(vo_ref.dtype)
        if vo_ref.shape[-1] > hd:
            vo_ref[0, hh, :, hd:] = jnp.ones((x.shape[0], vo_ref.shape[-1] - hd), vo_ref.dtype)


def kv_prep(x3, col, gain, bsum, *, tr, aug=None):
    bsz, t, _ = x3.shape
    w2 = 2 * NSA_KV_WIDTH
    hd = NSA_HEAD_DIM
    kw = hd if aug is None else aug.shape[1]
    vw = hd if aug is None else LANES
    vshape = jax.ShapeDtypeStruct((bsz, NSA_KV_HEADS, t, vw), MXU_DT)
    kshape = jax.ShapeDtypeStruct((bsz, NSA_KV_HEADS, t, kw), MXU_DT)
    spec = lambda w: pl.BlockSpec((1, NSA_KV_HEADS, tr, w), lambda b, i: (b, 0, i, 0))
    in_specs = [pl.BlockSpec((1, tr, w2), lambda b, i: (b, i, col)),
                pl.BlockSpec((1, NSA_KV_WIDTH), lambda b, i: (0, 0)),
                pl.BlockSpec((NSA_KV_WIDTH, NSA_KV_WIDTH), lambda b, i: (0, 0))]
    args = [x3, jnp.tile(gain, NSA_KV_HEADS).reshape(1, NSA_KV_WIDTH), bsum]
    if aug is not None:
        in_specs.append(pl.BlockSpec((tr, kw), lambda b, i: (i, 0)))
        args.append(aug)
    return pl.pallas_call(
        _kv_prep_kernel,
        grid=(bsz, t // tr),
        in_specs=in_specs,
        out_specs=[spec(kw), spec(vw)],
        out_shape=[kshape, vshape],
        compiler_params=_cparams("parallel", "parallel"),
        name="nsa_kv_prep",
    )(*args)


def _head_rms_t(xt, gain_t):
    x3 = xt.reshape(NSA_KV_HEADS, NSA_HEAD_DIM, xt.shape[1])
    ms = jnp.mean(x3 * x3, axis=1, keepdims=True)
    return (x3 * lax.rsqrt(ms + EPS)).reshape(xt.shape) * gain_t


def _kv_prep_t_kernel(*refs, n_in):
    refs = refs[-(n_in + 3):]
    gain_ref, ko_ref, vo_ref = refs[n_in:]
    eye = _eye(PAGE_SIZE)
    for u in range(n_in):
        kt = _head_rms_t(refs[u][0, 0], gain_ref[...]).astype(MXU_DT)
        vt = refs[u][0, 1].astype(MXU_DT)
        k = _dot_nt(eye, kt)
        v = _dot_nt(eye, vt)
        for hh in range(NSA_KV_HEADS):
            sl = slice(hh * NSA_HEAD_DIM, (hh + 1) * NSA_HEAD_DIM)
            ko_ref[0, hh, u * PAGE_SIZE:(u + 1) * PAGE_SIZE, :] = k[:, sl].astype(ko_ref.dtype)
            vo_ref[0, hh, u * PAGE_SIZE:(u + 1) * PAGE_SIZE, :] = v[:, sl].astype(vo_ref.dtype)


def _gain_t(gain):
    return jnp.broadcast_to(jnp.tile(gain, NSA_KV_HEADS)[:, None], (NSA_KV_WIDTH, LANES))


def kv_prep_paged(pool_t, pages, gain, *, extra_rows):
    bsz, n_pages = pages.shape
    pp = PAGES_PER_STEP
    kern = functools.partial(_kv_prep_t_kernel, n_in=pp)
    oshape = jax.ShapeDtypeStruct((bsz, NSA_KV_HEADS, n_pages * PAGE_SIZE + extra_rows, NSA_HEAD_DIM), MXU_DT)
    ospec = pl.BlockSpec((1, NSA_KV_HEADS, pp * PAGE_SIZE, NSA_HEAD_DIM), lambda b, i, pt: (b, 0, i, 0))
    page_spec = lambda u: pl.BlockSpec((1, 2, NSA_KV_WIDTH, PAGE_SIZE), lambda b, i, pt: (pt[b, i * pp + u], 1, 0, 0))
    return pl.pallas_call(
        kern,
        grid_spec=pltpu.PrefetchScalarGridSpec(
            num_scalar_prefetch=1,
            grid=(bsz, n_pages // pp),
            in_specs=[page_spec(u) for u in range(pp)]
                     + [pl.BlockSpec((NSA_KV_WIDTH, LANES), lambda b, i, pt: (0, 0))],
            out_specs=[ospec, ospec]),
        out_shape=[oshape, oshape],
        compiler_params=_cparams("parallel", "arbitrary"),
        name="nsa_kv_prep_paged",
    )(pages, *([pool_t] * pp), _gain_t(gain))


def _cmp_prep_kernel(*refs, n_in, rows):
    refs = refs[-(n_in + 7):]
    nxt_ref, tail_ref, w_ref, gain_ref, bsum_ref, kc_ref, vc_ref = refs[n_in:]
    cs = CMP_STRIDE
    nb = rows // cs
    w0 = w_ref[0]
    w1 = w_ref[1]
    a0 = []
    a1 = []
    for u in range(n_in):
        x3 = refs[u][0].reshape(nb, cs, 2 * NSA_KV_WIDTH)
        a0.append(jnp.sum(x3 * w0[None], axis=1))
        a1.append(jnp.sum(x3 * w1[None], axis=1))
    a0 = jnp.concatenate(a0, axis=0) if n_in > 1 else a0[0]
    a1 = jnp.concatenate(a1, axis=0) if n_in > 1 else a1[0]
    last = pl.program_id(1) == pl.num_programs(1) - 1
    nx = jnp.where(last, tail_ref[0], nxt_ref[0])
    a1_next = jnp.sum(nx * w1, axis=0, keepdims=True)
    tot = nb * n_in
    a1s = pltpu.roll(a1, tot - 1, 0)
    a1s = jnp.where(_iota(a1s.shape, 0) == tot - 1, a1_next, a1s)
    comp = a0 + a1s
    kc = _head_rms(comp[:, :NSA_KV_WIDTH], bsum_ref[...], gain_ref[...])
    _store_heads(kc_ref, kc, (0,))
    _store_heads(vc_ref, comp[:, NSA_KV_WIDTH:], (0,))


def _cmp_weight_table(cmp_w):
    w = cmp_w.reshape(2, CMP_BLOCK // CMP_STRIDE, CMP_STRIDE)
    w = jnp.transpose(w, (1, 2, 0))
    return jnp.repeat(w, NSA_KV_WIDTH, axis=2)


def cmp_prep(x3, col, tail16, cmp_w, gain, bsum, *, tr):
    bsz, t, _ = x3.shape
    w2 = 2 * NSA_KV_WIDTH
    nbt = tr // CMP_STRIDE
    n_steps = t // tr
    kern = functools.partial(_cmp_prep_kernel, n_in=1, rows=tr)
    oshape = jax.ShapeDtypeStruct((bsz, NSA_KV_HEADS, t // CMP_STRIDE, NSA_HEAD_DIM), MXU_DT)
    ospec = pl.BlockSpec((1, NSA_KV_HEADS, nbt, NSA_HEAD_DIM), lambda b, i: (b, 0, i, 0))
    chunks_per_step = tr // CMP_STRIDE
    return pl.pallas_call(
        kern,
        grid=(bsz, n_steps),
        in_specs=[pl.BlockSpec((1, tr, w2), lambda b, i: (b, i, col)),
                  pl.BlockSpec((1, CMP_STRIDE, w2),
                               lambda b, i: (b, jnp.minimum(i + 1, n_steps - 1) * chunks_per_step, col)),
                  pl.BlockSpec((1, CMP_STRIDE, w2), lambda b, i: (b, 0, 0)),
                  pl.BlockSpec((2, CMP_STRIDE, w2), lambda b, i: (0, 0, 0)),
                  pl.BlockSpec((1, NSA_KV_WIDTH), lambda b, i: (0, 0)),
                  pl.BlockSpec((NSA_KV_WIDTH, NSA_KV_WIDTH), lambda b, i: (0, 0))],
        out_specs=[ospec, ospec],
        out_shape=[oshape, oshape],
        compiler_params=_cparams("parallel", "arbitrary"),
        name="nsa_cmp_prep",
    )(x3, x3, tail16, _cmp_weight_table(cmp_w), jnp.tile(gain, NSA_KV_HEADS).reshape(1, NSA_KV_WIDTH), bsum)


def _cmp_prep_t_kernel(*refs, n_in):
    refs = refs[-(n_in + 9):]
    nxt_ref, tail_ref, wk_ref, wv_ref, wnk_ref, wnv_ref, gain_ref, kc_ref, vc_ref = refs[n_in:]
    nb = wk_ref.shape[2]

    def wdot(xt, w):
        xh = xt.astype(MXU_DT)
        wh, wl = _split(w, 2)
        return _dot(xh, wh) + _dot(xh, wl)

    last = pl.program_id(1) == pl.num_programs(1) - 1
    comp_k = wdot(jnp.where(last, tail_ref[0, 0], nxt_ref[0, 0]), wnk_ref[...])
    comp_v = wdot(jnp.where(last, tail_ref[0, 1], nxt_ref[0, 1]), wnv_ref[...])
    for u in range(n_in):
        comp_k = comp_k + wdot(refs[u][0, 0], wk_ref[u])
        comp_v = comp_v + wdot(refs[u][0, 1], wv_ref[u])
    kct = _head_rms_t(comp_k, gain_ref[...][:, :nb]).astype(MXU_DT)
    eye = _eye(nb)
    _store_heads(kc_ref, _dot_nt(eye, kct), (0,))
    _store_heads(vc_ref, _dot_nt(eye, comp_v.astype(MXU_DT)), (0,))


def _cmp_band_tables(cmp_w, n_pages_step):
    cs = CMP_STRIDE
    w = cmp_w.reshape(2, CMP_BLOCK // cs, cs)
    nb = n_pages_step * PAGE_SIZE // cs
    rho = jnp.arange(PAGE_SIZE)
    ch = (jnp.arange(n_pages_step)[:, None] * (PAGE_SIZE // cs) + rho[None, :] // cs)[..., None]
    n = jnp.arange(nb)[None, None, :]
    s = rho % cs
    tabs = []
    for c in range(2):
        full = (w[c, 0][s][None, :, None] * (ch == n) + w[c, 1][s][None, :, None] * (ch == n + 1)).astype(F32)
        nxt = jnp.where((rho[:, None] < cs) & (n[0] == nb - 1), w[c, 1][s][:, None], 0.0).astype(F32)
        tabs.append((full, nxt))
    return tabs[0][0], tabs[1][0], tabs[0][1], tabs[1][1]


def cmp_prep_paged(pool_t, pages, tail_t, cmp_w, gain):
    bsz, n_pages = pages.shape
    pp = PAGES_PER_STEP
    n_steps = n_pages // pp
    nbt = pp * PAGE_SIZE // CMP_STRIDE
    kern = functools.partial(_cmp_prep_t_kernel, n_in=pp)
    oshape = jax.ShapeDtypeStruct((bsz, NSA_KV_HEADS, n_pages * PAGE_SIZE // CMP_STRIDE, NSA_HEAD_DIM), MXU_DT)
    ospec = pl.BlockSpec((1, NSA_KV_HEADS, nbt, NSA_HEAD_DIM), lambda b, i, pt: (b, 0, i, 0))
    pblock = (1, 2, NSA_KV_WIDTH, PAGE_SIZE)
    page_spec = lambda u: pl.BlockSpec(pblock, lambda b, i, pt: (pt[b, i * pp + u], 0, 0, 0))
    const = lambda *shape: pl.BlockSpec(shape, lambda b, i, pt: (0,) * len(shape))
    wk, wv, wnk, wnv = _cmp_band_tables(cmp_w, pp)
    return pl.pallas_call(
        kern,
        grid_spec=pltpu.PrefetchScalarGridSpec(
            num_scalar_prefetch=1,
            grid=(bsz, n_steps),
            in_specs=[page_spec(u) for u in range(pp)]
                     + [pl.BlockSpec(pblock, lambda b, i, pt: (pt[b, jnp.minimum(i + 1, n_steps - 1) * pp], 0, 0, 0)),
                        pl.BlockSpec(pblock, lambda b, i, pt: (b, 0, 0, 0)),
                        const(pp, PAGE_SIZE, nbt), const(pp, PAGE_SIZE, nbt), const(PAGE_SIZE, nbt),
                        const(PAGE_SIZE, nbt), const(NSA_KV_WIDTH, LANES)],
            out_specs=[ospec, ospec]),
        out_shape=[oshape, oshape],
        compiler_params=_cparams("parallel", "arbitrary"),
        name="nsa_cmp_prep_paged",
    )(pages, *([pool_t] * pp), pool_t, tail_t, wk, wv, wnk, wnv, _gain_t(gain))


def _nsa_kernel(q_ref, sl_ref, kc_ref, vc_ref, ks_ref, vs_ref, kw_ref, vw_ref, at_ref,
                ocmp_ref, osel_ref, owin_ref, sc_ref, m_ref, l_ref, acc_ref,
                *, tq, sel_tk, qpos0, wpos0, n_sel, topk):
    rows = NSA_GROUP * tq
    hd = NSA_HEAD_DIM
    t0 = qpos0 + pl.program_id(2) * tq
    q = q_ref[0, 0].reshape(rows, hd)
    slope = sl_ref[0][:, :1]
    nbp = kc_ref.shape[2]
    n_sel_pad, tl = sc_ref.shape

    def tok(shape):
        return t0 + (_iota(shape, 0) & (tq - 1))

    s = _dot_nt(q, kc_ref[0, 0])
    dist = tok((rows, nbp)) - (_iota((rows, nbp), 1) * CMP_STRIDE + (CMP_BLOCK - 1))
    valid = dist >= 0
    s = jnp.where(valid, s - slope * dist.astype(F32), NEG_BIG)
    p = jnp.where(valid, jnp.exp(s - jnp.max(s, axis=-1, keepdims=True)), 0.0)
    p = p / jnp.maximum(jnp.sum(p, axis=-1, keepdims=True), 1e-30)
    o_cmp = _dot(p.astype(MXU_DT), vc_ref[0, 0])

    sel = _topk_block_mask(p, at_ref, t0, tq=tq, tl=tl, n_sel=n_sel, topk=topk)
    spread = ((_iota((rows, tl), 0) & (tq - 1)) == _iota((rows, tl), 1)).astype(MXU_DT)
    nblk = sel_tk // SEL_BLOCK
    if nblk % 8 == 0:
        sc_ref[...] = sel
        in_tile = (_iota((nblk, sel_tk), 0) == (_iota((nblk, sel_tk), 1) >> 6)).astype(MXU_DT)
    else:
        selb = _dot_nt(spread, sel.astype(MXU_DT)).astype(MXU_DT)

    def softmax_pass(k_ref, v_ref, tk, lo, hi, kpos0, mask_fn):
        m_ref[...] = jnp.full(m_ref.shape, NEG_BIG, F32)
        l_ref[...] = jnp.zeros(l_ref.shape, F32)
        acc_ref[...] = jnp.zeros(acc_ref.shape, F32)
        rel = (_iota((rows, tk), 0) & (tq - 1)) - _iota((rows, tk), 1)
        reps = tk // LANES

        def body(kt, carry):
            k0 = pl.multiple_of(kt * tk, tk)
            s = _dot_nt(q, k_ref[0, 0, pl.ds(k0, tk), :])
            d = rel + (t0 - kpos0 - k0)
            valid = mask_fn(d, k0)
            s = jnp.where(valid, s - slope * d.astype(F32), NEG_BIG)
            m_prev = m_ref[...]
            m_new = jnp.maximum(m_prev, jnp.max(s, axis=-1, keepdims=True))
            alpha = jnp.exp(m_prev - m_new)
            mrep = m_new if reps == 1 else jnp.concatenate([m_new] * reps, axis=1)
            p = jnp.where(valid, jnp.exp(s - mrep), 0.0)
            l_ref[...] = alpha * l_ref[...] + jnp.sum(p, axis=-1, keepdims=True)
            acc_ref[...] = alpha[:, :hd] * acc_ref[...] + _dot(p.astype(MXU_DT), v_ref[0, 0, pl.ds(k0, tk), :])
            m_ref[...] = m_new
            return carry

        lax.fori_loop(lo, hi, body, 0)
        return acc_ref[...] / jnp.maximum(l_ref[...][:, :hd], 1e-30)

    n_kt_all = ks_ref.shape[2] // sel_tk
    hi_sel = jnp.minimum(n_kt_all, (t0 + tq - 1) // sel_tk + 1)

    def sel_mask(d, k0):
        if nblk % 8 == 0:
            blk0 = pl.multiple_of(k0 // SEL_BLOCK, nblk)
            mine = _dot_nt(spread, sc_ref[pl.ds(blk0, nblk), :].astype(MXU_DT)).astype(MXU_DT)
            return (_dot(mine, in_tile) > 0.5) & (d >= 0)
        blk_of_key = (k0 + _iota((n_sel_pad, sel_tk), 1)) >> 6
        expand = (_iota((n_sel_pad, sel_tk), 0) == blk_of_key).astype(MXU_DT)
        return (_dot(selb, expand) > 0.5) & (d >= 0)

    o_sel = softmax_pass(ks_ref, vs_ref, sel_tk, 0, hi_sel, 0, sel_mask)

    n_wt_all = kw_ref.shape[2] // WIN_TK
    lo_win = jnp.maximum(t0 - (WINDOW - 1) - wpos0, 0) // WIN_TK
    hi_win = jnp.minimum(n_wt_all, (t0 + tq - 1 - wpos0) // WIN_TK + 1)
    o_win = softmax_pass(kw_ref, vw_ref, WIN_TK, lo_win, hi_win, wpos0,
                         lambda d, k0: (d >= 0) & (d < WINDOW))

    for g in range(NSA_GROUP):
        sl = slice(g * hd, (g + 1) * hd)
        ocmp_ref[0, :, sl] = o_cmp[g * tq:(g + 1) * tq]
        osel_ref[0, :, sl] = o_sel[g * tq:(g + 1) * tq]
        owin_ref[0, :, sl] = o_win[g * tq:(g + 1) * tq]


def _imp_to_sel_matrix(n_sel_pad, nbp):
    m = SEL_BLOCK // CMP_STRIDE
    r = CMP_BLOCK // CMP_STRIDE
    jj = jnp.arange(n_sel_pad)[:, None]
    ii = jnp.arange(nbp)[None, :]
    cnt = sum(((ii - rr) >= m * jj) & ((ii - rr) < m * (jj + 1)) for rr in range(r))
    return cnt.astype(MXU_DT)


def nsa_attention(qn, slopes, kc, vc, ks, vs, kw, vw, *, tq, sel_tk, qpos0, wpos0, n_sel):
    bsz, _, _, tql, hd = qn.shape
    nbp, tk_all, tw_all = kc.shape[2], ks.shape[2], kw.shape[2]
    assert tq & (tq - 1) == 0 and tk_all % sel_tk == 0 and tw_all % WIN_TK == 0
    n_sel_pad = -(-max(n_sel, tk_all // SEL_BLOCK) // 64) * 64
    tl = max(tq, LANES)
    rows = NSA_GROUP * tq
    at = _imp_to_sel_matrix(n_sel_pad, nbp)
    kern = functools.partial(_nsa_kernel, tq=tq, sel_tk=sel_tk, qpos0=qpos0, wpos0=wpos0, n_sel=n_sel,
                             topk=min(NSA_TOPK, n_sel))
    full = lambda n: pl.BlockSpec((1, 1, n, hd), lambda b, kh, i: (b, kh, 0, 0))
    ospec = pl.BlockSpec((1, tq, NSA_GROUP * hd), lambda b, kh, i: (b, i, kh))
    oshape = jax.ShapeDtypeStruct((bsz, tql, NSA_WIDTH), F32)
    return pl.pallas_call(
        kern,
        grid=(bsz, NSA_KV_HEADS, tql // tq),
        in_specs=[pl.BlockSpec((1, 1, NSA_GROUP, tq, hd), lambda b, kh, i: (b, kh, 0, i, 0)),
                  pl.BlockSpec((1, rows, LANES), lambda b, kh, i: (kh, 0, 0)),
                  full(nbp), full(nbp), full(tk_all), full(tk_all), full(tw_all), full(tw_all),
                  pl.BlockSpec((n_sel_pad, nbp), lambda b, kh, i: (0, 0))],
        out_specs=[ospec, ospec, ospec],
        out_shape=[oshape, oshape, oshape],
        scratch_shapes=[pltpu.VMEM((n_sel_pad, tl), F32),
                        pltpu.VMEM((rows, LANES), F32), pltpu.VMEM((rows, LANES), F32),
                        pltpu.VMEM((rows, hd), F32)],
        compiler_params=_cparams("parallel", "parallel", "arbitrary"),
        name="nsa_attention",
    )(qn, slopes, kc, vc, ks, vs, kw, vw, at)


def _topk_block_mask(p, at_ref, t0, *, tq, tl, n_sel, topk):
    n_sel_pad, nbp = at_ref.shape
    imp = p[0:tq] + p[tq:2 * tq] + p[2 * tq:3 * tq] + p[3 * tq:4 * tq]
    if tl > tq:
        imp = jnp.concatenate([imp, jnp.zeros((tl - tq, nbp), F32)], axis=0)
    imp_sel = sum(_dot_nt(at_ref[...], piece) for piece in _split(imp, 3))
    j = _iota((n_sel_pad, tl), 0)
    tt = t0 + _iota((n_sel_pad, tl), 1)
    forced = (j == 0) | (j == (tt >> 6))
    allowed = (j << 6) <= tt
    score = jnp.where(forced, POS_BIG, jnp.where(allowed, imp_sel, NEG_BIG))
    score = jnp.where(j >= n_sel, LOWEST, score)
    jf = j.astype(F32)
    sel = jnp.zeros((n_sel_pad, tl), F32)
    for _ in range(topk):
        mx = jnp.max(score, axis=0, keepdims=True)
        first = jnp.min(jnp.where(score == mx, jf, 1e9), axis=0, keepdims=True)
        pick = jf == first
        sel = jnp.where(pick, 1.0, sel)
        score = jnp.where(pick, -jnp.inf, score)
    return jnp.where(allowed, sel, 0.0)


def _nsa_prompt_kernel(q_ref, sl_ref, qts_ref, qtw_ref, kc_ref, vc_ref, ks_ref, vs_ref, kw_ref, vw_ref, at_ref,
                       ocmp_ref, osel_ref, owin_ref, m_ref, acc_ref, *, tq, n_sel, topk):
    rows = NSA_GROUP * tq
    hd = NSA_HEAD_DIM
    assert tq == LANES
    qi = pl.program_id(2)
    t0 = qi * tq
    q128 = q_ref[0, 0].reshape(rows, LANES)
    slope = sl_ref[0][:, :1]
    nbp = kc_ref.shape[2]

    s = _dot_nt(q128[:, :hd], kc_ref[0, 0])
    tok = t0 + (_iota((rows, nbp), 0) & (tq - 1))
    dist = tok - (_iota((rows, nbp), 1) * CMP_STRIDE + (CMP_BLOCK - 1))
    valid = dist >= 0
    s = jnp.where(valid, s - slope * dist.astype(F32), NEG_BIG)
    p = jnp.where(valid, jnp.exp(s - jnp.max(s, axis=-1, keepdims=True)), 0.0)
    p = p / jnp.maximum(jnp.sum(p, axis=-1, keepdims=True), 1e-30)
    o_cmp = _dot(p.astype(MXU_DT), vc_ref[0, 0])

    def normalised(acc):
        return acc[:, :hd] / jnp.maximum(acc[:, hd:], 1e-30)

    lane = _iota((rows, LANES), 1)
    qw = jnp.where(lane < hd, q128, qtw_ref[0])
    span = WINDOW + tq
    ws = pl.multiple_of(jnp.maximum(t0 - WINDOW, 0), tq)
    sw = _dot_nt(qw, kw_ref[0, 0, pl.ds(ws, span), :])
    d = (_iota((rows, span), 0) & (tq - 1)) - _iota((rows, span), 1) + (t0 - ws)
    sw = jnp.where((d >= 0) & (d < WINDOW), sw, NEG_BIG)
    pw = jnp.exp(sw - jnp.max(sw, axis=-1, keepdims=True))
    o_win = normalised(_dot(pw.astype(MXU_DT), vw_ref[0, 0, pl.ds(ws, span), :]))

    sel = _topk_block_mask(p, at_ref, t0, tq=tq, tl=tq, n_sel=n_sel, topk=topk)
    sel_t = jnp.concatenate([jnp.zeros_like(sel), sel], axis=0).astype(MXU_DT)
    spread = ((_iota((rows, tq), 0) & (tq - 1)) == _iota((rows, tq), 1)).astype(MXU_DT)
    selb = _dot_nt(spread, sel_t)
    neg = jnp.where((lane >= hd) & (selb < 0.5), NEG_BIG, 0.0)
    qa = jnp.concatenate([(q128.astype(F32) + neg).astype(MXU_DT), qts_ref[0]], axis=1)

    tk = PROMPT_SEL_TK
    kd = t0 // tk
    k0 = pl.multiple_of(kd * tk, tk)
    s = _dot_nt(qa, ks_ref[0, 0, pl.ds(k0, tk), :])
    causal = ((_iota((rows, tk), 0) & (tq - 1)) - _iota((rows, tk), 1) + (t0 - k0)) >= 0
    s = jnp.where(causal, s, NEG_BIG)
    m = jnp.max(s, axis=-1, keepdims=True)
    m_ref[...] = jnp.broadcast_to(m, m_ref.shape)
    acc_ref[...] = _dot(jnp.exp(s - m).astype(MXU_DT), vs_ref[0, 0, pl.ds(k0, tk), :])

    def sel_body(kt, carry):
        ka = pl.multiple_of(kt * tk, tk)
        s = _dot_nt(qa, ks_ref[0, 0, pl.ds(ka, tk), :])
        m_prev = m_ref[...]
        m_new = jnp.maximum(m_prev, jnp.max(s, axis=-1, keepdims=True))
        p = jnp.exp(s - jnp.concatenate([m_new] * (tk // LANES), axis=1))
        acc_ref[...] = jnp.exp(m_prev - m_new) * acc_ref[...] + _dot(p.astype(MXU_DT), vs_ref[0, 0, pl.ds(ka, tk), :])
        m_ref[...] = m_new
        return carry

    lax.fori_loop(0, kd, sel_body, 0)
    o_sel = normalised(acc_ref[...])

    for g in range(NSA_GROUP):
        sl = slice(g * hd, (g + 1) * hd)
        ocmp_ref[0, :, sl] = o_cmp[g * tq:(g + 1) * tq]
        osel_ref[0, :, sl] = o_sel[g * tq:(g + 1) * tq]
        owin_ref[0, :, sl] = o_win[g * tq:(g + 1) * tq]


def _slope_digits(tq, tile, lane0):
    hh = jnp.arange(1, NSA_HEADS + 1, dtype=F32)
    slopes = jnp.exp2(-8.0 * hh / NSA_HEADS)
    pieces = jnp.stack(_split(slopes, 3), axis=1).astype(F32)
    six = jnp.concatenate([pieces * float(tile), pieces], axis=1)
    tab = jnp.zeros((NSA_HEADS, LANES), F32).at[:, lane0:lane0 + 6].set(six)
    tab = tab.reshape(NSA_KV_HEADS, NSA_GROUP, 1, LANES)
    return jnp.broadcast_to(tab, (NSA_KV_HEADS, NSA_GROUP, tq, LANES)).reshape(
        NSA_KV_HEADS, NSA_GROUP * tq, LANES).astype(MXU_DT)


def _key_digits(t, tile, lane0, width, onehot):
    r = jnp.arange(t)
    lane = jnp.arange(width)[None, :]
    tab = jnp.zeros((t, width), F32)
    if onehot:
        tab = jnp.where(lane - NSA_HEAD_DIM == (r // SEL_BLOCK)[:, None], 1.0, tab)
    hi = (r // tile).astype(F32)[:, None]
    lo = (r % tile).astype(F32)[:, None]
    tab = jnp.where((lane >= lane0) & (lane < lane0 + 3), hi, tab)
    tab = jnp.where((lane >= lane0 + 3) & (lane < lane0 + 6), lo, tab)
    return tab.astype(MXU_DT)


def nsa_attention_prompt(q128, slopes, kc, vc, ks_aug, vs, kw_aug, vw, *, n_sel):
    bsz, _, _, t, _ = q128.shape
    tq = LANES
    hd = NSA_HEAD_DIM
    nbp = kc.shape[2]
    assert n_sel <= 64 and t % PROMPT_SEL_TK == 0 and t >= WINDOW + tq
    rows = NSA_GROUP * tq
    at = _imp_to_sel_matrix(64, nbp)
    kern = functools.partial(_nsa_prompt_kernel, tq=tq, n_sel=n_sel, topk=min(NSA_TOPK, n_sel))
    full = lambda n, w: pl.BlockSpec((1, 1, n, w), lambda b, kh, i: (b, kh, 0, 0))
    per_head = pl.BlockSpec((1, rows, LANES), lambda b, kh, i: (kh, 0, 0))
    ospec = pl.BlockSpec((1, tq, NSA_GROUP * hd), lambda b, kh, i: (b, i, kh))
    oshape = jax.ShapeDtypeStruct((bsz, t, NSA_WIDTH), F32)
    return pl.pallas_call(
        kern,
        grid=(bsz, NSA_KV_HEADS, t // tq),
        in_specs=[pl.BlockSpec((1, 1, NSA_GROUP, tq, LANES), lambda b, kh, i: (b, kh, 0, i, 0)),
                  per_head, per_head, per_head,
                  full(nbp, hd), full(nbp, hd), full(t, 2 * LANES), full(t, LANES), full(t, LANES), full(t, LANES),
                  pl.BlockSpec((64, nbp), lambda b, kh, i: (0, 0))],
        out_specs=[ospec, ospec, ospec],
        out_shape=[oshape, oshape, oshape],
        scratch_shapes=[pltpu.VMEM((rows, LANES), F32), pltpu.VMEM((rows, LANES), F32)],
        compiler_params=_cparams("parallel", "parallel", "arbitrary"),
        name="nsa_attention_prompt",
    )(q128, slopes, _slope_digits(tq, SEL_TK, 0), _slope_digits(tq, WIN_TK, hd),
      kc, vc, ks_aug, vs, kw_aug, vw, at)


def _layout(d_model):
    hw, rw = HG_HEADS * HG_DK, RET_HEADS * RET_DK
    names = [("hq", hw), ("hf", hw), ("hi", hw), ("hg", hw), ("nq", NSA_WIDTH), ("nkv", 4 * NSA_KV_WIDTH),
             ("nwkv", 2 * NSA_KV_WIDTH), ("rq", rw), ("rk", rw), ("rv", rw), ("rg", rw), ("mg", 3 * d_model),
             ("nbg", 3 * NSA_HEADS)]
    lay, off = {}, 0
    for name, width in names:
        lay[name] = off
        off += width
    lay["used"] = off
    lay["total"] = -(-off // 1024) * 1024
    return lay


def _reorder_w_in(w_in, lay):
    a = lay["rq"]
    nb = 3 * NSA_HEADS
    d = w_in.shape[0]
    pad = jnp.zeros((d, lay["total"] - lay["used"]), w_in.dtype)
    return jnp.concatenate([w_in[:, :a], w_in[:, a + nb:], w_in[:, a:a + nb], pad], axis=1).astype(MXU_DT)


def _block_sum_matrix(width):
    i = jnp.arange(width)
    return (i[:, None] // NSA_HEAD_DIM == i[None, :] // NSA_HEAD_DIM).astype(MXU_DT)


def _gate_expand_matrix():
    c = jnp.arange(LANES)[:, None]
    col = jnp.arange(NSA_WIDTH)[None, :]
    return jnp.stack([(c == (col // NSA_HEAD_DIM) * 3 + br) for br in range(3)]).astype(MXU_DT)


def _slope_rows(tq):
    hh = jnp.arange(1, NSA_HEADS + 1, dtype=F32)
    slopes = jnp.exp2(-8.0 * hh / NSA_HEADS).reshape(NSA_KV_HEADS, NSA_GROUP)
    rows = jnp.repeat(slopes, tq, axis=1)
    return jnp.broadcast_to(rows[:, :, None], (NSA_KV_HEADS, NSA_GROUP * tq, LANES))


def _pad_rows(x, n):
    return jnp.pad(x, ((0, 0), (0, n - x.shape[1])) + ((0, 0),) * (x.ndim - 2))


def _trunk_layer(x, lw, lay, consts, nsa_fn, hg_state, ret_state, *, seq_tiles):
    bsz, t, d = x.shape
    tm, tt_h, tt_r, t_valid = seq_tiles
    x2 = x.reshape(bsz * t, d)
    h = rms_matmul(x2, lw["norm_attn"], lw["w_in"], tm=tm, tn=1024)
    np_ = h.shape[1]
    h3 = h.reshape(bsz, t, np_)
    tp = -(-t // tt_h) * tt_h
    h3p = _pad_rows(h3, tp) if tp != t else h3
    oa, hg_new = hgrn_mixer(h3p, lay, lw["lb"], lw["hg_out_norm"], hg_state, tt=tt_h,
                            t_valid=None if tp == t else t_valid)
    tpr = -(-t // tt_r) * tt_r
    h3r = _pad_rows(h3, tpr) if tpr != t else h3
    oc, ret_new = retention_mixer(h3r, lay, consts["lg_tab"], ret_state, tt=tt_r,
                                  n_valid=tt_r if tpr == t else t_valid)
    o_cmp, o_sel, o_win = nsa_fn(h3)
    flat = lambda a: a[:, :t].reshape(bsz * t, a.shape[-1])
    merged = merge_branches(h, lay, flat(oa), flat(o_cmp), flat(o_sel), flat(o_win), flat(oc),
                            consts["gate_expand"], lw["w_branch_hg"], lw["w_branch_nsa"], lw["w_branch_ret"],
                            tm=min(tm, 512), tn=512)
    x2 = matmul_res(merged, lw["w_out"], x2, tm=tm, tn=512)
    hid = ffn_up(x2, lw["norm_ffn"], lw["w_gate"], lw["w_up"], tm=tm, tn=512)
    x2 = matmul_res(hid, lw["w_down"], x2, tm=tm, tn=512)
    return x2.reshape(bsz, t, d), h3, hg_new, ret_new


def kernel(x_prompt, x_sample, cache_nsa, cache_win, state_hgrn, state_ret, page_table,
           norm_attn, w_in, hgrn_lb_logits, hgrn_out_norm, nsa_q_norm, nsa_k_norm, nsa_cmp_w,
           w_branch_hg, w_branch_nsa, w_branch_ret, w_out, norm_ffn, w_gate, w_up, w_down):
    depth = w_in.shape[0]
    bp, tp, d = x_prompt.shape
    bs, ts, _ = x_sample.shape
    n_pool = cache_nsa.shape[1]
    n_pages = page_table.shape[1]
    past = n_pages * PAGE_SIZE
    wbuf = cache_win.shape[2]
    lay = _layout(d)

    sm = jax.nn.softmax(hgrn_lb_logits.astype(F32), axis=0)
    lower_bounds = jnp.clip(jnp.cumsum(sm, axis=0) - sm[0:1], 0.0, 1.0 - 1e-6)

    tt_p = 256 if tp % 256 == 0 else tp
    lg = jnp.log1p(-jnp.exp2(-5.0 - jnp.arange(RET_HEADS, dtype=F32)))
    consts = {
        "gate_expand": _gate_expand_matrix(),
        "lg_tab": jnp.broadcast_to(lg[:, None, None], (RET_HEADS, 1, max(tt_p, LANES))),
    }
    bsum = _block_sum_matrix(NSA_KV_WIDTH)
    pool_t = jnp.transpose(cache_nsa, (0, 1, 3, 4, 5, 2)).reshape(depth * n_pool, 4, NSA_KV_WIDTH, PAGE_SIZE)
    win3 = cache_win.reshape(depth, bs, wbuf, 2 * NSA_KV_WIDTH)

    ts_pad = 16
    tq_s = 16
    n_sel_p = -(-tp // SEL_BLOCK)
    n_sel_s = past // SEL_BLOCK + -(-ts // SEL_BLOCK)
    tk_tail = 2048 if past % 2048 == 0 else SEL_TK
    tq_p = 128 if tp % 128 == 0 else tp

    xp, xs = x_prompt, x_sample
    outs = {k: [] for k in ("kv_p", "kv_s", "win_p", "win_s", "hg_p", "hg_s", "ret_p", "ret_s")}
    for l in range(depth):
        lw = {
            "norm_attn": norm_attn[l], "w_in": _reorder_w_in(w_in[l], lay), "lb": lower_bounds[l].reshape(1, -1),
            "hg_out_norm": hgrn_out_norm[l],
            "w_branch_hg": w_branch_hg[l].astype(MXU_DT), "w_branch_nsa": w_branch_nsa[l].astype(MXU_DT),
            "w_branch_ret": w_branch_ret[l].astype(MXU_DT), "w_out": w_out[l].astype(MXU_DT),
            "norm_ffn": norm_ffn[l], "w_gate": w_gate[l].astype(MXU_DT), "w_up": w_up[l].astype(MXU_DT),
            "w_down": w_down[l].astype(MXU_DT),
        }
        gq, gk, cw = nsa_q_norm[l], nsa_k_norm[l], nsa_cmp_w[l]
        col_cmp = lay["nkv"] // (2 * NSA_KV_WIDTH)
        col_sel = col_cmp + 1
        col_win = lay["nwkv"] // (2 * NSA_KV_WIDTH)

        def nsa_prompt(h3):
            tr = 512 if tp % 512 == 0 else tp
            zeros16 = jnp.zeros((bp, CMP_STRIDE, 2 * NSA_KV_WIDTH), F32)
            kc, vc = cmp_prep(h3, col_cmp, zeros16, cw, gk[0], bsum, tr=tr)
            if tp % PROMPT_SEL_TK == 0 and n_sel_p <= 64 and tp >= WINDOW + LANES:
                qn = q_prep(h3, lay, gq, bsum, tr=tr, width=LANES)
                ks, vs = kv_prep(h3, col_sel, gk[1], bsum, tr=tr,
                                 aug=_key_digits(tp, SEL_TK, LANES, 2 * LANES, True))
                kw, vw = kv_prep(h3, col_win, gk[2], bsum, tr=tr,
                                 aug=_key_digits(tp, WIN_TK, NSA_HEAD_DIM, LANES, False))
                return nsa_attention_prompt(qn, _slope_rows(LANES), kc, vc, ks, vs, kw, vw, n_sel=n_sel_p)
            qn = q_prep(h3, lay, gq, bsum, tr=tr)
            ks, vs = kv_prep(h3, col_sel, gk[1], bsum, tr=tr)
            kw, vw = kv_prep(h3, col_win, gk[2], bsum, tr=tr)
            return nsa_attention(qn, _slope_rows(tq_p), kc, vc, ks, vs, kw, vw,
                                 tq=tq_p, sel_tk=SEL_TK, qpos0=0, wpos0=0, n_sel=n_sel_p)

        def nsa_sample(h3):
            pages = page_table + l * n_pool
            h16 = _pad_rows(h3, ts_pad)
            qn = q_prep(h16, lay, gq, bsum, tr=ts_pad)
            w2 = 2 * NSA_KV_WIDTH
            tail16 = h16[:, :, lay["nkv"]:lay["nkv"] + w2].reshape(bs, ts_pad, 2, NSA_KV_WIDTH)
            tail_t = jnp.pad(jnp.transpose(tail16, (0, 2, 3, 1)), ((0, 0), (0, 0), (0, 0), (0, PAGE_SIZE - ts_pad)))
            kc, vc = cmp_prep_paged(pool_t, pages, tail_t, cw, gk[0])
            ks, vs = kv_prep_paged(pool_t, pages, gk[1], extra_rows=tk_tail)
            htail = _pad_rows(h3[:, :, col_sel * w2:(col_sel + 1) * w2], tk_tail)
            ks_tail, vs_tail = kv_prep(htail, 0, gk[1], bsum, tr=tk_tail)
            ks = lax.dynamic_update_slice(ks, ks_tail, (0, 0, past, 0))
            vs = lax.dynamic_update_slice(vs, vs_tail, (0, 0, past, 0))
            kw_old, vw_old = kv_prep(win3[l], 0, gk[2], bsum, tr=wbuf)
            hw_new = _pad_rows(h3[:, :, col_win * w2:(col_win + 1) * w2], WIN_TK)
            kw_new, vw_new = kv_prep(hw_new, 0, gk[2], bsum, tr=WIN_TK)
            kw = jnp.concatenate([kw_old, kw_new], axis=2)
            vw = jnp.concatenate([vw_old, vw_new], axis=2)
            return nsa_attention(qn, _slope_rows(tq_s), kc, vc, ks, vs, kw, vw,
                                 tq=tq_s, sel_tk=tk_tail, qpos0=past, wpos0=past - wbuf, n_sel=n_sel_s)

        zeros_state = jnp.zeros((bp, HG_HEADS, HG_DK, HG_DV), F32)
        tm_p = 1024 if (bp * tp) % 1024 == 0 else bp * tp
        xp, h3p, hgp, rtp = _trunk_layer(xp, lw, lay, consts, nsa_prompt, zeros_state, zeros_state,
                                         seq_tiles=(tm_p, tt_p, tt_p, tp))
        xs, h3s, hgs, rts = _trunk_layer(xs, lw, lay, consts, nsa_sample, state_hgrn[l], state_ret[l],
                                         seq_tiles=(bs * ts, ts_pad, ts_pad, ts))

        kv_cols = slice(lay["nkv"], lay["nkv"] + 4 * NSA_KV_WIDTH)
        win_cols = slice(lay["nwkv"], lay["nwkv"] + 2 * NSA_KV_WIDTH)
        outs["kv_p"].append(h3p[:, :, kv_cols].reshape(bp, tp, 4, NSA_KV_HEADS, NSA_HEAD_DIM))
        outs["kv_s"].append(h3s[:, :, kv_cols].reshape(bs, ts, 4, NSA_KV_HEADS, NSA_HEAD_DIM))
        wlen = min(WINDOW, tp)
        outs["win_p"].append(h3p[:, tp - wlen:, win_cols].reshape(bp, wlen, 2, NSA_KV_HEADS, NSA_HEAD_DIM))
        ctx = jnp.concatenate([win3[l], h3s[:, :, win_cols]], axis=1)[:, ts:]
        outs["win_s"].append(ctx.reshape(bs, wbuf, 2, NSA_KV_HEADS, NSA_HEAD_DIM))
        outs["hg_p"].append(hgp); outs["hg_s"].append(hgs)
        outs["ret_p"].append(rtp); outs["ret_s"].append(rts)

    st = lambda k: jnp.stack(outs[k])
    return (xp, xs, st("kv_p"), st("kv_s"), st("win_p"), st("win_s"),
            st("hg_p"), st("hg_s"), st("ret_p"), st("ret_s"))
```

```python
import functools

import jax
import jax.numpy as jnp
from jax import lax
from jax.experimental import pallas as pl
from jax.experimental.pallas import tpu as pltpu

F32 = jnp.float32
MXU_DT = jnp.bfloat16

HG_HEADS, HG_DK, HG_DV = 8, 128, 128
NSA_HEADS, NSA_KV_HEADS, NSA_GROUP, NSA_HEAD_DIM = 16, 4, 4, 64
NSA_WIDTH = NSA_HEADS * NSA_HEAD_DIM
NSA_KV_WIDTH = NSA_KV_HEADS * NSA_HEAD_DIM
CMP_BLOCK, CMP_STRIDE, SEL_BLOCK, NSA_TOPK, WINDOW = 32, 16, 64, 16, 512
RET_HEADS, RET_DK, RET_DV = 8, 128, 128
PAGE_SIZE = 128
EPS = 1e-6
NEG_BIG = -1e30
POS_BIG = 1e30
MIN_F = 1e-20
LOWEST = -3.0e38

LANES = 128
VMEM_LIMIT_BYTES = 56 * 1024 * 1024

HG_SUB_SHIFT = 4
HG_SUB = 1 << HG_SUB_SHIFT
PAGES_PER_STEP = 8
SEL_TK = 256
PROMPT_SEL_TK = 512
PROMPT_TQ = 256
WIN_TK = 128


def _cparams(*sem):
    return pltpu.CompilerParams(dimension_semantics=sem, vmem_limit_bytes=VMEM_LIMIT_BYTES)


def _dot(a, b):
    return jnp.dot(a, b, preferred_element_type=F32)


def _dot_nt(a, b):
    return lax.dot_general(a, b, (((1,), (1,)), ((), ())), preferred_element_type=F32)


def _split(x, n):
    out = []
    r = x
    for _ in range(n):
        p = r.astype(MXU_DT)
        out.append(p)
        r = r - p.astype(F32)
    return out


def _sigmoid(x):
    return 1.0 / (1.0 + jnp.exp(-x))


def _silu(x):
    return x * _sigmoid(x)


def _iota(shape, dim):
    return lax.broadcasted_iota(jnp.int32, shape, dim)


def _eye(n):
    return (_iota((n, n), 0) == _iota((n, n), 1)).astype(MXU_DT)


def _transpose_exact(x):
    eye = _eye(x.shape[1])
    return sum(_dot_nt(eye, p) for p in _split(x, 3))


def _head_rms(x, bsum, gain):
    ssq = sum(_dot(p, bsum) for p in _split(x * x, 2))
    return x * lax.rsqrt(ssq * (1.0 / NSA_HEAD_DIM) + EPS) * gain


def _rms_matmul_kernel(x_ref, g_ref, w_ref, o_ref, xn_ref):
    @pl.when(pl.program_id(1) == 0)
    def _():
        x = x_ref[...]
        ms = jnp.mean(x * x, axis=-1, keepdims=True)
        xn_ref[...] = (x * lax.rsqrt(ms + EPS) * g_ref[...]).astype(xn_ref.dtype)

    o_ref[...] = _dot_nt(xn_ref[...], w_ref[...])


def rms_matmul(x, gain, wt, *, tm, tn):
    m, d = x.shape
    n = wt.shape[0]
    return pl.pallas_call(
        _rms_matmul_kernel,
        grid=(m // tm, n // tn),
        in_specs=[pl.BlockSpec((tm, d), lambda i, j: (i, 0)),
                  pl.BlockSpec((1, d), lambda i, j: (0, 0)),
                  pl.BlockSpec((tn, d), lambda i, j: (j, 0))],
        out_specs=pl.BlockSpec((tm, tn), lambda i, j: (i, j)),
        out_shape=jax.ShapeDtypeStruct((m, n), F32),
        scratch_shapes=[pltpu.VMEM((tm, d), MXU_DT)],
        compiler_params=_cparams("parallel", "arbitrary"),
        name="in_proj",
    )(x, gain.reshape(1, d), wt)


def _ffn_up_kernel(x_ref, g_ref, wg_ref, wu_ref, o_ref, xn_ref):
    @pl.when(pl.program_id(1) == 0)
    def _():
        x = x_ref[...]
        ms = jnp.mean(x * x, axis=-1, keepdims=True)
        xn_ref[...] = (x * lax.rsqrt(ms + EPS) * g_ref[...]).astype(xn_ref.dtype)

    xn = xn_ref[...]
    o_ref[...] = (_silu(_dot(xn, wg_ref[...])) * _dot(xn, wu_ref[...])).astype(o_ref.dtype)


def ffn_up(x, gain, wg, wu, *, tm, tn):
    m, d = x.shape
    f = wg.shape[1]
    return pl.pallas_call(
        _ffn_up_kernel,
        grid=(m // tm, f // tn),
        in_specs=[pl.BlockSpec((tm, d), lambda i, j: (i, 0)),
                  pl.BlockSpec((1, d), lambda i, j: (0, 0)),
                  pl.BlockSpec((d, tn), lambda i, j: (0, j)),
                  pl.BlockSpec((d, tn), lambda i, j: (0, j))],
        out_specs=pl.BlockSpec((tm, tn), lambda i, j: (i, j)),
        out_shape=jax.ShapeDtypeStruct((m, f), MXU_DT),
        scratch_shapes=[pltpu.VMEM((tm, d), MXU_DT)],
        compiler_params=_cparams("parallel", "arbitrary"),
        name="ffn_up",
    )(x, gain.reshape(1, d), wg, wu)


def _matmul_res_kernel(a_ref, w_ref, r_ref, o_ref):
    o_ref[...] = r_ref[...] + _dot(a_ref[...], w_ref[...])


def matmul_res(a, w, res, *, tm, tn):
    m, k = a.shape
    n = w.shape[1]
    return pl.pallas_call(
        _matmul_res_kernel,
        grid=(m // tm, n // tn),
        in_specs=[pl.BlockSpec((tm, k), lambda i, j: (i, 0)),
                  pl.BlockSpec((k, tn), lambda i, j: (0, j)),
                  pl.BlockSpec((tm, tn), lambda i, j: (i, j))],
        out_specs=pl.BlockSpec((tm, tn), lambda i, j: (i, j)),
        out_shape=jax.ShapeDtypeStruct((m, n), F32),
        compiler_params=_cparams("parallel", "arbitrary"),
        name="matmul_res",
    )(a, w, res)


def _merge_kernel(oa_ref, ob_ref, oc_ref, ga_ref, gb_ref, gc_ref, wa_ref, wb_ref, wc_ref, o_ref):
    o_ref[...] = (_sigmoid(ga_ref[...]) * _dot(oa_ref[...], wa_ref[...])
                  + _sigmoid(gb_ref[...]) * _dot(ob_ref[...], wb_ref[...])
                  + _sigmoid(gc_ref[...]) * _dot(oc_ref[...], wc_ref[...])).astype(o_ref.dtype)


def merge_branches(h, lay, oa, ob, oc, wa, wb, wc, *, tm, tn):
    m = h.shape[0]
    d = wa.shape[1]
    wdt = oa.shape[1]
    ca, cb, cc = ((lay["mg"] + i * d) // tn for i in range(3))
    row = lambda i, j: (i, 0)
    return pl.pallas_call(
        _merge_kernel,
        grid=(m // tm, d // tn),
        in_specs=[pl.BlockSpec((tm, wdt), row), pl.BlockSpec((tm, wdt), row), pl.BlockSpec((tm, wdt), row),
                  pl.BlockSpec((tm, tn), lambda i, j: (i, ca + j)),
                  pl.BlockSpec((tm, tn), lambda i, j: (i, cb + j)),
                  pl.BlockSpec((tm, tn), lambda i, j: (i, cc + j)),
                  pl.BlockSpec((wdt, tn), lambda i, j: (0, j)),
                  pl.BlockSpec((wdt, tn), lambda i, j: (0, j)),
                  pl.BlockSpec((wdt, tn), lambda i, j: (0, j))],
        out_specs=pl.BlockSpec((tm, tn), lambda i, j: (i, j)),
        out_shape=jax.ShapeDtypeStruct((m, d), MXU_DT),
        compiler_params=_cparams("parallel", "arbitrary"),
        name="merge",
    )(oa, ob, oc, h, h, h, wa, wb, wc)


def _hgrn_kernel(hq_ref, hf_ref, hi_ref, hg_ref, lb_ref, gain_ref, s0_ref, o_ref, sout_ref,
                 st_ref, ut_ref, *, tt, t_valid):
    c = HG_SUB
    nj = tt // c
    tb = pl.program_id(2)

    @pl.when(tb == 0)
    def _():
        st_ref[...] = _transpose_exact(s0_ref[0, 0])

    hq = hq_ref[0]
    lb = lb_ref[...]
    q = _silu(hq)
    sig = _sigmoid(hf_ref[0])
    g = jnp.log(jnp.maximum(lb + (1.0 - lb) * sig, MIN_F))
    kk = (1.0 - lb) * (1.0 - sig)
    v = hi_ref[0]
    row = _iota((tt, HG_DK), 0)
    if t_valid is not None:
        live = (tb * tt + row) < t_valid
        g = jnp.where(live, g, 0.0)
        kk = jnp.where(live, kk, 0.0)

    r2 = _iota((tt, tt), 0)
    c2 = _iota((tt, tt), 1)
    same = (r2 >> HG_SUB_SHIFT) == (c2 >> HG_SUB_SHIFT)
    gp = _split(g, 3)
    tri = (same & (c2 <= r2)).astype(MXU_DT)
    blk = same.astype(MXU_DT)
    b = sum(_dot(tri, p) for p in gp)
    dtot = sum(_dot(blk, p) for p in gp)
    qe = q * jnp.exp(b)
    ke = kk * jnp.exp(dtot - b)
    edec = jnp.exp(dtot)

    q3 = q.reshape(nj, c, HG_DK)
    kk3 = kk.reshape(nj, c, HG_DK)
    b3 = b.reshape(nj, c, HG_DK)
    v3 = v.reshape(nj, c, HG_DV)
    tpos = _iota((nj, c, HG_DK), 1)
    ones = jnp.ones((HG_DK, HG_DV), MXU_DT)
    o3 = jnp.zeros((nj, c, HG_DV), F32)
    for s in range(c):
        m = tpos >= s
        diff = jnp.where(m, b3 - b3[:, s:s + 1, :], 0.0)
        y = jnp.where(m, q3 * kk3[:, s:s + 1, :] * jnp.exp(diff), 0.0)
        z = _dot(y.reshape(tt, HG_DK).astype(MXU_DT), ones).reshape(nj, c, HG_DV)
        o3 = o3 + z * v3[:, s:s + 1, :]
    o = o3.reshape(tt, HG_DV)

    v_b = v.astype(MXU_DT)
    ke_b = ke.astype(MXU_DT)
    for j in range(nj):
        ut_ref[j] = lax.dot_general(v_b[j * c:(j + 1) * c], ke_b[j * c:(j + 1) * c],
                                    (((0,), (0,)), ((), ())), preferred_element_type=F32)

    qe_b = qe.astype(MXU_DT)
    outs = []
    for j in range(nj):
        st = st_ref[...]
        outs.append(_dot_nt(qe_b[j * c:(j + 1) * c], st.astype(MXU_DT)))
        st_ref[...] = st * edec[j * c:j * c + 1, :] + ut_ref[j]
    o = o + jnp.concatenate(outs, axis=0)

    on = o * lax.rsqrt(jnp.mean(o * o, axis=-1, keepdims=True) + EPS) * gain_ref[...]
    o_ref[0] = (on * _silu(hg_ref[0])).astype(o_ref.dtype)

    @pl.when(tb == pl.num_programs(2) - 1)
    def _():
        sout_ref[0, 0] = _transpose_exact(st_ref[...])


def hgrn_mixer(h3, lay, lb, out_gain, s0, *, tt, t_valid):
    bsz, t, _ = h3.shape
    cq, cf, ci, cg = (lay[k] // LANES for k in ("hq", "hf", "hi", "hg"))
    col = lambda c0: (lambda b, hd, tb: (b, tb, c0 + hd))
    kern = functools.partial(_hgrn_kernel, tt=tt, t_valid=t_valid)
    return pl.pallas_call(
        kern,
        grid=(bsz, HG_HEADS, t // tt),
        in_specs=[pl.BlockSpec((1, tt, LANES), col(cq)), pl.BlockSpec((1, tt, LANES), col(cf)),
                  pl.BlockSpec((1, tt, LANES), col(ci)), pl.BlockSpec((1, tt, LANES), col(cg)),
                  pl.BlockSpec((1, HG_DK), lambda b, hd, tb: (0, hd)),
                  pl.BlockSpec((1, HG_DV), lambda b, hd, tb: (0, 0)),
                  pl.BlockSpec((1, 1, HG_DK, HG_DV), lambda b, hd, tb: (b, hd, 0, 0))],
        out_specs=[pl.BlockSpec((1, tt, HG_DV), lambda b, hd, tb: (b, tb, hd)),
                   pl.BlockSpec((1, 1, HG_DK, HG_DV), lambda b, hd, tb: (b, hd, 0, 0))],
        out_shape=[jax.ShapeDtypeStruct((bsz, t, HG_HEADS * HG_DV), MXU_DT),
                   jax.ShapeDtypeStruct((bsz, HG_HEADS, HG_DK, HG_DV), F32)],
        scratch_shapes=[pltpu.VMEM((HG_DV, HG_DK), F32),
                        pltpu.VMEM((tt // HG_SUB, HG_DV, HG_DK), F32)],
        compiler_params=_cparams("parallel", "parallel", "arbitrary"),
        name="hgrn",
    )(h3, h3, h3, h3, lb, out_gain.reshape(1, HG_DV), s0)


def _ret_kernel(q_ref, k_ref, v_ref, g_ref, lg_ref, s0_ref, o_ref, sout_ref, s_ref, *, tt, n_valid):
    tb = pl.program_id(2)

    @pl.when(tb == 0)
    def _():
        s_ref[...] = s0_ref[0, 0]

    lgl = lg_ref[0][:, :LANES]
    lgt = lg_ref[0][:, :tt]
    q = q_ref[0] * (RET_DK ** -0.5)
    k = k_ref[0]
    vb = v_ref[0].astype(MXU_DT)
    rel = (_iota((tt, tt), 0) - _iota((tt, tt), 1)).astype(F32)
    dmat = jnp.where(rel >= 0, jnp.exp(jnp.maximum(rel, 0.0) * lgt), 0.0)
    att = _dot_nt(q.astype(MXU_DT), k.astype(MXU_DT)) * dmat
    q_dec = jnp.exp((_iota((tt, RET_DK), 0) + 1).astype(F32) * lgl)
    s = s_ref[...]
    o = _dot(att.astype(MXU_DT), vb) + _dot((q * q_dec).astype(MXU_DT), s.astype(MXU_DT))

    spos = _iota((RET_DK, tt), 1)
    k_dec = jnp.where(spos < n_valid, jnp.exp(jnp.maximum(n_valid - 1 - spos, 0).astype(F32) * lgt), 0.0)
    kt = _transpose_exact(k)
    s_ref[...] = jnp.exp(float(n_valid) * lgl) * s + _dot((kt * k_dec).astype(MXU_DT), vb)

    mu = jnp.mean(o, axis=-1, keepdims=True)
    var = jnp.mean(jnp.square(o - mu), axis=-1, keepdims=True)
    o_ref[0] = ((o - mu) * lax.rsqrt(var + EPS) * _silu(g_ref[0])).astype(o_ref.dtype)

    @pl.when(tb == pl.num_programs(2) - 1)
    def _():
        sout_ref[0, 0] = s_ref[...]


def retention_mixer(h3, lay, lg_tab, s0, *, tt, n_valid):
    bsz, t, _ = h3.shape
    assert n_valid == tt or t == tt
    cq, ck, cv, cg = (lay[k] // LANES for k in ("rq", "rk", "rv", "rg"))
    col = lambda c0: (lambda b, hd, tb: (b, tb, c0 + hd))
    kern = functools.partial(_ret_kernel, tt=tt, n_valid=n_valid)
    return pl.pallas_call(
        kern,
        grid=(bsz, RET_HEADS, t // tt),
        in_specs=[pl.BlockSpec((1, tt, LANES), col(cq)), pl.BlockSpec((1, tt, LANES), col(ck)),
                  pl.BlockSpec((1, tt, LANES), col(cv)), pl.BlockSpec((1, tt, LANES), col(cg)),
                  pl.BlockSpec((1, 1, lg_tab.shape[2]), lambda b, hd, tb: (hd, 0, 0)),
                  pl.BlockSpec((1, 1, RET_DK, RET_DV), lambda b, hd, tb: (b, hd, 0, 0))],
        out_specs=[pl.BlockSpec((1, tt, RET_DV), lambda b, hd, tb: (b, tb, hd)),
                   pl.BlockSpec((1, 1, RET_DK, RET_DV), lambda b, hd, tb: (b, hd, 0, 0))],
        out_shape=[jax.ShapeDtypeStruct((bsz, t, RET_HEADS * RET_DV), MXU_DT),
                   jax.ShapeDtypeStruct((bsz, RET_HEADS, RET_DK, RET_DV), F32)],
        scratch_shapes=[pltpu.VMEM((RET_DK, RET_DV), F32)],
        compiler_params=_cparams("parallel", "parallel", "arbitrary"),
        name="retention",
    )(h3, h3, h3, h3, lg_tab, s0)


def _store_heads(ref, x, lead=()):
    for hh in range(NSA_KV_HEADS):
        ref[lead + (hh,)] = x[:, hh * NSA_HEAD_DIM:(hh + 1) * NSA_HEAD_DIM].astype(ref.dtype)


def _q_prep_kernel(q_ref, gain_ref, bsum_ref, o_ref):
    x = q_ref[0]
    hd = NSA_HEAD_DIM
    for kvh in range(NSA_KV_HEADS):
        lo = kvh * NSA_KV_WIDTH
        qn = _head_rms(x[:, lo:lo + NSA_KV_WIDTH], bsum_ref[...], gain_ref[...]) * (hd ** -0.5)
        for g in range(NSA_GROUP):
            o_ref[0, kvh, g, :, 0:hd] = qn[:, g * hd:(g + 1) * hd].astype(o_ref.dtype)
            if o_ref.shape[-1] > hd:
                o_ref[0, kvh, g, :, hd:] = jnp.zeros((x.shape[0], o_ref.shape[-1] - hd), o_ref.dtype)


def q_prep(h3, lay, gain_q, bsum, *, tr, width=NSA_HEAD_DIM):
    bsz, t, _ = h3.shape
    cq = lay["nq"] // NSA_WIDTH
    return pl.pallas_call(
        _q_prep_kernel,
        grid=(bsz, t // tr),
        in_specs=[pl.BlockSpec((1, tr, NSA_WIDTH), lambda b, i: (b, i, cq)),
                  pl.BlockSpec((1, NSA_KV_WIDTH), lambda b, i: (0, 0)),
                  pl.BlockSpec((NSA_KV_WIDTH, NSA_KV_WIDTH), lambda b, i: (0, 0))],
        out_specs=pl.BlockSpec((1, NSA_KV_HEADS, NSA_GROUP, tr, width), lambda b, i: (b, 0, 0, i, 0)),
        out_shape=jax.ShapeDtypeStruct((bsz, NSA_KV_HEADS, NSA_GROUP, t, width), MXU_DT),
        compiler_params=_cparams("parallel", "parallel"),
        name="nsa_q_prep",
    )(h3, jnp.tile(gain_q, NSA_KV_HEADS).reshape(1, NSA_KV_WIDTH), bsum)


def _kv_prep_kernel(x_ref, gain_ref, bsum_ref, *rest):
    ko_ref, vo_ref = rest[-2:]
    hd = NSA_HEAD_DIM
    x = x_ref[0]
    kn = _head_rms(x[:, :NSA_KV_WIDTH], bsum_ref[...], gain_ref[...])
    for hh in range(NSA_KV_HEADS):
        ko_ref[0, hh, :, 0:hd] = kn[:, hh * hd:(hh + 1) * hd].astype(ko_ref.dtype)
        if len(rest) == 3:
            ko_ref[0, hh, :, hd:] = rest[0][:, hd:]
        lo = NSA_KV_WIDTH + hh * hd
        vo_ref[0, hh, :, 0:hd] = x[:, lo:lo + hd].astype(vo_ref.dtype)
        if vo_ref.shape[-1] > hd:
            vo_ref[0, hh, :, hd:] = jnp.ones((x.shape[0], vo_ref.shape[-1] - hd), vo_ref.dtype)


def kv_prep(x3, col, gain, bsum, *, tr, aug=None):
    bsz, t, _ = x3.shape
    w2 = 2 * NSA_KV_WIDTH
    hd = NSA_HEAD_DIM
    kw = hd if aug is None else aug.shape[1]
    vw = hd if aug is None else LANES
    vshape = jax.ShapeDtypeStruct((bsz, NSA_KV_HEADS, t, vw), MXU_DT)
    kshape = jax.ShapeDtypeStruct((bsz, NSA_KV_HEADS, t, kw), MXU_DT)
    spec = lambda w: pl.BlockSpec((1, NSA_KV_HEADS, tr, w), lambda b, i: (b, 0, i, 0))
    in_specs = [pl.BlockSpec((1, tr, w2), lambda b, i: (b, i, col)),
                pl.BlockSpec((1, NSA_KV_WIDTH), lambda b, i: (0, 0)),
                pl.BlockSpec((NSA_KV_WIDTH, NSA_KV_WIDTH), lambda b, i: (0, 0))]
    args = [x3, jnp.tile(gain, NSA_KV_HEADS).reshape(1, NSA_KV_WIDTH), bsum]
    if aug is not None:
        in_specs.append(pl.BlockSpec((tr, kw), lambda b, i: (i, 0)))
        args.append(aug)
    return pl.pallas_call(
        _kv_prep_kernel,
        grid=(bsz, t // tr),
        in_specs=in_specs,
        out_specs=[spec(kw), spec(vw)],
        out_shape=[kshape, vshape],
        compiler_params=_cparams("parallel", "parallel"),
        name="nsa_kv_prep",
    )(*args)


def _head_rms_t(xt, gain_t):
    x3 = xt.reshape(NSA_KV_HEADS, NSA_HEAD_DIM, xt.shape[1])
    ms = jnp.mean(x3 * x3, axis=1, keepdims=True)
    return (x3 * lax.rsqrt(ms + EPS)).reshape(xt.shape) * gain_t


def _kv_prep_t_kernel(*refs, n_in):
    refs = refs[-(n_in + 3):]
    gain_ref, ko_ref, vo_ref = refs[n_in:]
    eye = _eye(PAGE_SIZE)
    for u in range(n_in):
        kt = _head_rms_t(refs[u][0, 0], gain_ref[...]).astype(MXU_DT)
        vt = refs[u][0, 1].astype(MXU_DT)
        k = _dot_nt(eye, kt)
        v = _dot_nt(eye, vt)
        for hh in range(NSA_KV_HEADS):
            sl = slice(hh * NSA_HEAD_DIM, (hh + 1) * NSA_HEAD_DIM)
            ko_ref[0, hh, u * PAGE_SIZE:(u + 1) * PAGE_SIZE, :] = k[:, sl].astype(ko_ref.dtype)
            vo_ref[0, hh, u * PAGE_SIZE:(u + 1) * PAGE_SIZE, :] = v[:, sl].astype(vo_ref.dtype)


def _gain_t(gain):
    return jnp.broadcast_to(jnp.tile(gain, NSA_KV_HEADS)[:, None], (NSA_KV_WIDTH, LANES))


def kv_prep_paged(pool_t, pages, gain, *, extra_rows):
    bsz, n_pages = pages.shape
    pp = PAGES_PER_STEP
    kern = functools.partial(_kv_prep_t_kernel, n_in=pp)
    oshape = jax.ShapeDtypeStruct((bsz, NSA_KV_HEADS, n_pages * PAGE_SIZE + extra_rows, NSA_HEAD_DIM), MXU_DT)
    ospec = pl.BlockSpec((1, NSA_KV_HEADS, pp * PAGE_SIZE, NSA_HEAD_DIM), lambda b, i, pt: (b, 0, i, 0))
    page_spec = lambda u: pl.BlockSpec((1, 2, NSA_KV_WIDTH, PAGE_SIZE), lambda b, i, pt: (pt[b, i * pp + u], 1, 0, 0))
    return pl.pallas_call(
        kern,
        grid_spec=pltpu.PrefetchScalarGridSpec(
            num_scalar_prefetch=1,
            grid=(bsz, n_pages // pp),
            in_specs=[page_spec(u) for u in range(pp)]
                     + [pl.BlockSpec((NSA_KV_WIDTH, LANES), lambda b, i, pt: (0, 0))],
            out_specs=[ospec, ospec]),
        out_shape=[oshape, oshape],
        compiler_params=_cparams("parallel", "arbitrary"),
        name="nsa_kv_prep_paged",
    )(pages, *([pool_t] * pp), _gain_t(gain))


def _cmp_prep_kernel(*refs, n_in, rows):
    refs = refs[-(n_in + 7):]
    nxt_ref, tail_ref, w_ref, gain_ref, bsum_ref, kc_ref, vc_ref = refs[n_in:]
    cs = CMP_STRIDE
    nb = rows // cs
    w0 = w_ref[0]
    w1 = w_ref[1]
    a0 = []
    a1 = []
    for u in range(n_in):
        x3 = refs[u][0].reshape(nb, cs, 2 * NSA_KV_WIDTH)
        a0.append(jnp.sum(x3 * w0[None], axis=1))
        a1.append(jnp.sum(x3 * w1[None], axis=1))
    a0 = jnp.concatenate(a0, axis=0) if n_in > 1 else a0[0]
    a1 = jnp.concatenate(a1, axis=0) if n_in > 1 else a1[0]
    last = pl.program_id(1) == pl.num_programs(1) - 1
    nx = jnp.where(last, tail_ref[0], nxt_ref[0])
    a1_next = jnp.sum(nx * w1, axis=0, keepdims=True)
    tot = nb * n_in
    a1s = pltpu.roll(a1, tot - 1, 0)
    a1s = jnp.where(_iota(a1s.shape, 0) == tot - 1, a1_next, a1s)
    comp = a0 + a1s
    kc = _head_rms(comp[:, :NSA_KV_WIDTH], bsum_ref[...], gain_ref[...])
    _store_heads(kc_ref, kc, (0,))
    _store_heads(vc_ref, comp[:, NSA_KV_WIDTH:], (0,))


def _cmp_weight_table(cmp_w):
    w = cmp_w.reshape(2, CMP_BLOCK // CMP_STRIDE, CMP_STRIDE)
    w = jnp.transpose(w, (1, 2, 0))
    return jnp.repeat(w, NSA_KV_WIDTH, axis=2)


def cmp_prep(x3, col, tail16, cmp_w, gain, bsum, *, tr):
    bsz, t, _ = x3.shape
    w2 = 2 * NSA_KV_WIDTH
    nbt = tr // CMP_STRIDE
    n_steps = t // tr
    kern = functools.partial(_cmp_prep_kernel, n_in=1, rows=tr)
    oshape = jax.ShapeDtypeStruct((bsz, NSA_KV_HEADS, t // CMP_STRIDE, NSA_HEAD_DIM), MXU_DT)
    ospec = pl.BlockSpec((1, NSA_KV_HEADS, nbt, NSA_HEAD_DIM), lambda b, i: (b, 0, i, 0))
    chunks_per_step = tr // CMP_STRIDE
    return pl.pallas_call(
        kern,
        grid=(bsz, n_steps),
        in_specs=[pl.BlockSpec((1, tr, w2), lambda b, i: (b, i, col)),
                  pl.BlockSpec((1, CMP_STRIDE, w2),
                               lambda b, i: (b, jnp.minimum(i + 1, n_steps - 1) * chunks_per_step, col)),
                  pl.BlockSpec((1, CMP_STRIDE, w2), lambda b, i: (b, 0, 0)),
                  pl.BlockSpec((2, CMP_STRIDE, w2), lambda b, i: (0, 0, 0)),
                  pl.BlockSpec((1, NSA_KV_WIDTH), lambda b, i: (0, 0)),
                  pl.BlockSpec((NSA_KV_WIDTH, NSA_KV_WIDTH), lambda b, i: (0, 0))],
        out_specs=[ospec, ospec],
        out_shape=[oshape, oshape],
        compiler_params=_cparams("parallel", "arbitrary"),
        name="nsa_cmp_prep",
    )(x3, x3, tail16, _cmp_weight_table(cmp_w), jnp.tile(gain, NSA_KV_HEADS).reshape(1, NSA_KV_WIDTH), bsum)


def _cmp_prep_t_kernel(*refs, n_in):
    refs = refs[-(n_in + 9):]
    nxt_ref, tail_ref, wk_ref, wv_ref, wnk_ref, wnv_ref, gain_ref, kc_ref, vc_ref = refs[n_in:]
    nb = wk_ref.shape[2]

    def wdot(xt, w):
        xh = xt.astype(MXU_DT)
        wh, wl = _split(w, 2)
        return _dot(xh, wh) + _dot(xh, wl)

    last = pl.program_id(1) == pl.num_programs(1) - 1
    comp_k = wdot(jnp.where(last, tail_ref[0, 0], nxt_ref[0, 0]), wnk_ref[...])
    comp_v = wdot(jnp.where(last, tail_ref[0, 1], nxt_ref[0, 1]), wnv_ref[...])
    for u in range(n_in):
        comp_k = comp_k + wdot(refs[u][0, 0], wk_ref[u])
        comp_v = comp_v + wdot(refs[u][0, 1], wv_ref[u])
    kct = _head_rms_t(comp_k, gain_ref[...][:, :nb]).astype(MXU_DT)
    eye = _eye(nb)
    _store_heads(kc_ref, _dot_nt(eye, kct), (0,))
    _store_heads(vc_ref, _dot_nt(eye, comp_v.astype(MXU_DT)), (0,))


def _cmp_band_tables(cmp_w, n_pages_step):
    cs = CMP_STRIDE
    w = cmp_w.reshape(2, CMP_BLOCK // cs, cs)
    nb = n_pages_step * PAGE_SIZE // cs
    rho = jnp.arange(PAGE_SIZE)
    ch = (jnp.arange(n_pages_step)[:, None] * (PAGE_SIZE // cs) + rho[None, :] // cs)[..., None]
    n = jnp.arange(nb)[None, None, :]
    s = rho % cs
    tabs = []
    for c in range(2):
        full = (w[c, 0][s][None, :, None] * (ch == n) + w[c, 1][s][None, :, None] * (ch == n + 1)).astype(F32)
        nxt = jnp.where((rho[:, None] < cs) & (n[0] == nb - 1), w[c, 1][s][:, None], 0.0).astype(F32)
        tabs.append((full, nxt))
    return tabs[0][0], tabs[1][0], tabs[0][1], tabs[1][1]


def cmp_prep_paged(pool_t, pages, tail_t, cmp_w, gain):
    bsz, n_pages = pages.shape
    pp = PAGES_PER_STEP
    n_steps = n_pages // pp
    nbt = pp * PAGE_SIZE // CMP_STRIDE
    kern = functools.partial(_cmp_prep_t_kernel, n_in=pp)
    oshape = jax.ShapeDtypeStruct((bsz, NSA_KV_HEADS, n_pages * PAGE_SIZE // CMP_STRIDE, NSA_HEAD_DIM), MXU_DT)
    ospec = pl.BlockSpec((1, NSA_KV_HEADS, nbt, NSA_HEAD_DIM), lambda b, i, pt: (b, 0, i, 0))
    pblock = (1, 2, NSA_KV_WIDTH, PAGE_SIZE)
    page_spec = lambda u: pl.BlockSpec(pblock, lambda b, i, pt: (pt[b, i * pp + u], 0, 0, 0))
    const = lambda *shape: pl.BlockSpec(shape, lambda b, i, pt: (0,) * len(shape))
    wk, wv, wnk, wnv = _cmp_band_tables(cmp_w, pp)
    return pl.pallas_call(
        kern,
        grid_spec=pltpu.PrefetchScalarGridSpec(
            num_scalar_prefetch=1,
            grid=(bsz, n_steps),
            in_specs=[page_spec(u) for u in range(pp)]
                     + [pl.BlockSpec(pblock, lambda b, i, pt: (pt[b, jnp.minimum(i + 1, n_steps - 1) * pp], 0, 0, 0)),
                        pl.BlockSpec(pblock, lambda b, i, pt: (b, 0, 0, 0)),
                        const(pp, PAGE_SIZE, nbt), const(pp, PAGE_SIZE, nbt), const(PAGE_SIZE, nbt),
                        const(PAGE_SIZE, nbt), const(NSA_KV_WIDTH, LANES)],
            out_specs=[ospec, ospec]),
        out_shape=[oshape, oshape],
        compiler_params=_cparams("parallel", "arbitrary"),
        name="nsa_cmp_prep_paged",
    )(pages, *([pool_t] * pp), pool_t, tail_t, wk, wv, wnk, wnv, _gain_t(gain))


def _gate_and_store(o_ref, br_ref, nbg_ref, gx_ref, branches, tq):
    hd = NSA_HEAD_DIM
    for br, o in enumerate(branches):
        for g in range(NSA_GROUP):
            br_ref[br, :, g * hd:(g + 1) * hd] = o[g * tq:(g + 1) * tq]
    pieces = _split(_sigmoid(nbg_ref[0]), 2)
    acc = None
    for br in range(3):
        term = sum(_dot(p, gx_ref[br]) for p in pieces) * br_ref[br]
        acc = term if acc is None else acc + term
    o_ref[0] = acc.astype(o_ref.dtype)


def _nsa_kernel(q_ref, sl_ref, kc_ref, vc_ref, ks_ref, vs_ref, kw_ref, vw_ref, at_ref, nbg_ref, gx_ref,
                o_ref, sc_ref, m_ref, l_ref, acc_ref, br_ref,
                *, tq, sel_tk, qpos0, wpos0, n_sel, topk):
    rows = NSA_GROUP * tq
    hd = NSA_HEAD_DIM
    t0 = qpos0 + pl.program_id(2) * tq
    q = q_ref[0, 0].reshape(rows, hd)
    slope = sl_ref[0][:, :1]
    nbp = kc_ref.shape[2]
    n_sel_pad, tl = sc_ref.shape

    def tok(shape):
        return t0 + (_iota(shape, 0) & (tq - 1))

    s = _dot_nt(q, kc_ref[0, 0])
    dist = tok((rows, nbp)) - (_iota((rows, nbp), 1) * CMP_STRIDE + (CMP_BLOCK - 1))
    valid = dist >= 0
    s = jnp.where(valid, s - slope * dist.astype(F32), NEG_BIG)
    p = jnp.where(valid, jnp.exp(s - jnp.max(s, axis=-1, keepdims=True)), 0.0)
    p = p / jnp.maximum(jnp.sum(p, axis=-1, keepdims=True), 1e-30)
    o_cmp = _dot(p.astype(MXU_DT), vc_ref[0, 0])

    sel = _topk_block_mask(p, at_ref, t0, tq=tq, tl=tl, n_sel=n_sel, topk=topk)
    spread = ((_iota((rows, tl), 0) & (tq - 1)) == _iota((rows, tl), 1)).astype(MXU_DT)
    nblk = sel_tk // SEL_BLOCK
    if nblk % 8 == 0:
        sc_ref[...] = sel
        in_tile = (_iota((nblk, sel_tk), 0) == (_iota((nblk, sel_tk), 1) >> 6)).astype(MXU_DT)
    else:
        selb = _dot_nt(spread, sel.astype(MXU_DT)).astype(MXU_DT)

    def softmax_pass(k_ref, v_ref, tk, lo, hi, kpos0, mask_fn):
        m_ref[...] = jnp.full(m_ref.shape, NEG_BIG, F32)
        l_ref[...] = jnp.zeros(l_ref.shape, F32)
        acc_ref[...] = jnp.zeros(acc_ref.shape, F32)
        rel = (_iota((rows, tk), 0) & (tq - 1)) - _iota((rows, tk), 1)
        reps = tk // LANES

        def body(kt, carry):
            k0 = pl.multiple_of(kt * tk, tk)
            s = _dot_nt(q, k_ref[0, 0, pl.ds(k0, tk), :])
            d = rel + (t0 - kpos0 - k0)
            valid = mask_fn(d, k0)
            s = jnp.where(valid, s - slope * d.astype(F32), NEG_BIG)
            m_prev = m_ref[...]
            m_new = jnp.maximum(m_prev, jnp.max(s, axis=-1, keepdims=True))
            alpha = jnp.exp(m_prev - m_new)
            mrep = m_new if reps == 1 else jnp.concatenate([m_new] * reps, axis=1)
            p = jnp.where(valid, jnp.exp(s - mrep), 0.0)
            l_ref[...] = alpha * l_ref[...] + jnp.sum(p, axis=-1, keepdims=True)
            acc_ref[...] = alpha[:, :hd] * acc_ref[...] + _dot(p.astype(MXU_DT), v_ref[0, 0, pl.ds(k0, tk), :])
            m_ref[...] = m_new
            return carry

        lax.fori_loop(lo, hi, body, 0)
        return acc_ref[...] / jnp.maximum(l_ref[...][:, :hd], 1e-30)

    n_kt_all = ks_ref.shape[2] // sel_tk
    hi_sel = jnp.minimum(n_kt_all, (t0 + tq - 1) // sel_tk + 1)

    def sel_mask(d, k0):
        if nblk % 8 == 0:
            blk0 = pl.multiple_of(k0 // SEL_BLOCK, nblk)
            mine = _dot_nt(spread, sc_ref[pl.ds(blk0, nblk), :].astype(MXU_DT)).astype(MXU_DT)
            return (_dot(mine, in_tile) > 0.5) & (d >= 0)
        blk_of_key = (k0 + _iota((n_sel_pad, sel_tk), 1)) >> 6
        expand = (_iota((n_sel_pad, sel_tk), 0) == blk_of_key).astype(MXU_DT)
        return (_dot(selb, expand) > 0.5) & (d >= 0)

    o_sel = softmax_pass(ks_ref, vs_ref, sel_tk, 0, hi_sel, 0, sel_mask)

    n_wt_all = kw_ref.shape[2] // WIN_TK
    lo_win = jnp.maximum(t0 - (WINDOW - 1) - wpos0, 0) // WIN_TK
    hi_win = jnp.minimum(n_wt_all, (t0 + tq - 1 - wpos0) // WIN_TK + 1)
    o_win = softmax_pass(kw_ref, vw_ref, WIN_TK, lo_win, hi_win, wpos0,
                         lambda d, k0: (d >= 0) & (d < WINDOW))

    _gate_and_store(o_ref, br_ref, nbg_ref, gx_ref, (o_cmp, o_sel, o_win), tq)


def _imp_to_sel_matrix(n_sel_pad, nbp):
    m = SEL_BLOCK // CMP_STRIDE
    r = CMP_BLOCK // CMP_STRIDE
    jj = jnp.arange(n_sel_pad)[:, None]
    ii = jnp.arange(nbp)[None, :]
    cnt = sum(((ii - rr) >= m * jj) & ((ii - rr) < m * (jj + 1)) for rr in range(r))
    return cnt.astype(MXU_DT)


def nsa_attention(qn, slopes, kc, vc, ks, vs, kw, vw, h3, nbg_col, gate_expand, *, tq, sel_tk, qpos0, wpos0, n_sel):
    bsz, _, _, tql, hd = qn.shape
    nbp, tk_all, tw_all = kc.shape[2], ks.shape[2], kw.shape[2]
    assert tq & (tq - 1) == 0 and tk_all % sel_tk == 0 and tw_all % WIN_TK == 0
    n_sel_pad = -(-max(n_sel, tk_all // SEL_BLOCK) // 64) * 64
    tl = max(tq, LANES)
    rows = NSA_GROUP * tq
    at = _imp_to_sel_matrix(n_sel_pad, nbp)
    kern = functools.partial(_nsa_kernel, tq=tq, sel_tk=sel_tk, qpos0=qpos0, wpos0=wpos0, n_sel=n_sel,
                             topk=min(NSA_TOPK, n_sel))
    full = lambda n: pl.BlockSpec((1, 1, n, hd), lambda b, kh, i: (b, kh, 0, 0))
    gw = NSA_GROUP * hd
    return pl.pallas_call(
        kern,
        grid=(bsz, NSA_KV_HEADS, tql // tq),
        in_specs=[pl.BlockSpec((1, 1, NSA_GROUP, tq, hd), lambda b, kh, i: (b, kh, 0, i, 0)),
                  pl.BlockSpec((1, rows, LANES), lambda b, kh, i: (kh, 0, 0)),
                  full(nbp), full(nbp), full(tk_all), full(tk_all), full(tw_all), full(tw_all),
                  pl.BlockSpec((n_sel_pad, nbp), lambda b, kh, i: (0, 0)),
                  pl.BlockSpec((1, tq, LANES), lambda b, kh, i: (b, i, nbg_col)),
                  pl.BlockSpec((3, LANES, gw), lambda b, kh, i: (0, 0, kh))],
        out_specs=pl.BlockSpec((1, tq, gw), lambda b, kh, i: (b, i, kh)),
        out_shape=jax.ShapeDtypeStruct((bsz, tql, NSA_WIDTH), MXU_DT),
        scratch_shapes=[pltpu.VMEM((n_sel_pad, tl), F32),
                        pltpu.VMEM((rows, LANES), F32), pltpu.VMEM((rows, LANES), F32),
                        pltpu.VMEM((rows, hd), F32), pltpu.VMEM((3, tq, gw), F32)],
        compiler_params=_cparams("parallel", "parallel", "arbitrary"),
        name="nsa_attention",
    )(qn, slopes, kc, vc, ks, vs, kw, vw, at, h3, gate_expand)


def _topk_block_mask(p, at_ref, t0, *, tq, tl, n_sel, topk):
    n_sel_pad, nbp = at_ref.shape
    imp = p[0:tq] + p[tq:2 * tq] + p[2 * tq:3 * tq] + p[3 * tq:4 * tq]
    if tl > tq:
        imp = jnp.concatenate([imp, jnp.zeros((tl - tq, nbp), F32)], axis=0)
    imp_sel = sum(_dot_nt(at_ref[...], piece) for piece in _split(imp, 3))
    j = _iota((n_sel_pad, tl), 0)
    tt = t0 + _iota((n_sel_pad, tl), 1)
    forced = (j == 0) | (j == (tt >> 6))
    allowed = (j << 6) <= tt
    score = jnp.where(forced, POS_BIG, jnp.where(allowed, imp_sel, NEG_BIG))
    score = jnp.where(j >= n_sel, LOWEST, score)
    jf = j.astype(F32)
    sel = jnp.zeros((n_sel_pad, tl), F32)
    for _ in range(topk):
        mx = jnp.max(score, axis=0, keepdims=True)
        first = jnp.min(jnp.where(score == mx, jf, 1e9), axis=0, keepdims=True)
        pick = jf == first
        sel = jnp.where(pick, 1.0, sel)
        score = jnp.where(pick, -jnp.inf, score)
    return jnp.where(allowed, sel, 0.0)


def _nsa_prompt_kernel(q_ref, sl_ref, qts_ref, qtw_ref, kc_ref, vc_ref, ks_ref, vs_ref, kw_ref, vw_ref, at_ref,
                       nbg_ref, gx_ref, o_ref, m_ref, acc_ref, br_ref, *, tq, n_sel, topk):
    rows = NSA_GROUP * tq
    hd = NSA_HEAD_DIM
    qi = pl.program_id(2)
    t0 = qi * tq
    q128 = q_ref[0, 0].reshape(rows, LANES)
    slope = sl_ref[0][:, :1]
    nbp = kc_ref.shape[2]

    s = _dot_nt(q128[:, :hd], kc_ref[0, 0])
    tok = t0 + (_iota((rows, nbp), 0) & (tq - 1))
    dist = tok - (_iota((rows, nbp), 1) * CMP_STRIDE + (CMP_BLOCK - 1))
    valid = dist >= 0
    s = jnp.where(valid, s - slope * dist.astype(F32), NEG_BIG)
    p = jnp.where(valid, jnp.exp(s - jnp.max(s, axis=-1, keepdims=True)), 0.0)
    p = p / jnp.maximum(jnp.sum(p, axis=-1, keepdims=True), 1e-30)
    o_cmp = _dot(p.astype(MXU_DT), vc_ref[0, 0])

    def normalised(acc):
        return acc[:, :hd] / jnp.maximum(acc[:, hd:], 1e-30)

    lane = _iota((rows, LANES), 1)
    qw = jnp.where(lane < hd, q128, qtw_ref[0])
    span = WINDOW + tq
    ws = pl.multiple_of(jnp.maximum(t0 - WINDOW, 0), tq)
    sw = _dot_nt(qw, kw_ref[0, 0, pl.ds(ws, span), :])
    d = (_iota((rows, span), 0) & (tq - 1)) - _iota((rows, span), 1) + (t0 - ws)
    sw = jnp.where((d >= 0) & (d < WINDOW), sw, NEG_BIG)
    pw = jnp.exp(sw - jnp.max(sw, axis=-1, keepdims=True))
    o_win = normalised(_dot(pw.astype(MXU_DT), vw_ref[0, 0, pl.ds(ws, span), :]))

    sel = _topk_block_mask(p, at_ref, t0, tq=tq, tl=tq, n_sel=n_sel, topk=topk)
    sel_t = jnp.concatenate([jnp.zeros_like(sel), sel], axis=0).astype(MXU_DT)
    spread = ((_iota((rows, tq), 0) & (tq - 1)) == _iota((rows, tq), 1)).astype(MXU_DT)
    selb = _dot_nt(spread, sel_t)
    neg = jnp.where((lane >= hd) & (selb < 0.5), NEG_BIG, 0.0)
    qa = jnp.concatenate([(q128.astype(F32) + neg).astype(MXU_DT), qts_ref[0]], axis=1)

    tk = PROMPT_SEL_TK
    kd = t0 // tk
    k0 = pl.multiple_of(kd * tk, tk)
    s = _dot_nt(qa, ks_ref[0, 0, pl.ds(k0, tk), :])
    causal = ((_iota((rows, tk), 0) & (tq - 1)) - _iota((rows, tk), 1) + (t0 - k0)) >= 0
    s = jnp.where(causal, s, NEG_BIG)
    m = jnp.max(s, axis=-1, keepdims=True)
    m_ref[...] = jnp.broadcast_to(m, m_ref.shape)
    acc_ref[...] = _dot(jnp.exp(s - m).astype(MXU_DT), vs_ref[0, 0, pl.ds(k0, tk), :])

    def sel_body(kt, carry):
        ka = pl.multiple_of(kt * tk, tk)
        s = _dot_nt(qa, ks_ref[0, 0, pl.ds(ka, tk), :])
        m_prev = m_ref[...]
        m_new = jnp.maximum(m_prev, jnp.max(s, axis=-1, keepdims=True))
        p = jnp.exp(s - jnp.concatenate([m_new] * (tk // LANES), axis=1))
        acc_ref[...] = jnp.exp(m_prev - m_new) * acc_ref[...] + _dot(p.astype(MXU_DT), vs_ref[0, 0, pl.ds(ka, tk), :])
        m_ref[...] = m_new
        return carry

    lax.fori_loop(0, kd, sel_body, 0)
    o_sel = normalised(acc_ref[...])

    _gate_and_store(o_ref, br_ref, nbg_ref, gx_ref, (o_cmp, o_sel, o_win), tq)


def _slope_digits(tq, tile, lane0):
    hh = jnp.arange(1, NSA_HEADS + 1, dtype=F32)
    slopes = jnp.exp2(-8.0 * hh / NSA_HEADS)
    pieces = jnp.stack(_split(slopes, 3), axis=1).astype(F32)
    six = jnp.concatenate([pieces * float(tile), pieces], axis=1)
    tab = jnp.zeros((NSA_HEADS, LANES), F32).at[:, lane0:lane0 + 6].set(six)
    tab = tab.reshape(NSA_KV_HEADS, NSA_GROUP, 1, LANES)
    return jnp.broadcast_to(tab, (NSA_KV_HEADS, NSA_GROUP, tq, LANES)).reshape(
        NSA_KV_HEADS, NSA_GROUP * tq, LANES).astype(MXU_DT)


def _key_digits(t, tile, lane0, width, onehot):
    r = jnp.arange(t)
    lane = jnp.arange(width)[None, :]
    tab = jnp.zeros((t, width), F32)
    if onehot:
        tab = jnp.where(lane - NSA_HEAD_DIM == (r // SEL_BLOCK)[:, None], 1.0, tab)
    hi = (r // tile).astype(F32)[:, None]
    lo = (r % tile).astype(F32)[:, None]
    tab = jnp.where((lane >= lane0) & (lane < lane0 + 3), hi, tab)
    tab = jnp.where((lane >= lane0 + 3) & (lane < lane0 + 6), lo, tab)
    return tab.astype(MXU_DT)


def nsa_attention_prompt(q128, slopes, kc, vc, ks_aug, vs, kw_aug, vw, h3, nbg_col, gate_expand, *, n_sel):
    bsz, _, _, t, _ = q128.shape
    tq = PROMPT_TQ
    hd = NSA_HEAD_DIM
    nbp = kc.shape[2]
    assert n_sel <= 64 and t % PROMPT_SEL_TK == 0 and t >= WINDOW + tq and PROMPT_SEL_TK % tq == 0
    rows = NSA_GROUP * tq
    at = _imp_to_sel_matrix(64, nbp)
    kern = functools.partial(_nsa_prompt_kernel, tq=tq, n_sel=n_sel, topk=min(NSA_TOPK, n_sel))
    full = lambda n, w: pl.BlockSpec((1, 1, n, w), lambda b, kh, i: (b, kh, 0, 0))
    per_head = pl.BlockSpec((1, rows, LANES), lambda b, kh, i: (kh, 0, 0))
    gw = NSA_GROUP * hd
    return pl.pallas_call(
        kern,
        grid=(bsz, NSA_KV_HEADS, t // tq),
        in_specs=[pl.BlockSpec((1, 1, NSA_GROUP, tq, LANES), lambda b, kh, i: (b, kh, 0, i, 0)),
                  per_head, per_head, per_head,
                  full(nbp, hd), full(nbp, hd), full(t, 2 * LANES), full(t, LANES), full(t, LANES), full(t, LANES),
                  pl.BlockSpec((64, nbp), lambda b, kh, i: (0, 0)),
                  pl.BlockSpec((1, tq, LANES), lambda b, kh, i: (b, i, nbg_col)),
                  pl.BlockSpec((3, LANES, gw), lambda b, kh, i: (0, 0, kh))],
        out_specs=pl.BlockSpec((1, tq, gw), lambda b, kh, i: (b, i, kh)),
        out_shape=jax.ShapeDtypeStruct((bsz, t, NSA_WIDTH), MXU_DT),
        scratch_shapes=[pltpu.VMEM((rows, LANES), F32), pltpu.VMEM((rows, LANES), F32),
                        pltpu.VMEM((3, tq, gw), F32)],
        compiler_params=_cparams("parallel", "parallel", "arbitrary"),
        name="nsa_attention_prompt",
    )(q128, slopes, _slope_digits(tq, SEL_TK, 0), _slope_digits(tq, WIN_TK, hd),
      kc, vc, ks_aug, vs, kw_aug, vw, at, h3, gate_expand)


def _layout(d_model):
    hw, rw = HG_HEADS * HG_DK, RET_HEADS * RET_DK
    names = [("hq", hw), ("hf", hw), ("hi", hw), ("hg", hw), ("nq", NSA_WIDTH), ("nkv", 4 * NSA_KV_WIDTH),
             ("nwkv", 2 * NSA_KV_WIDTH), ("rq", rw), ("rk", rw), ("rv", rw), ("rg", rw), ("mg", 3 * d_model),
             ("nbg", 3 * NSA_HEADS)]
    lay, off = {}, 0
    for name, width in names:
        lay[name] = off
        off += width
    lay["used"] = off
    lay["total"] = -(-off // 1024) * 1024
    return lay


def _reorder_w_in(w_in_t, lay):
    a = lay["rq"]
    nb = 3 * NSA_HEADS
    d = w_in_t.shape[1]
    pad = jnp.zeros((lay["total"] - lay["used"], d), w_in_t.dtype)
    return jnp.concatenate([w_in_t[:a], w_in_t[a + nb:], w_in_t[a:a + nb], pad], axis=0).astype(MXU_DT)


def _block_sum_matrix(width):
    i = jnp.arange(width)
    return (i[:, None] // NSA_HEAD_DIM == i[None, :] // NSA_HEAD_DIM).astype(MXU_DT)


def _gate_expand_matrix():
    c = jnp.arange(LANES)[:, None]
    col = jnp.arange(NSA_WIDTH)[None, :]
    return jnp.stack([(c == (col // NSA_HEAD_DIM) * 3 + br) for br in range(3)]).astype(MXU_DT)


def _slope_rows(tq):
    hh = jnp.arange(1, NSA_HEADS + 1, dtype=F32)
    slopes = jnp.exp2(-8.0 * hh / NSA_HEADS).reshape(NSA_KV_HEADS, NSA_GROUP)
    rows = jnp.repeat(slopes, tq, axis=1)
    return jnp.broadcast_to(rows[:, :, None], (NSA_KV_HEADS, NSA_GROUP * tq, LANES))


def _pad_rows(x, n):
    return jnp.pad(x, ((0, 0), (0, n - x.shape[1])) + ((0, 0),) * (x.ndim - 2))


def _trunk_layer(x, lw, lay, consts, nsa_fn, hg_state, ret_state, *, seq_tiles):
    bsz, t, d = x.shape
    tm, tt_h, tt_r, t_valid = seq_tiles
    x2 = x.reshape(bsz * t, d)
    h = rms_matmul(x2, lw["norm_attn"], lw["w_in"], tm=tm, tn=1024)
    np_ = h.shape[1]
    h3 = h.reshape(bsz, t, np_)
    tp = -(-t // tt_h) * tt_h
    h3p = _pad_rows(h3, tp) if tp != t else h3
    oa, hg_new = hgrn_mixer(h3p, lay, lw["lb"], lw["hg_out_norm"], hg_state, tt=tt_h,
                            t_valid=None if tp == t else t_valid)
    tpr = -(-t // tt_r) * tt_r
    h3r = _pad_rows(h3, tpr) if tpr != t else h3
    oc, ret_new = retention_mixer(h3r, lay, consts["lg_tab"], ret_state, tt=tt_r,
                                  n_valid=tt_r if tpr == t else t_valid)
    ob = nsa_fn(h3)
    flat = lambda a: a[:, :t].reshape(bsz * t, a.shape[-1])
    merged = merge_branches(h, lay, flat(oa), flat(ob), flat(oc),
                            lw["w_branch_hg"], lw["w_branch_nsa"], lw["w_branch_ret"], tm=tm, tn=512)
    x2 = matmul_res(merged, lw["w_out"], x2, tm=tm, tn=512)
    hid = ffn_up(x2, lw["norm_ffn"], lw["w_gate"], lw["w_up"], tm=tm, tn=512)
    x2 = matmul_res(hid, lw["w_down"], x2, tm=tm, tn=512)
    return x2.reshape(bsz, t, d), h3, hg_new, ret_new


def kernel(x_prompt, x_sample, cache_nsa, cache_win, state_hgrn, state_ret, page_table,
           norm_attn, w_in, hgrn_lb_logits, hgrn_out_norm, nsa_q_norm, nsa_k_norm, nsa_cmp_w,
           w_branch_hg, w_branch_nsa, w_branch_ret, w_out, norm_ffn, w_gate, w_up, w_down):
    depth = w_in.shape[0]
    bp, tp, d = x_prompt.shape
    bs, ts, _ = x_sample.shape
    n_pool = cache_nsa.shape[1]
    n_pages = page_table.shape[1]
    past = n_pages * PAGE_SIZE
    wbuf = cache_win.shape[2]
    lay = _layout(d)

    sm = jax.nn.softmax(hgrn_lb_logits.astype(F32), axis=0)
    lower_bounds = jnp.clip(jnp.cumsum(sm, axis=0) - sm[0:1], 0.0, 1.0 - 1e-6)

    tt_p = 256 if tp % 256 == 0 else tp
    lg = jnp.log1p(-jnp.exp2(-5.0 - jnp.arange(RET_HEADS, dtype=F32)))
    consts = {
        "gate_expand": _gate_expand_matrix(),
        "lg_tab": jnp.broadcast_to(lg[:, None, None], (RET_HEADS, 1, max(tt_p, LANES))),
    }
    bsum = _block_sum_matrix(NSA_KV_WIDTH)
    pool_t = jnp.transpose(cache_nsa, (0, 1, 3, 4, 5, 2)).reshape(depth * n_pool, 4, NSA_KV_WIDTH, PAGE_SIZE)
    win3 = cache_win.reshape(depth, bs, wbuf, 2 * NSA_KV_WIDTH)

    ts_pad = 16
    tq_s = 16
    n_sel_p = -(-tp // SEL_BLOCK)
    n_sel_s = past // SEL_BLOCK + -(-ts // SEL_BLOCK)
    tk_tail = 2048 if past % 2048 == 0 else SEL_TK
    tq_p = 128 if tp % 128 == 0 else tp

    xp, xs = x_prompt, x_sample
    outs = {k: [] for k in ("kv_p", "kv_s", "win_p", "win_s", "hg_p", "hg_s", "ret_p", "ret_s")}
    for l in range(depth):
        lw = {
            "norm_attn": norm_attn[l], "w_in": _reorder_w_in(jnp.swapaxes(w_in[l], 0, 1), lay),
            "lb": lower_bounds[l].reshape(1, -1),
            "hg_out_norm": hgrn_out_norm[l],
            "w_branch_hg": w_branch_hg[l].astype(MXU_DT), "w_branch_nsa": w_branch_nsa[l].astype(MXU_DT),
            "w_branch_ret": w_branch_ret[l].astype(MXU_DT), "w_out": w_out[l].astype(MXU_DT),
            "norm_ffn": norm_ffn[l], "w_gate": w_gate[l].astype(MXU_DT), "w_up": w_up[l].astype(MXU_DT),
            "w_down": w_down[l].astype(MXU_DT),
        }
        gq, gk, cw = nsa_q_norm[l], nsa_k_norm[l], nsa_cmp_w[l]
        col_cmp = lay["nkv"] // (2 * NSA_KV_WIDTH)
        col_sel = col_cmp + 1
        col_win = lay["nwkv"] // (2 * NSA_KV_WIDTH)
        nbg_col = lay["nbg"] // LANES

        def nsa_prompt(h3):
            tr = 512 if tp % 512 == 0 else tp
            zeros16 = jnp.zeros((bp, CMP_STRIDE, 2 * NSA_KV_WIDTH), F32)
            kc, vc = cmp_prep(h3, col_cmp, zeros16, cw, gk[0], bsum, tr=tr)
            if tp % PROMPT_SEL_TK == 0 and n_sel_p <= 64 and tp >= WINDOW + PROMPT_TQ:
                qn = q_prep(h3, lay, gq, bsum, tr=tr, width=LANES)
                ks, vs = kv_prep(h3, col_sel, gk[1], bsum, tr=tr,
                                 aug=_key_digits(tp, SEL_TK, LANES, 2 * LANES, True))
                kw, vw = kv_prep(h3, col_win, gk[2], bsum, tr=tr,
                                 aug=_key_digits(tp, WIN_TK, NSA_HEAD_DIM, LANES, False))
                return nsa_attention_prompt(qn, _slope_rows(PROMPT_TQ), kc, vc, ks, vs, kw, vw,
                                            h3, nbg_col, consts["gate_expand"], n_sel=n_sel_p)
            qn = q_prep(h3, lay, gq, bsum, tr=tr)
            ks, vs = kv_prep(h3, col_sel, gk[1], bsum, tr=tr)
            kw, vw = kv_prep(h3, col_win, gk[2], bsum, tr=tr)
            return nsa_attention(qn, _slope_rows(tq_p), kc, vc, ks, vs, kw, vw, h3, nbg_col, consts["gate_expand"],
                                 tq=tq_p, sel_tk=SEL_TK, qpos0=0, wpos0=0, n_sel=n_sel_p)

        def nsa_sample(h3):
            pages = page_table + l * n_pool
            h16 = _pad_rows(h3, ts_pad)
            qn = q_prep(h16, lay, gq, bsum, tr=ts_pad)
            w2 = 2 * NSA_KV_WIDTH
            tail16 = h16[:, :, lay["nkv"]:lay["nkv"] + w2].reshape(bs, ts_pad, 2, NSA_KV_WIDTH)
            tail_t = jnp.pad(jnp.transpose(tail16, (0, 2, 3, 1)), ((0, 0), (0, 0), (0, 0), (0, PAGE_SIZE - ts_pad)))
            kc, vc = cmp_prep_paged(pool_t, pages, tail_t, cw, gk[0])
            ks, vs = kv_prep_paged(pool_t, pages, gk[1], extra_rows=tk_tail)
            htail = _pad_rows(h3[:, :, col_sel * w2:(col_sel + 1) * w2], tk_tail)
            ks_tail, vs_tail = kv_prep(htail, 0, gk[1], bsum, tr=tk_tail)
            ks = lax.dynamic_update_slice(ks, ks_tail, (0, 0, past, 0))
            vs = lax.dynamic_update_slice(vs, vs_tail, (0, 0, past, 0))
            kw_old, vw_old = kv_prep(win3[l], 0, gk[2], bsum, tr=wbuf)
            hw_new = _pad_rows(h3[:, :, col_win * w2:(col_win + 1) * w2], WIN_TK)
            kw_new, vw_new = kv_prep(hw_new, 0, gk[2], bsum, tr=WIN_TK)
            kw = jnp.concatenate([kw_old, kw_new], axis=2)
            vw = jnp.concatenate([vw_old, vw_new], axis=2)
            return nsa_attention(qn, _slope_rows(tq_s), kc, vc, ks, vs, kw, vw, h16, nbg_col, consts["gate_expand"],
                                 tq=tq_s, sel_tk=tk_tail, qpos0=past, wpos0=past - wbuf, n_sel=n_sel_s)

        zeros_state = jnp.zeros((bp, HG_HEADS, HG_DK, HG_DV), F32)
        tm_p = 1024 if (bp * tp) % 1024 == 0 else bp * tp
        xp, h3p, hgp, rtp = _trunk_layer(xp, lw, lay, consts, nsa_prompt, zeros_state, zeros_state,
                                         seq_tiles=(tm_p, tt_p, tt_p, tp))
        xs, h3s, hgs, rts = _trunk_layer(xs, lw, lay, consts, nsa_sample, state_hgrn[l], state_ret[l],
                                         seq_tiles=(bs * ts, ts_pad, ts_pad, ts))

        kv_cols = slice(lay["nkv"], lay["nkv"] + 4 * NSA_KV_WIDTH)
        win_cols = slice(lay["nwkv"], lay["nwkv"] + 2 * NSA_KV_WIDTH)
        outs["kv_p"].append(h3p[:, :, kv_cols].reshape(bp, tp, 4, NSA_KV_HEADS, NSA_HEAD_DIM))
        outs["kv_s"].append(h3s[:, :, kv_cols].reshape(bs, ts, 4, NSA_KV_HEADS, NSA_HEAD_DIM))
        wlen = min(WINDOW, tp)
        outs["win_p"].append(h3p[:, tp - wlen:, win_cols].reshape(bp, wlen, 2, NSA_KV_HEADS, NSA_HEAD_DIM))
        ctx = jnp.concatenate([win3[l], h3s[:, :, win_cols]], axis=1)[:, ts:]
        outs["win_s"].append(ctx.reshape(bs, wbuf, 2, NSA_KV_HEADS, NSA_HEAD_DIM))
        outs["hg_p"].append(hgp); outs["hg_s"].append(hgs)
        outs["ret_p"].append(rtp); outs["ret_s"].append(rts)

    st = lambda k: jnp.stack(outs[k])
    return (xp, xs, st("kv_p"), st("kv_s"), st("win_p"), st("win_s"),
            st("hg_p"), st("hg_s"), st("ret_p"), st("ret_s"))
```

```python
import functools

import jax
import jax.numpy as jnp
from jax import lax
from jax.experimental import pallas as pl
from jax.experimental.pallas import tpu as pltpu

F32 = jnp.float32
MXU_DT = jnp.bfloat16

HG_HEADS, HG_DK, HG_DV = 8, 128, 128
NSA_HEADS, NSA_KV_HEADS, NSA_GROUP, NSA_HEAD_DIM = 16, 4, 4, 64
NSA_WIDTH = NSA_HEADS * NSA_HEAD_DIM
NSA_KV_WIDTH = NSA_KV_HEADS * NSA_HEAD_DIM
CMP_BLOCK, CMP_STRIDE, SEL_BLOCK, NSA_TOPK, WINDOW = 32, 16, 64, 16, 512
RET_HEADS, RET_DK, RET_DV = 8, 128, 128
PAGE_SIZE = 128
EPS = 1e-6
NEG_BIG = -1e30
POS_BIG = 1e30
MIN_F = 1e-20
LOWEST = -3.0e38

LANES = 128
VMEM_LIMIT_BYTES = 56 * 1024 * 1024

HG_SUB_SHIFT = 4
HG_SUB = 1 << HG_SUB_SHIFT
PAGES_PER_STEP = 8
SEL_TK = 256
PROMPT_SEL_TK = 512
PROMPT_TQ = 256
WIN_TK = 128


def _cparams(*sem):
    return pltpu.CompilerParams(dimension_semantics=sem, vmem_limit_bytes=VMEM_LIMIT_BYTES)


def _dot(a, b):
    return jnp.dot(a, b, preferred_element_type=F32)


def _dot_nt(a, b):
    return lax.dot_general(a, b, (((1,), (1,)), ((), ())), preferred_element_type=F32)


def _split(x, n):
    out = []
    r = x
    for _ in range(n):
        p = r.astype(MXU_DT)
        out.append(p)
        r = r - p.astype(F32)
    return out


def _sigmoid(x):
    return 1.0 / (1.0 + jnp.exp(-x))


def _silu(x):
    return x * _sigmoid(x)


def _iota(shape, dim):
    return lax.broadcasted_iota(jnp.int32, shape, dim)


def _eye(n):
    return (_iota((n, n), 0) == _iota((n, n), 1)).astype(MXU_DT)


def _transpose_exact(x):
    eye = _eye(x.shape[1])
    return sum(_dot_nt(eye, p) for p in _split(x, 3))


def _head_rms(x, bsum, gain):
    ssq = sum(_dot(p, bsum) for p in _split(x * x, 2))
    return x * lax.rsqrt(ssq * (1.0 / NSA_HEAD_DIM) + EPS) * gain


def _rms_matmul_kernel(x_ref, g_ref, w_ref, o_ref, xn_ref):
    @pl.when(pl.program_id(1) == 0)
    def _():
        x = x_ref[...]
        ms = jnp.mean(x * x, axis=-1, keepdims=True)
        xn_ref[...] = (x * lax.rsqrt(ms + EPS) * g_ref[...]).astype(xn_ref.dtype)

    o_ref[...] = _dot_nt(xn_ref[...], w_ref[...])


def rms_matmul(x, gain, wt, *, tm, tn):
    m, d = x.shape
    n = wt.shape[0]
    return pl.pallas_call(
        _rms_matmul_kernel,
        grid=(m // tm, n // tn),
        in_specs=[pl.BlockSpec((tm, d), lambda i, j: (i, 0)),
                  pl.BlockSpec((1, d), lambda i, j: (0, 0)),
                  pl.BlockSpec((tn, d), lambda i, j: (j, 0))],
        out_specs=pl.BlockSpec((tm, tn), lambda i, j: (i, j)),
        out_shape=jax.ShapeDtypeStruct((m, n), F32),
        scratch_shapes=[pltpu.VMEM((tm, d), MXU_DT)],
        compiler_params=_cparams("parallel", "arbitrary"),
        name="in_proj",
    )(x, gain.reshape(1, d), wt)


def _ffn_up_kernel(x_ref, g_ref, wg_ref, wu_ref, o_ref, xn_ref):
    @pl.when(pl.program_id(1) == 0)
    def _():
        x = x_ref[...]
        ms = jnp.mean(x * x, axis=-1, keepdims=True)
        xn_ref[...] = (x * lax.rsqrt(ms + EPS) * g_ref[...]).astype(xn_ref.dtype)

    xn = xn_ref[...]
    o_ref[...] = (_silu(_dot(xn, wg_ref[...])) * _dot(xn, wu_ref[...])).astype(o_ref.dtype)


def ffn_up(x, gain, wg, wu, *, tm, tn):
    m, d = x.shape
    f = wg.shape[1]
    return pl.pallas_call(
        _ffn_up_kernel,
        grid=(m // tm, f // tn),
        in_specs=[pl.BlockSpec((tm, d), lambda i, j: (i, 0)),
                  pl.BlockSpec((1, d), lambda i, j: (0, 0)),
                  pl.BlockSpec((d, tn), lambda i, j: (0, j)),
                  pl.BlockSpec((d, tn), lambda i, j: (0, j))],
        out_specs=pl.BlockSpec((tm, tn), lambda i, j: (i, j)),
        out_shape=jax.ShapeDtypeStruct((m, f), MXU_DT),
        scratch_shapes=[pltpu.VMEM((tm, d), MXU_DT)],
        compiler_params=_cparams("parallel", "arbitrary"),
        name="ffn_up",
    )(x, gain.reshape(1, d), wg, wu)


def _matmul_res_kernel(a_ref, w_ref, r_ref, o_ref):
    o_ref[...] = r_ref[...] + _dot(a_ref[...], w_ref[...])


def matmul_res(a, w, res, *, tm, tn):
    m, k = a.shape
    n = w.shape[1]
    return pl.pallas_call(
        _matmul_res_kernel,
        grid=(m // tm, n // tn),
        in_specs=[pl.BlockSpec((tm, k), lambda i, j: (i, 0)),
                  pl.BlockSpec((k, tn), lambda i, j: (0, j)),
                  pl.BlockSpec((tm, tn), lambda i, j: (i, j))],
        out_specs=pl.BlockSpec((tm, tn), lambda i, j: (i, j)),
        out_shape=jax.ShapeDtypeStruct((m, n), F32),
        compiler_params=_cparams("parallel", "arbitrary"),
        name="matmul_res",
    )(a, w, res)


def _merge_kernel(oa_ref, ob_ref, oc_ref, ga_ref, gb_ref, gc_ref, wa_ref, wb_ref, wc_ref, o_ref):
    o_ref[...] = (_sigmoid(ga_ref[...]) * _dot(oa_ref[...], wa_ref[...])
                  + _sigmoid(gb_ref[...]) * _dot(ob_ref[...], wb_ref[...])
                  + _sigmoid(gc_ref[...]) * _dot(oc_ref[...], wc_ref[...])).astype(o_ref.dtype)


def merge_branches(h, lay, oa, ob, oc, wa, wb, wc, *, tm, tn):
    m = h.shape[0]
    d = wa.shape[1]
    wdt = oa.shape[1]
    ca, cb, cc = ((lay["mg"] + i * d) // tn for i in range(3))
    row = lambda i, j: (i, 0)
    return pl.pallas_call(
        _merge_kernel,
        grid=(m // tm, d // tn),
        in_specs=[pl.BlockSpec((tm, wdt), row), pl.BlockSpec((tm, wdt), row), pl.BlockSpec((tm, wdt), row),
                  pl.BlockSpec((tm, tn), lambda i, j: (i, ca + j)),
                  pl.BlockSpec((tm, tn), lambda i, j: (i, cb + j)),
                  pl.BlockSpec((tm, tn), lambda i, j: (i, cc + j)),
                  pl.BlockSpec((wdt, tn), lambda i, j: (0, j)),
                  pl.BlockSpec((wdt, tn), lambda i, j: (0, j)),
                  pl.BlockSpec((wdt, tn), lambda i, j: (0, j))],
        out_specs=pl.BlockSpec((tm, tn), lambda i, j: (i, j)),
        out_shape=jax.ShapeDtypeStruct((m, d), MXU_DT),
        compiler_params=_cparams("parallel", "arbitrary"),
        name="merge",
    )(oa, ob, oc, h, h, h, wa, wb, wc)


def _hgrn_kernel(hq_ref, hf_ref, hi_ref, hg_ref, lb_ref, gain_ref, s0_ref, o_ref, sout_ref,
                 st_ref, ut_ref, *, tt, t_valid):
    c = HG_SUB
    nj = tt // c
    tb = pl.program_id(2)

    @pl.when(tb == 0)
    def _():
        st_ref[...] = _transpose_exact(s0_ref[0, 0])

    hq = hq_ref[0]
    lb = lb_ref[...]
    q = _silu(hq)
    sig = _sigmoid(hf_ref[0])
    g = jnp.log(jnp.maximum(lb + (1.0 - lb) * sig, MIN_F))
    kk = (1.0 - lb) * (1.0 - sig)
    v = hi_ref[0]
    row = _iota((tt, HG_DK), 0)
    if t_valid is not None:
        live = (tb * tt + row) < t_valid
        g = jnp.where(live, g, 0.0)
        kk = jnp.where(live, kk, 0.0)

    r2 = _iota((tt, tt), 0)
    c2 = _iota((tt, tt), 1)
    same = (r2 >> HG_SUB_SHIFT) == (c2 >> HG_SUB_SHIFT)
    gp = _split(g, 3)
    tri = (same & (c2 <= r2)).astype(MXU_DT)
    blk = same.astype(MXU_DT)
    b = sum(_dot(tri, p) for p in gp)
    dtot = sum(_dot(blk, p) for p in gp)
    qe = q * jnp.exp(b)
    ke = kk * jnp.exp(dtot - b)
    edec = jnp.exp(dtot)

    q3 = q.reshape(nj, c, HG_DK)
    kk3 = kk.reshape(nj, c, HG_DK)
    b3 = b.reshape(nj, c, HG_DK)
    v3 = v.reshape(nj, c, HG_DV)
    tpos = _iota((nj, c, HG_DK), 1)
    ones = jnp.ones((HG_DK, HG_DV), MXU_DT)
    o3 = jnp.zeros((nj, c, HG_DV), F32)
    for s in range(c):
        m = tpos >= s
        diff = jnp.where(m, b3 - b3[:, s:s + 1, :], 0.0)
        y = jnp.where(m, q3 * kk3[:, s:s + 1, :] * jnp.exp(diff), 0.0)
        z = _dot(y.reshape(tt, HG_DK).astype(MXU_DT), ones).reshape(nj, c, HG_DV)
        o3 = o3 + z * v3[:, s:s + 1, :]
    o = o3.reshape(tt, HG_DV)

    v_b = v.astype(MXU_DT)
    ke_b = ke.astype(MXU_DT)
    for j in range(nj):
        ut_ref[j] = lax.dot_general(v_b[j * c:(j + 1) * c], ke_b[j * c:(j + 1) * c],
                                    (((0,), (0,)), ((), ())), preferred_element_type=F32)

    qe_b = qe.astype(MXU_DT)
    outs = []
    for j in range(nj):
        st = st_ref[...]
        outs.append(_dot_nt(qe_b[j * c:(j + 1) * c], st.astype(MXU_DT)))
        st_ref[...] = st * edec[j * c:j * c + 1, :] + ut_ref[j]
    o = o + jnp.concatenate(outs, axis=0)

    on = o * lax.rsqrt(jnp.mean(o * o, axis=-1, keepdims=True) + EPS) * gain_ref[...]
    o_ref[0] = (on * _silu(hg_ref[0])).astype(o_ref.dtype)

    @pl.when(tb == pl.num_programs(2) - 1)
    def _():
        sout_ref[0, 0] = _transpose_exact(st_ref[...])


def hgrn_mixer(h3, lay, lb, out_gain, s0, *, tt, t_valid):
    bsz, t, _ = h3.shape
    cq, cf, ci, cg = (lay[k] // LANES for k in ("hq", "hf", "hi", "hg"))
    col = lambda c0: (lambda b, hd, tb: (b, tb, c0 + hd))
    kern = functools.partial(_hgrn_kernel, tt=tt, t_valid=t_valid)
    return pl.pallas_call(
        kern,
        grid=(bsz, HG_HEADS, t // tt),
        in_specs=[pl.BlockSpec((1, tt, LANES), col(cq)), pl.BlockSpec((1, tt, LANES), col(cf)),
                  pl.BlockSpec((1, tt, LANES), col(ci)), pl.BlockSpec((1, tt, LANES), col(cg)),
                  pl.BlockSpec((1, HG_DK), lambda b, hd, tb: (0, hd)),
                  pl.BlockSpec((1, HG_DV), lambda b, hd, tb: (0, 0)),
                  pl.BlockSpec((1, 1, HG_DK, HG_DV), lambda b, hd, tb: (b, hd, 0, 0))],
        out_specs=[pl.BlockSpec((1, tt, HG_DV), lambda b, hd, tb: (b, tb, hd)),
                   pl.BlockSpec((1, 1, HG_DK, HG_DV), lambda b, hd, tb: (b, hd, 0, 0))],
        out_shape=[jax.ShapeDtypeStruct((bsz, t, HG_HEADS * HG_DV), MXU_DT),
                   jax.ShapeDtypeStruct((bsz, HG_HEADS, HG_DK, HG_DV), F32)],
        scratch_shapes=[pltpu.VMEM((HG_DV, HG_DK), F32),
                        pltpu.VMEM((tt // HG_SUB, HG_DV, HG_DK), F32)],
        compiler_params=_cparams("parallel", "parallel", "arbitrary"),
        name="hgrn",
    )(h3, h3, h3, h3, lb, out_gain.reshape(1, HG_DV), s0)


def _ret_kernel(q_ref, k_ref, v_ref, g_ref, lg_ref, s0_ref, o_ref, sout_ref, s_ref, *, tt, n_valid):
    tb = pl.program_id(2)

    @pl.when(tb == 0)
    def _():
        s_ref[...] = s0_ref[0, 0]

    lgl = lg_ref[0][:, :LANES]
    lgt = lg_ref[0][:, :tt]
    q = q_ref[0] * (RET_DK ** -0.5)
    k = k_ref[0]
    vb = v_ref[0].astype(MXU_DT)
    rel = (_iota((tt, tt), 0) - _iota((tt, tt), 1)).astype(F32)
    dmat = jnp.where(rel >= 0, jnp.exp(jnp.maximum(rel, 0.0) * lgt), 0.0)
    att = _dot_nt(q.astype(MXU_DT), k.astype(MXU_DT)) * dmat
    q_dec = jnp.exp((_iota((tt, RET_DK), 0) + 1).astype(F32) * lgl)
    s = s_ref[...]
    o = _dot(att.astype(MXU_DT), vb) + _dot((q * q_dec).astype(MXU_DT), s.astype(MXU_DT))

    spos = _iota((RET_DK, tt), 1)
    k_dec = jnp.where(spos < n_valid, jnp.exp(jnp.maximum(n_valid - 1 - spos, 0).astype(F32) * lgt), 0.0)
    kt = _transpose_exact(k)
    s_ref[...] = jnp.exp(float(n_valid) * lgl) * s + _dot((kt * k_dec).astype(MXU_DT), vb)

    mu = jnp.mean(o, axis=-1, keepdims=True)
    var = jnp.mean(jnp.square(o - mu), axis=-1, keepdims=True)
    o_ref[0] = ((o - mu) * lax.rsqrt(var + EPS) * _silu(g_ref[0])).astype(o_ref.dtype)

    @pl.when(tb == pl.num_programs(2) - 1)
    def _():
        sout_ref[0, 0] = s_ref[...]


def retention_mixer(h3, lay, lg_tab, s0, *, tt, n_valid):
    bsz, t, _ = h3.shape
    assert n_valid == tt or t == tt
    cq, ck, cv, cg = (lay[k] // LANES for k in ("rq", "rk", "rv", "rg"))
    col = lambda c0: (lambda b, hd, tb: (b, tb, c0 + hd))
    kern = functools.partial(_ret_kernel, tt=tt, n_valid=n_valid)
    return pl.pallas_call(
        kern,
        grid=(bsz, RET_HEADS, t // tt),
        in_specs=[pl.BlockSpec((1, tt, LANES), col(cq)), pl.BlockSpec((1, tt, LANES), col(ck)),
                  pl.BlockSpec((1, tt, LANES), col(cv)), pl.BlockSpec((1, tt, LANES), col(cg)),
                  pl.BlockSpec((1, 1, lg_tab.shape[2]), lambda b, hd, tb: (hd, 0, 0)),
                  pl.BlockSpec((1, 1, RET_DK, RET_DV), lambda b, hd, tb: (b, hd, 0, 0))],
        out_specs=[pl.BlockSpec((1, tt, RET_DV), lambda b, hd, tb: (b, tb, hd)),
                   pl.BlockSpec((1, 1, RET_DK, RET_DV), lambda b, hd, tb: (b, hd, 0, 0))],
        out_shape=[jax.ShapeDtypeStruct((bsz, t, RET_HEADS * RET_DV), MXU_DT),
                   jax.ShapeDtypeStruct((bsz, RET_HEADS, RET_DK, RET_DV), F32)],
        scratch_shapes=[pltpu.VMEM((RET_DK, RET_DV), F32)],
        compiler_params=_cparams("parallel", "parallel", "arbitrary"),
        name="retention",
    )(h3, h3, h3, h3, lg_tab, s0)


def _store_heads(ref, x, lead=()):
    for hh in range(NSA_KV_HEADS):
        ref[lead + (hh,)] = x[:, hh * NSA_HEAD_DIM:(hh + 1) * NSA_HEAD_DIM].astype(ref.dtype)


def _q_prep_kernel(q_ref, gain_ref, bsum_ref, o_ref):
    x = q_ref[0]
    hd = NSA_HEAD_DIM
    for kvh in range(NSA_KV_HEADS):
        lo = kvh * NSA_KV_WIDTH
        qn = _head_rms(x[:, lo:lo + NSA_KV_WIDTH], bsum_ref[...], gain_ref[...]) * (hd ** -0.5)
        for g in range(NSA_GROUP):
            o_ref[0, kvh, g, :, 0:hd] = qn[:, g * hd:(g + 1) * hd].astype(o_ref.dtype)
            if o_ref.shape[-1] > hd:
                o_ref[0, kvh, g, :, hd:] = jnp.zeros((x.shape[0], o_ref.shape[-1] - hd), o_ref.dtype)


def q_prep(h3, lay, gain_q, bsum, *, tr, width=NSA_HEAD_DIM):
    bsz, t, _ = h3.shape
    cq = lay["nq"] // NSA_WIDTH
    return pl.pallas_call(
        _q_prep_kernel,
        grid=(bsz, t // tr),
        in_specs=[pl.BlockSpec((1, tr, NSA_WIDTH), lambda b, i: (b, i, cq)),
                  pl.BlockSpec((1, NSA_KV_WIDTH), lambda b, i: (0, 0)),
                  pl.BlockSpec((NSA_KV_WIDTH, NSA_KV_WIDTH), lambda b, i: (0, 0))],
        out_specs=pl.BlockSpec((1, NSA_KV_HEADS, NSA_GROUP, tr, width), lambda b, i: (b, 0, 0, i, 0)),
        out_shape=jax.ShapeDtypeStruct((bsz, NSA_KV_HEADS, NSA_GROUP, t, width), MXU_DT),
        compiler_params=_cparams("parallel", "parallel"),
        name="nsa_q_prep",
    )(h3, jnp.tile(gain_q, NSA_KV_HEADS).reshape(1, NSA_KV_WIDTH), bsum)


def _kv_prep_kernel(x_ref, gain_ref, bsum_ref, *rest):
    ko_ref, vo_ref = rest[-2:]
    hd = NSA_HEAD_DIM
    x = x_ref[0]
    kn = _head_rms(x[:, :NSA_KV_WIDTH], bsum_ref[...], gain_ref[...])
    for hh in range(NSA_KV_HEADS):
        ko_ref[0, hh, :, 0:hd] = kn[:, hh * hd:(hh + 1) * hd].astype(ko_ref.dtype)
        if len(rest) == 3:
            ko_ref[0, hh, :, hd:] = rest[0][:, hd:]
        lo = NSA_KV_WIDTH + hh * hd
        vo_ref[0, hh, :, 0:hd] = x[:, lo:lo + hd].astype(vo_ref.dtype)
        if vo_ref.shape[-1] > hd:
            vo_ref[0, hh, :, hd:] = jnp.ones((x.shape[0], vo_ref.shape[-1] - hd), vo_ref.dtype)


def kv_prep(x3, col, gain, bsum, *, tr, aug=None):
    bsz, t, _ = x3.shape
    w2 = 2 * NSA_KV_WIDTH
    hd = NSA_HEAD_DIM
    kw = hd if aug is None else aug.shape[1]
    vw = hd if aug is None else LANES
    vshape = jax.ShapeDtypeStruct((bsz, NSA_KV_HEADS, t, vw), MXU_DT)
    kshape = jax.ShapeDtypeStruct((bsz, NSA_KV_HEADS, t, kw), MXU_DT)
    spec = lambda w: pl.BlockSpec((1, NSA_KV_HEADS, tr, w), lambda b, i: (b, 0, i, 0))
    in_specs = [pl.BlockSpec((1, tr, w2), lambda b, i: (b, i, col)),
                pl.BlockSpec((1, NSA_KV_WIDTH), lambda b, i: (0, 0)),
                pl.BlockSpec((NSA_KV_WIDTH, NSA_KV_WIDTH), lambda b, i: (0, 0))]
    args = [x3, jnp.tile(gain, NSA_KV_HEADS).reshape(1, NSA_KV_WIDTH), bsum]
    if aug is not None:
        in_specs.append(pl.BlockSpec((tr, kw), lambda b, i: (i, 0)))
        args.append(aug)
    return pl.pallas_call(
        _kv_prep_kernel,
        grid=(bsz, t // tr),
        in_specs=in_specs,
        out_specs=[spec(kw), spec(vw)],
        out_shape=[kshape, vshape],
        compiler_params=_cparams("parallel", "parallel"),
        name="nsa_kv_prep",
    )(*args)


def _head_rms_t(xt, gain_t):
    x3 = xt.reshape(NSA_KV_HEADS, NSA_HEAD_DIM, xt.shape[1])
    ms = jnp.mean(x3 * x3, axis=1, keepdims=True)
    return (x3 * lax.rsqrt(ms + EPS)).reshape(xt.shape) * gain_t


def _gain_t(gain):
    return jnp.broadcast_to(jnp.tile(gain, NSA_KV_HEADS)[:, None], (NSA_KV_WIDTH, LANES))


def _cmp_prep_kernel(*refs, n_in, rows):
    refs = refs[-(n_in + 7):]
    nxt_ref, tail_ref, w_ref, gain_ref, bsum_ref, kc_ref, vc_ref = refs[n_in:]
    cs = CMP_STRIDE
    nb = rows // cs
    w0 = w_ref[0]
    w1 = w_ref[1]
    a0 = []
    a1 = []
    for u in range(n_in):
        x3 = refs[u][0].reshape(nb, cs, 2 * NSA_KV_WIDTH)
        a0.append(jnp.sum(x3 * w0[None], axis=1))
        a1.append(jnp.sum(x3 * w1[None], axis=1))
    a0 = jnp.concatenate(a0, axis=0) if n_in > 1 else a0[0]
    a1 = jnp.concatenate(a1, axis=0) if n_in > 1 else a1[0]
    last = pl.program_id(1) == pl.num_programs(1) - 1
    nx = jnp.where(last, tail_ref[0], nxt_ref[0])
    a1_next = jnp.sum(nx * w1, axis=0, keepdims=True)
    tot = nb * n_in
    a1s = pltpu.roll(a1, tot - 1, 0)
    a1s = jnp.where(_iota(a1s.shape, 0) == tot - 1, a1_next, a1s)
    comp = a0 + a1s
    kc = _head_rms(comp[:, :NSA_KV_WIDTH], bsum_ref[...], gain_ref[...])
    _store_heads(kc_ref, kc, (0,))
    _store_heads(vc_ref, comp[:, NSA_KV_WIDTH:], (0,))


def _cmp_weight_table(cmp_w):
    w = cmp_w.reshape(2, CMP_BLOCK // CMP_STRIDE, CMP_STRIDE)
    w = jnp.transpose(w, (1, 2, 0))
    return jnp.repeat(w, NSA_KV_WIDTH, axis=2)


def cmp_prep(x3, col, tail16, cmp_w, gain, bsum, *, tr):
    bsz, t, _ = x3.shape
    w2 = 2 * NSA_KV_WIDTH
    nbt = tr // CMP_STRIDE
    n_steps = t // tr
    kern = functools.partial(_cmp_prep_kernel, n_in=1, rows=tr)
    oshape = jax.ShapeDtypeStruct((bsz, NSA_KV_HEADS, t // CMP_STRIDE, NSA_HEAD_DIM), MXU_DT)
    ospec = pl.BlockSpec((1, NSA_KV_HEADS, nbt, NSA_HEAD_DIM), lambda b, i: (b, 0, i, 0))
    chunks_per_step = tr // CMP_STRIDE
    return pl.pallas_call(
        kern,
        grid=(bsz, n_steps),
        in_specs=[pl.BlockSpec((1, tr, w2), lambda b, i: (b, i, col)),
                  pl.BlockSpec((1, CMP_STRIDE, w2),
                               lambda b, i: (b, jnp.minimum(i + 1, n_steps - 1) * chunks_per_step, col)),
                  pl.BlockSpec((1, CMP_STRIDE, w2), lambda b, i: (b, 0, 0)),
                  pl.BlockSpec((2, CMP_STRIDE, w2), lambda b, i: (0, 0, 0)),
                  pl.BlockSpec((1, NSA_KV_WIDTH), lambda b, i: (0, 0)),
                  pl.BlockSpec((NSA_KV_WIDTH, NSA_KV_WIDTH), lambda b, i: (0, 0))],
        out_specs=[ospec, ospec],
        out_shape=[oshape, oshape],
        compiler_params=_cparams("parallel", "arbitrary"),
        name="nsa_cmp_prep",
    )(x3, x3, tail16, _cmp_weight_table(cmp_w), jnp.tile(gain, NSA_KV_HEADS).reshape(1, NSA_KV_WIDTH), bsum)


def _cmp_prep_t_kernel(*refs, n_in):
    refs = refs[-(n_in + 9):]
    nxt_ref, tail_ref, wk_ref, wv_ref, wnk_ref, wnv_ref, gain_ref, kc_ref, vc_ref = refs[n_in:]
    nb = wk_ref.shape[2]

    def wdot(xt, w):
        xh = xt.astype(MXU_DT)
        wh, wl = _split(w, 2)
        return _dot(xh, wh) + _dot(xh, wl)

    last = pl.program_id(1) == pl.num_programs(1) - 1
    comp_k = wdot(jnp.where(last, tail_ref[0, 0], nxt_ref[0, 0]), wnk_ref[...])
    comp_v = wdot(jnp.where(last, tail_ref[0, 1], nxt_ref[0, 1]), wnv_ref[...])
    for u in range(n_in):
        comp_k = comp_k + wdot(refs[u][0, 0], wk_ref[u])
        comp_v = comp_v + wdot(refs[u][0, 1], wv_ref[u])
    kct = _head_rms_t(comp_k, gain_ref[...][:, :nb]).astype(MXU_DT)
    eye = _eye(nb)
    _store_heads(kc_ref, _dot_nt(eye, kct), (0,))
    _store_heads(vc_ref, _dot_nt(eye, comp_v.astype(MXU_DT)), (0,))


def _cmp_band_tables(cmp_w, n_pages_step):
    cs = CMP_STRIDE
    w = cmp_w.reshape(2, CMP_BLOCK // cs, cs)
    nb = n_pages_step * PAGE_SIZE // cs
    rho = jnp.arange(PAGE_SIZE)
    ch = (jnp.arange(n_pages_step)[:, None] * (PAGE_SIZE // cs) + rho[None, :] // cs)[..., None]
    n = jnp.arange(nb)[None, None, :]
    s = rho % cs
    tabs = []
    for c in range(2):
        full = (w[c, 0][s][None, :, None] * (ch == n) + w[c, 1][s][None, :, None] * (ch == n + 1)).astype(F32)
        nxt = jnp.where((rho[:, None] < cs) & (n[0] == nb - 1), w[c, 1][s][:, None], 0.0).astype(F32)
        tabs.append((full, nxt))
    return tabs[0][0], tabs[1][0], tabs[0][1], tabs[1][1]


def cmp_prep_paged(pool_t, pages, tail_t, cmp_w, gain):
    bsz, n_pages = pages.shape
    pp = PAGES_PER_STEP
    n_steps = n_pages // pp
    nbt = pp * PAGE_SIZE // CMP_STRIDE
    kern = functools.partial(_cmp_prep_t_kernel, n_in=pp)
    oshape = jax.ShapeDtypeStruct((bsz, NSA_KV_HEADS, n_pages * PAGE_SIZE // CMP_STRIDE, NSA_HEAD_DIM), MXU_DT)
    ospec = pl.BlockSpec((1, NSA_KV_HEADS, nbt, NSA_HEAD_DIM), lambda b, i, pt: (b, 0, i, 0))
    pblock = (1, 2, NSA_KV_WIDTH, PAGE_SIZE)
    page_spec = lambda u: pl.BlockSpec(pblock, lambda b, i, pt: (pt[b, i * pp + u], 0, 0, 0))
    const = lambda *shape: pl.BlockSpec(shape, lambda b, i, pt: (0,) * len(shape))
    wk, wv, wnk, wnv = _cmp_band_tables(cmp_w, pp)
    return pl.pallas_call(
        kern,
        grid_spec=pltpu.PrefetchScalarGridSpec(
            num_scalar_prefetch=1,
            grid=(bsz, n_steps),
            in_specs=[page_spec(u) for u in range(pp)]
                     + [pl.BlockSpec(pblock, lambda b, i, pt: (pt[b, jnp.minimum(i + 1, n_steps - 1) * pp], 0, 0, 0)),
                        pl.BlockSpec(pblock, lambda b, i, pt: (b, 0, 0, 0)),
                        const(pp, PAGE_SIZE, nbt), const(pp, PAGE_SIZE, nbt), const(PAGE_SIZE, nbt),
                        const(PAGE_SIZE, nbt), const(NSA_KV_WIDTH, LANES)],
            out_specs=[ospec, ospec]),
        out_shape=[oshape, oshape],
        compiler_params=_cparams("parallel", "arbitrary"),
        name="nsa_cmp_prep_paged",
    )(pages, *([pool_t] * pp), pool_t, tail_t, wk, wv, wnk, wnv, _gain_t(gain))


def _gate_and_store(o_ref, br_ref, nbg_ref, gx_ref, branches, tq):
    hd = NSA_HEAD_DIM
    for br, o in enumerate(branches):
        for g in range(NSA_GROUP):
            br_ref[br, :, g * hd:(g + 1) * hd] = o[g * tq:(g + 1) * tq]
    pieces = _split(_sigmoid(nbg_ref[0]), 2)
    acc = None
    for br in range(3):
        term = sum(_dot(p, gx_ref[br]) for p in pieces) * br_ref[br]
        acc = term if acc is None else acc + term
    o_ref[0] = acc.astype(o_ref.dtype)


def _nsa_kernel(q_ref, sl_ref, kc_ref, vc_ref, ks_ref, vs_ref, kw_ref, vw_ref, at_ref, nbg_ref, gx_ref,
                o_ref, sc_ref, m_ref, l_ref, acc_ref, br_ref,
                *, tq, sel_tk, qpos0, wpos0, n_sel, topk):
    rows = NSA_GROUP * tq
    hd = NSA_HEAD_DIM
    t0 = qpos0 + pl.program_id(2) * tq
    q = q_ref[0, 0].reshape(rows, hd)
    slope = sl_ref[0][:, :1]
    nbp = kc_ref.shape[2]
    n_sel_pad, tl = sc_ref.shape

    def tok(shape):
        return t0 + (_iota(shape, 0) & (tq - 1))

    s = _dot_nt(q, kc_ref[0, 0])
    dist = tok((rows, nbp)) - (_iota((rows, nbp), 1) * CMP_STRIDE + (CMP_BLOCK - 1))
    valid = dist >= 0
    s = jnp.where(valid, s - slope * dist.astype(F32), NEG_BIG)
    p = jnp.where(valid, jnp.exp(s - jnp.max(s, axis=-1, keepdims=True)), 0.0)
    p = p / jnp.maximum(jnp.sum(p, axis=-1, keepdims=True), 1e-30)
    o_cmp = _dot(p.astype(MXU_DT), vc_ref[0, 0])

    sel = _topk_block_mask(p, at_ref, t0, tq=tq, tl=tl, n_sel=n_sel, topk=topk)
    spread = ((_iota((rows, tl), 0) & (tq - 1)) == _iota((rows, tl), 1)).astype(MXU_DT)
    nblk = sel_tk // SEL_BLOCK
    if nblk % 8 == 0:
        sc_ref[...] = sel
        in_tile = (_iota((nblk, sel_tk), 0) == (_iota((nblk, sel_tk), 1) >> 6)).astype(MXU_DT)
    else:
        selb = _dot_nt(spread, sel.astype(MXU_DT)).astype(MXU_DT)

    def softmax_pass(k_ref, v_ref, tk, lo, hi, kpos0, mask_fn):
        m_ref[...] = jnp.full(m_ref.shape, NEG_BIG, F32)
        l_ref[...] = jnp.zeros(l_ref.shape, F32)
        acc_ref[...] = jnp.zeros(acc_ref.shape, F32)
        rel = (_iota((rows, tk), 0) & (tq - 1)) - _iota((rows, tk), 1)
        reps = tk // LANES

        def body(kt, carry):
            k0 = pl.multiple_of(kt * tk, tk)
            s = _dot_nt(q, k_ref[0, 0, pl.ds(k0, tk), :])
            d = rel + (t0 - kpos0 - k0)
            valid = mask_fn(d, k0)
            s = jnp.where(valid, s - slope * d.astype(F32), NEG_BIG)
            m_prev = m_ref[...]
            m_new = jnp.maximum(m_prev, jnp.max(s, axis=-1, keepdims=True))
            alpha = jnp.exp(m_prev - m_new)
            mrep = m_new if reps == 1 else jnp.concatenate([m_new] * reps, axis=1)
            p = jnp.where(valid, jnp.exp(s - mrep), 0.0)
            l_ref[...] = alpha * l_ref[...] + jnp.sum(p, axis=-1, keepdims=True)
            acc_ref[...] = alpha[:, :hd] * acc_ref[...] + _dot(p.astype(MXU_DT), v_ref[0, 0, pl.ds(k0, tk), :])
            m_ref[...] = m_new
            return carry

        lax.fori_loop(lo, hi, body, 0)
        return acc_ref[...] / jnp.maximum(l_ref[...][:, :hd], 1e-30)

    n_kt_all = ks_ref.shape[2] // sel_tk
    hi_sel = jnp.minimum(n_kt_all, (t0 + tq - 1) // sel_tk + 1)

    def sel_mask(d, k0):
        if nblk % 8 == 0:
            blk0 = pl.multiple_of(k0 // SEL_BLOCK, nblk)
            mine = _dot_nt(spread, sc_ref[pl.ds(blk0, nblk), :].astype(MXU_DT)).astype(MXU_DT)
            return (_dot(mine, in_tile) > 0.5) & (d >= 0)
        blk_of_key = (k0 + _iota((n_sel_pad, sel_tk), 1)) >> 6
        expand = (_iota((n_sel_pad, sel_tk), 0) == blk_of_key).astype(MXU_DT)
        return (_dot(selb, expand) > 0.5) & (d >= 0)

    o_sel = softmax_pass(ks_ref, vs_ref, sel_tk, 0, hi_sel, 0, sel_mask)

    n_wt_all = kw_ref.shape[2] // WIN_TK
    lo_win = jnp.maximum(t0 - (WINDOW - 1) - wpos0, 0) // WIN_TK
    hi_win = jnp.minimum(n_wt_all, (t0 + tq - 1 - wpos0) // WIN_TK + 1)
    o_win = softmax_pass(kw_ref, vw_ref, WIN_TK, lo_win, hi_win, wpos0,
                         lambda d, k0: (d >= 0) & (d < WINDOW))

    _gate_and_store(o_ref, br_ref, nbg_ref, gx_ref, (o_cmp, o_sel, o_win), tq)


def _imp_to_sel_matrix(n_sel_pad, nbp):
    m = SEL_BLOCK // CMP_STRIDE
    r = CMP_BLOCK // CMP_STRIDE
    jj = jnp.arange(n_sel_pad)[:, None]
    ii = jnp.arange(nbp)[None, :]
    cnt = sum(((ii - rr) >= m * jj) & ((ii - rr) < m * (jj + 1)) for rr in range(r))
    return cnt.astype(MXU_DT)


def nsa_attention(qn, slopes, kc, vc, ks, vs, kw, vw, h3, nbg_col, gate_expand, *, tq, sel_tk, qpos0, wpos0, n_sel):
    bsz, _, _, tql, hd = qn.shape
    nbp, tk_all, tw_all = kc.shape[2], ks.shape[2], kw.shape[2]
    assert tq & (tq - 1) == 0 and tk_all % sel_tk == 0 and tw_all % WIN_TK == 0
    n_sel_pad = -(-max(n_sel, tk_all // SEL_BLOCK) // 64) * 64
    tl = max(tq, LANES)
    rows = NSA_GROUP * tq
    at = _imp_to_sel_matrix(n_sel_pad, nbp)
    kern = functools.partial(_nsa_kernel, tq=tq, sel_tk=sel_tk, qpos0=qpos0, wpos0=wpos0, n_sel=n_sel,
                             topk=min(NSA_TOPK, n_sel))
    full = lambda n: pl.BlockSpec((1, 1, n, hd), lambda b, kh, i: (b, kh, 0, 0))
    gw = NSA_GROUP * hd
    return pl.pallas_call(
        kern,
        grid=(bsz, NSA_KV_HEADS, tql // tq),
        in_specs=[pl.BlockSpec((1, 1, NSA_GROUP, tq, hd), lambda b, kh, i: (b, kh, 0, i, 0)),
                  pl.BlockSpec((1, rows, LANES), lambda b, kh, i: (kh, 0, 0)),
                  full(nbp), full(nbp), full(tk_all), full(tk_all), full(tw_all), full(tw_all),
                  pl.BlockSpec((n_sel_pad, nbp), lambda b, kh, i: (0, 0)),
                  pl.BlockSpec((1, tq, LANES), lambda b, kh, i: (b, i, nbg_col)),
                  pl.BlockSpec((3, LANES, gw), lambda b, kh, i: (0, 0, kh))],
        out_specs=pl.BlockSpec((1, tq, gw), lambda b, kh, i: (b, i, kh)),
        out_shape=jax.ShapeDtypeStruct((bsz, tql, NSA_WIDTH), MXU_DT),
        scratch_shapes=[pltpu.VMEM((n_sel_pad, tl), F32),
                        pltpu.VMEM((rows, LANES), F32), pltpu.VMEM((rows, LANES), F32),
                        pltpu.VMEM((rows, hd), F32), pltpu.VMEM((3, tq, gw), F32)],
        compiler_params=_cparams("parallel", "parallel", "arbitrary"),
        name="nsa_attention",
    )(qn, slopes, kc, vc, ks, vs, kw, vw, at, h3, gate_expand)


def _topk_block_mask(p, at_ref, t0, *, tq, tl, n_sel, topk):
    n_sel_pad, nbp = at_ref.shape
    imp = p[0:tq] + p[tq:2 * tq] + p[2 * tq:3 * tq] + p[3 * tq:4 * tq]
    if tl > tq:
        imp = jnp.concatenate([imp, jnp.zeros((tl - tq, nbp), F32)], axis=0)
    imp_sel = sum(_dot_nt(at_ref[...], piece) for piece in _split(imp, 3))
    j = _iota((n_sel_pad, tl), 0)
    tt = t0 + _iota((n_sel_pad, tl), 1)
    forced = (j == 0) | (j == (tt >> 6))
    allowed = (j << 6) <= tt
    score = jnp.where(forced, POS_BIG, jnp.where(allowed, imp_sel, NEG_BIG))
    score = jnp.where(j >= n_sel, LOWEST, score)
    jf = j.astype(F32)
    sel = jnp.zeros((n_sel_pad, tl), F32)
    for _ in range(topk):
        mx = jnp.max(score, axis=0, keepdims=True)
        first = jnp.min(jnp.where(score == mx, jf, 1e9), axis=0, keepdims=True)
        pick = jf == first
        sel = jnp.where(pick, 1.0, sel)
        score = jnp.where(pick, -jnp.inf, score)
    return jnp.where(allowed, sel, 0.0)


def _nsa_prompt_kernel(q_ref, sl_ref, qts_ref, qtw_ref, kc_ref, vc_ref, ks_ref, vs_ref, kw_ref, vw_ref, at_ref,
                       nbg_ref, gx_ref, o_ref, m_ref, acc_ref, br_ref, *, tq, n_sel, topk):
    rows = NSA_GROUP * tq
    hd = NSA_HEAD_DIM
    qi = pl.program_id(2)
    t0 = qi * tq
    q128 = q_ref[0, 0].reshape(rows, LANES)
    slope = sl_ref[0][:, :1]
    nbp = kc_ref.shape[2]

    s = _dot_nt(q128[:, :hd], kc_ref[0, 0])
    tok = t0 + (_iota((rows, nbp), 0) & (tq - 1))
    dist = tok - (_iota((rows, nbp), 1) * CMP_STRIDE + (CMP_BLOCK - 1))
    valid = dist >= 0
    s = jnp.where(valid, s - slope * dist.astype(F32), NEG_BIG)
    p = jnp.where(valid, jnp.exp(s - jnp.max(s, axis=-1, keepdims=True)), 0.0)
    p = p / jnp.maximum(jnp.sum(p, axis=-1, keepdims=True), 1e-30)
    o_cmp = _dot(p.astype(MXU_DT), vc_ref[0, 0])

    def normalised(acc):
        return acc[:, :hd] / jnp.maximum(acc[:, hd:], 1e-30)

    lane = _iota((rows, LANES), 1)
    qw = jnp.where(lane < hd, q128, qtw_ref[0])
    span = WINDOW + tq
    ws = pl.multiple_of(jnp.maximum(t0 - WINDOW, 0), tq)
    sw = _dot_nt(qw, kw_ref[0, 0, pl.ds(ws, span), :])
    d = (_iota((rows, span), 0) & (tq - 1)) - _iota((rows, span), 1) + (t0 - ws)
    sw = jnp.where((d >= 0) & (d < WINDOW), sw, NEG_BIG)
    pw = jnp.exp(sw - jnp.max(sw, axis=-1, keepdims=True))
    o_win = normalised(_dot(pw.astype(MXU_DT), vw_ref[0, 0, pl.ds(ws, span), :]))

    sel = _topk_block_mask(p, at_ref, t0, tq=tq, tl=tq, n_sel=n_sel, topk=topk)
    sel_t = jnp.concatenate([jnp.zeros_like(sel), sel], axis=0).astype(MXU_DT)
    spread = ((_iota((rows, tq), 0) & (tq - 1)) == _iota((rows, tq), 1)).astype(MXU_DT)
    selb = _dot_nt(spread, sel_t)
    neg = jnp.where((lane >= hd) & (selb < 0.5), NEG_BIG, 0.0)
    qa = jnp.concatenate([(q128.astype(F32) + neg).astype(MXU_DT), qts_ref[0]], axis=1)

    tk = PROMPT_SEL_TK
    kd = t0 // tk
    k0 = pl.multiple_of(kd * tk, tk)
    s = _dot_nt(qa, ks_ref[0, 0, pl.ds(k0, tk), :])
    causal = ((_iota((rows, tk), 0) & (tq - 1)) - _iota((rows, tk), 1) + (t0 - k0)) >= 0
    s = jnp.where(causal, s, NEG_BIG)
    m = jnp.max(s, axis=-1, keepdims=True)
    m_ref[...] = jnp.broadcast_to(m, m_ref.shape)
    acc_ref[...] = _dot(jnp.exp(s - m).astype(MXU_DT), vs_ref[0, 0, pl.ds(k0, tk), :])

    def sel_body(kt, carry):
        ka = pl.multiple_of(kt * tk, tk)
        s = _dot_nt(qa, ks_ref[0, 0, pl.ds(ka, tk), :])
        m_prev = m_ref[...]
        m_new = jnp.maximum(m_prev, jnp.max(s, axis=-1, keepdims=True))
        p = jnp.exp(s - jnp.concatenate([m_new] * (tk // LANES), axis=1))
        acc_ref[...] = jnp.exp(m_prev - m_new) * acc_ref[...] + _dot(p.astype(MXU_DT), vs_ref[0, 0, pl.ds(ka, tk), :])
        m_ref[...] = m_new
        return carry

    lax.fori_loop(0, kd, sel_body, 0)
    o_sel = normalised(acc_ref[...])

    _gate_and_store(o_ref, br_ref, nbg_ref, gx_ref, (o_cmp, o_sel, o_win), tq)


def _slope_digits(tq, tile, lane0):
    hh = jnp.arange(1, NSA_HEADS + 1, dtype=F32)
    slopes = jnp.exp2(-8.0 * hh / NSA_HEADS)
    pieces = jnp.stack(_split(slopes, 3), axis=1).astype(F32)
    six = jnp.concatenate([pieces * float(tile), pieces], axis=1)
    tab = jnp.zeros((NSA_HEADS, LANES), F32).at[:, lane0:lane0 + 6].set(six)
    tab = tab.reshape(NSA_KV_HEADS, NSA_GROUP, 1, LANES)
    return jnp.broadcast_to(tab, (NSA_KV_HEADS, NSA_GROUP, tq, LANES)).reshape(
        NSA_KV_HEADS, NSA_GROUP * tq, LANES).astype(MXU_DT)


def _key_digits(t, tile, lane0, width, onehot):
    r = jnp.arange(t)
    lane = jnp.arange(width)[None, :]
    tab = jnp.zeros((t, width), F32)
    if onehot:
        tab = jnp.where(lane - NSA_HEAD_DIM == (r // SEL_BLOCK)[:, None], 1.0, tab)
    hi = (r // tile).astype(F32)[:, None]
    lo = (r % tile).astype(F32)[:, None]
    tab = jnp.where((lane >= lane0) & (lane < lane0 + 3), hi, tab)
    tab = jnp.where((lane >= lane0 + 3) & (lane < lane0 + 6), lo, tab)
    return tab.astype(MXU_DT)


def nsa_attention_prompt(q128, slopes, kc, vc, ks_aug, vs, kw_aug, vw, h3, nbg_col, gate_expand, *, n_sel):
    bsz, _, _, t, _ = q128.shape
    tq = PROMPT_TQ
    hd = NSA_HEAD_DIM
    nbp = kc.shape[2]
    assert n_sel <= 64 and t % PROMPT_SEL_TK == 0 and t >= WINDOW + tq and PROMPT_SEL_TK % tq == 0
    rows = NSA_GROUP * tq
    at = _imp_to_sel_matrix(64, nbp)
    kern = functools.partial(_nsa_prompt_kernel, tq=tq, n_sel=n_sel, topk=min(NSA_TOPK, n_sel))
    full = lambda n, w: pl.BlockSpec((1, 1, n, w), lambda b, kh, i: (b, kh, 0, 0))
    per_head = pl.BlockSpec((1, rows, LANES), lambda b, kh, i: (kh, 0, 0))
    gw = NSA_GROUP * hd
    return pl.pallas_call(
        kern,
        grid=(bsz, NSA_KV_HEADS, t // tq),
        in_specs=[pl.BlockSpec((1, 1, NSA_GROUP, tq, LANES), lambda b, kh, i: (b, kh, 0, i, 0)),
                  per_head, per_head, per_head,
                  full(nbp, hd), full(nbp, hd), full(t, 2 * LANES), full(t, LANES), full(t, LANES), full(t, LANES),
                  pl.BlockSpec((64, nbp), lambda b, kh, i: (0, 0)),
                  pl.BlockSpec((1, tq, LANES), lambda b, kh, i: (b, i, nbg_col)),
                  pl.BlockSpec((3, LANES, gw), lambda b, kh, i: (0, 0, kh))],
        out_specs=pl.BlockSpec((1, tq, gw), lambda b, kh, i: (b, i, kh)),
        out_shape=jax.ShapeDtypeStruct((bsz, t, NSA_WIDTH), MXU_DT),
        scratch_shapes=[pltpu.VMEM((rows, LANES), F32), pltpu.VMEM((rows, LANES), F32),
                        pltpu.VMEM((3, tq, gw), F32)],
        compiler_params=_cparams("parallel", "parallel", "arbitrary"),
        name="nsa_attention_prompt",
    )(q128, slopes, _slope_digits(tq, SEL_TK, 0), _slope_digits(tq, WIN_TK, hd),
      kc, vc, ks_aug, vs, kw_aug, vw, at, h3, gate_expand)


def _nsa_paged_pre_kernel(q_ref, sl_ref, kc_ref, vc_ref, kw_ref, vw_ref, at_ref,
                          ocmp_ref, owin_ref, sel_ref, *, tq, qpos0, wpos0, n_sel, topk):
    rows = NSA_GROUP * tq
    hd = NSA_HEAD_DIM
    q = q_ref[0, 0].reshape(rows, hd)
    slope = sl_ref[0][:, :1]
    nbp = kc_ref.shape[2]
    tw = kw_ref.shape[2]

    s = _dot_nt(q, kc_ref[0, 0])
    tok = qpos0 + (_iota((rows, nbp), 0) & (tq - 1))
    dist = tok - (_iota((rows, nbp), 1) * CMP_STRIDE + (CMP_BLOCK - 1))
    valid = dist >= 0
    s = jnp.where(valid, s - slope * dist.astype(F32), NEG_BIG)
    p = jnp.where(valid, jnp.exp(s - jnp.max(s, axis=-1, keepdims=True)), 0.0)
    p = p / jnp.maximum(jnp.sum(p, axis=-1, keepdims=True), 1e-30)
    o_cmp = _dot(p.astype(MXU_DT), vc_ref[0, 0])
    sel_ref[0, 0] = _topk_block_mask(p, at_ref, qpos0, tq=tq, tl=sel_ref.shape[3], n_sel=n_sel, topk=topk)

    sw = _dot_nt(q, kw_ref[0, 0])
    d = (qpos0 - wpos0) + (_iota((rows, tw), 0) & (tq - 1)) - _iota((rows, tw), 1)
    valid = (d >= 0) & (d < WINDOW)
    sw = jnp.where(valid, sw - slope * d.astype(F32), NEG_BIG)
    pw = jnp.where(valid, jnp.exp(sw - jnp.max(sw, axis=-1, keepdims=True)), 0.0)
    o_win = _dot(pw.astype(MXU_DT), vw_ref[0, 0]) / jnp.maximum(jnp.sum(pw, axis=-1, keepdims=True), 1e-30)

    for g in range(NSA_GROUP):
        ocmp_ref[0, :, g * hd:(g + 1) * hd] = o_cmp[g * tq:(g + 1) * tq]
        owin_ref[0, :, g * hd:(g + 1) * hd] = o_win[g * tq:(g + 1) * tq]


def _nsa_paged_sel_kernel(*refs, n_in, tq, qpos0, n_tiles):
    refs = refs[1:]
    q_ref, sl_ref, sel_ref = refs[:3]
    pages = refs[3:3 + n_in]
    (ktn_ref, vtn_ref, gain_ref, ocmp_ref, owin_ref, nbg_ref, gx_ref,
     o_ref, m_ref, l_ref, acc_ref, kt_ref, vt_ref, br_ref) = refs[3 + n_in:]
    rows = NSA_GROUP * tq
    hd = NSA_HEAD_DIM
    tk = n_in * PAGE_SIZE
    nblk = tk // SEL_BLOCK
    i = pl.program_id(1)

    @pl.when(i == 0)
    def _():
        m_ref[...] = jnp.full(m_ref.shape, NEG_BIG, F32)
        l_ref[...] = jnp.zeros(l_ref.shape, F32)
        acc_ref[...] = jnp.zeros(acc_ref.shape, F32)

    @pl.when(i < n_tiles)
    def _():
        for u in range(n_in):
            sl = slice(u * PAGE_SIZE, (u + 1) * PAGE_SIZE)
            kt_ref[:, sl] = _head_rms_t(pages[u][0, 0], gain_ref[...]).astype(kt_ref.dtype)
            vt_ref[:, sl] = pages[u][0, 1].astype(vt_ref.dtype)

    @pl.when(i == n_tiles)
    def _():
        kt_ref[...] = ktn_ref[0]
        vt_ref[...] = vtn_ref[0]

    k0 = i * tk
    d = (qpos0 - k0) + (_iota((rows, tk), 0) & (tq - 1)) - _iota((rows, tk), 1)
    causal = d >= 0
    dist = d.astype(F32)
    spread = ((_iota((rows, LANES), 0) & (tq - 1)) == _iota((rows, LANES), 1)).astype(MXU_DT)
    in_tile = (_iota((nblk, tk), 0) == (_iota((nblk, tk), 1) >> 6)).astype(MXU_DT)
    blk0 = pl.multiple_of(i * nblk, nblk)
    reps = tk // LANES
    for kvh in range(NSA_KV_HEADS):
        hs = slice(kvh * hd, (kvh + 1) * hd)
        q = q_ref[0, kvh].reshape(rows, hd)
        s = _dot(q, kt_ref[hs, :])
        mine = _dot_nt(spread, sel_ref[0, kvh, pl.ds(blk0, nblk), :].astype(MXU_DT)).astype(MXU_DT)
        valid = (_dot(mine, in_tile) > 0.5) & causal
        s = jnp.where(valid, s - sl_ref[kvh][:, :1] * dist, NEG_BIG)
        m_prev = m_ref[kvh]
        m_new = jnp.maximum(m_prev, jnp.max(s, axis=-1, keepdims=True))
        alpha = jnp.exp(m_prev - m_new)
        p = jnp.where(valid, jnp.exp(s - jnp.concatenate([m_new] * reps, axis=1)), 0.0)
        l_ref[kvh] = alpha * l_ref[kvh] + jnp.sum(p, axis=-1, keepdims=True)
        acc_ref[kvh] = alpha[:, :hd] * acc_ref[kvh] + _dot_nt(p.astype(MXU_DT), vt_ref[hs, :])
        m_ref[kvh] = m_new

    @pl.when(i == n_tiles)
    def _():
        br_ref[0] = ocmp_ref[0]
        br_ref[2] = owin_ref[0]
        for kvh in range(NSA_KV_HEADS):
            o_sel = acc_ref[kvh] / jnp.maximum(l_ref[kvh][:, :hd], 1e-30)
            for g in range(NSA_GROUP):
                c0 = (kvh * NSA_GROUP + g) * hd
                br_ref[1, :, c0:c0 + hd] = o_sel[g * tq:(g + 1) * tq]
        pieces = _split(_sigmoid(nbg_ref[0]), 2)
        acc = None
        for br in range(3):
            term = sum(_dot(p, gx_ref[br]) for p in pieces) * br_ref[br]
            acc = term if acc is None else acc + term
        o_ref[0] = acc.astype(o_ref.dtype)


def nsa_attention_paged(qn, slopes, kc, vc, kw, vw, pool_t, pages, kt_new, vt_new, gain_sel, h3, nbg_col,
                        gate_expand, *, tq, qpos0, wpos0, n_sel):
    bsz, _, _, tql, hd = qn.shape
    assert tql == tq
    nbp, tw = kc.shape[2], kw.shape[2]
    n_pages = pages.shape[1]
    tk = kt_new.shape[2]
    n_in = tk // PAGE_SIZE
    n_tiles = n_pages // n_in
    assert n_pages % n_in == 0 and qpos0 == n_pages * PAGE_SIZE
    n_sel_pad = -(-max(n_sel, (n_tiles + 1) * tk // SEL_BLOCK) // 64) * 64
    rows = NSA_GROUP * tq
    gw = NSA_GROUP * hd
    at = _imp_to_sel_matrix(n_sel_pad, nbp)
    full = lambda n: pl.BlockSpec((1, 1, n, hd), lambda b, kh: (b, kh, 0, 0))
    obr = pl.BlockSpec((1, tq, gw), lambda b, kh: (b, 0, kh))
    o_cmp, o_win, sel = pl.pallas_call(
        functools.partial(_nsa_paged_pre_kernel, tq=tq, qpos0=qpos0, wpos0=wpos0, n_sel=n_sel,
                          topk=min(NSA_TOPK, n_sel)),
        grid=(bsz, NSA_KV_HEADS),
        in_specs=[pl.BlockSpec((1, 1, NSA_GROUP, tq, hd), lambda b, kh: (b, kh, 0, 0, 0)),
                  pl.BlockSpec((1, rows, LANES), lambda b, kh: (kh, 0, 0)),
                  full(nbp), full(nbp), full(tw), full(tw),
                  pl.BlockSpec((n_sel_pad, nbp), lambda b, kh: (0, 0))],
        out_specs=[obr, obr, pl.BlockSpec((1, 1, n_sel_pad, LANES), lambda b, kh: (b, kh, 0, 0))],
        out_shape=[jax.ShapeDtypeStruct((bsz, tq, NSA_WIDTH), F32), jax.ShapeDtypeStruct((bsz, tq, NSA_WIDTH), F32),
                   jax.ShapeDtypeStruct((bsz, NSA_KV_HEADS, n_sel_pad, LANES), F32)],
        compiler_params=_cparams("parallel", "parallel"),
        name="nsa_paged_pre",
    )(qn, slopes, kc, vc, kw, vw, at)

    page_spec = lambda u: pl.BlockSpec(
        (1, 2, NSA_KV_WIDTH, PAGE_SIZE), lambda b, i, pt: (pt[b, jnp.minimum(i, n_tiles - 1) * n_in + u], 1, 0, 0))
    const = lambda *shape: pl.BlockSpec(shape, lambda b, i, pt: (0,) * len(shape))
    per_seq = lambda *shape: pl.BlockSpec((1,) + shape, lambda b, i, pt: (b,) + (0,) * len(shape))
    return pl.pallas_call(
        functools.partial(_nsa_paged_sel_kernel, n_in=n_in, tq=tq, qpos0=qpos0, n_tiles=n_tiles),
        grid_spec=pltpu.PrefetchScalarGridSpec(
            num_scalar_prefetch=1,
            grid=(bsz, n_tiles + 1),
            in_specs=[per_seq(NSA_KV_HEADS, NSA_GROUP, tq, hd), const(NSA_KV_HEADS, rows, LANES),
                      per_seq(NSA_KV_HEADS, n_sel_pad, LANES)]
                     + [page_spec(u) for u in range(n_in)]
                     + [per_seq(NSA_KV_WIDTH, tk), per_seq(NSA_KV_WIDTH, tk), const(NSA_KV_WIDTH, LANES),
                        per_seq(tq, NSA_WIDTH), per_seq(tq, NSA_WIDTH),
                        pl.BlockSpec((1, tq, LANES), lambda b, i, pt: (b, 0, nbg_col)),
                        const(3, LANES, NSA_WIDTH)],
            out_specs=per_seq(tq, NSA_WIDTH),
            scratch_shapes=[pltpu.VMEM((NSA_KV_HEADS, rows, LANES), F32), pltpu.VMEM((NSA_KV_HEADS, rows, LANES), F32),
                            pltpu.VMEM((NSA_KV_HEADS, rows, hd), F32),
                            pltpu.VMEM((NSA_KV_WIDTH, tk), MXU_DT), pltpu.VMEM((NSA_KV_WIDTH, tk), MXU_DT),
                            pltpu.VMEM((3, tq, NSA_WIDTH), F32)]),
        out_shape=jax.ShapeDtypeStruct((bsz, tq, NSA_WIDTH), MXU_DT),
        compiler_params=_cparams("parallel", "arbitrary"),
        name="nsa_paged_sel",
    )(pages, qn, slopes, sel, *([pool_t] * n_in), kt_new, vt_new, _gain_t(gain_sel),
      o_cmp, o_win, h3, gate_expand)


def _layout(d_model):
    hw, rw = HG_HEADS * HG_DK, RET_HEADS * RET_DK
    names = [("hq", hw), ("hf", hw), ("hi", hw), ("hg", hw), ("nq", NSA_WIDTH), ("nkv", 4 * NSA_KV_WIDTH),
             ("nwkv", 2 * NSA_KV_WIDTH), ("rq", rw), ("rk", rw), ("rv", rw), ("rg", rw), ("mg", 3 * d_model),
             ("nbg", 3 * NSA_HEADS)]
    lay, off = {}, 0
    for name, width in names:
        lay[name] = off
        off += width
    lay["used"] = off
    lay["total"] = -(-off // 1024) * 1024
    return lay


def _reorder_w_in(w_in_t, lay):
    a = lay["rq"]
    nb = 3 * NSA_HEADS
    d = w_in_t.shape[1]
    pad = jnp.zeros((lay["total"] - lay["used"], d), w_in_t.dtype)
    return jnp.concatenate([w_in_t[:a], w_in_t[a + nb:], w_in_t[a:a + nb], pad], axis=0).astype(MXU_DT)


def _block_sum_matrix(width):
    i = jnp.arange(width)
    return (i[:, None] // NSA_HEAD_DIM == i[None, :] // NSA_HEAD_DIM).astype(MXU_DT)


def _gate_expand_matrix():
    c = jnp.arange(LANES)[:, None]
    col = jnp.arange(NSA_WIDTH)[None, :]
    return jnp.stack([(c == (col // NSA_HEAD_DIM) * 3 + br) for br in range(3)]).astype(MXU_DT)


def _slope_rows(tq):
    hh = jnp.arange(1, NSA_HEADS + 1, dtype=F32)
    slopes = jnp.exp2(-8.0 * hh / NSA_HEADS).reshape(NSA_KV_HEADS, NSA_GROUP)
    rows = jnp.repeat(slopes, tq, axis=1)
    return jnp.broadcast_to(rows[:, :, None], (NSA_KV_HEADS, NSA_GROUP * tq, LANES))


def _pad_rows(x, n):
    return jnp.pad(x, ((0, 0), (0, n - x.shape[1])) + ((0, 0),) * (x.ndim - 2))


def _trunk_layer(x, lw, lay, consts, nsa_fn, hg_state, ret_state, *, seq_tiles):
    bsz, t, d = x.shape
    tm, tt_h, tt_r, t_valid = seq_tiles
    x2 = x.reshape(bsz * t, d)
    h = rms_matmul(x2, lw["norm_attn"], lw["w_in"], tm=tm, tn=1024)
    np_ = h.shape[1]
    h3 = h.reshape(bsz, t, np_)
    tp = -(-t // tt_h) * tt_h
    h3p = _pad_rows(h3, tp) if tp != t else h3
    oa, hg_new = hgrn_mixer(h3p, lay, lw["lb"], lw["hg_out_norm"], hg_state, tt=tt_h,
                            t_valid=None if tp == t else t_valid)
    tpr = -(-t // tt_r) * tt_r
    h3r = _pad_rows(h3, tpr) if tpr != t else h3
    oc, ret_new = retention_mixer(h3r, lay, consts["lg_tab"], ret_state, tt=tt_r,
                                  n_valid=tt_r if tpr == t else t_valid)
    ob = nsa_fn(h3)
    flat = lambda a: a[:, :t].reshape(bsz * t, a.shape[-1])
    merged = merge_branches(h, lay, flat(oa), flat(ob), flat(oc),
                            lw["w_branch_hg"], lw["w_branch_nsa"], lw["w_branch_ret"], tm=tm, tn=512)
    x2 = matmul_res(merged, lw["w_out"], x2, tm=tm, tn=512)
    hid = ffn_up(x2, lw["norm_ffn"], lw["w_gate"], lw["w_up"], tm=tm, tn=512)
    x2 = matmul_res(hid, lw["w_down"], x2, tm=tm, tn=512)
    return x2.reshape(bsz, t, d), h3, hg_new, ret_new


def kernel(x_prompt, x_sample, cache_nsa, cache_win, state_hgrn, state_ret, page_table,
           norm_attn, w_in, hgrn_lb_logits, hgrn_out_norm, nsa_q_norm, nsa_k_norm, nsa_cmp_w,
           w_branch_hg, w_branch_nsa, w_branch_ret, w_out, norm_ffn, w_gate, w_up, w_down):
    depth = w_in.shape[0]
    bp, tp, d = x_prompt.shape
    bs, ts, _ = x_sample.shape
    n_pool = cache_nsa.shape[1]
    n_pages = page_table.shape[1]
    past = n_pages * PAGE_SIZE
    wbuf = cache_win.shape[2]
    lay = _layout(d)

    sm = jax.nn.softmax(hgrn_lb_logits.astype(F32), axis=0)
    lower_bounds = jnp.clip(jnp.cumsum(sm, axis=0) - sm[0:1], 0.0, 1.0 - 1e-6)

    tt_p = 256 if tp % 256 == 0 else tp
    lg = jnp.log1p(-jnp.exp2(-5.0 - jnp.arange(RET_HEADS, dtype=F32)))
    consts = {
        "gate_expand": _gate_expand_matrix(),
        "lg_tab": jnp.broadcast_to(lg[:, None, None], (RET_HEADS, 1, max(tt_p, LANES))),
    }
    bsum = _block_sum_matrix(NSA_KV_WIDTH)
    pool_t = jnp.transpose(cache_nsa, (0, 1, 3, 4, 5, 2)).reshape(depth * n_pool, 4, NSA_KV_WIDTH, PAGE_SIZE)
    win3 = cache_win.reshape(depth, bs, wbuf, 2 * NSA_KV_WIDTH)

    ts_pad = 16
    tq_s = 16
    n_sel_p = -(-tp // SEL_BLOCK)
    n_sel_s = past // SEL_BLOCK + -(-ts // SEL_BLOCK)
    tk_tail = 2048 if past % 2048 == 0 else 512
    tq_p = 128 if tp % 128 == 0 else tp

    xp, xs = x_prompt, x_sample
    outs = {k: [] for k in ("kv_p", "kv_s", "win_p", "win_s", "hg_p", "hg_s", "ret_p", "ret_s")}
    for l in range(depth):
        lw = {
            "norm_attn": norm_attn[l], "w_in": _reorder_w_in(jnp.swapaxes(w_in[l], 0, 1), lay),
            "lb": lower_bounds[l].reshape(1, -1),
            "hg_out_norm": hgrn_out_norm[l],
            "w_branch_hg": w_branch_hg[l].astype(MXU_DT), "w_branch_nsa": w_branch_nsa[l].astype(MXU_DT),
            "w_branch_ret": w_branch_ret[l].astype(MXU_DT), "w_out": w_out[l].astype(MXU_DT),
            "norm_ffn": norm_ffn[l], "w_gate": w_gate[l].astype(MXU_DT), "w_up": w_up[l].astype(MXU_DT),
            "w_down": w_down[l].astype(MXU_DT),
        }
        gq, gk, cw = nsa_q_norm[l], nsa_k_norm[l], nsa_cmp_w[l]
        col_cmp = lay["nkv"] // (2 * NSA_KV_WIDTH)
        col_sel = col_cmp + 1
        col_win = lay["nwkv"] // (2 * NSA_KV_WIDTH)
        nbg_col = lay["nbg"] // LANES

        def nsa_prompt(h3):
            tr = 512 if tp % 512 == 0 else tp
            zeros16 = jnp.zeros((bp, CMP_STRIDE, 2 * NSA_KV_WIDTH), F32)
            kc, vc = cmp_prep(h3, col_cmp, zeros16, cw, gk[0], bsum, tr=tr)
            if tp % PROMPT_SEL_TK == 0 and n_sel_p <= 64 and tp >= WINDOW + PROMPT_TQ:
                qn = q_prep(h3, lay, gq, bsum, tr=tr, width=LANES)
                ks, vs = kv_prep(h3, col_sel, gk[1], bsum, tr=tr,
                                 aug=_key_digits(tp, SEL_TK, LANES, 2 * LANES, True))
                kw, vw = kv_prep(h3, col_win, gk[2], bsum, tr=tr,
                                 aug=_key_digits(tp, WIN_TK, NSA_HEAD_DIM, LANES, False))
                return nsa_attention_prompt(qn, _slope_rows(PROMPT_TQ), kc, vc, ks, vs, kw, vw,
                                            h3, nbg_col, consts["gate_expand"], n_sel=n_sel_p)
            qn = q_prep(h3, lay, gq, bsum, tr=tr)
            ks, vs = kv_prep(h3, col_sel, gk[1], bsum, tr=tr)
            kw, vw = kv_prep(h3, col_win, gk[2], bsum, tr=tr)
            return nsa_attention(qn, _slope_rows(tq_p), kc, vc, ks, vs, kw, vw, h3, nbg_col, consts["gate_expand"],
                                 tq=tq_p, sel_tk=SEL_TK, qpos0=0, wpos0=0, n_sel=n_sel_p)

        def nsa_sample(h3):
            pages = page_table + l * n_pool
            h16 = _pad_rows(h3, ts_pad)
            qn = q_prep(h16, lay, gq, bsum, tr=ts_pad)
            w2 = 2 * NSA_KV_WIDTH
            tail16 = h16[:, :, lay["nkv"]:lay["nkv"] + w2].reshape(bs, ts_pad, 2, NSA_KV_WIDTH)
            tail_t = jnp.pad(jnp.transpose(tail16, (0, 2, 3, 1)), ((0, 0), (0, 0), (0, 0), (0, PAGE_SIZE - ts_pad)))
            kc, vc = cmp_prep_paged(pool_t, pages, tail_t, cw, gk[0])
            htail = _pad_rows(h3[:, :, col_sel * w2:(col_sel + 1) * w2], tk_tail)
            ks_tail, vs_tail = kv_prep(htail, 0, gk[1], bsum, tr=tk_tail)
            feature_major = lambda a: jnp.swapaxes(a, 2, 3).reshape(bs, NSA_KV_WIDTH, tk_tail)
            kw_old, vw_old = kv_prep(win3[l], 0, gk[2], bsum, tr=wbuf)
            hw_new = _pad_rows(h3[:, :, col_win * w2:(col_win + 1) * w2], WIN_TK)
            kw_new, vw_new = kv_prep(hw_new, 0, gk[2], bsum, tr=WIN_TK)
            kw = jnp.concatenate([kw_old, kw_new], axis=2)
            vw = jnp.concatenate([vw_old, vw_new], axis=2)
            return nsa_attention_paged(qn, _slope_rows(tq_s), kc, vc, kw, vw, pool_t, pages,
                                       feature_major(ks_tail), feature_major(vs_tail), gk[1], h16, nbg_col,
                                       consts["gate_expand"], tq=tq_s, qpos0=past, wpos0=past - wbuf, n_sel=n_sel_s)

        zeros_state = jnp.zeros((bp, HG_HEADS, HG_DK, HG_DV), F32)
        tm_p = 1024 if (bp * tp) % 1024 == 0 else bp * tp
        xp, h3p, hgp, rtp = _trunk_layer(xp, lw, lay, consts, nsa_prompt, zeros_state, zeros_state,
                                         seq_tiles=(tm_p, tt_p, tt_p, tp))
        xs, h3s, hgs, rts = _trunk_layer(xs, lw, lay, consts, nsa_sample, state_hgrn[l], state_ret[l],
                                         seq_tiles=(bs * ts, ts_pad, ts_pad, ts))

        kv_cols = slice(lay["nkv"], lay["nkv"] + 4 * NSA_KV_WIDTH)
        win_cols = slice(lay["nwkv"], lay["nwkv"] + 2 * NSA_KV_WIDTH)
        outs["kv_p"].append(h3p[:, :, kv_cols].reshape(bp, tp, 4, NSA_KV_HEADS, NSA_HEAD_DIM))
        outs["kv_s"].append(h3s[:, :, kv_cols].reshape(bs, ts, 4, NSA_KV_HEADS, NSA_HEAD_DIM))
        wlen = min(WINDOW, tp)
        outs["win_p"].append(h3p[:, tp - wlen:, win_cols].reshape(bp, wlen, 2, NSA_KV_HEADS, NSA_HEAD_DIM))
        ctx = jnp.concatenate([win3[l], h3s[:, :, win_cols]], axis=1)[:, ts:]
        outs["win_s"].append(ctx.reshape(bs, wbuf, 2, NSA_KV_HEADS, NSA_HEAD_DIM))
        outs["hg_p"].append(hgp); outs["hg_s"].append(hgs)
        outs["ret_p"].append(rtp); outs["ret_s"].append(rts)

    st = lambda k: jnp.stack(outs[k])
    return (xp, xs, st("kv_p"), st("kv_s"), st("win_p"), st("win_s"),
            st("hg_p"), st("hg_s"), st("ret_p"), st("ret_s"))
```

```python
import functools

import jax
import jax.numpy as jnp
from jax import lax
from jax.experimental import pallas as pl
from jax.experimental.pallas import tpu as pltpu

F32 = jnp.float32
MXU_DT = jnp.bfloat16

HG_HEADS, HG_DK, HG_DV = 8, 128, 128
NSA_HEADS, NSA_KV_HEADS, NSA_GROUP, NSA_HEAD_DIM = 16, 4, 4, 64
NSA_WIDTH = NSA_HEADS * NSA_HEAD_DIM
NSA_KV_WIDTH = NSA_KV_HEADS * NSA_HEAD_DIM
CMP_BLOCK, CMP_STRIDE, SEL_BLOCK, NSA_TOPK, WINDOW = 32, 16, 64, 16, 512
RET_HEADS, RET_DK, RET_DV = 8, 128, 128
PAGE_SIZE = 128
EPS = 1e-6
NEG_BIG = -1e30
POS_BIG = 1e30
MIN_F = 1e-20
LOWEST = -3.0e38

LANES = 128
VMEM_LIMIT_BYTES = 56 * 1024 * 1024

HG_SUB_SHIFT = 4
HG_SUB = 1 << HG_SUB_SHIFT
PAGES_PER_STEP = 8
SEL_TK = 256
PROMPT_SEL_TK = 512
PROMPT_TQ = 256
RET_HPS = 2
WIN_TK = 128


def _cparams(*sem):
    return pltpu.CompilerParams(dimension_semantics=sem, vmem_limit_bytes=VMEM_LIMIT_BYTES)


def _dot(a, b):
    return jnp.dot(a, b, preferred_element_type=F32)


def _dot_nt(a, b):
    return lax.dot_general(a, b, (((1,), (1,)), ((), ())), preferred_element_type=F32)


def _split(x, n):
    out = []
    r = x
    for _ in range(n):
        p = r.astype(MXU_DT)
        out.append(p)
        r = r - p.astype(F32)
    return out


def _sigmoid(x):
    return 1.0 / (1.0 + jnp.exp(-x))


def _silu(x):
    return x * _sigmoid(x)


def _iota(shape, dim):
    return lax.broadcasted_iota(jnp.int32, shape, dim)


def _eye(n):
    return (_iota((n, n), 0) == _iota((n, n), 1)).astype(MXU_DT)


def _transpose_exact(x):
    eye = _eye(x.shape[1])
    return sum(_dot_nt(eye, p) for p in _split(x, 3))


def _head_rms(x, bsum, gain):
    ssq = sum(_dot(p, bsum) for p in _split(x * x, 2))
    return x * lax.rsqrt(ssq * (1.0 / NSA_HEAD_DIM) + EPS) * gain


def _rms_matmul_kernel(x_ref, g_ref, w_ref, o_ref, xn_ref):
    @pl.when(pl.program_id(1) == 0)
    def _():
        x = x_ref[...]
        ms = jnp.mean(x * x, axis=-1, keepdims=True)
        xn_ref[...] = (x * lax.rsqrt(ms + EPS) * g_ref[...]).astype(xn_ref.dtype)

    o_ref[...] = _dot_nt(xn_ref[...], w_ref[...])


def rms_matmul(x, gain, wt, lay, *, tm, tn):
    m, d = x.shape
    a = lay["rq"]
    gates = 3 * NSA_HEADS
    assert a % tn == 0 and (lay["used"] - a - gates) % tn == 0 and lay["total"] - lay["nbg"] == tn
    nb1 = a // tn
    nb_all = lay["total"] // tn

    def w_row(i, j):
        row = jnp.where(j < nb1, j * tn, jnp.where(j < nb_all - 1, j * tn + gates, a))
        return (pl.multiple_of(row, 16), 0)

    return pl.pallas_call(
        _rms_matmul_kernel,
        grid=(m // tm, nb_all),
        in_specs=[pl.BlockSpec((tm, d), lambda i, j: (i, 0)),
                  pl.BlockSpec((1, d), lambda i, j: (0, 0)),
                  pl.BlockSpec((pl.Element(tn), pl.Element(d)), w_row)],
        out_specs=pl.BlockSpec((tm, tn), lambda i, j: (i, j)),
        out_shape=jax.ShapeDtypeStruct((m, lay["total"]), F32),
        scratch_shapes=[pltpu.VMEM((tm, d), MXU_DT)],
        compiler_params=_cparams("parallel", "arbitrary"),
        name="in_proj",
    )(x, gain.reshape(1, d), wt)


def _ffn_up_kernel(x_ref, g_ref, wg_ref, wu_ref, o_ref, xn_ref):
    @pl.when(pl.program_id(1) == 0)
    def _():
        x = x_ref[...]
        ms = jnp.mean(x * x, axis=-1, keepdims=True)
        xn_ref[...] = (x * lax.rsqrt(ms + EPS) * g_ref[...]).astype(xn_ref.dtype)

    xn = xn_ref[...]
    o_ref[...] = (_silu(_dot(xn, wg_ref[...])) * _dot(xn, wu_ref[...])).astype(o_ref.dtype)


def ffn_up(x, gain, wg, wu, *, tm, tn):
    m, d = x.shape
    f = wg.shape[1]
    return pl.pallas_call(
        _ffn_up_kernel,
        grid=(m // tm, f // tn),
        in_specs=[pl.BlockSpec((tm, d), lambda i, j: (i, 0)),
                  pl.BlockSpec((1, d), lambda i, j: (0, 0)),
                  pl.BlockSpec((d, tn), lambda i, j: (0, j)),
                  pl.BlockSpec((d, tn), lambda i, j: (0, j))],
        out_specs=pl.BlockSpec((tm, tn), lambda i, j: (i, j)),
        out_shape=jax.ShapeDtypeStruct((m, f), MXU_DT),
        scratch_shapes=[pltpu.VMEM((tm, d), MXU_DT)],
        compiler_params=_cparams("parallel", "arbitrary"),
        name="ffn_up",
    )(x, gain.reshape(1, d), wg, wu)


def _matmul_res_kernel(a_ref, w_ref, r_ref, o_ref):
    o_ref[...] = r_ref[...] + _dot(a_ref[...], w_ref[...])


def matmul_res(a, w, res, *, tm, tn):
    m, k = a.shape
    n = w.shape[1]
    return pl.pallas_call(
        _matmul_res_kernel,
        grid=(m // tm, n // tn),
        in_specs=[pl.BlockSpec((tm, k), lambda i, j: (i, 0)),
                  pl.BlockSpec((k, tn), lambda i, j: (0, j)),
                  pl.BlockSpec((tm, tn), lambda i, j: (i, j))],
        out_specs=pl.BlockSpec((tm, tn), lambda i, j: (i, j)),
        out_shape=jax.ShapeDtypeStruct((m, n), F32),
        compiler_params=_cparams("parallel", "arbitrary"),
        name="matmul_res",
    )(a, w, res)


def _merge_kernel(oa_ref, ob_ref, oc_ref, ga_ref, gb_ref, gc_ref, wa_ref, wb_ref, wc_ref, o_ref):
    o_ref[...] = (_sigmoid(ga_ref[...]) * _dot(oa_ref[...], wa_ref[...])
                  + _sigmoid(gb_ref[...]) * _dot(ob_ref[...], wb_ref[...])
                  + _sigmoid(gc_ref[...]) * _dot(oc_ref[...], wc_ref[...])).astype(o_ref.dtype)


def merge_branches(h, lay, oa, ob, oc, wa, wb, wc, *, tm, tn):
    m = h.shape[0]
    d = wa.shape[1]
    wdt = oa.shape[1]
    ca, cb, cc = ((lay["mg"] + i * d) // tn for i in range(3))
    row = lambda i, j: (i, 0)
    return pl.pallas_call(
        _merge_kernel,
        grid=(m // tm, d // tn),
        in_specs=[pl.BlockSpec((tm, wdt), row), pl.BlockSpec((tm, wdt), row), pl.BlockSpec((tm, wdt), row),
                  pl.BlockSpec((tm, tn), lambda i, j: (i, ca + j)),
                  pl.BlockSpec((tm, tn), lambda i, j: (i, cb + j)),
                  pl.BlockSpec((tm, tn), lambda i, j: (i, cc + j)),
                  pl.BlockSpec((wdt, tn), lambda i, j: (0, j)),
                  pl.BlockSpec((wdt, tn), lambda i, j: (0, j)),
                  pl.BlockSpec((wdt, tn), lambda i, j: (0, j))],
        out_specs=pl.BlockSpec((tm, tn), lambda i, j: (i, j)),
        out_shape=jax.ShapeDtypeStruct((m, d), MXU_DT),
        compiler_params=_cparams("parallel", "arbitrary"),
        name="merge",
    )(oa, ob, oc, h, h, h, wa, wb, wc)


def _hgrn_kernel(hq_ref, hf_ref, hi_ref, hg_ref, lb_ref, gain_ref, s0_ref, o_ref, sout_ref,
                 st_ref, ut_ref, *, tt, t_valid):
    c = HG_SUB
    nj = tt // c
    tb = pl.program_id(2)

    @pl.when(tb == 0)
    def _():
        st_ref[...] = _transpose_exact(s0_ref[0, 0])

    hq = hq_ref[0]
    lb = lb_ref[...]
    q = _silu(hq)
    sig = _sigmoid(hf_ref[0])
    g = jnp.log(jnp.maximum(lb + (1.0 - lb) * sig, MIN_F))
    kk = (1.0 - lb) * (1.0 - sig)
    v = hi_ref[0]
    row = _iota((tt, HG_DK), 0)
    if t_valid is not None:
        live = (tb * tt + row) < t_valid
        g = jnp.where(live, g, 0.0)
        kk = jnp.where(live, kk, 0.0)

    r2 = _iota((tt, tt), 0)
    c2 = _iota((tt, tt), 1)
    same = (r2 >> HG_SUB_SHIFT) == (c2 >> HG_SUB_SHIFT)
    gp = _split(g, 3)
    tri = (same & (c2 <= r2)).astype(MXU_DT)
    blk = same.astype(MXU_DT)
    b = sum(_dot(tri, p) for p in gp)
    dtot = sum(_dot(blk, p) for p in gp)
    qe = q * jnp.exp(b)
    ke = kk * jnp.exp(dtot - b)
    edec = jnp.exp(dtot)

    q3 = q.reshape(nj, c, HG_DK)
    kk3 = kk.reshape(nj, c, HG_DK)
    b3 = b.reshape(nj, c, HG_DK)
    v3 = v.reshape(nj, c, HG_DV)
    tpos = _iota((nj, c, HG_DK), 1)
    ones = jnp.ones((HG_DK, HG_DV), MXU_DT)
    o3 = jnp.zeros((nj, c, HG_DV), F32)
    for s in range(c):
        m = tpos >= s
        diff = jnp.where(m, b3 - b3[:, s:s + 1, :], 0.0)
        y = jnp.where(m, q3 * kk3[:, s:s + 1, :] * jnp.exp(diff), 0.0)
        z = _dot(y.reshape(tt, HG_DK).astype(MXU_DT), ones).reshape(nj, c, HG_DV)
        o3 = o3 + z * v3[:, s:s + 1, :]
    o = o3.reshape(tt, HG_DV)

    v_b = v.astype(MXU_DT)
    ke_b = ke.astype(MXU_DT)
    for j in range(nj):
        ut_ref[j] = lax.dot_general(v_b[j * c:(j + 1) * c], ke_b[j * c:(j + 1) * c],
                                    (((0,), (0,)), ((), ())), preferred_element_type=F32)

    qe_b = qe.astype(MXU_DT)
    outs = []
    for j in range(nj):
        st = st_ref[...]
        outs.append(_dot_nt(qe_b[j * c:(j + 1) * c], st.astype(MXU_DT)))
        st_ref[...] = st * edec[j * c:j * c + 1, :] + ut_ref[j]
    o = o + jnp.concatenate(outs, axis=0)

    on = o * lax.rsqrt(jnp.mean(o * o, axis=-1, keepdims=True) + EPS) * gain_ref[...]
    o_ref[0] = (on * _silu(hg_ref[0])).astype(o_ref.dtype)

    @pl.when(tb == pl.num_programs(2) - 1)
    def _():
        sout_ref[0, 0] = _transpose_exact(st_ref[...])


def hgrn_mixer(h3, lay, lb, out_gain, s0, *, tt, t_valid):
    bsz, t, _ = h3.shape
    cq, cf, ci, cg = (lay[k] // LANES for k in ("hq", "hf", "hi", "hg"))
    col = lambda c0: (lambda b, hd, tb: (b, tb, c0 + hd))
    kern = functools.partial(_hgrn_kernel, tt=tt, t_valid=t_valid)
    return pl.pallas_call(
        kern,
        grid=(bsz, HG_HEADS, t // tt),
        in_specs=[pl.BlockSpec((1, tt, LANES), col(cq)), pl.BlockSpec((1, tt, LANES), col(cf)),
                  pl.BlockSpec((1, tt, LANES), col(ci)), pl.BlockSpec((1, tt, LANES), col(cg)),
                  pl.BlockSpec((1, HG_DK), lambda b, hd, tb: (0, hd)),
                  pl.BlockSpec((1, HG_DV), lambda b, hd, tb: (0, 0)),
                  pl.BlockSpec((1, 1, HG_DK, HG_DV), lambda b, hd, tb: (b, hd, 0, 0))],
        out_specs=[pl.BlockSpec((1, tt, HG_DV), lambda b, hd, tb: (b, tb, hd)),
                   pl.BlockSpec((1, 1, HG_DK, HG_DV), lambda b, hd, tb: (b, hd, 0, 0))],
        out_shape=[jax.ShapeDtypeStruct((bsz, t, HG_HEADS * HG_DV), MXU_DT),
                   jax.ShapeDtypeStruct((bsz, HG_HEADS, HG_DK, HG_DV), F32)],
        scratch_shapes=[pltpu.VMEM((HG_DV, HG_DK), F32),
                        pltpu.VMEM((tt // HG_SUB, HG_DV, HG_DK), F32)],
        compiler_params=_cparams("parallel", "parallel", "arbitrary"),
        name="hgrn",
    )(h3, h3, h3, h3, lb, out_gain.reshape(1, HG_DV), s0)


def _ret_kernel(q_ref, k_ref, v_ref, g_ref, lg_ref, s0_ref, o_ref, sout_ref, s_ref, *, tt, n_valid):
    tb = pl.program_id(2)

    @pl.when(tb == 0)
    def _():
        s_ref[...] = s0_ref[0]

    rel = (_iota((tt, tt), 0) - _iota((tt, tt), 1)).astype(F32)
    spos = _iota((RET_DK, tt), 1)
    for hh in range(RET_HPS):
        cols = slice(hh * RET_DK, (hh + 1) * RET_DK)
        lgl = lg_ref[hh][:, :LANES]
        lgt = lg_ref[hh][:, :tt]
        q = q_ref[0][:, cols] * (RET_DK ** -0.5)
        k = k_ref[0][:, cols]
        vb = v_ref[0][:, cols].astype(MXU_DT)
        dmat = jnp.where(rel >= 0, jnp.exp(jnp.maximum(rel, 0.0) * lgt), 0.0)
        att = _dot_nt(q.astype(MXU_DT), k.astype(MXU_DT)) * dmat
        q_dec = jnp.exp((_iota((tt, RET_DK), 0) + 1).astype(F32) * lgl)
        s = s_ref[hh]
        o = _dot(att.astype(MXU_DT), vb) + _dot((q * q_dec).astype(MXU_DT), s.astype(MXU_DT))

        k_dec = jnp.where(spos < n_valid, jnp.exp(jnp.maximum(n_valid - 1 - spos, 0).astype(F32) * lgt), 0.0)
        kt = _transpose_exact(k)
        s_ref[hh] = jnp.exp(float(n_valid) * lgl) * s + _dot((kt * k_dec).astype(MXU_DT), vb)

        mu = jnp.mean(o, axis=-1, keepdims=True)
        var = jnp.mean(jnp.square(o - mu), axis=-1, keepdims=True)
        o_ref[0, :, cols] = ((o - mu) * lax.rsqrt(var + EPS) * _silu(g_ref[0][:, cols])).astype(o_ref.dtype)

    @pl.when(tb == pl.num_programs(2) - 1)
    def _():
        sout_ref[0] = s_ref[...]


def retention_mixer(h3, lay, lg_tab, s0, *, tt, n_valid):
    bsz, t, _ = h3.shape
    assert n_valid == tt or t == tt
    hps = RET_HPS
    wd = hps * RET_DK
    cq, ck, cv, cg = (lay[k] // wd for k in ("rq", "rk", "rv", "rg"))
    col = lambda c0: (lambda b, hp, tb: (b, tb, c0 + hp))
    kern = functools.partial(_ret_kernel, tt=tt, n_valid=n_valid)
    return pl.pallas_call(
        kern,
        grid=(bsz, RET_HEADS // hps, t // tt),
        in_specs=[pl.BlockSpec((1, tt, wd), col(cq)), pl.BlockSpec((1, tt, wd), col(ck)),
                  pl.BlockSpec((1, tt, wd), col(cv)), pl.BlockSpec((1, tt, wd), col(cg)),
                  pl.BlockSpec((hps, 1, lg_tab.shape[2]), lambda b, hp, tb: (hp, 0, 0)),
                  pl.BlockSpec((1, hps, RET_DK, RET_DV), lambda b, hp, tb: (b, hp, 0, 0))],
        out_specs=[pl.BlockSpec((1, tt, hps * RET_DV), lambda b, hp, tb: (b, tb, hp)),
                   pl.BlockSpec((1, hps, RET_DK, RET_DV), lambda b, hp, tb: (b, hp, 0, 0))],
        out_shape=[jax.ShapeDtypeStruct((bsz, t, RET_HEADS * RET_DV), MXU_DT),
                   jax.ShapeDtypeStruct((bsz, RET_HEADS, RET_DK, RET_DV), F32)],
        scratch_shapes=[pltpu.VMEM((hps, RET_DK, RET_DV), F32)],
        compiler_params=_cparams("parallel", "parallel", "arbitrary"),
        name="retention",
    )(h3, h3, h3, h3, lg_tab, s0)


def _store_heads(ref, x, lead=()):
    for hh in range(NSA_KV_HEADS):
        ref[lead + (hh,)] = x[:, hh * NSA_HEAD_DIM:(hh + 1) * NSA_HEAD_DIM].astype(ref.dtype)


def _q_prep_kernel(q_ref, gain_ref, bsum_ref, o_ref):
    x = q_ref[0]
    hd = NSA_HEAD_DIM
    for kvh in range(NSA_KV_HEADS):
        lo = kvh * NSA_KV_WIDTH
        qn = _head_rms(x[:, lo:lo + NSA_KV_WIDTH], bsum_ref[...], gain_ref[...]) * (hd ** -0.5)
        for g in range(NSA_GROUP):
            o_ref[0, kvh, g, :, 0:hd] = qn[:, g * hd:(g + 1) * hd].astype(o_ref.dtype)
            if o_ref.shape[-1] > hd:
                o_ref[0, kvh, g, :, hd:] = jnp.zeros((x.shape[0], o_ref.shape[-1] - hd), o_ref.dtype)


def q_prep(h3, lay, gain_q, bsum, *, tr, width=NSA_HEAD_DIM):
    bsz, t, _ = h3.shape
    cq = lay["nq"] // NSA_WIDTH
    return pl.pallas_call(
        _q_prep_kernel,
        grid=(bsz, t // tr),
        in_specs=[pl.BlockSpec((1, tr, NSA_WIDTH), lambda b, i: (b, i, cq)),
                  pl.BlockSpec((1, NSA_KV_WIDTH), lambda b, i: (0, 0)),
                  pl.BlockSpec((NSA_KV_WIDTH, NSA_KV_WIDTH), lambda b, i: (0, 0))],
        out_specs=pl.BlockSpec((1, NSA_KV_HEADS, NSA_GROUP, tr, width), lambda b, i: (b, 0, 0, i, 0)),
        out_shape=jax.ShapeDtypeStruct((bsz, NSA_KV_HEADS, NSA_GROUP, t, width), MXU_DT),
        compiler_params=_cparams("parallel", "parallel"),
        name="nsa_q_prep",
    )(h3, jnp.tile(gain_q, NSA_KV_HEADS).reshape(1, NSA_KV_WIDTH), bsum)


def _kv_prep_kernel(x_ref, gain_ref, bsum_ref, *rest):
    ko_ref, vo_ref = rest[-2:]
    hd = NSA_HEAD_DIM
    x = x_ref[0]
    kn = _head_rms(x[:, :NSA_KV_WIDTH], bsum_ref[...], gain_ref[...])
    for hh in range(NSA_KV_HEADS):
        ko_ref[0, hh, :, 0:hd] = kn[:, hh * hd:(hh + 1) * hd].astype(ko_ref.dtype)
        if len(rest) == 3:
            ko_ref[0, hh, :, hd:] = rest[0][:, hd:]
        lo = NSA_KV_WIDTH + hh * hd
        vo_ref[0, hh, :, 0:hd] = x[:, lo:lo + hd].astype(vo_ref.dtype)
        if vo_ref.shape[-1] > hd:
            vo_ref[0, hh, :, hd:] = jnp.ones((x.shape[0], vo_ref.shape[-1] - hd), vo_ref.dtype)


def kv_prep(x3, col, gain, bsum, *, tr, aug=None):
    bsz, t, _ = x3.shape
    w2 = 2 * NSA_KV_WIDTH
    hd = NSA_HEAD_DIM
    kw = hd if aug is None else aug.shape[1]
    vw = hd if aug is None else LANES
    vshape = jax.ShapeDtypeStruct((bsz, NSA_KV_HEADS, t, vw), MXU_DT)
    kshape = jax.ShapeDtypeStruct((bsz, NSA_KV_HEADS, t, kw), MXU_DT)
    spec = lambda w: pl.BlockSpec((1, NSA_KV_HEADS, tr, w), lambda b, i: (b, 0, i, 0))
    in_specs = [pl.BlockSpec((1, tr, w2), lambda b, i: (b, i, col)),
                pl.BlockSpec((1, NSA_KV_WIDTH), lambda b, i: (0, 0)),
                pl.BlockSpec((NSA_KV_WIDTH, NSA_KV_WIDTH), lambda b, i: (0, 0))]
    args = [x3, jnp.tile(gain, NSA_KV_HEADS).reshape(1, NSA_KV_WIDTH), bsum]
    if aug is not None:
        in_specs.append(pl.BlockSpec((tr, kw), lambda b, i: (i, 0)))
        args.append(aug)
    return pl.pallas_call(
        _kv_prep_kernel,
        grid=(bsz, t // tr),
        in_specs=in_specs,
        out_specs=[spec(kw), spec(vw)],
        out_shape=[kshape, vshape],
        compiler_params=_cparams("parallel", "parallel"),
        name="nsa_kv_prep",
    )(*args)


def _head_rms_t(xt, gain_t):
    x3 = xt.reshape(NSA_KV_HEADS, NSA_HEAD_DIM, xt.shape[1])
    ms = jnp.mean(x3 * x3, axis=1, keepdims=True)
    return (x3 * lax.rsqrt(ms + EPS)).reshape(xt.shape) * gain_t


def _gain_t(gain):
    return jnp.broadcast_to(jnp.tile(gain, NSA_KV_HEADS)[:, None], (NSA_KV_WIDTH, LANES))


def _cmp_prep_kernel(*refs, n_in, rows):
    refs = refs[-(n_in + 7):]
    nxt_ref, tail_ref, w_ref, gain_ref, bsum_ref, kc_ref, vc_ref = refs[n_in:]
    cs = CMP_STRIDE
    nb = rows // cs
    w0 = w_ref[0]
    w1 = w_ref[1]
    a0 = []
    a1 = []
    for u in range(n_in):
        x3 = refs[u][0].reshape(nb, cs, 2 * NSA_KV_WIDTH)
        a0.append(jnp.sum(x3 * w0[None], axis=1))
        a1.append(jnp.sum(x3 * w1[None], axis=1))
    a0 = jnp.concatenate(a0, axis=0) if n_in > 1 else a0[0]
    a1 = jnp.concatenate(a1, axis=0) if n_in > 1 else a1[0]
    last = pl.program_id(1) == pl.num_programs(1) - 1
    nx = jnp.where(last, tail_ref[0], nxt_ref[0])
    a1_next = jnp.sum(nx * w1, axis=0, keepdims=True)
    tot = nb * n_in
    a1s = pltpu.roll(a1, tot - 1, 0)
    a1s = jnp.where(_iota(a1s.shape, 0) == tot - 1, a1_next, a1s)
    comp = a0 + a1s
    kc = _head_rms(comp[:, :NSA_KV_WIDTH], bsum_ref[...], gain_ref[...])
    _store_heads(kc_ref, kc, (0,))
    _store_heads(vc_ref, comp[:, NSA_KV_WIDTH:], (0,))


def _cmp_weight_table(cmp_w):
    w = cmp_w.reshape(2, CMP_BLOCK // CMP_STRIDE, CMP_STRIDE)
    w = jnp.transpose(w, (1, 2, 0))
    return jnp.repeat(w, NSA_KV_WIDTH, axis=2)


def cmp_prep(x3, col, tail16, cmp_w, gain, bsum, *, tr):
    bsz, t, _ = x3.shape
    w2 = 2 * NSA_KV_WIDTH
    nbt = tr // CMP_STRIDE
    n_steps = t // tr
    kern = functools.partial(_cmp_prep_kernel, n_in=1, rows=tr)
    oshape = jax.ShapeDtypeStruct((bsz, NSA_KV_HEADS, t // CMP_STRIDE, NSA_HEAD_DIM), MXU_DT)
    ospec = pl.BlockSpec((1, NSA_KV_HEADS, nbt, NSA_HEAD_DIM), lambda b, i: (b, 0, i, 0))
    chunks_per_step = tr // CMP_STRIDE
    return pl.pallas_call(
        kern,
        grid=(bsz, n_steps),
        in_specs=[pl.BlockSpec((1, tr, w2), lambda b, i: (b, i, col)),
                  pl.BlockSpec((1, CMP_STRIDE, w2),
                               lambda b, i: (b, jnp.minimum(i + 1, n_steps - 1) * chunks_per_step, col)),
                  pl.BlockSpec((1, CMP_STRIDE, w2), lambda b, i: (b, 0, 0)),
                  pl.BlockSpec((2, CMP_STRIDE, w2), lambda b, i: (0, 0, 0)),
                  pl.BlockSpec((1, NSA_KV_WIDTH), lambda b, i: (0, 0)),
                  pl.BlockSpec((NSA_KV_WIDTH, NSA_KV_WIDTH), lambda b, i: (0, 0))],
        out_specs=[ospec, ospec],
        out_shape=[oshape, oshape],
        compiler_params=_cparams("parallel", "arbitrary"),
        name="nsa_cmp_prep",
    )(x3, x3, tail16, _cmp_weight_table(cmp_w), jnp.tile(gain, NSA_KV_HEADS).reshape(1, NSA_KV_WIDTH), bsum)


def _cmp_prep_t_kernel(*refs, n_in):
    refs = refs[-(n_in + 6):]
    nxt_ref, tail_ref, w_ref, gain_ref, kc_ref, vc_ref = refs[n_in:]
    nb = w_ref.shape[2] // 4
    kvw = NSA_KV_WIDTH

    last = pl.program_id(1) == pl.num_programs(1) - 1
    nxt = jnp.where(last, tail_ref[0], nxt_ref[0])
    rows = [refs[u][0].reshape(2 * kvw, PAGE_SIZE).astype(MXU_DT) for u in range(n_in)]
    rows.append(nxt.reshape(2 * kvw, PAGE_SIZE).astype(MXU_DT))
    acc = _dot(jnp.concatenate(rows, axis=1), w_ref[...].reshape((n_in + 1) * PAGE_SIZE, 4 * nb))
    comp_k = acc[:kvw, 0:nb] + acc[:kvw, 2 * nb:3 * nb]
    comp_v = acc[kvw:, nb:2 * nb] + acc[kvw:, 3 * nb:4 * nb]
    kct = _head_rms_t(comp_k, gain_ref[...][:, :nb]).astype(MXU_DT)
    eye = _eye(nb)
    _store_heads(kc_ref, _dot_nt(eye, kct), (0,))
    _store_heads(vc_ref, _dot_nt(eye, comp_v.astype(MXU_DT)), (0,))


def _cmp_band_tables(cmp_w, n_pages_step):
    cs = CMP_STRIDE
    w = cmp_w.reshape(2, CMP_BLOCK // cs, cs)
    nb = n_pages_step * PAGE_SIZE // cs
    rho = jnp.arange(PAGE_SIZE)
    ch = (jnp.arange(n_pages_step)[:, None] * (PAGE_SIZE // cs) + rho[None, :] // cs)[..., None]
    n = jnp.arange(nb)[None, None, :]
    s = rho % cs
    tabs = []
    for c in range(2):
        full = (w[c, 0][s][None, :, None] * (ch == n) + w[c, 1][s][None, :, None] * (ch == n + 1)).astype(F32)
        nxt = jnp.where((rho[:, None] < cs) & (n[0] == nb - 1), w[c, 1][s][:, None], 0.0).astype(F32)
        tabs.append(jnp.concatenate([full, nxt[None]], axis=0))
    hi = [t.astype(MXU_DT) for t in tabs]
    lo = [(t - h.astype(F32)).astype(MXU_DT) for t, h in zip(tabs, hi)]
    return jnp.concatenate(hi + lo, axis=2)


def cmp_prep_paged(pool_t, pages, tail_t, cmp_w, gain):
    bsz, n_pages = pages.shape
    pp = PAGES_PER_STEP
    n_steps = n_pages // pp
    nbt = pp * PAGE_SIZE // CMP_STRIDE
    kern = functools.partial(_cmp_prep_t_kernel, n_in=pp)
    oshape = jax.ShapeDtypeStruct((bsz, NSA_KV_HEADS, n_pages * PAGE_SIZE // CMP_STRIDE, NSA_HEAD_DIM), MXU_DT)
    ospec = pl.BlockSpec((1, NSA_KV_HEADS, nbt, NSA_HEAD_DIM), lambda b, i, pt: (b, 0, i, 0))
    pblock = (1, 2, NSA_KV_WIDTH, PAGE_SIZE)
    page_spec = lambda u: pl.BlockSpec(pblock, lambda b, i, pt: (pt[b, i * pp + u], 0, 0, 0))
    const = lambda *shape: pl.BlockSpec(shape, lambda b, i, pt: (0,) * len(shape))
    return pl.pallas_call(
        kern,
        grid_spec=pltpu.PrefetchScalarGridSpec(
            num_scalar_prefetch=1,
            grid=(bsz, n_steps),
            in_specs=[page_spec(u) for u in range(pp)]
                     + [pl.BlockSpec(pblock, lambda b, i, pt: (pt[b, jnp.minimum(i + 1, n_steps - 1) * pp], 0, 0, 0)),
                        pl.BlockSpec(pblock, lambda b, i, pt: (b, 0, 0, 0)),
                        const(pp + 1, PAGE_SIZE, 4 * nbt), const(NSA_KV_WIDTH, LANES)],
            out_specs=[ospec, ospec]),
        out_shape=[oshape, oshape],
        compiler_params=_cparams("parallel", "arbitrary"),
        name="nsa_cmp_prep_paged",
    )(pages, *([pool_t] * pp), pool_t, tail_t, _cmp_band_tables(cmp_w, pp), _gain_t(gain))


def _gate_and_store(o_ref, br_ref, nbg_ref, gx_ref, branches, tq):
    hd = NSA_HEAD_DIM
    for br, o in enumerate(branches):
        for g in range(NSA_GROUP):
            br_ref[br, :, g * hd:(g + 1) * hd] = o[g * tq:(g + 1) * tq]
    pieces = _split(_sigmoid(nbg_ref[0]), 2)
    acc = None
    for br in range(3):
        term = sum(_dot(p, gx_ref[br]) for p in pieces) * br_ref[br]
        acc = term if acc is None else acc + term
    o_ref[0] = acc.astype(o_ref.dtype)


def _nsa_kernel(q_ref, sl_ref, kc_ref, vc_ref, ks_ref, vs_ref, kw_ref, vw_ref, at_ref, nbg_ref, gx_ref,
                o_ref, sc_ref, m_ref, l_ref, acc_ref, br_ref,
                *, tq, sel_tk, qpos0, wpos0, n_sel, topk):
    rows = NSA_GROUP * tq
    hd = NSA_HEAD_DIM
    t0 = qpos0 + pl.program_id(2) * tq
    q = q_ref[0, 0].reshape(rows, hd)
    slope = sl_ref[0][:, :1]
    nbp = kc_ref.shape[2]
    n_sel_pad, tl = sc_ref.shape

    def tok(shape):
        return t0 + (_iota(shape, 0) & (tq - 1))

    s = _dot_nt(q, kc_ref[0, 0])
    dist = tok((rows, nbp)) - (_iota((rows, nbp), 1) * CMP_STRIDE + (CMP_BLOCK - 1))
    valid = dist >= 0
    s = jnp.where(valid, s - slope * dist.astype(F32), NEG_BIG)
    p = jnp.where(valid, jnp.exp(s - jnp.max(s, axis=-1, keepdims=True)), 0.0)
    p = p / jnp.maximum(jnp.sum(p, axis=-1, keepdims=True), 1e-30)
    o_cmp = _dot(p.astype(MXU_DT), vc_ref[0, 0])

    sel = _topk_block_mask(p, at_ref, t0, tq=tq, tl=tl, n_sel=n_sel, topk=topk)
    spread = ((_iota((rows, tl), 0) & (tq - 1)) == _iota((rows, tl), 1)).astype(MXU_DT)
    nblk = sel_tk // SEL_BLOCK
    if nblk % 8 == 0:
        sc_ref[...] = sel
        in_tile = (_iota((nblk, sel_tk), 0) == (_iota((nblk, sel_tk), 1) >> 6)).astype(MXU_DT)
    else:
        selb = _dot_nt(spread, sel.astype(MXU_DT)).astype(MXU_DT)

    def softmax_pass(k_ref, v_ref, tk, lo, hi, kpos0, mask_fn):
        m_ref[...] = jnp.full(m_ref.shape, NEG_BIG, F32)
        l_ref[...] = jnp.zeros(l_ref.shape, F32)
        acc_ref[...] = jnp.zeros(acc_ref.shape, F32)
        rel = (_iota((rows, tk), 0) & (tq - 1)) - _iota((rows, tk), 1)
        reps = tk // LANES

        def body(kt, carry):
            k0 = pl.multiple_of(kt * tk, tk)
            s = _dot_nt(q, k_ref[0, 0, pl.ds(k0, tk), :])
            d = rel + (t0 - kpos0 - k0)
            valid = mask_fn(d, k0)
            s = jnp.where(valid, s - slope * d.astype(F32), NEG_BIG)
            m_prev = m_ref[...]
            m_new = jnp.maximum(m_prev, jnp.max(s, axis=-1, keepdims=True))
            alpha = jnp.exp(m_prev - m_new)
            mrep = m_new if reps == 1 else jnp.concatenate([m_new] * reps, axis=1)
            p = jnp.where(valid, jnp.exp(s - mrep), 0.0)
            l_ref[...] = alpha * l_ref[...] + jnp.sum(p, axis=-1, keepdims=True)
            acc_ref[...] = alpha[:, :hd] * acc_ref[...] + _dot(p.astype(MXU_DT), v_ref[0, 0, pl.ds(k0, tk), :])
            m_ref[...] = m_new
            return carry

        lax.fori_loop(lo, hi, body, 0)
        return acc_ref[...] / jnp.maximum(l_ref[...][:, :hd], 1e-30)

    n_kt_all = ks_ref.shape[2] // sel_tk
    hi_sel = jnp.minimum(n_kt_all, (t0 + tq - 1) // sel_tk + 1)

    def sel_mask(d, k0):
        if nblk % 8 == 0:
            blk0 = pl.multiple_of(k0 // SEL_BLOCK, nblk)
            mine = _dot_nt(spread, sc_ref[pl.ds(blk0, nblk), :].astype(MXU_DT)).astype(MXU_DT)
            return (_dot(mine, in_tile) > 0.5) & (d >= 0)
        blk_of_key = (k0 + _iota((n_sel_pad, sel_tk), 1)) >> 6
        expand = (_iota((n_sel_pad, sel_tk), 0) == blk_of_key).astype(MXU_DT)
        return (_dot(selb, expand) > 0.5) & (d >= 0)

    o_sel = softmax_pass(ks_ref, vs_ref, sel_tk, 0, hi_sel, 0, sel_mask)

    n_wt_all = kw_ref.shape[2] // WIN_TK
    lo_win = jnp.maximum(t0 - (WINDOW - 1) - wpos0, 0) // WIN_TK
    hi_win = jnp.minimum(n_wt_all, (t0 + tq - 1 - wpos0) // WIN_TK + 1)
    o_win = softmax_pass(kw_ref, vw_ref, WIN_TK, lo_win, hi_win, wpos0,
                         lambda d, k0: (d >= 0) & (d < WINDOW))

    _gate_and_store(o_ref, br_ref, nbg_ref, gx_ref, (o_cmp, o_sel, o_win), tq)


def _imp_to_sel_matrix(n_sel_pad, nbp):
    m = SEL_BLOCK // CMP_STRIDE
    r = CMP_BLOCK // CMP_STRIDE
    jj = jnp.arange(n_sel_pad)[:, None]
    ii = jnp.arange(nbp)[None, :]
    cnt = sum(((ii - rr) >= m * jj) & ((ii - rr) < m * (jj + 1)) for rr in range(r))
    return cnt.astype(MXU_DT)


def nsa_attention(qn, slopes, kc, vc, ks, vs, kw, vw, h3, nbg_col, gate_expand, *, tq, sel_tk, qpos0, wpos0, n_sel):
    bsz, _, _, tql, hd = qn.shape
    nbp, tk_all, tw_all = kc.shape[2], ks.shape[2], kw.shape[2]
    assert tq & (tq - 1) == 0 and tk_all % sel_tk == 0 and tw_all % WIN_TK == 0
    n_sel_pad = -(-max(n_sel, tk_all // SEL_BLOCK) // 64) * 64
    tl = max(tq, LANES)
    rows = NSA_GROUP * tq
    at = _imp_to_sel_matrix(n_sel_pad, nbp)
    kern = functools.partial(_nsa_kernel, tq=tq, sel_tk=sel_tk, qpos0=qpos0, wpos0=wpos0, n_sel=n_sel,
                             topk=min(NSA_TOPK, n_sel))
    full = lambda n: pl.BlockSpec((1, 1, n, hd), lambda b, kh, i: (b, kh, 0, 0))
    gw = NSA_GROUP * hd
    return pl.pallas_call(
        kern,
        grid=(bsz, NSA_KV_HEADS, tql // tq),
        in_specs=[pl.BlockSpec((1, 1, NSA_GROUP, tq, hd), lambda b, kh, i: (b, kh, 0, i, 0)),
                  pl.BlockSpec((1, rows, LANES), lambda b, kh, i: (kh, 0, 0)),
                  full(nbp), full(nbp), full(tk_all), full(tk_all), full(tw_all), full(tw_all),
                  pl.BlockSpec((n_sel_pad, nbp), lambda b, kh, i: (0, 0)),
                  pl.BlockSpec((1, tq, LANES), lambda b, kh, i: (b, i, nbg_col)),
                  pl.BlockSpec((3, LANES, gw), lambda b, kh, i: (0, 0, kh))],
        out_specs=pl.BlockSpec((1, tq, gw), lambda b, kh, i: (b, i, kh)),
        out_shape=jax.ShapeDtypeStruct((bsz, tql, NSA_WIDTH), MXU_DT),
        scratch_shapes=[pltpu.VMEM((n_sel_pad, tl), F32),
                        pltpu.VMEM((rows, LANES), F32), pltpu.VMEM((rows, LANES), F32),
                        pltpu.VMEM((rows, hd), F32), pltpu.VMEM((3, tq, gw), F32)],
        compiler_params=_cparams("parallel", "parallel", "arbitrary"),
        name="nsa_attention",
    )(qn, slopes, kc, vc, ks, vs, kw, vw, at, h3, gate_expand)


def _topk_block_mask(p, at_ref, t0, *, tq, tl, n_sel, topk):
    n_sel_pad, nbp = at_ref.shape
    imp = p[0:tq] + p[tq:2 * tq] + p[2 * tq:3 * tq] + p[3 * tq:4 * tq]
    if tl > tq:
        imp = jnp.concatenate([imp, jnp.zeros((tl - tq, nbp), F32)], axis=0)
    imp_sel = sum(_dot_nt(at_ref[...], piece) for piece in _split(imp, 3))
    j = _iota((n_sel_pad, tl), 0)
    tt = t0 + _iota((n_sel_pad, tl), 1)
    forced = (j == 0) | (j == (tt >> 6))
    allowed = (j << 6) <= tt
    score = jnp.where(forced, POS_BIG, jnp.where(allowed, imp_sel, NEG_BIG))
    score = jnp.where(j >= n_sel, LOWEST, score)
    jf = j.astype(F32)
    sel = jnp.zeros((n_sel_pad, tl), F32)
    for _ in range(topk):
        mx = jnp.max(score, axis=0, keepdims=True)
        first = jnp.min(jnp.where(score == mx, jf, 1e9), axis=0, keepdims=True)
        pick = jf == first
        sel = jnp.where(pick, 1.0, sel)
        score = jnp.where(pick, -jnp.inf, score)
    return jnp.where(allowed, sel, 0.0)


def _nsa_prompt_kernel(q_ref, sl_ref, qts_ref, qtw_ref, kc_ref, vc_ref, ks_ref, vs_ref, kw_ref, vw_ref, at_ref,
                       nbg_ref, gx_ref, o_ref, m_ref, acc_ref, br_ref, *, tq, n_sel, topk):
    rows = NSA_GROUP * tq
    hd = NSA_HEAD_DIM
    qi = pl.program_id(2)
    t0 = qi * tq
    q128 = q_ref[0, 0].reshape(rows, LANES)
    slope = sl_ref[0][:, :1]
    nbp = kc_ref.shape[2]

    s = _dot_nt(q128[:, :hd], kc_ref[0, 0])
    tok = t0 + (_iota((rows, nbp), 0) & (tq - 1))
    dist = tok - (_iota((rows, nbp), 1) * CMP_STRIDE + (CMP_BLOCK - 1))
    valid = dist >= 0
    s = jnp.where(valid, s - slope * dist.astype(F32), NEG_BIG)
    p = jnp.where(valid, jnp.exp(s - jnp.max(s, axis=-1, keepdims=True)), 0.0)
    p = p / jnp.maximum(jnp.sum(p, axis=-1, keepdims=True), 1e-30)
    o_cmp = _dot(p.astype(MXU_DT), vc_ref[0, 0])

    def normalised(acc):
        return acc[:, :hd] / jnp.maximum(acc[:, hd:], 1e-30)

    lane = _iota((rows, LANES), 1)
    qw = jnp.where(lane < hd, q128, qtw_ref[0])
    span = WINDOW + tq
    ws = pl.multiple_of(jnp.maximum(t0 - WINDOW, 0), tq)
    sw = _dot_nt(qw, kw_ref[0, 0, pl.ds(ws, span), :])
    d = (_iota((rows, span), 0) & (tq - 1)) - _iota((rows, span), 1) + (t0 - ws)
    sw = jnp.where((d >= 0) & (d < WINDOW), sw, NEG_BIG)
    pw = jnp.exp(sw - jnp.max(sw, axis=-1, keepdims=True))
    o_win = normalised(_dot(pw.astype(MXU_DT), vw_ref[0, 0, pl.ds(ws, span), :]))

    sel = _topk_block_mask(p, at_ref, t0, tq=tq, tl=tq, n_sel=n_sel, topk=topk)
    sel_t = jnp.concatenate([jnp.zeros_like(sel), sel], axis=0).astype(MXU_DT)
    spread = ((_iota((rows, tq), 0) & (tq - 1)) == _iota((rows, tq), 1)).astype(MXU_DT)
    selb = _dot_nt(spread, sel_t)
    neg = jnp.where((lane >= hd) & (selb < 0.5), NEG_BIG, 0.0)
    qa = jnp.concatenate([(q128.astype(F32) + neg).astype(MXU_DT), qts_ref[0]], axis=1)

    tk = PROMPT_SEL_TK
    kd = t0 // tk
    k0 = pl.multiple_of(kd * tk, tk)
    s = _dot_nt(qa, ks_ref[0, 0, pl.ds(k0, tk), :])
    causal = ((_iota((rows, tk), 0) & (tq - 1)) - _iota((rows, tk), 1) + (t0 - k0)) >= 0
    s = jnp.where(causal, s, NEG_BIG)
    m = jnp.max(s, axis=-1, keepdims=True)
    m_ref[...] = jnp.broadcast_to(m, m_ref.shape)
    acc_ref[...] = _dot(jnp.exp(s - m).astype(MXU_DT), vs_ref[0, 0, pl.ds(k0, tk), :])

    def sel_body(kt, carry):
        ka = pl.multiple_of(kt * tk, tk)
        s = _dot_nt(qa, ks_ref[0, 0, pl.ds(ka, tk), :])
        m_prev = m_ref[...]
        m_new = jnp.maximum(m_prev, jnp.max(s, axis=-1, keepdims=True))
        p = jnp.exp(s - jnp.concatenate([m_new] * (tk // LANES), axis=1))
        acc_ref[...] = jnp.exp(m_prev - m_new) * acc_ref[...] + _dot(p.astype(MXU_DT), vs_ref[0, 0, pl.ds(ka, tk), :])
        m_ref[...] = m_new
        return carry

    lax.fori_loop(0, kd, sel_body, 0)
    o_sel = normalised(acc_ref[...])

    _gate_and_store(o_ref, br_ref, nbg_ref, gx_ref, (o_cmp, o_sel, o_win), tq)


def _slope_digits(tq, tile, lane0):
    hh = jnp.arange(1, NSA_HEADS + 1, dtype=F32)
    slopes = jnp.exp2(-8.0 * hh / NSA_HEADS)
    pieces = jnp.stack(_split(slopes, 3), axis=1).astype(F32)
    six = jnp.concatenate([pieces * float(tile), pieces], axis=1)
    tab = jnp.zeros((NSA_HEADS, LANES), F32).at[:, lane0:lane0 + 6].set(six)
    tab = tab.reshape(NSA_KV_HEADS, NSA_GROUP, 1, LANES)
    return jnp.broadcast_to(tab, (NSA_KV_HEADS, NSA_GROUP, tq, LANES)).reshape(
        NSA_KV_HEADS, NSA_GROUP * tq, LANES).astype(MXU_DT)


def _key_digits(t, tile, lane0, width, onehot):
    r = jnp.arange(t)
    lane = jnp.arange(width)[None, :]
    tab = jnp.zeros((t, width), F32)
    if onehot:
        tab = jnp.where(lane - NSA_HEAD_DIM == (r // SEL_BLOCK)[:, None], 1.0, tab)
    hi = (r // tile).astype(F32)[:, None]
    lo = (r % tile).astype(F32)[:, None]
    tab = jnp.where((lane >= lane0) & (lane < lane0 + 3), hi, tab)
    tab = jnp.where((lane >= lane0 + 3) & (lane < lane0 + 6), lo, tab)
    return tab.astype(MXU_DT)


def nsa_attention_prompt(q128, slopes, kc, vc, ks_aug, vs, kw_aug, vw, h3, nbg_col, gate_expand, *, n_sel):
    bsz, _, _, t, _ = q128.shape
    tq = PROMPT_TQ
    hd = NSA_HEAD_DIM
    nbp = kc.shape[2]
    assert n_sel <= 64 and t % PROMPT_SEL_TK == 0 and t >= WINDOW + tq and PROMPT_SEL_TK % tq == 0
    rows = NSA_GROUP * tq
    at = _imp_to_sel_matrix(64, nbp)
    kern = functools.partial(_nsa_prompt_kernel, tq=tq, n_sel=n_sel, topk=min(NSA_TOPK, n_sel))
    full = lambda n, w: pl.BlockSpec((1, 1, n, w), lambda b, kh, i: (b, kh, 0, 0))
    per_head = pl.BlockSpec((1, rows, LANES), lambda b, kh, i: (kh, 0, 0))
    gw = NSA_GROUP * hd
    return pl.pallas_call(
        kern,
        grid=(bsz, NSA_KV_HEADS, t // tq),
        in_specs=[pl.BlockSpec((1, 1, NSA_GROUP, tq, LANES), lambda b, kh, i: (b, kh, 0, i, 0)),
                  per_head, per_head, per_head,
                  full(nbp, hd), full(nbp, hd), full(t, 2 * LANES), full(t, LANES), full(t, LANES), full(t, LANES),
                  pl.BlockSpec((64, nbp), lambda b, kh, i: (0, 0)),
                  pl.BlockSpec((1, tq, LANES), lambda b, kh, i: (b, i, nbg_col)),
                  pl.BlockSpec((3, LANES, gw), lambda b, kh, i: (0, 0, kh))],
        out_specs=pl.BlockSpec((1, tq, gw), lambda b, kh, i: (b, i, kh)),
        out_shape=jax.ShapeDtypeStruct((bsz, t, NSA_WIDTH), MXU_DT),
        scratch_shapes=[pltpu.VMEM((rows, LANES), F32), pltpu.VMEM((rows, LANES), F32),
                        pltpu.VMEM((3, tq, gw), F32)],
        compiler_params=_cparams("parallel", "parallel", "arbitrary"),
        name="nsa_attention_prompt",
    )(q128, slopes, _slope_digits(tq, SEL_TK, 0), _slope_digits(tq, WIN_TK, hd),
      kc, vc, ks_aug, vs, kw_aug, vw, at, h3, gate_expand)


def _nsa_paged_pre_kernel(q_ref, sl_ref, kc_ref, vc_ref, kw_ref, vw_ref, at_ref,
                          ocmp_ref, owin_ref, sel_ref, *, tq, qpos0, wpos0, n_sel, topk):
    rows = NSA_GROUP * tq
    hd = NSA_HEAD_DIM
    q = q_ref[0, 0].reshape(rows, hd)
    slope = sl_ref[0][:, :1]
    nbp = kc_ref.shape[2]
    tw = kw_ref.shape[2]

    s = _dot_nt(q, kc_ref[0, 0])
    tok = qpos0 + (_iota((rows, nbp), 0) & (tq - 1))
    dist = tok - (_iota((rows, nbp), 1) * CMP_STRIDE + (CMP_BLOCK - 1))
    valid = dist >= 0
    s = jnp.where(valid, s - slope * dist.astype(F32), NEG_BIG)
    p = jnp.where(valid, jnp.exp(s - jnp.max(s, axis=-1, keepdims=True)), 0.0)
    p = p / jnp.maximum(jnp.sum(p, axis=-1, keepdims=True), 1e-30)
    o_cmp = _dot(p.astype(MXU_DT), vc_ref[0, 0])
    sel_ref[0, 0] = _topk_block_mask(p, at_ref, qpos0, tq=tq, tl=sel_ref.shape[3], n_sel=n_sel, topk=topk)

    sw = _dot_nt(q, kw_ref[0, 0])
    d = (qpos0 - wpos0) + (_iota((rows, tw), 0) & (tq - 1)) - _iota((rows, tw), 1)
    valid = (d >= 0) & (d < WINDOW)
    sw = jnp.where(valid, sw - slope * d.astype(F32), NEG_BIG)
    pw = jnp.where(valid, jnp.exp(sw - jnp.max(sw, axis=-1, keepdims=True)), 0.0)
    o_win = _dot(pw.astype(MXU_DT), vw_ref[0, 0]) / jnp.maximum(jnp.sum(pw, axis=-1, keepdims=True), 1e-30)

    for g in range(NSA_GROUP):
        ocmp_ref[0, :, g * hd:(g + 1) * hd] = o_cmp[g * tq:(g + 1) * tq]
        owin_ref[0, :, g * hd:(g + 1) * hd] = o_win[g * tq:(g + 1) * tq]


def _nsa_paged_sel_kernel(*refs, n_in, tq, qpos0, n_tiles):
    refs = refs[1:]
    q_ref, sl_ref, sel_ref = refs[:3]
    pages = refs[3:3 + n_in]
    (ktn_ref, vtn_ref, gain_ref, ocmp_ref, owin_ref, nbg_ref, gx_ref,
     o_ref, m_ref, l_ref, acc_ref, kt_ref, vt_ref, br_ref) = refs[3 + n_in:]
    rows = NSA_GROUP * tq
    hd = NSA_HEAD_DIM
    tk = n_in * PAGE_SIZE
    nblk = tk // SEL_BLOCK
    i = pl.program_id(1)

    @pl.when(i == 0)
    def _():
        m_ref[...] = jnp.full(m_ref.shape, NEG_BIG, F32)
        l_ref[...] = jnp.zeros(l_ref.shape, F32)
        acc_ref[...] = jnp.zeros(acc_ref.shape, F32)

    @pl.when(i < n_tiles)
    def _():
        for u in range(n_in):
            sl = slice(u * PAGE_SIZE, (u + 1) * PAGE_SIZE)
            kt_ref[:, sl] = _head_rms_t(pages[u][0, 0], gain_ref[...]).astype(kt_ref.dtype)
            vt_ref[:, sl] = pages[u][0, 1].astype(vt_ref.dtype)

    @pl.when(i == n_tiles)
    def _():
        kt_ref[...] = ktn_ref[0]
        vt_ref[...] = vtn_ref[0]

    k0 = i * tk
    d = (qpos0 - k0) + (_iota((rows, tk), 0) & (tq - 1)) - _iota((rows, tk), 1)
    causal = d >= 0
    dist = d.astype(F32)
    spread = ((_iota((rows, LANES), 0) & (tq - 1)) == _iota((rows, LANES), 1)).astype(MXU_DT)
    in_tile = (_iota((nblk, tk), 0) == (_iota((nblk, tk), 1) >> 6)).astype(MXU_DT)
    blk0 = pl.multiple_of(i * nblk, nblk)
    reps = tk // LANES
    for kvh in range(NSA_KV_HEADS):
        hs = slice(kvh * hd, (kvh + 1) * hd)
        q = q_ref[0, kvh].reshape(rows, hd)
        s = _dot(q, kt_ref[hs, :])
        mine = _dot_nt(spread, sel_ref[0, kvh, pl.ds(blk0, nblk), :].astype(MXU_DT)).astype(MXU_DT)
        valid = (_dot(mine, in_tile) > 0.5) & causal
        s = jnp.where(valid, s - sl_ref[kvh][:, :1] * dist, NEG_BIG)
        m_prev = m_ref[kvh]
        m_new = jnp.maximum(m_prev, jnp.max(s, axis=-1, keepdims=True))
        alpha = jnp.exp(m_prev - m_new)
        p = jnp.where(valid, jnp.exp(s - jnp.concatenate([m_new] * reps, axis=1)), 0.0)
        l_ref[kvh] = alpha * l_ref[kvh] + jnp.sum(p, axis=-1, keepdims=True)
        acc_ref[kvh] = alpha[:, :hd] * acc_ref[kvh] + _dot_nt(p.astype(MXU_DT), vt_ref[hs, :])
        m_ref[kvh] = m_new

    @pl.when(i == n_tiles)
    def _():
        br_ref[0] = ocmp_ref[0]
        br_ref[2] = owin_ref[0]
        for kvh in range(NSA_KV_HEADS):
            o_sel = acc_ref[kvh] / jnp.maximum(l_ref[kvh][:, :hd], 1e-30)
            for g in range(NSA_GROUP):
                c0 = (kvh * NSA_GROUP + g) * hd
                br_ref[1, :, c0:c0 + hd] = o_sel[g * tq:(g + 1) * tq]
        pieces = _split(_sigmoid(nbg_ref[0]), 2)
        acc = None
        for br in range(3):
            term = sum(_dot(p, gx_ref[br]) for p in pieces) * br_ref[br]
            acc = term if acc is None else acc + term
        o_ref[0] = acc.astype(o_ref.dtype)


def nsa_attention_paged(qn, slopes, kc, vc, kw, vw, pool_t, pages, kt_new, vt_new, gain_sel, h3, nbg_col,
                        gate_expand, *, tq, qpos0, wpos0, n_sel):
    bsz, _, _, tql, hd = qn.shape
    assert tql == tq
    nbp, tw = kc.shape[2], kw.shape[2]
    n_pages = pages.shape[1]
    tk = kt_new.shape[2]
    n_in = tk // PAGE_SIZE
    n_tiles = n_pages // n_in
    assert n_pages % n_in == 0 and qpos0 == n_pages * PAGE_SIZE
    n_sel_pad = -(-max(n_sel, (n_tiles + 1) * tk // SEL_BLOCK) // 64) * 64
    rows = NSA_GROUP * tq
    gw = NSA_GROUP * hd
    at = _imp_to_sel_matrix(n_sel_pad, nbp)
    full = lambda n: pl.BlockSpec((1, 1, n, hd), lambda b, kh: (b, kh, 0, 0))
    obr = pl.BlockSpec((1, tq, gw), lambda b, kh: (b, 0, kh))
    o_cmp, o_win, sel = pl.pallas_call(
        functools.partial(_nsa_paged_pre_kernel, tq=tq, qpos0=qpos0, wpos0=wpos0, n_sel=n_sel,
                          topk=min(NSA_TOPK, n_sel)),
        grid=(bsz, NSA_KV_HEADS),
        in_specs=[pl.BlockSpec((1, 1, NSA_GROUP, tq, hd), lambda b, kh: (b, kh, 0, 0, 0)),
                  pl.BlockSpec((1, rows, LANES), lambda b, kh: (kh, 0, 0)),
                  full(nbp), full(nbp), full(tw), full(tw),
                  pl.BlockSpec((n_sel_pad, nbp), lambda b, kh: (0, 0))],
        out_specs=[obr, obr, pl.BlockSpec((1, 1, n_sel_pad, LANES), lambda b, kh: (b, kh, 0, 0))],
        out_shape=[jax.ShapeDtypeStruct((bsz, tq, NSA_WIDTH), F32), jax.ShapeDtypeStruct((bsz, tq, NSA_WIDTH), F32),
                   jax.ShapeDtypeStruct((bsz, NSA_KV_HEADS, n_sel_pad, LANES), F32)],
        compiler_params=_cparams("parallel", "parallel"),
        name="nsa_paged_pre",
    )(qn, slopes, kc, vc, kw, vw, at)

    page_spec = lambda u: pl.BlockSpec(
        (1, 2, NSA_KV_WIDTH, PAGE_SIZE), lambda b, i, pt: (pt[b, jnp.minimum(i, n_tiles - 1) * n_in + u], 1, 0, 0))
    const = lambda *shape: pl.BlockSpec(shape, lambda b, i, pt: (0,) * len(shape))
    per_seq = lambda *shape: pl.BlockSpec((1,) + shape, lambda b, i, pt: (b,) + (0,) * len(shape))
    return pl.pallas_call(
        functools.partial(_nsa_paged_sel_kernel, n_in=n_in, tq=tq, qpos0=qpos0, n_tiles=n_tiles),
        grid_spec=pltpu.PrefetchScalarGridSpec(
            num_scalar_prefetch=1,
            grid=(bsz, n_tiles + 1),
            in_specs=[per_seq(NSA_KV_HEADS, NSA_GROUP, tq, hd), const(NSA_KV_HEADS, rows, LANES),
                      per_seq(NSA_KV_HEADS, n_sel_pad, LANES)]
                     + [page_spec(u) for u in range(n_in)]
                     + [per_seq(NSA_KV_WIDTH, tk), per_seq(NSA_KV_WIDTH, tk), const(NSA_KV_WIDTH, LANES),
                        per_seq(tq, NSA_WIDTH), per_seq(tq, NSA_WIDTH),
                        pl.BlockSpec((1, tq, LANES), lambda b, i, pt: (b, 0, nbg_col)),
                        const(3, LANES, NSA_WIDTH)],
            out_specs=per_seq(tq, NSA_WIDTH),
            scratch_shapes=[pltpu.VMEM((NSA_KV_HEADS, rows, LANES), F32), pltpu.VMEM((NSA_KV_HEADS, rows, LANES), F32),
                            pltpu.VMEM((NSA_KV_HEADS, rows, hd), F32),
                            pltpu.VMEM((NSA_KV_WIDTH, tk), MXU_DT), pltpu.VMEM((NSA_KV_WIDTH, tk), MXU_DT),
                            pltpu.VMEM((3, tq, NSA_WIDTH), F32)]),
        out_shape=jax.ShapeDtypeStruct((bsz, tq, NSA_WIDTH), MXU_DT),
        compiler_params=_cparams("parallel", "arbitrary"),
        name="nsa_paged_sel",
    )(pages, qn, slopes, sel, *([pool_t] * n_in), kt_new, vt_new, _gain_t(gain_sel),
      o_cmp, o_win, h3, gate_expand)


def _layout(d_model):
    hw, rw = HG_HEADS * HG_DK, RET_HEADS * RET_DK
    names = [("hq", hw), ("hf", hw), ("hi", hw), ("hg", hw), ("nq", NSA_WIDTH), ("nkv", 4 * NSA_KV_WIDTH),
             ("nwkv", 2 * NSA_KV_WIDTH), ("rq", rw), ("rk", rw), ("rv", rw), ("rg", rw), ("mg", 3 * d_model),
             ("nbg", 3 * NSA_HEADS)]
    lay, off = {}, 0
    for name, width in names:
        lay[name] = off
        off += width
    lay["used"] = off
    lay["total"] = -(-off // 1024) * 1024
    return lay


def _block_sum_matrix(width):
    i = jnp.arange(width)
    return (i[:, None] // NSA_HEAD_DIM == i[None, :] // NSA_HEAD_DIM).astype(MXU_DT)


def _gate_expand_matrix():
    c = jnp.arange(LANES)[:, None]
    col = jnp.arange(NSA_WIDTH)[None, :]
    return jnp.stack([(c == (col // NSA_HEAD_DIM) * 3 + br) for br in range(3)]).astype(MXU_DT)


def _slope_rows(tq):
    hh = jnp.arange(1, NSA_HEADS + 1, dtype=F32)
    slopes = jnp.exp2(-8.0 * hh / NSA_HEADS).reshape(NSA_KV_HEADS, NSA_GROUP)
    rows = jnp.repeat(slopes, tq, axis=1)
    return jnp.broadcast_to(rows[:, :, None], (NSA_KV_HEADS, NSA_GROUP * tq, LANES))


def _pad_rows(x, n):
    return jnp.pad(x, ((0, 0), (0, n - x.shape[1])) + ((0, 0),) * (x.ndim - 2))


def _trunk_layer(x, lw, lay, consts, nsa_fn, hg_state, ret_state, *, seq_tiles):
    bsz, t, d = x.shape
    tm, tt_h, tt_r, t_valid = seq_tiles
    x2 = x.reshape(bsz * t, d)
    h = rms_matmul(x2, lw["norm_attn"], lw["w_in"], lay, tm=tm, tn=512)
    np_ = h.shape[1]
    h3 = h.reshape(bsz, t, np_)
    tp = -(-t // tt_h) * tt_h
    h3p = _pad_rows(h3, tp) if tp != t else h3
    oa, hg_new = hgrn_mixer(h3p, lay, lw["lb"], lw["hg_out_norm"], hg_state, tt=tt_h,
                            t_valid=None if tp == t else t_valid)
    tpr = -(-t // tt_r) * tt_r
    h3r = _pad_rows(h3, tpr) if tpr != t else h3
    oc, ret_new = retention_mixer(h3r, lay, consts["lg_tab"], ret_state, tt=tt_r,
                                  n_valid=tt_r if tpr == t else t_valid)
    ob = nsa_fn(h3)
    flat = lambda a: a[:, :t].reshape(bsz * t, a.shape[-1])
    merged = merge_branches(h, lay, flat(oa), flat(ob), flat(oc),
                            lw["w_branch_hg"], lw["w_branch_nsa"], lw["w_branch_ret"], tm=tm, tn=512)
    x2 = matmul_res(merged, lw["w_out"], x2, tm=tm, tn=512)
    hid = ffn_up(x2, lw["norm_ffn"], lw["w_gate"], lw["w_up"], tm=tm, tn=512)
    x2 = matmul_res(hid, lw["w_down"], x2, tm=tm, tn=512)
    return x2.reshape(bsz, t, d), h3, hg_new, ret_new


def kernel(x_prompt, x_sample, cache_nsa, cache_win, state_hgrn, state_ret, page_table,
           norm_attn, w_in, hgrn_lb_logits, hgrn_out_norm, nsa_q_norm, nsa_k_norm, nsa_cmp_w,
           w_branch_hg, w_branch_nsa, w_branch_ret, w_out, norm_ffn, w_gate, w_up, w_down):
    depth = w_in.shape[0]
    bp, tp, d = x_prompt.shape
    bs, ts, _ = x_sample.shape
    n_pool = cache_nsa.shape[1]
    n_pages = page_table.shape[1]
    past = n_pages * PAGE_SIZE
    wbuf = cache_win.shape[2]
    lay = _layout(d)

    sm = jax.nn.softmax(hgrn_lb_logits.astype(F32), axis=0)
    lower_bounds = jnp.clip(jnp.cumsum(sm, axis=0) - sm[0:1], 0.0, 1.0 - 1e-6)

    tt_p = 256 if tp % 256 == 0 else tp
    lg = jnp.log1p(-jnp.exp2(-5.0 - jnp.arange(RET_HEADS, dtype=F32)))
    consts = {
        "gate_expand": _gate_expand_matrix(),
        "lg_tab": jnp.broadcast_to(lg[:, None, None], (RET_HEADS, 1, max(tt_p, LANES))),
    }
    bsum = _block_sum_matrix(NSA_KV_WIDTH)
    pool_t = jnp.transpose(cache_nsa, (0, 1, 3, 4, 5, 2)).reshape(depth * n_pool, 4, NSA_KV_WIDTH, PAGE_SIZE)
    win3 = cache_win.reshape(depth, bs, wbuf, 2 * NSA_KV_WIDTH)

    ts_pad = 16
    tq_s = 16
    n_sel_p = -(-tp // SEL_BLOCK)
    n_sel_s = past // SEL_BLOCK + -(-ts // SEL_BLOCK)
    tk_tail = 2048 if past % 2048 == 0 else 512
    tq_p = 128 if tp % 128 == 0 else tp

    xp, xs = x_prompt, x_sample
    outs = {k: [] for k in ("kv_p", "kv_s", "win_p", "win_s", "hg_p", "hg_s", "ret_p", "ret_s")}
    for l in range(depth):
        lw = {
            "norm_attn": norm_attn[l], "w_in": jnp.swapaxes(w_in[l], 0, 1).astype(MXU_DT),
            "lb": lower_bounds[l].reshape(1, -1),
            "hg_out_norm": hgrn_out_norm[l],
            "w_branch_hg": w_branch_hg[l].astype(MXU_DT), "w_branch_nsa": w_branch_nsa[l].astype(MXU_DT),
            "w_branch_ret": w_branch_ret[l].astype(MXU_DT), "w_out": w_out[l].astype(MXU_DT),
            "norm_ffn": norm_ffn[l], "w_gate": w_gate[l].astype(MXU_DT), "w_up": w_up[l].astype(MXU_DT),
            "w_down": w_down[l].astype(MXU_DT),
        }
        gq, gk, cw = nsa_q_norm[l], nsa_k_norm[l], nsa_cmp_w[l]
        col_cmp = lay["nkv"] // (2 * NSA_KV_WIDTH)
        col_sel = col_cmp + 1
        col_win = lay["nwkv"] // (2 * NSA_KV_WIDTH)
        nbg_col = lay["nbg"] // LANES

        def nsa_prompt(h3):
            tr = 512 if tp % 512 == 0 else tp
            zeros16 = jnp.zeros((bp, CMP_STRIDE, 2 * NSA_KV_WIDTH), F32)
            kc, vc = cmp_prep(h3, col_cmp, zeros16, cw, gk[0], bsum, tr=tr)
            if tp % PROMPT_SEL_TK == 0 and n_sel_p <= 64 and tp >= WINDOW + PROMPT_TQ:
                qn = q_prep(h3, lay, gq, bsum, tr=tr, width=LANES)
                ks, vs = kv_prep(h3, col_sel, gk[1], bsum, tr=tr,
                                 aug=_key_digits(tp, SEL_TK, LANES, 2 * LANES, True))
                kw, vw = kv_prep(h3, col_win, gk[2], bsum, tr=tr,
                                 aug=_key_digits(tp, WIN_TK, NSA_HEAD_DIM, LANES, False))
                return nsa_attention_prompt(qn, _slope_rows(PROMPT_TQ), kc, vc, ks, vs, kw, vw,
                                            h3, nbg_col, consts["gate_expand"], n_sel=n_sel_p)
            qn = q_prep(h3, lay, gq, bsum, tr=tr)
            ks, vs = kv_prep(h3, col_sel, gk[1], bsum, tr=tr)
            kw, vw = kv_prep(h3, col_win, gk[2], bsum, tr=tr)
            return nsa_attention(qn, _slope_rows(tq_p), kc, vc, ks, vs, kw, vw, h3, nbg_col, consts["gate_expand"],
                                 tq=tq_p, sel_tk=SEL_TK, qpos0=0, wpos0=0, n_sel=n_sel_p)

        def nsa_sample(h3):
            pages = page_table + l * n_pool
            h16 = _pad_rows(h3, ts_pad)
            qn = q_prep(h16, lay, gq, bsum, tr=ts_pad)
            w2 = 2 * NSA_KV_WIDTH
            tail16 = h16[:, :, lay["nkv"]:lay["nkv"] + w2].reshape(bs, ts_pad, 2, NSA_KV_WIDTH)
            tail_t = jnp.pad(jnp.transpose(tail16, (0, 2, 3, 1)), ((0, 0), (0, 0), (0, 0), (0, PAGE_SIZE - ts_pad)))
            kc, vc = cmp_prep_paged(pool_t, pages, tail_t, cw, gk[0])
            htail = _pad_rows(h3[:, :, col_sel * w2:(col_sel + 1) * w2], tk_tail)
            ks_tail, vs_tail = kv_prep(htail, 0, gk[1], bsum, tr=tk_tail)
            feature_major = lambda a: jnp.swapaxes(a, 2, 3).reshape(bs, NSA_KV_WIDTH, tk_tail)
            kw_old, vw_old = kv_prep(win3[l], 0, gk[2], bsum, tr=wbuf)
            hw_new = _pad_rows(h3[:, :, col_win * w2:(col_win + 1) * w2], WIN_TK)
            kw_new, vw_new = kv_prep(hw_new, 0, gk[2], bsum, tr=WIN_TK)
            kw = jnp.concatenate([kw_old, kw_new], axis=2)
            vw = jnp.concatenate([vw_old, vw_new], axis=2)
            return nsa_attention_paged(qn, _slope_rows(tq_s), kc, vc, kw, vw, pool_t, pages,
                                       feature_major(ks_tail), feature_major(vs_tail), gk[1], h16, nbg_col,
                                       consts["gate_expand"], tq=tq_s, qpos0=past, wpos0=past - wbuf, n_sel=n_sel_s)

        zeros_state = jnp.zeros((bp, HG_HEADS, HG_DK, HG_DV), F32)
        tm_p = 1024 if (bp * tp) % 1024 == 0 else bp * tp
        xp, h3p, hgp, rtp = _trunk_layer(xp, lw, lay, consts, nsa_prompt, zeros_state, zeros_state,
                                         seq_tiles=(tm_p, tt_p, tt_p, tp))
        xs, h3s, hgs, rts = _trunk_layer(xs, lw, lay, consts, nsa_sample, state_hgrn[l], state_ret[l],
                                         seq_tiles=(bs * ts, ts_pad, ts_pad, ts))

        kv_cols = slice(lay["nkv"], lay["nkv"] + 4 * NSA_KV_WIDTH)
        win_cols = slice(lay["nwkv"], lay["nwkv"] + 2 * NSA_KV_WIDTH)
        outs["kv_p"].append(h3p[:, :, kv_cols].reshape(bp, tp, 4, NSA_KV_HEADS, NSA_HEAD_DIM))
        outs["kv_s"].append(h3s[:, :, kv_cols].reshape(bs, ts, 4, NSA_KV_HEADS, NSA_HEAD_DIM))
        wlen = min(WINDOW, tp)
        outs["win_p"].append(h3p[:, tp - wlen:, win_cols].reshape(bp, wlen, 2, NSA_KV_HEADS, NSA_HEAD_DIM))
        ctx = jnp.concatenate([win3[l], h3s[:, :, win_cols]], axis=1)[:, ts:]
        outs["win_s"].append(ctx.reshape(bs, wbuf, 2, NSA_KV_HEADS, NSA_HEAD_DIM))
        outs["hg_p"].append(hgp); outs["hg_s"].append(hgs)
        outs["ret_p"].append(rtp); outs["ret_s"].append(rts)

    st = lambda k: jnp.stack(outs[k])
    return (xp, xs, st("kv_p"), st("kv_s"), st("win_p"), st("win_s"),
            st("hg_p"), st("hg_s"), st("ret_p"), st("ret_s"))
```

```python
import functools

import jax
import jax.numpy as jnp
from jax import lax
from jax.experimental import pallas as pl
from jax.experimental.pallas import tpu as pltpu

F32 = jnp.float32
MXU_DT = jnp.bfloat16

HG_HEADS, HG_DK, HG_DV = 8, 128, 128
NSA_HEADS, NSA_KV_HEADS, NSA_GROUP, NSA_HEAD_DIM = 16, 4, 4, 64
NSA_WIDTH = NSA_HEADS * NSA_HEAD_DIM
NSA_KV_WIDTH = NSA_KV_HEADS * NSA_HEAD_DIM
CMP_BLOCK, CMP_STRIDE, SEL_BLOCK, NSA_TOPK, WINDOW = 32, 16, 64, 16, 512
RET_HEADS, RET_DK, RET_DV = 8, 128, 128
PAGE_SIZE = 128
EPS = 1e-6
NEG_BIG = -1e30
POS_BIG = 1e30
MIN_F = 1e-20
LOWEST = -3.0e38

LANES = 128
VMEM_LIMIT_BYTES = 56 * 1024 * 1024

HG_SUB_SHIFT = 4
HG_SUB = 1 << HG_SUB_SHIFT
PAGES_PER_STEP = 8
SEL_TK = 256
PROMPT_SEL_TK = 512
PROMPT_TQ = 256
RET_HPS = 2
WIN_TK = 128


def _cparams(*sem):
    return pltpu.CompilerParams(dimension_semantics=sem, vmem_limit_bytes=VMEM_LIMIT_BYTES)


def _dot(a, b):
    return jnp.dot(a, b, preferred_element_type=F32)


def _dot_nt(a, b):
    return lax.dot_general(a, b, (((1,), (1,)), ((), ())), preferred_element_type=F32)


def _split(x, n):
    out = []
    r = x
    for _ in range(n):
        p = r.astype(MXU_DT)
        out.append(p)
        r = r - p.astype(F32)
    return out


def _sigmoid(x):
    return 1.0 / (1.0 + jnp.exp(-x))


def _silu(x):
    return x * _sigmoid(x)


def _iota(shape, dim):
    return lax.broadcasted_iota(jnp.int32, shape, dim)


def _eye(n):
    return (_iota((n, n), 0) == _iota((n, n), 1)).astype(MXU_DT)


def _transpose_exact(x):
    eye = _eye(x.shape[1])
    return sum(_dot_nt(eye, p) for p in _split(x, 3))


def _head_rms(x, bsum, gain):
    ssq = sum(_dot(p, bsum) for p in _split(x * x, 2))
    return x * lax.rsqrt(ssq * (1.0 / NSA_HEAD_DIM) + EPS) * gain


def _rms_matmul_kernel(x_ref, g_ref, w_ref, o_ref, xn_ref):
    @pl.when(pl.program_id(1) == 0)
    def _():
        x = x_ref[...]
        ms = jnp.mean(x * x, axis=-1, keepdims=True)
        xn_ref[...] = (x * lax.rsqrt(ms + EPS) * g_ref[...]).astype(xn_ref.dtype)

    o_ref[...] = _dot_nt(xn_ref[...], w_ref[...])


def rms_matmul(x, gain, wt, layer, lay, *, tm, tn):
    m, d = x.shape
    a = lay["rq"]
    gates = 3 * NSA_HEADS
    assert a % tn == 0 and (lay["used"] - a - gates) % tn == 0 and lay["total"] - lay["nbg"] == tn
    assert lay["used"] % 16 == 0
    nb1 = a // tn
    nb_all = lay["total"] // tn
    base = layer * lay["used"]

    def w_row(i, j):
        row = base + jnp.where(j < nb1, j * tn, jnp.where(j < nb_all - 1, j * tn + gates, a))
        return (pl.multiple_of(row, 16), 0)

    return pl.pallas_call(
        _rms_matmul_kernel,
        grid=(m // tm, nb_all),
        in_specs=[pl.BlockSpec((tm, d), lambda i, j: (i, 0)),
                  pl.BlockSpec((1, d), lambda i, j: (0, 0)),
                  pl.BlockSpec((pl.Element(tn), pl.Element(d)), w_row)],
        out_specs=pl.BlockSpec((tm, tn), lambda i, j: (i, j)),
        out_shape=jax.ShapeDtypeStruct((m, lay["total"]), F32),
        scratch_shapes=[pltpu.VMEM((tm, d), MXU_DT)],
        compiler_params=_cparams("parallel", "arbitrary"),
        name="in_proj",
    )(x, gain.reshape(1, d), wt)


def _ffn_up_kernel(x_ref, g_ref, wg_ref, wu_ref, o_ref, xn_ref):
    @pl.when(pl.program_id(1) == 0)
    def _():
        x = x_ref[...]
        ms = jnp.mean(x * x, axis=-1, keepdims=True)
        xn_ref[...] = (x * lax.rsqrt(ms + EPS) * g_ref[...]).astype(xn_ref.dtype)

    xn = xn_ref[...]
    o_ref[...] = (_silu(_dot(xn, wg_ref[...])) * _dot(xn, wu_ref[...])).astype(o_ref.dtype)


def _layer_cols(k, tn, layer):
    return pl.BlockSpec((None, k, tn), lambda i, j: (layer, 0, j))


def ffn_up(x, gain, wg, wu, layer, *, tm, tn):
    m, d = x.shape
    f = wg.shape[2]
    return pl.pallas_call(
        _ffn_up_kernel,
        grid=(m // tm, f // tn),
        in_specs=[pl.BlockSpec((tm, d), lambda i, j: (i, 0)),
                  pl.BlockSpec((1, d), lambda i, j: (0, 0)),
                  _layer_cols(d, tn, layer), _layer_cols(d, tn, layer)],
        out_specs=pl.BlockSpec((tm, tn), lambda i, j: (i, j)),
        out_shape=jax.ShapeDtypeStruct((m, f), MXU_DT),
        scratch_shapes=[pltpu.VMEM((tm, d), MXU_DT)],
        compiler_params=_cparams("parallel", "arbitrary"),
        name="ffn_up",
    )(x, gain.reshape(1, d), wg, wu)


def _matmul_res_kernel(a_ref, w_ref, r_ref, o_ref):
    o_ref[...] = r_ref[...] + _dot(a_ref[...], w_ref[...])


def matmul_res(a, w, layer, res, *, tm, tn):
    m, k = a.shape
    n = w.shape[2]
    return pl.pallas_call(
        _matmul_res_kernel,
        grid=(m // tm, n // tn),
        in_specs=[pl.BlockSpec((tm, k), lambda i, j: (i, 0)),
                  _layer_cols(k, tn, layer),
                  pl.BlockSpec((tm, tn), lambda i, j: (i, j))],
        out_specs=pl.BlockSpec((tm, tn), lambda i, j: (i, j)),
        out_shape=jax.ShapeDtypeStruct((m, n), F32),
        compiler_params=_cparams("parallel", "arbitrary"),
        name="matmul_res",
    )(a, w, res)


def _merge_kernel(oa_ref, ob_ref, oc_ref, ga_ref, gb_ref, gc_ref, wa_ref, wb_ref, wc_ref, o_ref):
    o_ref[...] = (_sigmoid(ga_ref[...]) * _dot(oa_ref[...], wa_ref[...])
                  + _sigmoid(gb_ref[...]) * _dot(ob_ref[...], wb_ref[...])
                  + _sigmoid(gc_ref[...]) * _dot(oc_ref[...], wc_ref[...])).astype(o_ref.dtype)


def merge_branches(h, lay, oa, ob, oc, wa, wb, wc, layer, *, tm, tn):
    m = h.shape[0]
    d = wa.shape[2]
    wdt = oa.shape[1]
    ca, cb, cc = ((lay["mg"] + i * d) // tn for i in range(3))
    row = lambda i, j: (i, 0)
    return pl.pallas_call(
        _merge_kernel,
        grid=(m // tm, d // tn),
        in_specs=[pl.BlockSpec((tm, wdt), row), pl.BlockSpec((tm, wdt), row), pl.BlockSpec((tm, wdt), row),
                  pl.BlockSpec((tm, tn), lambda i, j: (i, ca + j)),
                  pl.BlockSpec((tm, tn), lambda i, j: (i, cb + j)),
                  pl.BlockSpec((tm, tn), lambda i, j: (i, cc + j)),
                  _layer_cols(wdt, tn, layer), _layer_cols(wdt, tn, layer), _layer_cols(wdt, tn, layer)],
        out_specs=pl.BlockSpec((tm, tn), lambda i, j: (i, j)),
        out_shape=jax.ShapeDtypeStruct((m, d), MXU_DT),
        compiler_params=_cparams("parallel", "arbitrary"),
        name="merge",
    )(oa, ob, oc, h, h, h, wa, wb, wc)


def _hgrn_kernel(hq_ref, hf_ref, hi_ref, hg_ref, lb_ref, gain_ref, s0_ref, o_ref, sout_ref,
                 st_ref, ut_ref, *, tt, t_valid):
    c = HG_SUB
    nj = tt // c
    tb = pl.program_id(2)

    @pl.when(tb == 0)
    def _():
        st_ref[...] = _transpose_exact(s0_ref[0, 0])

    hq = hq_ref[0]
    lb = lb_ref[...]
    q = _silu(hq)
    sig = _sigmoid(hf_ref[0])
    g = jnp.log(jnp.maximum(lb + (1.0 - lb) * sig, MIN_F))
    kk = (1.0 - lb) * (1.0 - sig)
    v = hi_ref[0]
    row = _iota((tt, HG_DK), 0)
    if t_valid is not None:
        live = (tb * tt + row) < t_valid
        g = jnp.where(live, g, 0.0)
        kk = jnp.where(live, kk, 0.0)

    r2 = _iota((tt, tt), 0)
    c2 = _iota((tt, tt), 1)
    same = (r2 >> HG_SUB_SHIFT) == (c2 >> HG_SUB_SHIFT)
    gp = _split(g, 3)
    tri = (same & (c2 <= r2)).astype(MXU_DT)
    blk = same.astype(MXU_DT)
    b = sum(_dot(tri, p) for p in gp)
    dtot = sum(_dot(blk, p) for p in gp)
    qe = q * jnp.exp(b)
    ke = kk * jnp.exp(dtot - b)
    edec = jnp.exp(dtot)

    q3 = q.reshape(nj, c, HG_DK)
    kk3 = kk.reshape(nj, c, HG_DK)
    b3 = b.reshape(nj, c, HG_DK)
    v3 = v.reshape(nj, c, HG_DV)
    tpos = _iota((nj, c, HG_DK), 1)
    ones = jnp.ones((HG_DK, HG_DV), MXU_DT)
    o3 = jnp.zeros((nj, c, HG_DV), F32)
    for s in range(c):
        m = tpos >= s
        diff = jnp.where(m, b3 - b3[:, s:s + 1, :], 0.0)
        y = jnp.where(m, q3 * kk3[:, s:s + 1, :] * jnp.exp(diff), 0.0)
        z = _dot(y.reshape(tt, HG_DK).astype(MXU_DT), ones).reshape(nj, c, HG_DV)
        o3 = o3 + z * v3[:, s:s + 1, :]
    o = o3.reshape(tt, HG_DV)

    v_b = v.astype(MXU_DT)
    ke_b = ke.astype(MXU_DT)
    for j in range(nj):
        ut_ref[j] = lax.dot_general(v_b[j * c:(j + 1) * c], ke_b[j * c:(j + 1) * c],
                                    (((0,), (0,)), ((), ())), preferred_element_type=F32)

    qe_b = qe.astype(MXU_DT)
    outs = []
    for j in range(nj):
        st = st_ref[...]
        outs.append(_dot_nt(qe_b[j * c:(j + 1) * c], st.astype(MXU_DT)))
        st_ref[...] = st * edec[j * c:j * c + 1, :] + ut_ref[j]
    o = o + jnp.concatenate(outs, axis=0)

    on = o * lax.rsqrt(jnp.mean(o * o, axis=-1, keepdims=True) + EPS) * gain_ref[...]
    o_ref[0] = (on * _silu(hg_ref[0])).astype(o_ref.dtype)

    @pl.when(tb == pl.num_programs(2) - 1)
    def _():
        sout_ref[0, 0] = _transpose_exact(st_ref[...])


def hgrn_mixer(h3, lay, lb, out_gain, s0, *, tt, t_valid):
    bsz, t, _ = h3.shape
    cq, cf, ci, cg = (lay[k] // LANES for k in ("hq", "hf", "hi", "hg"))
    col = lambda c0: (lambda b, hd, tb: (b, tb, c0 + hd))
    kern = functools.partial(_hgrn_kernel, tt=tt, t_valid=t_valid)
    return pl.pallas_call(
        kern,
        grid=(bsz, HG_HEADS, t // tt),
        in_specs=[pl.BlockSpec((1, tt, LANES), col(cq)), pl.BlockSpec((1, tt, LANES), col(cf)),
                  pl.BlockSpec((1, tt, LANES), col(ci)), pl.BlockSpec((1, tt, LANES), col(cg)),
                  pl.BlockSpec((1, HG_DK), lambda b, hd, tb: (0, hd)),
                  pl.BlockSpec((1, HG_DV), lambda b, hd, tb: (0, 0)),
                  pl.BlockSpec((1, 1, HG_DK, HG_DV), lambda b, hd, tb: (b, hd, 0, 0))],
        out_specs=[pl.BlockSpec((1, tt, HG_DV), lambda b, hd, tb: (b, tb, hd)),
                   pl.BlockSpec((1, 1, HG_DK, HG_DV), lambda b, hd, tb: (b, hd, 0, 0))],
        out_shape=[jax.ShapeDtypeStruct((bsz, t, HG_HEADS * HG_DV), MXU_DT),
                   jax.ShapeDtypeStruct((bsz, HG_HEADS, HG_DK, HG_DV), F32)],
        scratch_shapes=[pltpu.VMEM((HG_DV, HG_DK), F32),
                        pltpu.VMEM((tt // HG_SUB, HG_DV, HG_DK), F32)],
        compiler_params=_cparams("parallel", "parallel", "arbitrary"),
        name="hgrn",
    )(h3, h3, h3, h3, lb, out_gain.reshape(1, HG_DV), s0)


def _ret_kernel(q_ref, k_ref, v_ref, g_ref, lg_ref, s0_ref, o_ref, sout_ref, s_ref, *, tt, n_valid):
    tb = pl.program_id(2)

    @pl.when(tb == 0)
    def _():
        s_ref[...] = s0_ref[0]

    rel = (_iota((tt, tt), 0) - _iota((tt, tt), 1)).astype(F32)
    spos = _iota((RET_DK, tt), 1)
    for hh in range(RET_HPS):
        cols = slice(hh * RET_DK, (hh + 1) * RET_DK)
        lgl = lg_ref[hh][:, :LANES]
        lgt = lg_ref[hh][:, :tt]
        q = q_ref[0][:, cols] * (RET_DK ** -0.5)
        k = k_ref[0][:, cols]
        vb = v_ref[0][:, cols].astype(MXU_DT)
        dmat = jnp.where(rel >= 0, jnp.exp(jnp.maximum(rel, 0.0) * lgt), 0.0)
        att = _dot_nt(q.astype(MXU_DT), k.astype(MXU_DT)) * dmat
        q_dec = jnp.exp((_iota((tt, RET_DK), 0) + 1).astype(F32) * lgl)
        s = s_ref[hh]
        o = _dot(att.astype(MXU_DT), vb) + _dot((q * q_dec).astype(MXU_DT), s.astype(MXU_DT))

        k_dec = jnp.where(spos < n_valid, jnp.exp(jnp.maximum(n_valid - 1 - spos, 0).astype(F32) * lgt), 0.0)
        kt = _transpose_exact(k)
        s_ref[hh] = jnp.exp(float(n_valid) * lgl) * s + _dot((kt * k_dec).astype(MXU_DT), vb)

        mu = jnp.mean(o, axis=-1, keepdims=True)
        var = jnp.mean(jnp.square(o - mu), axis=-1, keepdims=True)
        o_ref[0, :, cols] = ((o - mu) * lax.rsqrt(var + EPS) * _silu(g_ref[0][:, cols])).astype(o_ref.dtype)

    @pl.when(tb == pl.num_programs(2) - 1)
    def _():
        sout_ref[0] = s_ref[...]


def retention_mixer(h3, lay, lg_tab, s0, *, tt, n_valid):
    bsz, t, _ = h3.shape
    assert n_valid == tt or t == tt
    hps = RET_HPS
    wd = hps * RET_DK
    cq, ck, cv, cg = (lay[k] // wd for k in ("rq", "rk", "rv", "rg"))
    col = lambda c0: (lambda b, hp, tb: (b, tb, c0 + hp))
    kern = functools.partial(_ret_kernel, tt=tt, n_valid=n_valid)
    return pl.pallas_call(
        kern,
        grid=(bsz, RET_HEADS // hps, t // tt),
        in_specs=[pl.BlockSpec((1, tt, wd), col(cq)), pl.BlockSpec((1, tt, wd), col(ck)),
                  pl.BlockSpec((1, tt, wd), col(cv)), pl.BlockSpec((1, tt, wd), col(cg)),
                  pl.BlockSpec((hps, 1, lg_tab.shape[2]), lambda b, hp, tb: (hp, 0, 0)),
                  pl.BlockSpec((1, hps, RET_DK, RET_DV), lambda b, hp, tb: (b, hp, 0, 0))],
        out_specs=[pl.BlockSpec((1, tt, hps * RET_DV), lambda b, hp, tb: (b, tb, hp)),
                   pl.BlockSpec((1, hps, RET_DK, RET_DV), lambda b, hp, tb: (b, hp, 0, 0))],
        out_shape=[jax.ShapeDtypeStruct((bsz, t, RET_HEADS * RET_DV), MXU_DT),
                   jax.ShapeDtypeStruct((bsz, RET_HEADS, RET_DK, RET_DV), F32)],
        scratch_shapes=[pltpu.VMEM((hps, RET_DK, RET_DV), F32)],
        compiler_params=_cparams("parallel", "parallel", "arbitrary"),
        name="retention",
    )(h3, h3, h3, h3, lg_tab, s0)


def _store_heads(ref, x, lead=()):
    for hh in range(NSA_KV_HEADS):
        ref[lead + (hh,)] = x[:, hh * NSA_HEAD_DIM:(hh + 1) * NSA_HEAD_DIM].astype(ref.dtype)


def _q_prep_kernel(q_ref, gain_ref, bsum_ref, o_ref):
    x = q_ref[0]
    hd = NSA_HEAD_DIM
    for kvh in range(NSA_KV_HEADS):
        lo = kvh * NSA_KV_WIDTH
        qn = _head_rms(x[:, lo:lo + NSA_KV_WIDTH], bsum_ref[...], gain_ref[...]) * (hd ** -0.5)
        for g in range(NSA_GROUP):
            o_ref[0, kvh, g, :, 0:hd] = qn[:, g * hd:(g + 1) * hd].astype(o_ref.dtype)
            if o_ref.shape[-1] > hd:
                o_ref[0, kvh, g, :, hd:] = jnp.zeros((x.shape[0], o_ref.shape[-1] - hd), o_ref.dtype)


def q_prep(h3, lay, gain_q, bsum, *, tr, width=NSA_HEAD_DIM):
    bsz, t, _ = h3.shape
    cq = lay["nq"] // NSA_WIDTH
    return pl.pallas_call(
        _q_prep_kernel,
        grid=(bsz, t // tr),
        in_specs=[pl.BlockSpec((1, tr, NSA_WIDTH), lambda b, i: (b, i, cq)),
                  pl.BlockSpec((1, NSA_KV_WIDTH), lambda b, i: (0, 0)),
                  pl.BlockSpec((NSA_KV_WIDTH, NSA_KV_WIDTH), lambda b, i: (0, 0))],
        out_specs=pl.BlockSpec((1, NSA_KV_HEADS, NSA_GROUP, tr, width), lambda b, i: (b, 0, 0, i, 0)),
        out_shape=jax.ShapeDtypeStruct((bsz, NSA_KV_HEADS, NSA_GROUP, t, width), MXU_DT),
        compiler_params=_cparams("parallel", "parallel"),
        name="nsa_q_prep",
    )(h3, jnp.tile(gain_q, NSA_KV_HEADS).reshape(1, NSA_KV_WIDTH), bsum)


def _kv_prep_kernel(x_ref, gain_ref, bsum_ref, *rest):
    ko_ref, vo_ref = rest[-2:]
    hd = NSA_HEAD_DIM
    x = x_ref[0]
    kn = _head_rms(x[:, :NSA_KV_WIDTH], bsum_ref[...], gain_ref[...])
    for hh in range(NSA_KV_HEADS):
        ko_ref[0, hh, :, 0:hd] = kn[:, hh * hd:(hh + 1) * hd].astype(ko_ref.dtype)
        if len(rest) == 3:
            ko_ref[0, hh, :, hd:] = rest[0][:, hd:]
        lo = NSA_KV_WIDTH + hh * hd
        vo_ref[0, hh, :, 0:hd] = x[:, lo:lo + hd].astype(vo_ref.dtype)
        if vo_ref.shape[-1] > hd:
            vo_ref[0, hh, :, hd:] = jnp.ones((x.shape[0], vo_ref.shape[-1] - hd), vo_ref.dtype)


def kv_prep(x3, col, gain, bsum, *, tr, aug=None):
    bsz, t, _ = x3.shape
    w2 = 2 * NSA_KV_WIDTH
    hd = NSA_HEAD_DIM
    kw = hd if aug is None else aug.shape[1]
    vw = hd if aug is None else LANES
    vshape = jax.ShapeDtypeStruct((bsz, NSA_KV_HEADS, t, vw), MXU_DT)
    kshape = jax.ShapeDtypeStruct((bsz, NSA_KV_HEADS, t, kw), MXU_DT)
    spec = lambda w: pl.BlockSpec((1, NSA_KV_HEADS, tr, w), lambda b, i: (b, 0, i, 0))
    in_specs = [pl.BlockSpec((1, tr, w2), lambda b, i: (b, i, col)),
                pl.BlockSpec((1, NSA_KV_WIDTH), lambda b, i: (0, 0)),
                pl.BlockSpec((NSA_KV_WIDTH, NSA_KV_WIDTH), lambda b, i: (0, 0))]
    args = [x3, jnp.tile(gain, NSA_KV_HEADS).reshape(1, NSA_KV_WIDTH), bsum]
    if aug is not None:
        in_specs.append(pl.BlockSpec((tr, kw), lambda b, i: (i, 0)))
        args.append(aug)
    return pl.pallas_call(
        _kv_prep_kernel,
        grid=(bsz, t // tr),
        in_specs=in_specs,
        out_specs=[spec(kw), spec(vw)],
        out_shape=[kshape, vshape],
        compiler_params=_cparams("parallel", "parallel"),
        name="nsa_kv_prep",
    )(*args)


def _head_rms_t(xt, gain_t):
    x3 = xt.reshape(NSA_KV_HEADS, NSA_HEAD_DIM, xt.shape[1])
    ms = jnp.mean(x3 * x3, axis=1, keepdims=True)
    return (x3 * lax.rsqrt(ms + EPS)).reshape(xt.shape) * gain_t


def _gain_t(gain):
    return jnp.broadcast_to(jnp.tile(gain, NSA_KV_HEADS)[:, None], (NSA_KV_WIDTH, LANES))


def _cmp_prep_kernel(*refs, n_in, rows):
    refs = refs[-(n_in + 7):]
    nxt_ref, tail_ref, w_ref, gain_ref, bsum_ref, kc_ref, vc_ref = refs[n_in:]
    cs = CMP_STRIDE
    nb = rows // cs
    w0 = w_ref[0]
    w1 = w_ref[1]
    a0 = []
    a1 = []
    for u in range(n_in):
        x3 = refs[u][0].reshape(nb, cs, 2 * NSA_KV_WIDTH)
        a0.append(jnp.sum(x3 * w0[None], axis=1))
        a1.append(jnp.sum(x3 * w1[None], axis=1))
    a0 = jnp.concatenate(a0, axis=0) if n_in > 1 else a0[0]
    a1 = jnp.concatenate(a1, axis=0) if n_in > 1 else a1[0]
    last = pl.program_id(1) == pl.num_programs(1) - 1
    nx = jnp.where(last, tail_ref[0], nxt_ref[0])
    a1_next = jnp.sum(nx * w1, axis=0, keepdims=True)
    tot = nb * n_in
    a1s = pltpu.roll(a1, tot - 1, 0)
    a1s = jnp.where(_iota(a1s.shape, 0) == tot - 1, a1_next, a1s)
    comp = a0 + a1s
    kc = _head_rms(comp[:, :NSA_KV_WIDTH], bsum_ref[...], gain_ref[...])
    _store_heads(kc_ref, kc, (0,))
    _store_heads(vc_ref, comp[:, NSA_KV_WIDTH:], (0,))


def _cmp_weight_table(cmp_w):
    w = cmp_w.reshape(2, CMP_BLOCK // CMP_STRIDE, CMP_STRIDE)
    w = jnp.transpose(w, (1, 2, 0))
    return jnp.repeat(w, NSA_KV_WIDTH, axis=2)


def cmp_prep(x3, col, tail16, cmp_w, gain, bsum, *, tr):
    bsz, t, _ = x3.shape
    w2 = 2 * NSA_KV_WIDTH
    nbt = tr // CMP_STRIDE
    n_steps = t // tr
    kern = functools.partial(_cmp_prep_kernel, n_in=1, rows=tr)
    oshape = jax.ShapeDtypeStruct((bsz, NSA_KV_HEADS, t // CMP_STRIDE, NSA_HEAD_DIM), MXU_DT)
    ospec = pl.BlockSpec((1, NSA_KV_HEADS, nbt, NSA_HEAD_DIM), lambda b, i: (b, 0, i, 0))
    chunks_per_step = tr // CMP_STRIDE
    return pl.pallas_call(
        kern,
        grid=(bsz, n_steps),
        in_specs=[pl.BlockSpec((1, tr, w2), lambda b, i: (b, i, col)),
                  pl.BlockSpec((1, CMP_STRIDE, w2),
                               lambda b, i: (b, jnp.minimum(i + 1, n_steps - 1) * chunks_per_step, col)),
                  pl.BlockSpec((1, CMP_STRIDE, w2), lambda b, i: (b, 0, 0)),
                  pl.BlockSpec((2, CMP_STRIDE, w2), lambda b, i: (0, 0, 0)),
                  pl.BlockSpec((1, NSA_KV_WIDTH), lambda b, i: (0, 0)),
                  pl.BlockSpec((NSA_KV_WIDTH, NSA_KV_WIDTH), lambda b, i: (0, 0))],
        out_specs=[ospec, ospec],
        out_shape=[oshape, oshape],
        compiler_params=_cparams("parallel", "arbitrary"),
        name="nsa_cmp_prep",
    )(x3, x3, tail16, _cmp_weight_table(cmp_w), jnp.tile(gain, NSA_KV_HEADS).reshape(1, NSA_KV_WIDTH), bsum)


def _cmp_prep_t_kernel(*refs, n_in):
    refs = refs[-(n_in + 6):]
    nxt_ref, tail_ref, w_ref, gain_ref, kc_ref, vc_ref = refs[n_in:]
    nb = w_ref.shape[2] // 4
    kvw = NSA_KV_WIDTH

    last = pl.program_id(1) == pl.num_programs(1) - 1
    nxt = jnp.where(last, tail_ref[0], nxt_ref[0])
    rows = [refs[u][0].reshape(2 * kvw, PAGE_SIZE).astype(MXU_DT) for u in range(n_in)]
    rows.append(nxt.reshape(2 * kvw, PAGE_SIZE).astype(MXU_DT))
    acc = _dot(jnp.concatenate(rows, axis=1), w_ref[...].reshape((n_in + 1) * PAGE_SIZE, 4 * nb))
    comp_k = acc[:kvw, 0:nb] + acc[:kvw, 2 * nb:3 * nb]
    comp_v = acc[kvw:, nb:2 * nb] + acc[kvw:, 3 * nb:4 * nb]
    kct = _head_rms_t(comp_k, gain_ref[...][:, :nb]).astype(MXU_DT)
    eye = _eye(nb)
    _store_heads(kc_ref, _dot_nt(eye, kct), (0,))
    _store_heads(vc_ref, _dot_nt(eye, comp_v.astype(MXU_DT)), (0,))


def _cmp_band_tables(cmp_w, n_pages_step):
    cs = CMP_STRIDE
    w = cmp_w.reshape(2, CMP_BLOCK // cs, cs)
    nb = n_pages_step * PAGE_SIZE // cs
    rho = jnp.arange(PAGE_SIZE)
    ch = (jnp.arange(n_pages_step)[:, None] * (PAGE_SIZE // cs) + rho[None, :] // cs)[..., None]
    n = jnp.arange(nb)[None, None, :]
    s = rho % cs
    tabs = []
    for c in range(2):
        full = (w[c, 0][s][None, :, None] * (ch == n) + w[c, 1][s][None, :, None] * (ch == n + 1)).astype(F32)
        nxt = jnp.where((rho[:, None] < cs) & (n[0] == nb - 1), w[c, 1][s][:, None], 0.0).astype(F32)
        tabs.append(jnp.concatenate([full, nxt[None]], axis=0))
    hi = [t.astype(MXU_DT) for t in tabs]
    lo = [(t - h.astype(F32)).astype(MXU_DT) for t, h in zip(tabs, hi)]
    return jnp.concatenate(hi + lo, axis=2)


def cmp_prep_paged(pool_t, pages, tail_t, cmp_w, gain):
    bsz, n_pages = pages.shape
    pp = PAGES_PER_STEP
    n_steps = n_pages // pp
    nbt = pp * PAGE_SIZE // CMP_STRIDE
    kern = functools.partial(_cmp_prep_t_kernel, n_in=pp)
    oshape = jax.ShapeDtypeStruct((bsz, NSA_KV_HEADS, n_pages * PAGE_SIZE // CMP_STRIDE, NSA_HEAD_DIM), MXU_DT)
    ospec = pl.BlockSpec((1, NSA_KV_HEADS, nbt, NSA_HEAD_DIM), lambda b, i, pt: (b, 0, i, 0))
    pblock = (1, 2, NSA_KV_WIDTH, PAGE_SIZE)
    page_spec = lambda u: pl.BlockSpec(pblock, lambda b, i, pt: (pt[b, i * pp + u], 0, 0, 0))
    const = lambda *shape: pl.BlockSpec(shape, lambda b, i, pt: (0,) * len(shape))
    return pl.pallas_call(
        kern,
        grid_spec=pltpu.PrefetchScalarGridSpec(
            num_scalar_prefetch=1,
            grid=(bsz, n_steps),
            in_specs=[page_spec(u) for u in range(pp)]
                     + [pl.BlockSpec(pblock, lambda b, i, pt: (pt[b, jnp.minimum(i + 1, n_steps - 1) * pp], 0, 0, 0)),
                        pl.BlockSpec(pblock, lambda b, i, pt: (b, 0, 0, 0)),
                        const(pp + 1, PAGE_SIZE, 4 * nbt), const(NSA_KV_WIDTH, LANES)],
            out_specs=[ospec, ospec]),
        out_shape=[oshape, oshape],
        compiler_params=_cparams("parallel", "arbitrary"),
        name="nsa_cmp_prep_paged",
    )(pages, *([pool_t] * pp), pool_t, tail_t, _cmp_band_tables(cmp_w, pp), _gain_t(gain))


def _gate_and_store(o_ref, br_ref, nbg_ref, gx_ref, branches, tq):
    hd = NSA_HEAD_DIM
    for br, o in enumerate(branches):
        for g in range(NSA_GROUP):
            br_ref[br, :, g * hd:(g + 1) * hd] = o[g * tq:(g + 1) * tq]
    pieces = _split(_sigmoid(nbg_ref[0]), 2)
    acc = None
    for br in range(3):
        term = sum(_dot(p, gx_ref[br]) for p in pieces) * br_ref[br]
        acc = term if acc is None else acc + term
    o_ref[0] = acc.astype(o_ref.dtype)


def _nsa_kernel(q_ref, sl_ref, kc_ref, vc_ref, ks_ref, vs_ref, kw_ref, vw_ref, at_ref, nbg_ref, gx_ref,
                o_ref, sc_ref, m_ref, l_ref, acc_ref, br_ref,
                *, tq, sel_tk, qpos0, wpos0, n_sel, topk):
    rows = NSA_GROUP * tq
    hd = NSA_HEAD_DIM
    t0 = qpos0 + pl.program_id(2) * tq
    q = q_ref[0, 0].reshape(rows, hd)
    slope = sl_ref[0][:, :1]
    nbp = kc_ref.shape[2]
    n_sel_pad, tl = sc_ref.shape

    def tok(shape):
        return t0 + (_iota(shape, 0) & (tq - 1))

    s = _dot_nt(q, kc_ref[0, 0])
    dist = tok((rows, nbp)) - (_iota((rows, nbp), 1) * CMP_STRIDE + (CMP_BLOCK - 1))
    valid = dist >= 0
    s = jnp.where(valid, s - slope * dist.astype(F32), NEG_BIG)
    p = jnp.where(valid, jnp.exp(s - jnp.max(s, axis=-1, keepdims=True)), 0.0)
    p = p / jnp.maximum(jnp.sum(p, axis=-1, keepdims=True), 1e-30)
    o_cmp = _dot(p.astype(MXU_DT), vc_ref[0, 0])

    sel = _topk_block_mask(p, at_ref, t0, tq=tq, tl=tl, n_sel=n_sel, topk=topk)
    spread = ((_iota((rows, tl), 0) & (tq - 1)) == _iota((rows, tl), 1)).astype(MXU_DT)
    nblk = sel_tk // SEL_BLOCK
    if nblk % 8 == 0:
        sc_ref[...] = sel
        in_tile = (_iota((nblk, sel_tk), 0) == (_iota((nblk, sel_tk), 1) >> 6)).astype(MXU_DT)
    else:
        selb = _dot_nt(spread, sel.astype(MXU_DT)).astype(MXU_DT)

    def softmax_pass(k_ref, v_ref, tk, lo, hi, kpos0, mask_fn):
        m_ref[...] = jnp.full(m_ref.shape, NEG_BIG, F32)
        l_ref[...] = jnp.zeros(l_ref.shape, F32)
        acc_ref[...] = jnp.zeros(acc_ref.shape, F32)
        rel = (_iota((rows, tk), 0) & (tq - 1)) - _iota((rows, tk), 1)
        reps = tk // LANES

        def body(kt, carry):
            k0 = pl.multiple_of(kt * tk, tk)
            s = _dot_nt(q, k_ref[0, 0, pl.ds(k0, tk), :])
            d = rel + (t0 - kpos0 - k0)
            valid = mask_fn(d, k0)
            s = jnp.where(valid, s - slope * d.astype(F32), NEG_BIG)
            m_prev = m_ref[...]
            m_new = jnp.maximum(m_prev, jnp.max(s, axis=-1, keepdims=True))
            alpha = jnp.exp(m_prev - m_new)
            mrep = m_new if reps == 1 else jnp.concatenate([m_new] * reps, axis=1)
            p = jnp.where(valid, jnp.exp(s - mrep), 0.0)
            l_ref[...] = alpha * l_ref[...] + jnp.sum(p, axis=-1, keepdims=True)
            acc_ref[...] = alpha[:, :hd] * acc_ref[...] + _dot(p.astype(MXU_DT), v_ref[0, 0, pl.ds(k0, tk), :])
            m_ref[...] = m_new
            return carry

        lax.fori_loop(lo, hi, body, 0)
        return acc_ref[...] / jnp.maximum(l_ref[...][:, :hd], 1e-30)

    n_kt_all = ks_ref.shape[2] // sel_tk
    hi_sel = jnp.minimum(n_kt_all, (t0 + tq - 1) // sel_tk + 1)

    def sel_mask(d, k0):
        if nblk % 8 == 0:
            blk0 = pl.multiple_of(k0 // SEL_BLOCK, nblk)
            mine = _dot_nt(spread, sc_ref[pl.ds(blk0, nblk), :].astype(MXU_DT)).astype(MXU_DT)
            return (_dot(mine, in_tile) > 0.5) & (d >= 0)
        blk_of_key = (k0 + _iota((n_sel_pad, sel_tk), 1)) >> 6
        expand = (_iota((n_sel_pad, sel_tk), 0) == blk_of_key).astype(MXU_DT)
        return (_dot(selb, expand) > 0.5) & (d >= 0)

    o_sel = softmax_pass(ks_ref, vs_ref, sel_tk, 0, hi_sel, 0, sel_mask)

    n_wt_all = kw_ref.shape[2] // WIN_TK
    lo_win = jnp.maximum(t0 - (WINDOW - 1) - wpos0, 0) // WIN_TK
    hi_win = jnp.minimum(n_wt_all, (t0 + tq - 1 - wpos0) // WIN_TK + 1)
    o_win = softmax_pass(kw_ref, vw_ref, WIN_TK, lo_win, hi_win, wpos0,
                         lambda d, k0: (d >= 0) & (d < WINDOW))

    _gate_and_store(o_ref, br_ref, nbg_ref, gx_ref, (o_cmp, o_sel, o_win), tq)


def _imp_to_sel_matrix(n_sel_pad, nbp):
    m = SEL_BLOCK // CMP_STRIDE
    r = CMP_BLOCK // CMP_STRIDE
    jj = jnp.arange(n_sel_pad)[:, None]
    ii = jnp.arange(nbp)[None, :]
    cnt = sum(((ii - rr) >= m * jj) & ((ii - rr) < m * (jj + 1)) for rr in range(r))
    return cnt.astype(MXU_DT)


def nsa_attention(qn, slopes, kc, vc, ks, vs, kw, vw, h3, nbg_col, gate_expand, *, tq, sel_tk, qpos0, wpos0, n_sel):
    bsz, _, _, tql, hd = qn.shape
    nbp, tk_all, tw_all = kc.shape[2], ks.shape[2], kw.shape[2]
    assert tq & (tq - 1) == 0 and tk_all % sel_tk == 0 and tw_all % WIN_TK == 0
    n_sel_pad = -(-max(n_sel, tk_all // SEL_BLOCK) // 64) * 64
    tl = max(tq, LANES)
    rows = NSA_GROUP * tq
    at = _imp_to_sel_matrix(n_sel_pad, nbp)
    kern = functools.partial(_nsa_kernel, tq=tq, sel_tk=sel_tk, qpos0=qpos0, wpos0=wpos0, n_sel=n_sel,
                             topk=min(NSA_TOPK, n_sel))
    full = lambda n: pl.BlockSpec((1, 1, n, hd), lambda b, kh, i: (b, kh, 0, 0))
    gw = NSA_GROUP * hd
    return pl.pallas_call(
        kern,
        grid=(bsz, NSA_KV_HEADS, tql // tq),
        in_specs=[pl.BlockSpec((1, 1, NSA_GROUP, tq, hd), lambda b, kh, i: (b, kh, 0, i, 0)),
                  pl.BlockSpec((1, rows, LANES), lambda b, kh, i: (kh, 0, 0)),
                  full(nbp), full(nbp), full(tk_all), full(tk_all), full(tw_all), full(tw_all),
                  pl.BlockSpec((n_sel_pad, nbp), lambda b, kh, i: (0, 0)),
                  pl.BlockSpec((1, tq, LANES), lambda b, kh, i: (b, i, nbg_col)),
                  pl.BlockSpec((3, LANES, gw), lambda b, kh, i: (0, 0, kh))],
        out_specs=pl.BlockSpec((1, tq, gw), lambda b, kh, i: (b, i, kh)),
        out_shape=jax.ShapeDtypeStruct((bsz, tql, NSA_WIDTH), MXU_DT),
        scratch_shapes=[pltpu.VMEM((n_sel_pad, tl), F32),
                        pltpu.VMEM((rows, LANES), F32), pltpu.VMEM((rows, LANES), F32),
                        pltpu.VMEM((rows, hd), F32), pltpu.VMEM((3, tq, gw), F32)],
        compiler_params=_cparams("parallel", "parallel", "arbitrary"),
        name="nsa_attention",
    )(qn, slopes, kc, vc, ks, vs, kw, vw, at, h3, gate_expand)


def _topk_block_mask(p, at_ref, t0, *, tq, tl, n_sel, topk):
    n_sel_pad, nbp = at_ref.shape
    imp = p[0:tq] + p[tq:2 * tq] + p[2 * tq:3 * tq] + p[3 * tq:4 * tq]
    if tl > tq:
        imp = jnp.concatenate([imp, jnp.zeros((tl - tq, nbp), F32)], axis=0)
    imp_sel = sum(_dot_nt(at_ref[...], piece) for piece in _split(imp, 3))
    j = _iota((n_sel_pad, tl), 0)
    tt = t0 + _iota((n_sel_pad, tl), 1)
    forced = (j == 0) | (j == (tt >> 6))
    allowed = (j << 6) <= tt
    score = jnp.where(forced, POS_BIG, jnp.where(allowed, imp_sel, NEG_BIG))
    score = jnp.where(j >= n_sel, LOWEST, score)
    jf = j.astype(F32)
    sel = jnp.zeros((n_sel_pad, tl), F32)
    for _ in range(topk):
        mx = jnp.max(score, axis=0, keepdims=True)
        first = jnp.min(jnp.where(score == mx, jf, 1e9), axis=0, keepdims=True)
        pick = jf == first
        sel = jnp.where(pick, 1.0, sel)
        score = jnp.where(pick, -jnp.inf, score)
    return jnp.where(allowed, sel, 0.0)


def _nsa_prompt_kernel(q_ref, sl_ref, qts_ref, qtw_ref, kc_ref, vc_ref, ks_ref, vs_ref, kw_ref, vw_ref, at_ref,
                       nbg_ref, gx_ref, o_ref, m_ref, acc_ref, br_ref, *, tq, n_sel, topk):
    rows = NSA_GROUP * tq
    hd = NSA_HEAD_DIM
    qi = pl.program_id(2)
    t0 = qi * tq
    q128 = q_ref[0, 0].reshape(rows, LANES)
    slope = sl_ref[0][:, :1]
    nbp = kc_ref.shape[2]

    s = _dot_nt(q128[:, :hd], kc_ref[0, 0])
    tok = t0 + (_iota((rows, nbp), 0) & (tq - 1))
    dist = tok - (_iota((rows, nbp), 1) * CMP_STRIDE + (CMP_BLOCK - 1))
    valid = dist >= 0
    s = jnp.where(valid, s - slope * dist.astype(F32), NEG_BIG)
    p = jnp.where(valid, jnp.exp(s - jnp.max(s, axis=-1, keepdims=True)), 0.0)
    p = p / jnp.maximum(jnp.sum(p, axis=-1, keepdims=True), 1e-30)
    o_cmp = _dot(p.astype(MXU_DT), vc_ref[0, 0])

    def normalised(acc):
        return acc[:, :hd] / jnp.maximum(acc[:, hd:], 1e-30)

    lane = _iota((rows, LANES), 1)
    qw = jnp.where(lane < hd, q128, qtw_ref[0])
    span = WINDOW + tq
    ws = pl.multiple_of(jnp.maximum(t0 - WINDOW, 0), tq)
    sw = _dot_nt(qw, kw_ref[0, 0, pl.ds(ws, span), :])
    d = (_iota((rows, span), 0) & (tq - 1)) - _iota((rows, span), 1) + (t0 - ws)
    sw = jnp.where((d >= 0) & (d < WINDOW), sw, NEG_BIG)
    pw = jnp.exp(sw - jnp.max(sw, axis=-1, keepdims=True))
    o_win = normalised(_dot(pw.astype(MXU_DT), vw_ref[0, 0, pl.ds(ws, span), :]))

    sel = _topk_block_mask(p, at_ref, t0, tq=tq, tl=tq, n_sel=n_sel, topk=topk)
    sel_t = jnp.concatenate([jnp.zeros_like(sel), sel], axis=0).astype(MXU_DT)
    spread = ((_iota((rows, tq), 0) & (tq - 1)) == _iota((rows, tq), 1)).astype(MXU_DT)
    selb = _dot_nt(spread, sel_t)
    neg = jnp.where((lane >= hd) & (selb < 0.5), NEG_BIG, 0.0)
    qa = jnp.concatenate([(q128.astype(F32) + neg).astype(MXU_DT), qts_ref[0]], axis=1)

    tk = PROMPT_SEL_TK
    kd = t0 // tk
    k0 = pl.multiple_of(kd * tk, tk)
    s = _dot_nt(qa, ks_ref[0, 0, pl.ds(k0, tk), :])
    causal = ((_iota((rows, tk), 0) & (tq - 1)) - _iota((rows, tk), 1) + (t0 - k0)) >= 0
    s = jnp.where(causal, s, NEG_BIG)
    m = jnp.max(s, axis=-1, keepdims=True)
    m_ref[...] = jnp.broadcast_to(m, m_ref.shape)
    acc_ref[...] = _dot(jnp.exp(s - m).astype(MXU_DT), vs_ref[0, 0, pl.ds(k0, tk), :])

    def sel_body(kt, carry):
        ka = pl.multiple_of(kt * tk, tk)
        s = _dot_nt(qa, ks_ref[0, 0, pl.ds(ka, tk), :])
        m_prev = m_ref[...]
        m_new = jnp.maximum(m_prev, jnp.max(s, axis=-1, keepdims=True))
        p = jnp.exp(s - jnp.concatenate([m_new] * (tk // LANES), axis=1))
        acc_ref[...] = jnp.exp(m_prev - m_new) * acc_ref[...] + _dot(p.astype(MXU_DT), vs_ref[0, 0, pl.ds(ka, tk), :])
        m_ref[...] = m_new
        return carry

    lax.fori_loop(0, kd, sel_body, 0)
    o_sel = normalised(acc_ref[...])

    _gate_and_store(o_ref, br_ref, nbg_ref, gx_ref, (o_cmp, o_sel, o_win), tq)


def _slope_digits(tq, tile, lane0):
    hh = jnp.arange(1, NSA_HEADS + 1, dtype=F32)
    slopes = jnp.exp2(-8.0 * hh / NSA_HEADS)
    pieces = jnp.stack(_split(slopes, 3), axis=1).astype(F32)
    six = jnp.concatenate([pieces * float(tile), pieces], axis=1)
    tab = jnp.zeros((NSA_HEADS, LANES), F32).at[:, lane0:lane0 + 6].set(six)
    tab = tab.reshape(NSA_KV_HEADS, NSA_GROUP, 1, LANES)
    return jnp.broadcast_to(tab, (NSA_KV_HEADS, NSA_GROUP, tq, LANES)).reshape(
        NSA_KV_HEADS, NSA_GROUP * tq, LANES).astype(MXU_DT)


def _key_digits(t, tile, lane0, width, onehot):
    r = jnp.arange(t)
    lane = jnp.arange(width)[None, :]
    tab = jnp.zeros((t, width), F32)
    if onehot:
        tab = jnp.where(lane - NSA_HEAD_DIM == (r // SEL_BLOCK)[:, None], 1.0, tab)
    hi = (r // tile).astype(F32)[:, None]
    lo = (r % tile).astype(F32)[:, None]
    tab = jnp.where((lane >= lane0) & (lane < lane0 + 3), hi, tab)
    tab = jnp.where((lane >= lane0 + 3) & (lane < lane0 + 6), lo, tab)
    return tab.astype(MXU_DT)


def nsa_attention_prompt(q128, slopes, kc, vc, ks_aug, vs, kw_aug, vw, h3, nbg_col, gate_expand, *, n_sel):
    bsz, _, _, t, _ = q128.shape
    tq = PROMPT_TQ
    hd = NSA_HEAD_DIM
    nbp = kc.shape[2]
    assert n_sel <= 64 and t % PROMPT_SEL_TK == 0 and t >= WINDOW + tq and PROMPT_SEL_TK % tq == 0
    rows = NSA_GROUP * tq
    at = _imp_to_sel_matrix(64, nbp)
    kern = functools.partial(_nsa_prompt_kernel, tq=tq, n_sel=n_sel, topk=min(NSA_TOPK, n_sel))
    full = lambda n, w: pl.BlockSpec((1, 1, n, w), lambda b, kh, i: (b, kh, 0, 0))
    per_head = pl.BlockSpec((1, rows, LANES), lambda b, kh, i: (kh, 0, 0))
    gw = NSA_GROUP * hd
    return pl.pallas_call(
        kern,
        grid=(bsz, NSA_KV_HEADS, t // tq),
        in_specs=[pl.BlockSpec((1, 1, NSA_GROUP, tq, LANES), lambda b, kh, i: (b, kh, 0, i, 0)),
                  per_head, per_head, per_head,
                  full(nbp, hd), full(nbp, hd), full(t, 2 * LANES), full(t, LANES), full(t, LANES), full(t, LANES),
                  pl.BlockSpec((64, nbp), lambda b, kh, i: (0, 0)),
                  pl.BlockSpec((1, tq, LANES), lambda b, kh, i: (b, i, nbg_col)),
                  pl.BlockSpec((3, LANES, gw), lambda b, kh, i: (0, 0, kh))],
        out_specs=pl.BlockSpec((1, tq, gw), lambda b, kh, i: (b, i, kh)),
        out_shape=jax.ShapeDtypeStruct((bsz, t, NSA_WIDTH), MXU_DT),
        scratch_shapes=[pltpu.VMEM((rows, LANES), F32), pltpu.VMEM((rows, LANES), F32),
                        pltpu.VMEM((3, tq, gw), F32)],
        compiler_params=_cparams("parallel", "parallel", "arbitrary"),
        name="nsa_attention_prompt",
    )(q128, slopes, _slope_digits(tq, SEL_TK, 0), _slope_digits(tq, WIN_TK, hd),
      kc, vc, ks_aug, vs, kw_aug, vw, at, h3, gate_expand)


def _nsa_paged_pre_kernel(q_ref, sl_ref, kc_ref, vc_ref, kw_ref, vw_ref, at_ref,
                          ocmp_ref, owin_ref, sel_ref, *, tq, qpos0, wpos0, n_sel, topk):
    rows = NSA_GROUP * tq
    hd = NSA_HEAD_DIM
    q = q_ref[0, 0].reshape(rows, hd)
    slope = sl_ref[0][:, :1]
    nbp = kc_ref.shape[2]
    tw = kw_ref.shape[2]

    s = _dot_nt(q, kc_ref[0, 0])
    tok = qpos0 + (_iota((rows, nbp), 0) & (tq - 1))
    dist = tok - (_iota((rows, nbp), 1) * CMP_STRIDE + (CMP_BLOCK - 1))
    valid = dist >= 0
    s = jnp.where(valid, s - slope * dist.astype(F32), NEG_BIG)
    p = jnp.where(valid, jnp.exp(s - jnp.max(s, axis=-1, keepdims=True)), 0.0)
    p = p / jnp.maximum(jnp.sum(p, axis=-1, keepdims=True), 1e-30)
    o_cmp = _dot(p.astype(MXU_DT), vc_ref[0, 0])
    sel_ref[0, 0] = _topk_block_mask(p, at_ref, qpos0, tq=tq, tl=sel_ref.shape[3], n_sel=n_sel, topk=topk)

    sw = _dot_nt(q, kw_ref[0, 0])
    d = (qpos0 - wpos0) + (_iota((rows, tw), 0) & (tq - 1)) - _iota((rows, tw), 1)
    valid = (d >= 0) & (d < WINDOW)
    sw = jnp.where(valid, sw - slope * d.astype(F32), NEG_BIG)
    pw = jnp.where(valid, jnp.exp(sw - jnp.max(sw, axis=-1, keepdims=True)), 0.0)
    o_win = _dot(pw.astype(MXU_DT), vw_ref[0, 0]) / jnp.maximum(jnp.sum(pw, axis=-1, keepdims=True), 1e-30)

    for g in range(NSA_GROUP):
        ocmp_ref[0, :, g * hd:(g + 1) * hd] = o_cmp[g * tq:(g + 1) * tq]
        owin_ref[0, :, g * hd:(g + 1) * hd] = o_win[g * tq:(g + 1) * tq]


def _nsa_paged_sel_kernel(*refs, n_in, tq, qpos0, n_tiles):
    hit_ref = refs[1]
    refs = refs[2:]
    q_ref, sl_ref, sel_ref = refs[:3]
    pages = refs[3:3 + n_in]
    (ktn_ref, vtn_ref, gain_ref, ocmp_ref, owin_ref, nbg_ref, gx_ref,
     o_ref, m_ref, l_ref, acc_ref, kt_ref, vt_ref, br_ref) = refs[3 + n_in:]
    rows = NSA_GROUP * tq
    hd = NSA_HEAD_DIM
    tk = n_in * PAGE_SIZE
    nblk = tk // SEL_BLOCK
    i = pl.program_id(1)

    @pl.when(i == 0)
    def _():
        m_ref[...] = jnp.full(m_ref.shape, NEG_BIG, F32)
        l_ref[...] = jnp.zeros(l_ref.shape, F32)
        acc_ref[...] = jnp.zeros(acc_ref.shape, F32)

    @pl.when(i < n_tiles)
    def _():
        for u in range(n_in):
            sl = slice(u * PAGE_SIZE, (u + 1) * PAGE_SIZE)
            kt_ref[:, sl] = _head_rms_t(pages[u][0, 0], gain_ref[...]).astype(kt_ref.dtype)
            vt_ref[:, sl] = pages[u][0, 1].astype(vt_ref.dtype)

    @pl.when(i == n_tiles)
    def _():
        kt_ref[...] = ktn_ref[0]
        vt_ref[...] = vtn_ref[0]

    k0 = i * tk
    d = (qpos0 - k0) + (_iota((rows, tk), 0) & (tq - 1)) - _iota((rows, tk), 1)
    causal = d >= 0
    dist = d.astype(F32)
    spread = ((_iota((rows, LANES), 0) & (tq - 1)) == _iota((rows, LANES), 1)).astype(MXU_DT)
    in_tile = (_iota((nblk, tk), 0) == (_iota((nblk, tk), 1) >> 6)).astype(MXU_DT)
    blk0 = pl.multiple_of(i * nblk, nblk)
    reps = tk // LANES
    def head_update(kvh):
        hs = slice(kvh * hd, (kvh + 1) * hd)
        q = q_ref[0, kvh].reshape(rows, hd)
        s = _dot(q, kt_ref[hs, :])
        mine = _dot_nt(spread, sel_ref[0, kvh, pl.ds(blk0, nblk), :].astype(MXU_DT)).astype(MXU_DT)
        valid = (_dot(mine, in_tile) > 0.5) & causal
        s = jnp.where(valid, s - sl_ref[kvh][:, :1] * dist, NEG_BIG)
        m_prev = m_ref[kvh]
        m_new = jnp.maximum(m_prev, jnp.max(s, axis=-1, keepdims=True))
        alpha = jnp.exp(m_prev - m_new)
        p = jnp.where(valid, jnp.exp(s - jnp.concatenate([m_new] * reps, axis=1)), 0.0)
        l_ref[kvh] = alpha * l_ref[kvh] + jnp.sum(p, axis=-1, keepdims=True)
        acc_ref[kvh] = alpha[:, :hd] * acc_ref[kvh] + _dot_nt(p.astype(MXU_DT), vt_ref[hs, :])
        m_ref[kvh] = m_new

    for kvh in range(NSA_KV_HEADS):
        pl.when(hit_ref[pl.program_id(0), kvh * (n_tiles + 1) + i] > 0)(functools.partial(head_update, kvh))

    @pl.when(i == n_tiles)
    def _():
        br_ref[0] = ocmp_ref[0]
        br_ref[2] = owin_ref[0]
        for kvh in range(NSA_KV_HEADS):
            o_sel = acc_ref[kvh] / jnp.maximum(l_ref[kvh][:, :hd], 1e-30)
            for g in range(NSA_GROUP):
                c0 = (kvh * NSA_GROUP + g) * hd
                br_ref[1, :, c0:c0 + hd] = o_sel[g * tq:(g + 1) * tq]
        pieces = _split(_sigmoid(nbg_ref[0]), 2)
        acc = None
        for br in range(3):
            term = sum(_dot(p, gx_ref[br]) for p in pieces) * br_ref[br]
            acc = term if acc is None else acc + term
        o_ref[0] = acc.astype(o_ref.dtype)


def nsa_attention_paged(qn, slopes, kc, vc, kw, vw, pool_t, pages, kt_new, vt_new, gain_sel, h3, nbg_col,
                        gate_expand, *, tq, t_real, qpos0, wpos0, n_sel):
    bsz, _, _, tql, hd = qn.shape
    assert tql == tq
    nbp, tw = kc.shape[2], kw.shape[2]
    n_pages = pages.shape[1]
    tk = kt_new.shape[2]
    n_in = tk // PAGE_SIZE
    n_tiles = n_pages // n_in
    assert n_pages % n_in == 0 and qpos0 == n_pages * PAGE_SIZE
    n_sel_pad = -(-max(n_sel, (n_tiles + 1) * tk // SEL_BLOCK) // 64) * 64
    rows = NSA_GROUP * tq
    gw = NSA_GROUP * hd
    at = _imp_to_sel_matrix(n_sel_pad, nbp)
    full = lambda n: pl.BlockSpec((1, 1, n, hd), lambda b, kh: (b, kh, 0, 0))
    obr = pl.BlockSpec((1, tq, gw), lambda b, kh: (b, 0, kh))
    o_cmp, o_win, sel = pl.pallas_call(
        functools.partial(_nsa_paged_pre_kernel, tq=tq, qpos0=qpos0, wpos0=wpos0, n_sel=n_sel,
                          topk=min(NSA_TOPK, n_sel)),
        grid=(bsz, NSA_KV_HEADS),
        in_specs=[pl.BlockSpec((1, 1, NSA_GROUP, tq, hd), lambda b, kh: (b, kh, 0, 0, 0)),
                  pl.BlockSpec((1, rows, LANES), lambda b, kh: (kh, 0, 0)),
                  full(nbp), full(nbp), full(tw), full(tw),
                  pl.BlockSpec((n_sel_pad, nbp), lambda b, kh: (0, 0))],
        out_specs=[obr, obr, pl.BlockSpec((1, 1, n_sel_pad, LANES), lambda b, kh: (b, kh, 0, 0))],
        out_shape=[jax.ShapeDtypeStruct((bsz, tq, NSA_WIDTH), F32), jax.ShapeDtypeStruct((bsz, tq, NSA_WIDTH), F32),
                   jax.ShapeDtypeStruct((bsz, NSA_KV_HEADS, n_sel_pad, LANES), F32)],
        compiler_params=_cparams("parallel", "parallel"),
        name="nsa_paged_pre",
    )(qn, slopes, kc, vc, kw, vw, at)

    nblk = tk // SEL_BLOCK
    hit = sel[:, :, :(n_tiles + 1) * nblk, :t_real].reshape(bsz, NSA_KV_HEADS, n_tiles + 1, nblk * t_real)
    hit = (jnp.max(hit, axis=-1) > 0).astype(jnp.int32).reshape(bsz, NSA_KV_HEADS * (n_tiles + 1))

    page_spec = lambda u: pl.BlockSpec(
        (1, 2, NSA_KV_WIDTH, PAGE_SIZE),
        lambda b, i, pt, ht: (pt[b, jnp.minimum(i, n_tiles - 1) * n_in + u], 1, 0, 0))
    const = lambda *shape: pl.BlockSpec(shape, lambda b, i, pt, ht: (0,) * len(shape))
    per_seq = lambda *shape: pl.BlockSpec((1,) + shape, lambda b, i, pt, ht: (b,) + (0,) * len(shape))
    return pl.pallas_call(
        functools.partial(_nsa_paged_sel_kernel, n_in=n_in, tq=tq, qpos0=qpos0, n_tiles=n_tiles),
        grid_spec=pltpu.PrefetchScalarGridSpec(
            num_scalar_prefetch=2,
            grid=(bsz, n_tiles + 1),
            in_specs=[per_seq(NSA_KV_HEADS, NSA_GROUP, tq, hd), const(NSA_KV_HEADS, rows, LANES),
                      per_seq(NSA_KV_HEADS, n_sel_pad, LANES)]
                     + [page_spec(u) for u in range(n_in)]
                     + [per_seq(NSA_KV_WIDTH, tk), per_seq(NSA_KV_WIDTH, tk), const(NSA_KV_WIDTH, LANES),
                        per_seq(tq, NSA_WIDTH), per_seq(tq, NSA_WIDTH),
                        pl.BlockSpec((1, tq, LANES), lambda b, i, pt, ht: (b, 0, nbg_col)),
                        const(3, LANES, NSA_WIDTH)],
            out_specs=per_seq(tq, NSA_WIDTH),
            scratch_shapes=[pltpu.VMEM((NSA_KV_HEADS, rows, LANES), F32), pltpu.VMEM((NSA_KV_HEADS, rows, LANES), F32),
                            pltpu.VMEM((NSA_KV_HEADS, rows, hd), F32),
                            pltpu.VMEM((NSA_KV_WIDTH, tk), MXU_DT), pltpu.VMEM((NSA_KV_WIDTH, tk), MXU_DT),
                            pltpu.VMEM((3, tq, NSA_WIDTH), F32)]),
        out_shape=jax.ShapeDtypeStruct((bsz, tq, NSA_WIDTH), MXU_DT),
        compiler_params=_cparams("parallel", "arbitrary"),
        name="nsa_paged_sel",
    )(pages, hit, qn, slopes, sel, *([pool_t] * n_in), kt_new, vt_new, _gain_t(gain_sel),
      o_cmp, o_win, h3, gate_expand)


def _layout(d_model):
    hw, rw = HG_HEADS * HG_DK, RET_HEADS * RET_DK
    names = [("hq", hw), ("hf", hw), ("hi", hw), ("hg", hw), ("nq", NSA_WIDTH), ("nkv", 4 * NSA_KV_WIDTH),
             ("nwkv", 2 * NSA_KV_WIDTH), ("rq", rw), ("rk", rw), ("rv", rw), ("rg", rw), ("mg", 3 * d_model),
             ("nbg", 3 * NSA_HEADS)]
    lay, off = {}, 0
    for name, width in names:
        lay[name] = off
        off += width
    lay["used"] = off
    lay["total"] = -(-off // 1024) * 1024
    return lay


def _block_sum_matrix(width):
    i = jnp.arange(width)
    return (i[:, None] // NSA_HEAD_DIM == i[None, :] // NSA_HEAD_DIM).astype(MXU_DT)


def _gate_expand_matrix():
    c = jnp.arange(LANES)[:, None]
    col = jnp.arange(NSA_WIDTH)[None, :]
    return jnp.stack([(c == (col // NSA_HEAD_DIM) * 3 + br) for br in range(3)]).astype(MXU_DT)


def _slope_rows(tq):
    hh = jnp.arange(1, NSA_HEADS + 1, dtype=F32)
    slopes = jnp.exp2(-8.0 * hh / NSA_HEADS).reshape(NSA_KV_HEADS, NSA_GROUP)
    rows = jnp.repeat(slopes, tq, axis=1)
    return jnp.broadcast_to(rows[:, :, None], (NSA_KV_HEADS, NSA_GROUP * tq, LANES))


def _pad_rows(x, n):
    return jnp.pad(x, ((0, 0), (0, n - x.shape[1])) + ((0, 0),) * (x.ndim - 2))


def _trunk_layer(x, lw, lay, consts, nsa_fn, hg_state, ret_state, *, seq_tiles):
    bsz, t, d = x.shape
    tm, tt_h, tt_r, t_valid = seq_tiles
    x2 = x.reshape(bsz * t, d)
    wts, layer = lw["weights"], lw["layer"]
    h = rms_matmul(x2, lw["norm_attn"], wts["w_in"], layer, lay, tm=tm, tn=512)
    np_ = h.shape[1]
    h3 = h.reshape(bsz, t, np_)
    tp = -(-t // tt_h) * tt_h
    h3p = _pad_rows(h3, tp) if tp != t else h3
    oa, hg_new = hgrn_mixer(h3p, lay, lw["lb"], lw["hg_out_norm"], hg_state, tt=tt_h,
                            t_valid=None if tp == t else t_valid)
    tpr = -(-t // tt_r) * tt_r
    h3r = _pad_rows(h3, tpr) if tpr != t else h3
    oc, ret_new = retention_mixer(h3r, lay, consts["lg_tab"], ret_state, tt=tt_r,
                                  n_valid=tt_r if tpr == t else t_valid)
    ob = nsa_fn(h3)
    flat = lambda a: a[:, :t].reshape(bsz * t, a.shape[-1])
    merged = merge_branches(h, lay, flat(oa), flat(ob), flat(oc),
                            wts["w_branch_hg"], wts["w_branch_nsa"], wts["w_branch_ret"], layer, tm=tm, tn=512)
    x2 = matmul_res(merged, wts["w_out"], layer, x2, tm=tm, tn=512)
    hid = ffn_up(x2, lw["norm_ffn"], wts["w_gate"], wts["w_up"], layer, tm=tm, tn=512)
    x2 = matmul_res(hid, wts["w_down"], layer, x2, tm=tm, tn=512)
    return x2.reshape(bsz, t, d), h3, hg_new, ret_new


def kernel(x_prompt, x_sample, cache_nsa, cache_win, state_hgrn, state_ret, page_table,
           norm_attn, w_in, hgrn_lb_logits, hgrn_out_norm, nsa_q_norm, nsa_k_norm, nsa_cmp_w,
           w_branch_hg, w_branch_nsa, w_branch_ret, w_out, norm_ffn, w_gate, w_up, w_down):
    depth = w_in.shape[0]
    bp, tp, d = x_prompt.shape
    bs, ts, _ = x_sample.shape
    n_pool = cache_nsa.shape[1]
    n_pages = page_table.shape[1]
    past = n_pages * PAGE_SIZE
    wbuf = cache_win.shape[2]
    lay = _layout(d)

    sm = jax.nn.softmax(hgrn_lb_logits.astype(F32), axis=0)
    lower_bounds = jnp.clip(jnp.cumsum(sm, axis=0) - sm[0:1], 0.0, 1.0 - 1e-6)

    tt_p = 256 if tp % 256 == 0 else tp
    lg = jnp.log1p(-jnp.exp2(-5.0 - jnp.arange(RET_HEADS, dtype=F32)))
    consts = {
        "gate_expand": _gate_expand_matrix(),
        "lg_tab": jnp.broadcast_to(lg[:, None, None], (RET_HEADS, 1, max(tt_p, LANES))),
    }
    bsum = _block_sum_matrix(NSA_KV_WIDTH)
    pool_t = jnp.transpose(cache_nsa, (0, 1, 3, 4, 5, 2)).reshape(depth * n_pool, 4, NSA_KV_WIDTH, PAGE_SIZE)
    win3 = cache_win.reshape(depth, bs, wbuf, 2 * NSA_KV_WIDTH)

    ts_pad = 16
    tq_s = 16
    n_sel_p = -(-tp // SEL_BLOCK)
    n_sel_s = past // SEL_BLOCK + -(-ts // SEL_BLOCK)
    tk_tail = 2048 if past % 2048 == 0 else 512
    tq_p = 128 if tp % 128 == 0 else tp

    weights = {
        "w_in": jnp.swapaxes(w_in, 1, 2).astype(MXU_DT).reshape(depth * w_in.shape[2], d),
        "w_branch_hg": w_branch_hg.astype(MXU_DT), "w_branch_nsa": w_branch_nsa.astype(MXU_DT),
        "w_branch_ret": w_branch_ret.astype(MXU_DT), "w_out": w_out.astype(MXU_DT),
        "w_gate": w_gate.astype(MXU_DT), "w_up": w_up.astype(MXU_DT), "w_down": w_down.astype(MXU_DT),
    }
    assert w_in.shape[2] == lay["used"]

    xp, xs = x_prompt, x_sample
    outs = {k: [] for k in ("kv_p", "kv_s", "win_p", "win_s", "hg_p", "hg_s", "ret_p", "ret_s")}
    for l in range(depth):
        lw = {
            "weights": weights, "layer": l,
            "norm_attn": norm_attn[l], "lb": lower_bounds[l].reshape(1, -1),
            "hg_out_norm": hgrn_out_norm[l], "norm_ffn": norm_ffn[l],
        }
        gq, gk, cw = nsa_q_norm[l], nsa_k_norm[l], nsa_cmp_w[l]
        col_cmp = lay["nkv"] // (2 * NSA_KV_WIDTH)
        col_sel = col_cmp + 1
        col_win = lay["nwkv"] // (2 * NSA_KV_WIDTH)
        nbg_col = lay["nbg"] // LANES

        def nsa_prompt(h3):
            tr = 512 if tp % 512 == 0 else tp
            zeros16 = jnp.zeros((bp, CMP_STRIDE, 2 * NSA_KV_WIDTH), F32)
            kc, vc = cmp_prep(h3, col_cmp, zeros16, cw, gk[0], bsum, tr=tr)
            if tp % PROMPT_SEL_TK == 0 and n_sel_p <= 64 and tp >= WINDOW + PROMPT_TQ:
                qn = q_prep(h3, lay, gq, bsum, tr=tr, width=LANES)
                ks, vs = kv_prep(h3, col_sel, gk[1], bsum, tr=tr,
                                 aug=_key_digits(tp, SEL_TK, LANES, 2 * LANES, True))
                kw, vw = kv_prep(h3, col_win, gk[2], bsum, tr=tr,
                                 aug=_key_digits(tp, WIN_TK, NSA_HEAD_DIM, LANES, False))
                return nsa_attention_prompt(qn, _slope_rows(PROMPT_TQ), kc, vc, ks, vs, kw, vw,
                                            h3, nbg_col, consts["gate_expand"], n_sel=n_sel_p)
            qn = q_prep(h3, lay, gq, bsum, tr=tr)
            ks, vs = kv_prep(h3, col_sel, gk[1], bsum, tr=tr)
            kw, vw = kv_prep(h3, col_win, gk[2], bsum, tr=tr)
            return nsa_attention(qn, _slope_rows(tq_p), kc, vc, ks, vs, kw, vw, h3, nbg_col, consts["gate_expand"],
                                 tq=tq_p, sel_tk=SEL_TK, qpos0=0, wpos0=0, n_sel=n_sel_p)

        def nsa_sample(h3):
            pages = page_table + l * n_pool
            h16 = _pad_rows(h3, ts_pad)
            qn = q_prep(h16, lay, gq, bsum, tr=ts_pad)
            w2 = 2 * NSA_KV_WIDTH
            tail16 = h16[:, :, lay["nkv"]:lay["nkv"] + w2].reshape(bs, ts_pad, 2, NSA_KV_WIDTH)
            tail_t = jnp.pad(jnp.transpose(tail16, (0, 2, 3, 1)), ((0, 0), (0, 0), (0, 0), (0, PAGE_SIZE - ts_pad)))
            kc, vc = cmp_prep_paged(pool_t, pages, tail_t, cw, gk[0])
            htail = _pad_rows(h3[:, :, col_sel * w2:(col_sel + 1) * w2], tk_tail)
            ks_tail, vs_tail = kv_prep(htail, 0, gk[1], bsum, tr=tk_tail)
            feature_major = lambda a: jnp.swapaxes(a, 2, 3).reshape(bs, NSA_KV_WIDTH, tk_tail)
            kw_old, vw_old = kv_prep(win3[l], 0, gk[2], bsum, tr=wbuf)
            hw_new = _pad_rows(h3[:, :, col_win * w2:(col_win + 1) * w2], WIN_TK)
            kw_new, vw_new = kv_prep(hw_new, 0, gk[2], bsum, tr=WIN_TK)
            kw = jnp.concatenate([kw_old, kw_new], axis=2)
            vw = jnp.concatenate([vw_old, vw_new], axis=2)
            return nsa_attention_paged(qn, _slope_rows(tq_s), kc, vc, kw, vw, pool_t, pages,
                                       feature_major(ks_tail), feature_major(vs_tail), gk[1], h16, nbg_col,
                                       consts["gate_expand"], tq=tq_s, t_real=ts, qpos0=past, wpos0=past - wbuf,
                                       n_sel=n_sel_s)

        zeros_state = jnp.zeros((bp, HG_HEADS, HG_DK, HG_DV), F32)
        tm_p = 1024 if (bp * tp) % 1024 == 0 else bp * tp
        xp, h3p, hgp, rtp = _trunk_layer(xp, lw, lay, consts, nsa_prompt, zeros_state, zeros_state,
                                         seq_tiles=(tm_p, tt_p, tt_p, tp))
        xs, h3s, hgs, rts = _trunk_layer(xs, lw, lay, consts, nsa_sample, state_hgrn[l], state_ret[l],
                                         seq_tiles=(bs * ts, ts_pad, ts_pad, ts))

        kv_cols = slice(lay["nkv"], lay["nkv"] + 4 * NSA_KV_WIDTH)
        win_cols = slice(lay["nwkv"], lay["nwkv"] + 2 * NSA_KV_WIDTH)
        outs["kv_p"].append(h3p[:, :, kv_cols].reshape(bp, tp, 4, NSA_KV_HEADS, NSA_HEAD_DIM))
        outs["kv_s"].append(h3s[:, :, kv_cols].reshape(bs, ts, 4, NSA_KV_HEADS, NSA_HEAD_DIM))
        wlen = min(WINDOW, tp)
        outs["win_p"].append(h3p[:, tp - wlen:, win_cols].reshape(bp, wlen, 2, NSA_KV_HEADS, NSA_HEAD_DIM))
        ctx = jnp.concatenate([win3[l], h3s[:, :, win_cols]], axis=1)[:, ts:]
        outs["win_s"].append(ctx.reshape(bs, wbuf, 2, NSA_KV_HEADS, NSA_HEAD_DIM))
        outs["hg_p"].append(hgp); outs["hg_s"].append(hgs)
        outs["ret_p"].append(rtp); outs["ret_s"].append(rts)

    st = lambda k: jnp.stack(outs[k])
    return (xp, xs, st("kv_p"), st("kv_s"), st("win_p"), st("win_s"),
            st("hg_p"), st("hg_s"), st("ret_p"), st("ret_s"))
```

```python
import functools

import numpy as np

import jax
import jax.numpy as jnp
from jax import lax
from jax.experimental import pallas as pl
from jax.experimental.pallas import tpu as pltpu

F32 = jnp.float32
MXU_DT = jnp.bfloat16

HG_HEADS, HG_DK, HG_DV = 8, 128, 128
NSA_HEADS, NSA_KV_HEADS, NSA_GROUP, NSA_HEAD_DIM = 16, 4, 4, 64
NSA_WIDTH = NSA_HEADS * NSA_HEAD_DIM
NSA_KV_WIDTH = NSA_KV_HEADS * NSA_HEAD_DIM
CMP_BLOCK, CMP_STRIDE, SEL_BLOCK, NSA_TOPK, WINDOW = 32, 16, 64, 16, 512
RET_HEADS, RET_DK, RET_DV = 8, 128, 128
PAGE_SIZE = 128
EPS = 1e-6
NEG_BIG = -1e30
POS_BIG = 1e30
MIN_F = 1e-20
LOWEST = -3.0e38

LANES = 128
VMEM_LIMIT_BYTES = 56 * 1024 * 1024

HG_SUB_SHIFT = 4
HG_SUB = 1 << HG_SUB_SHIFT
PAGES_PER_STEP = 16
IN_PROJ_TN = 1024
SEL_TK = 256
PROMPT_SEL_TK = 512
PROMPT_TQ = 256
RET_HPS = 2
WIN_TK = 128


def _cparams(*sem):
    return pltpu.CompilerParams(dimension_semantics=sem, vmem_limit_bytes=VMEM_LIMIT_BYTES)


def _dot(a, b):
    return jnp.dot(a, b, preferred_element_type=F32)


def _dot_nt(a, b):
    return lax.dot_general(a, b, (((1,), (1,)), ((), ())), preferred_element_type=F32)


def _split(x, n):
    out = []
    r = x
    for _ in range(n):
        p = r.astype(MXU_DT)
        out.append(p)
        r = r - p.astype(F32)
    return out


def _sigmoid(x):
    return 1.0 / (1.0 + jnp.exp(-x))


def _silu(x):
    return x * _sigmoid(x)


def _iota(shape, dim):
    return lax.broadcasted_iota(jnp.int32, shape, dim)


def _eye(n):
    return (_iota((n, n), 0) == _iota((n, n), 1)).astype(MXU_DT)


def _transpose_exact(x):
    eye = _eye(x.shape[1])
    return sum(_dot_nt(eye, p) for p in _split(x, 3))


def _head_rms(x, bsum, gain):
    ssq = sum(_dot(p, bsum) for p in _split(x * x, 2))
    return x * lax.rsqrt(ssq * (1.0 / NSA_HEAD_DIM) + EPS) * gain


def _rms_matmul_kernel(x_ref, g_ref, w_ref, o_ref, xn_ref):
    @pl.when(pl.program_id(1) == 0)
    def _():
        x = x_ref[...]
        ms = jnp.mean(x * x, axis=-1, keepdims=True)
        xn_ref[...] = (x * lax.rsqrt(ms + EPS) * g_ref[...]).astype(xn_ref.dtype)

    o_ref[...] = _dot_nt(xn_ref[...], w_ref[...])


def rms_matmul(x, gain, wt, layer, lay, *, tm, tn):
    m, d = x.shape
    assert tn == IN_PROJ_TN and lay["used"] % 16 == 0
    nb_pre = lay["post_col"] // tn
    nb_all = lay["total"] // tn
    base = layer * lay["used"]

    def w_row(i, j):
        row = base + jnp.where(j < nb_pre, j * tn, lay["post_row"] + (j - nb_pre) * tn)
        return (pl.multiple_of(row, 16), 0)

    return pl.pallas_call(
        _rms_matmul_kernel,
        grid=(m // tm, nb_all),
        in_specs=[pl.BlockSpec((tm, d), lambda i, j: (i, 0)),
                  pl.BlockSpec((1, d), lambda i, j: (0, 0)),
                  pl.BlockSpec((pl.Element(tn), pl.Element(d)), w_row)],
        out_specs=pl.BlockSpec((tm, tn), lambda i, j: (i, j)),
        out_shape=jax.ShapeDtypeStruct((m, lay["total"]), F32),
        scratch_shapes=[pltpu.VMEM((tm, d), MXU_DT)],
        compiler_params=_cparams("parallel", "arbitrary"),
        name="in_proj",
    )(x, gain.reshape(1, d), wt)


def _ffn_up_kernel(x_ref, g_ref, wg_ref, wu_ref, o_ref, xn_ref):
    @pl.when(pl.program_id(1) == 0)
    def _():
        x = x_ref[...]
        ms = jnp.mean(x * x, axis=-1, keepdims=True)
        xn_ref[...] = (x * lax.rsqrt(ms + EPS) * g_ref[...]).astype(xn_ref.dtype)

    xn = xn_ref[...]
    o_ref[...] = (_silu(_dot(xn, wg_ref[...])) * _dot(xn, wu_ref[...])).astype(o_ref.dtype)


def _layer_cols(k, tn, layer):
    return pl.BlockSpec((None, k, tn), lambda i, j: (layer, 0, j))


def ffn_up(x, gain, wg, wu, layer, *, tm, tn):
    m, d = x.shape
    f = wg.shape[2]
    return pl.pallas_call(
        _ffn_up_kernel,
        grid=(m // tm, f // tn),
        in_specs=[pl.BlockSpec((tm, d), lambda i, j: (i, 0)),
                  pl.BlockSpec((1, d), lambda i, j: (0, 0)),
                  _layer_cols(d, tn, layer), _layer_cols(d, tn, layer)],
        out_specs=pl.BlockSpec((tm, tn), lambda i, j: (i, j)),
        out_shape=jax.ShapeDtypeStruct((m, f), MXU_DT),
        scratch_shapes=[pltpu.VMEM((tm, d), MXU_DT)],
        compiler_params=_cparams("parallel", "arbitrary"),
        name="ffn_up",
    )(x, gain.reshape(1, d), wg, wu)


def _matmul_res_kernel(a_ref, w_ref, r_ref, o_ref):
    o_ref[...] = r_ref[...] + _dot(a_ref[...], w_ref[...])


def matmul_res(a, w, layer, res, *, tm, tn):
    m, k = a.shape
    n = w.shape[2]
    return pl.pallas_call(
        _matmul_res_kernel,
        grid=(m // tm, n // tn),
        in_specs=[pl.BlockSpec((tm, k), lambda i, j: (i, 0)),
                  _layer_cols(k, tn, layer),
                  pl.BlockSpec((tm, tn), lambda i, j: (i, j))],
        out_specs=pl.BlockSpec((tm, tn), lambda i, j: (i, j)),
        out_shape=jax.ShapeDtypeStruct((m, n), F32),
        compiler_params=_cparams("parallel", "arbitrary"),
        name="matmul_res",
    )(a, w, res)


def _merge_kernel(oa_ref, ob_ref, oc_ref, ga_ref, gb_ref, gc_ref, wa_ref, wb_ref, wc_ref, o_ref):
    o_ref[...] = (_sigmoid(ga_ref[...]) * _dot(oa_ref[...], wa_ref[...])
                  + _sigmoid(gb_ref[...]) * _dot(ob_ref[...], wb_ref[...])
                  + _sigmoid(gc_ref[...]) * _dot(oc_ref[...], wc_ref[...])).astype(o_ref.dtype)


def merge_branches(h, lay, oa, ob, oc, wa, wb, wc, layer, *, tm, tn):
    m = h.shape[0]
    d = wa.shape[2]
    wdt = oa.shape[1]
    ca, cb, cc = ((lay["mg"] + i * d) // tn for i in range(3))
    row = lambda i, j: (i, 0)
    return pl.pallas_call(
        _merge_kernel,
        grid=(m // tm, d // tn),
        in_specs=[pl.BlockSpec((tm, wdt), row), pl.BlockSpec((tm, wdt), row), pl.BlockSpec((tm, wdt), row),
                  pl.BlockSpec((tm, tn), lambda i, j: (i, ca + j)),
                  pl.BlockSpec((tm, tn), lambda i, j: (i, cb + j)),
                  pl.BlockSpec((tm, tn), lambda i, j: (i, cc + j)),
                  _layer_cols(wdt, tn, layer), _layer_cols(wdt, tn, layer), _layer_cols(wdt, tn, layer)],
        out_specs=pl.BlockSpec((tm, tn), lambda i, j: (i, j)),
        out_shape=jax.ShapeDtypeStruct((m, d), MXU_DT),
        compiler_params=_cparams("parallel", "arbitrary"),
        name="merge",
    )(oa, ob, oc, h, h, h, wa, wb, wc)


def _hgrn_kernel(hq_ref, hf_ref, hi_ref, hg_ref, lb_ref, gain_ref, s0_ref, o_ref, sout_ref,
                 st_ref, ut_ref, *, tt, t_valid):
    c = HG_SUB
    nj = tt // c
    tb = pl.program_id(2)

    @pl.when(tb == 0)
    def _():
        st_ref[...] = _transpose_exact(s0_ref[0, 0])

    hq = hq_ref[0]
    lb = lb_ref[...]
    q = _silu(hq)
    sig = _sigmoid(hf_ref[0])
    g = jnp.log(jnp.maximum(lb + (1.0 - lb) * sig, MIN_F))
    kk = (1.0 - lb) * (1.0 - sig)
    v = hi_ref[0]
    row = _iota((tt, HG_DK), 0)
    if t_valid is not None:
        live = (tb * tt + row) < t_valid
        g = jnp.where(live, g, 0.0)
        kk = jnp.where(live, kk, 0.0)

    r2 = _iota((tt, tt), 0)
    c2 = _iota((tt, tt), 1)
    same = (r2 >> HG_SUB_SHIFT) == (c2 >> HG_SUB_SHIFT)
    gp = _split(g, 3)
    tri = (same & (c2 <= r2)).astype(MXU_DT)
    blk = same.astype(MXU_DT)
    b = sum(_dot(tri, p) for p in gp)
    dtot = sum(_dot(blk, p) for p in gp)
    qe = q * jnp.exp(b)
    ke = kk * jnp.exp(dtot - b)
    edec = jnp.exp(dtot)

    q3 = q.reshape(nj, c, HG_DK)
    kk3 = kk.reshape(nj, c, HG_DK)
    b3 = b.reshape(nj, c, HG_DK)
    v3 = v.reshape(nj, c, HG_DV)
    tpos = _iota((nj, c, HG_DK), 1)
    ones = jnp.ones((HG_DK, HG_DV), MXU_DT)
    o3 = jnp.zeros((nj, c, HG_DV), F32)
    for s in range(c):
        m = tpos >= s
        diff = jnp.where(m, b3 - b3[:, s:s + 1, :], 0.0)
        y = jnp.where(m, q3 * kk3[:, s:s + 1, :] * jnp.exp(diff), 0.0)
        z = _dot(y.reshape(tt, HG_DK).astype(MXU_DT), ones).reshape(nj, c, HG_DV)
        o3 = o3 + z * v3[:, s:s + 1, :]
    o = o3.reshape(tt, HG_DV)

    v_b = v.astype(MXU_DT)
    ke_b = ke.astype(MXU_DT)
    for j in range(nj):
        ut_ref[j] = lax.dot_general(v_b[j * c:(j + 1) * c], ke_b[j * c:(j + 1) * c],
                                    (((0,), (0,)), ((), ())), preferred_element_type=F32)

    qe_b = qe.astype(MXU_DT)
    outs = []
    for j in range(nj):
        st = st_ref[...]
        outs.append(_dot_nt(qe_b[j * c:(j + 1) * c], st.astype(MXU_DT)))
        st_ref[...] = st * edec[j * c:j * c + 1, :] + ut_ref[j]
    o = o + jnp.concatenate(outs, axis=0)

    on = o * lax.rsqrt(jnp.mean(o * o, axis=-1, keepdims=True) + EPS) * gain_ref[...]
    o_ref[0] = (on * _silu(hg_ref[0])).astype(o_ref.dtype)

    @pl.when(tb == pl.num_programs(2) - 1)
    def _():
        sout_ref[0, 0] = _transpose_exact(st_ref[...])


def hgrn_mixer(h3, lay, lb, out_gain, s0, *, tt, t_valid):
    bsz, t, _ = h3.shape
    cq, cf, ci, cg = (lay[k] // LANES for k in ("hq", "hf", "hi", "hg"))
    col = lambda c0: (lambda b, hd, tb: (b, tb, c0 + hd))
    kern = functools.partial(_hgrn_kernel, tt=tt, t_valid=t_valid)
    return pl.pallas_call(
        kern,
        grid=(bsz, HG_HEADS, t // tt),
        in_specs=[pl.BlockSpec((1, tt, LANES), col(cq)), pl.BlockSpec((1, tt, LANES), col(cf)),
                  pl.BlockSpec((1, tt, LANES), col(ci)), pl.BlockSpec((1, tt, LANES), col(cg)),
                  pl.BlockSpec((1, HG_DK), lambda b, hd, tb: (0, hd)),
                  pl.BlockSpec((1, HG_DV), lambda b, hd, tb: (0, 0)),
                  pl.BlockSpec((1, 1, HG_DK, HG_DV), lambda b, hd, tb: (b, hd, 0, 0))],
        out_specs=[pl.BlockSpec((1, tt, HG_DV), lambda b, hd, tb: (b, tb, hd)),
                   pl.BlockSpec((1, 1, HG_DK, HG_DV), lambda b, hd, tb: (b, hd, 0, 0))],
        out_shape=[jax.ShapeDtypeStruct((bsz, t, HG_HEADS * HG_DV), MXU_DT),
                   jax.ShapeDtypeStruct((bsz, HG_HEADS, HG_DK, HG_DV), F32)],
        scratch_shapes=[pltpu.VMEM((HG_DV, HG_DK), F32),
                        pltpu.VMEM((tt // HG_SUB, HG_DV, HG_DK), F32)],
        compiler_params=_cparams("parallel", "parallel", "arbitrary"),
        name="hgrn",
    )(h3, h3, h3, h3, lb, out_gain.reshape(1, HG_DV), s0)


def _ret_kernel(q_ref, k_ref, v_ref, g_ref, lg_ref, s0_ref, o_ref, sout_ref, s_ref, *, tt, n_valid):
    tb = pl.program_id(2)

    @pl.when(tb == 0)
    def _():
        s_ref[...] = s0_ref[0]

    rel = (_iota((tt, tt), 0) - _iota((tt, tt), 1)).astype(F32)
    spos = _iota((RET_DK, tt), 1)
    for hh in range(RET_HPS):
        cols = slice(hh * RET_DK, (hh + 1) * RET_DK)
        lgl = lg_ref[hh][:, :LANES]
        lgt = lg_ref[hh][:, :tt]
        q = q_ref[0][:, cols] * (RET_DK ** -0.5)
        k = k_ref[0][:, cols]
        vb = v_ref[0][:, cols].astype(MXU_DT)
        dmat = jnp.where(rel >= 0, jnp.exp(jnp.maximum(rel, 0.0) * lgt), 0.0)
        att = _dot_nt(q.astype(MXU_DT), k.astype(MXU_DT)) * dmat
        q_dec = jnp.exp((_iota((tt, RET_DK), 0) + 1).astype(F32) * lgl)
        s = s_ref[hh]
        o = _dot(att.astype(MXU_DT), vb) + _dot((q * q_dec).astype(MXU_DT), s.astype(MXU_DT))

        k_dec = jnp.where(spos < n_valid, jnp.exp(jnp.maximum(n_valid - 1 - spos, 0).astype(F32) * lgt), 0.0)
        kt = _transpose_exact(k)
        s_ref[hh] = jnp.exp(float(n_valid) * lgl) * s + _dot((kt * k_dec).astype(MXU_DT), vb)

        mu = jnp.mean(o, axis=-1, keepdims=True)
        var = jnp.mean(jnp.square(o - mu), axis=-1, keepdims=True)
        o_ref[0, :, cols] = ((o - mu) * lax.rsqrt(var + EPS) * _silu(g_ref[0][:, cols])).astype(o_ref.dtype)

    @pl.when(tb == pl.num_programs(2) - 1)
    def _():
        sout_ref[0] = s_ref[...]


def retention_mixer(h3, lay, lg_tab, s0, *, tt, n_valid):
    bsz, t, _ = h3.shape
    assert n_valid == tt or t == tt
    hps = RET_HPS
    wd = hps * RET_DK
    cq, ck, cv, cg = (lay[k] // wd for k in ("rq", "rk", "rv", "rg"))
    col = lambda c0: (lambda b, hp, tb: (b, tb, c0 + hp))
    kern = functools.partial(_ret_kernel, tt=tt, n_valid=n_valid)
    return pl.pallas_call(
        kern,
        grid=(bsz, RET_HEADS // hps, t // tt),
        in_specs=[pl.BlockSpec((1, tt, wd), col(cq)), pl.BlockSpec((1, tt, wd), col(ck)),
                  pl.BlockSpec((1, tt, wd), col(cv)), pl.BlockSpec((1, tt, wd), col(cg)),
                  pl.BlockSpec((hps, 1, lg_tab.shape[2]), lambda b, hp, tb: (hp, 0, 0)),
                  pl.BlockSpec((1, hps, RET_DK, RET_DV), lambda b, hp, tb: (b, hp, 0, 0))],
        out_specs=[pl.BlockSpec((1, tt, hps * RET_DV), lambda b, hp, tb: (b, tb, hp)),
                   pl.BlockSpec((1, hps, RET_DK, RET_DV), lambda b, hp, tb: (b, hp, 0, 0))],
        out_shape=[jax.ShapeDtypeStruct((bsz, t, RET_HEADS * RET_DV), MXU_DT),
                   jax.ShapeDtypeStruct((bsz, RET_HEADS, RET_DK, RET_DV), F32)],
        scratch_shapes=[pltpu.VMEM((hps, RET_DK, RET_DV), F32)],
        compiler_params=_cparams("parallel", "parallel", "arbitrary"),
        name="retention",
    )(h3, h3, h3, h3, lg_tab, s0)


def _store_heads(ref, x, lead=()):
    for hh in range(NSA_KV_HEADS):
        ref[lead + (hh,)] = x[:, hh * NSA_HEAD_DIM:(hh + 1) * NSA_HEAD_DIM].astype(ref.dtype)


def _q_prep_kernel(q_ref, gain_ref, bsum_ref, o_ref):
    x = q_ref[0]
    hd = NSA_HEAD_DIM
    for kvh in range(NSA_KV_HEADS):
        lo = kvh * NSA_KV_WIDTH
        qn = _head_rms(x[:, lo:lo + NSA_KV_WIDTH], bsum_ref[...], gain_ref[...]) * (hd ** -0.5)
        for g in range(NSA_GROUP):
            o_ref[0, kvh, g, :, 0:hd] = qn[:, g * hd:(g + 1) * hd].astype(o_ref.dtype)
            if o_ref.shape[-1] > hd:
                o_ref[0, kvh, g, :, hd:] = jnp.zeros((x.shape[0], o_ref.shape[-1] - hd), o_ref.dtype)


def q_prep(h3, lay, gain_q, bsum, *, tr, width=NSA_HEAD_DIM):
    bsz, t, _ = h3.shape
    cq = lay["nq"] // NSA_WIDTH
    return pl.pallas_call(
        _q_prep_kernel,
        grid=(bsz, t // tr),
        in_specs=[pl.BlockSpec((1, tr, NSA_WIDTH), lambda b, i: (b, i, cq)),
                  pl.BlockSpec((1, NSA_KV_WIDTH), lambda b, i: (0, 0)),
                  pl.BlockSpec((NSA_KV_WIDTH, NSA_KV_WIDTH), lambda b, i: (0, 0))],
        out_specs=pl.BlockSpec((1, NSA_KV_HEADS, NSA_GROUP, tr, width), lambda b, i: (b, 0, 0, i, 0)),
        out_shape=jax.ShapeDtypeStruct((bsz, NSA_KV_HEADS, NSA_GROUP, t, width), MXU_DT),
        compiler_params=_cparams("parallel", "parallel"),
        name="nsa_q_prep",
    )(h3, jnp.tile(gain_q, NSA_KV_HEADS).reshape(1, NSA_KV_WIDTH), bsum)


def _kv_prep_kernel(x_ref, gain_ref, bsum_ref, *rest):
    ko_ref, vo_ref = rest[-2:]
    hd = NSA_HEAD_DIM
    x = x_ref[0]
    kn = _head_rms(x[:, :NSA_KV_WIDTH], bsum_ref[...], gain_ref[...])
    for hh in range(NSA_KV_HEADS):
        ko_ref[0, hh, :, 0:hd] = kn[:, hh * hd:(hh + 1) * hd].astype(ko_ref.dtype)
        if len(rest) == 3:
            ko_ref[0, hh, :, hd:] = rest[0][:, hd:]
        lo = NSA_KV_WIDTH + hh * hd
        vo_ref[0, hh, :, 0:hd] = x[:, lo:lo + hd].astype(vo_ref.dtype)
        if vo_ref.shape[-1] > hd:
            vo_ref[0, hh, :, hd:] = jnp.ones((x.shape[0], vo_ref.shape[-1] - hd), vo_ref.dtype)


def kv_prep(x3, col, gain, bsum, *, tr, aug=None):
    bsz, t, _ = x3.shape
    w2 = 2 * NSA_KV_WIDTH
    hd = NSA_HEAD_DIM
    kw = hd if aug is None else aug.shape[1]
    vw = hd if aug is None else LANES
    vshape = jax.ShapeDtypeStruct((bsz, NSA_KV_HEADS, t, vw), MXU_DT)
    kshape = jax.ShapeDtypeStruct((bsz, NSA_KV_HEADS, t, kw), MXU_DT)
    spec = lambda w: pl.BlockSpec((1, NSA_KV_HEADS, tr, w), lambda b, i: (b, 0, i, 0))
    in_specs = [pl.BlockSpec((1, tr, w2), lambda b, i: (b, i, col)),
                pl.BlockSpec((1, NSA_KV_WIDTH), lambda b, i: (0, 0)),
                pl.BlockSpec((NSA_KV_WIDTH, NSA_KV_WIDTH), lambda b, i: (0, 0))]
    args = [x3, jnp.tile(gain, NSA_KV_HEADS).reshape(1, NSA_KV_WIDTH), bsum]
    if aug is not None:
        in_specs.append(pl.BlockSpec((tr, kw), lambda b, i: (i, 0)))
        args.append(aug)
    return pl.pallas_call(
        _kv_prep_kernel,
        grid=(bsz, t // tr),
        in_specs=in_specs,
        out_specs=[spec(kw), spec(vw)],
        out_shape=[kshape, vshape],
        compiler_params=_cparams("parallel", "parallel"),
        name="nsa_kv_prep",
    )(*args)


def _head_rms_t(xt, gain_t):
    x3 = xt.reshape(NSA_KV_HEADS, NSA_HEAD_DIM, xt.shape[1])
    ms = jnp.mean(x3 * x3, axis=1, keepdims=True)
    return (x3 * lax.rsqrt(ms + EPS)).reshape(xt.shape) * gain_t


def _gain_t(gain):
    return jnp.broadcast_to(jnp.tile(gain, NSA_KV_HEADS)[:, None], (NSA_KV_WIDTH, LANES))


def _cmp_prep_kernel(*refs, n_in, rows):
    refs = refs[-(n_in + 7):]
    nxt_ref, tail_ref, w_ref, gain_ref, bsum_ref, kc_ref, vc_ref = refs[n_in:]
    cs = CMP_STRIDE
    nb = rows // cs
    w0 = w_ref[0]
    w1 = w_ref[1]
    a0 = []
    a1 = []
    for u in range(n_in):
        x3 = refs[u][0].reshape(nb, cs, 2 * NSA_KV_WIDTH)
        a0.append(jnp.sum(x3 * w0[None], axis=1))
        a1.append(jnp.sum(x3 * w1[None], axis=1))
    a0 = jnp.concatenate(a0, axis=0) if n_in > 1 else a0[0]
    a1 = jnp.concatenate(a1, axis=0) if n_in > 1 else a1[0]
    last = pl.program_id(1) == pl.num_programs(1) - 1
    nx = jnp.where(last, tail_ref[0], nxt_ref[0])
    a1_next = jnp.sum(nx * w1, axis=0, keepdims=True)
    tot = nb * n_in
    a1s = pltpu.roll(a1, tot - 1, 0)
    a1s = jnp.where(_iota(a1s.shape, 0) == tot - 1, a1_next, a1s)
    comp = a0 + a1s
    kc = _head_rms(comp[:, :NSA_KV_WIDTH], bsum_ref[...], gain_ref[...])
    _store_heads(kc_ref, kc, (0,))
    _store_heads(vc_ref, comp[:, NSA_KV_WIDTH:], (0,))


def _cmp_weight_table(cmp_w):
    w = cmp_w.reshape(2, CMP_BLOCK // CMP_STRIDE, CMP_STRIDE)
    w = jnp.transpose(w, (1, 2, 0))
    return jnp.repeat(w, NSA_KV_WIDTH, axis=2)


def cmp_prep(x3, col, tail16, cmp_w, gain, bsum, *, tr):
    bsz, t, _ = x3.shape
    w2 = 2 * NSA_KV_WIDTH
    nbt = tr // CMP_STRIDE
    n_steps = t // tr
    kern = functools.partial(_cmp_prep_kernel, n_in=1, rows=tr)
    oshape = jax.ShapeDtypeStruct((bsz, NSA_KV_HEADS, t // CMP_STRIDE, NSA_HEAD_DIM), MXU_DT)
    ospec = pl.BlockSpec((1, NSA_KV_HEADS, nbt, NSA_HEAD_DIM), lambda b, i: (b, 0, i, 0))
    chunks_per_step = tr // CMP_STRIDE
    return pl.pallas_call(
        kern,
        grid=(bsz, n_steps),
        in_specs=[pl.BlockSpec((1, tr, w2), lambda b, i: (b, i, col)),
                  pl.BlockSpec((1, CMP_STRIDE, w2),
                               lambda b, i: (b, jnp.minimum(i + 1, n_steps - 1) * chunks_per_step, col)),
                  pl.BlockSpec((1, CMP_STRIDE, w2), lambda b, i: (b, 0, 0)),
                  pl.BlockSpec((2, CMP_STRIDE, w2), lambda b, i: (0, 0, 0)),
                  pl.BlockSpec((1, NSA_KV_WIDTH), lambda b, i: (0, 0)),
                  pl.BlockSpec((NSA_KV_WIDTH, NSA_KV_WIDTH), lambda b, i: (0, 0))],
        out_specs=[ospec, ospec],
        out_shape=[oshape, oshape],
        compiler_params=_cparams("parallel", "arbitrary"),
        name="nsa_cmp_prep",
    )(x3, x3, tail16, _cmp_weight_table(cmp_w), jnp.tile(gain, NSA_KV_HEADS).reshape(1, NSA_KV_WIDTH), bsum)


def _cmp_prep_t_kernel(*refs, n_in):
    refs = refs[-(n_in + 6):]
    nxt_ref, tail_ref, w_ref, gain_ref, kc_ref, vc_ref = refs[n_in:]
    nb = w_ref.shape[2] // 4
    kvw = NSA_KV_WIDTH

    last = pl.program_id(1) == pl.num_programs(1) - 1
    nxt = jnp.where(last, tail_ref[0], nxt_ref[0])
    rows = [refs[u][0].reshape(2 * kvw, PAGE_SIZE).astype(MXU_DT) for u in range(n_in)]
    rows.append(nxt.reshape(2 * kvw, PAGE_SIZE).astype(MXU_DT))
    acc = _dot(jnp.concatenate(rows, axis=1), w_ref[...].reshape((n_in + 1) * PAGE_SIZE, 4 * nb))
    comp_k = acc[:kvw, 0:nb] + acc[:kvw, 2 * nb:3 * nb]
    comp_v = acc[kvw:, nb:2 * nb] + acc[kvw:, 3 * nb:4 * nb]
    kct = _head_rms_t(comp_k, gain_ref[...][:, :nb]).astype(MXU_DT)
    eye = _eye(nb)
    _store_heads(kc_ref, _dot_nt(eye, kct), (0,))
    _store_heads(vc_ref, _dot_nt(eye, comp_v.astype(MXU_DT)), (0,))


def _cmp_band_tables(cmp_w, n_pages_step):
    cs = CMP_STRIDE
    w = cmp_w.reshape(2, CMP_BLOCK // cs, cs)
    nb = n_pages_step * PAGE_SIZE // cs
    rho = jnp.arange(PAGE_SIZE)
    ch = (jnp.arange(n_pages_step)[:, None] * (PAGE_SIZE // cs) + rho[None, :] // cs)[..., None]
    n = jnp.arange(nb)[None, None, :]
    s = rho % cs
    tabs = []
    for c in range(2):
        full = (w[c, 0][s][None, :, None] * (ch == n) + w[c, 1][s][None, :, None] * (ch == n + 1)).astype(F32)
        nxt = jnp.where((rho[:, None] < cs) & (n[0] == nb - 1), w[c, 1][s][:, None], 0.0).astype(F32)
        tabs.append(jnp.concatenate([full, nxt[None]], axis=0))
    hi = [t.astype(MXU_DT) for t in tabs]
    lo = [(t - h.astype(F32)).astype(MXU_DT) for t, h in zip(tabs, hi)]
    return jnp.concatenate(hi + lo, axis=2)


def cmp_prep_paged(pool_t, pages, tail_t, cmp_w, gain):
    bsz, n_pages = pages.shape
    pp = PAGES_PER_STEP
    n_steps = n_pages // pp
    nbt = pp * PAGE_SIZE // CMP_STRIDE
    kern = functools.partial(_cmp_prep_t_kernel, n_in=pp)
    oshape = jax.ShapeDtypeStruct((bsz, NSA_KV_HEADS, n_pages * PAGE_SIZE // CMP_STRIDE, NSA_HEAD_DIM), MXU_DT)
    ospec = pl.BlockSpec((1, NSA_KV_HEADS, nbt, NSA_HEAD_DIM), lambda b, i, pt: (b, 0, i, 0))
    pblock = (1, 2, NSA_KV_WIDTH, PAGE_SIZE)
    page_spec = lambda u: pl.BlockSpec(pblock, lambda b, i, pt: (pt[b, i * pp + u], 0, 0, 0))
    const = lambda *shape: pl.BlockSpec(shape, lambda b, i, pt: (0,) * len(shape))
    return pl.pallas_call(
        kern,
        grid_spec=pltpu.PrefetchScalarGridSpec(
            num_scalar_prefetch=1,
            grid=(bsz, n_steps),
            in_specs=[page_spec(u) for u in range(pp)]
                     + [pl.BlockSpec(pblock, lambda b, i, pt: (pt[b, jnp.minimum(i + 1, n_steps - 1) * pp], 0, 0, 0)),
                        pl.BlockSpec(pblock, lambda b, i, pt: (b, 0, 0, 0)),
                        const(pp + 1, PAGE_SIZE, 4 * nbt), const(NSA_KV_WIDTH, LANES)],
            out_specs=[ospec, ospec]),
        out_shape=[oshape, oshape],
        compiler_params=_cparams("parallel", "arbitrary"),
        name="nsa_cmp_prep_paged",
    )(pages, *([pool_t] * pp), pool_t, tail_t, _cmp_band_tables(cmp_w, pp), _gain_t(gain))


def _gate_and_store(o_ref, br_ref, nbg_ref, gx_ref, branches, tq):
    hd = NSA_HEAD_DIM
    for br, o in enumerate(branches):
        for g in range(NSA_GROUP):
            br_ref[br, :, g * hd:(g + 1) * hd] = o[g * tq:(g + 1) * tq]
    pieces = _split(_sigmoid(nbg_ref[0]), 2)
    acc = None
    for br in range(3):
        term = sum(_dot(p, gx_ref[br]) for p in pieces) * br_ref[br]
        acc = term if acc is None else acc + term
    o_ref[0] = acc.astype(o_ref.dtype)


def _nsa_kernel(q_ref, sl_ref, kc_ref, vc_ref, ks_ref, vs_ref, kw_ref, vw_ref, at_ref, nbg_ref, gx_ref,
                o_ref, sc_ref, m_ref, l_ref, acc_ref, br_ref,
                *, tq, sel_tk, qpos0, wpos0, n_sel, topk):
    rows = NSA_GROUP * tq
    hd = NSA_HEAD_DIM
    t0 = qpos0 + pl.program_id(2) * tq
    q = q_ref[0, 0].reshape(rows, hd)
    slope = sl_ref[0][:, :1]
    nbp = kc_ref.shape[2]
    n_sel_pad, tl = sc_ref.shape

    def tok(shape):
        return t0 + (_iota(shape, 0) & (tq - 1))

    s = _dot_nt(q, kc_ref[0, 0])
    dist = tok((rows, nbp)) - (_iota((rows, nbp), 1) * CMP_STRIDE + (CMP_BLOCK - 1))
    valid = dist >= 0
    s = jnp.where(valid, s - slope * dist.astype(F32), NEG_BIG)
    p = jnp.where(valid, jnp.exp(s - jnp.max(s, axis=-1, keepdims=True)), 0.0)
    p = p / jnp.maximum(jnp.sum(p, axis=-1, keepdims=True), 1e-30)
    o_cmp = _dot(p.astype(MXU_DT), vc_ref[0, 0])

    sel = _topk_block_mask(p, at_ref, t0, tq=tq, tl=tl, n_sel=n_sel, topk=topk)
    spread = ((_iota((rows, tl), 0) & (tq - 1)) == _iota((rows, tl), 1)).astype(MXU_DT)
    nblk = sel_tk // SEL_BLOCK
    if nblk % 8 == 0:
        sc_ref[...] = sel
        in_tile = (_iota((nblk, sel_tk), 0) == (_iota((nblk, sel_tk), 1) >> 6)).astype(MXU_DT)
    else:
        selb = _dot_nt(spread, sel.astype(MXU_DT)).astype(MXU_DT)

    def softmax_pass(k_ref, v_ref, tk, lo, hi, kpos0, mask_fn):
        m_ref[...] = jnp.full(m_ref.shape, NEG_BIG, F32)
        l_ref[...] = jnp.zeros(l_ref.shape, F32)
        acc_ref[...] = jnp.zeros(acc_ref.shape, F32)
        rel = (_iota((rows, tk), 0) & (tq - 1)) - _iota((rows, tk), 1)
        reps = tk // LANES

        def body(kt, carry):
            k0 = pl.multiple_of(kt * tk, tk)
            s = _dot_nt(q, k_ref[0, 0, pl.ds(k0, tk), :])
            d = rel + (t0 - kpos0 - k0)
            valid = mask_fn(d, k0)
            s = jnp.where(valid, s - slope * d.astype(F32), NEG_BIG)
            m_prev = m_ref[...]
            m_new = jnp.maximum(m_prev, jnp.max(s, axis=-1, keepdims=True))
            alpha = jnp.exp(m_prev - m_new)
            mrep = m_new if reps == 1 else jnp.concatenate([m_new] * reps, axis=1)
            p = jnp.where(valid, jnp.exp(s - mrep), 0.0)
            l_ref[...] = alpha * l_ref[...] + jnp.sum(p, axis=-1, keepdims=True)
            acc_ref[...] = alpha[:, :hd] * acc_ref[...] + _dot(p.astype(MXU_DT), v_ref[0, 0, pl.ds(k0, tk), :])
            m_ref[...] = m_new
            return carry

        lax.fori_loop(lo, hi, body, 0)
        return acc_ref[...] / jnp.maximum(l_ref[...][:, :hd], 1e-30)

    n_kt_all = ks_ref.shape[2] // sel_tk
    hi_sel = jnp.minimum(n_kt_all, (t0 + tq - 1) // sel_tk + 1)

    def sel_mask(d, k0):
        if nblk % 8 == 0:
            blk0 = pl.multiple_of(k0 // SEL_BLOCK, nblk)
            mine = _dot_nt(spread, sc_ref[pl.ds(blk0, nblk), :].astype(MXU_DT)).astype(MXU_DT)
            return (_dot(mine, in_tile) > 0.5) & (d >= 0)
        blk_of_key = (k0 + _iota((n_sel_pad, sel_tk), 1)) >> 6
        expand = (_iota((n_sel_pad, sel_tk), 0) == blk_of_key).astype(MXU_DT)
        return (_dot(selb, expand) > 0.5) & (d >= 0)

    o_sel = softmax_pass(ks_ref, vs_ref, sel_tk, 0, hi_sel, 0, sel_mask)

    n_wt_all = kw_ref.shape[2] // WIN_TK
    lo_win = jnp.maximum(t0 - (WINDOW - 1) - wpos0, 0) // WIN_TK
    hi_win = jnp.minimum(n_wt_all, (t0 + tq - 1 - wpos0) // WIN_TK + 1)
    o_win = softmax_pass(kw_ref, vw_ref, WIN_TK, lo_win, hi_win, wpos0,
                         lambda d, k0: (d >= 0) & (d < WINDOW))

    _gate_and_store(o_ref, br_ref, nbg_ref, gx_ref, (o_cmp, o_sel, o_win), tq)


def _imp_to_sel_matrix(n_sel_pad, nbp):
    m = SEL_BLOCK // CMP_STRIDE
    r = CMP_BLOCK // CMP_STRIDE
    jj = np.arange(n_sel_pad)[:, None]
    ii = np.arange(nbp)[None, :]
    cnt = sum((((ii - rr) >= m * jj) & ((ii - rr) < m * (jj + 1))).astype(np.float32) for rr in range(r))
    return cnt.astype(MXU_DT)


def nsa_attention(qn, slopes, kc, vc, ks, vs, kw, vw, h3, nbg_col, gate_expand, *, tq, sel_tk, qpos0, wpos0, n_sel):
    bsz, _, _, tql, hd = qn.shape
    nbp, tk_all, tw_all = kc.shape[2], ks.shape[2], kw.shape[2]
    assert tq & (tq - 1) == 0 and tk_all % sel_tk == 0 and tw_all % WIN_TK == 0
    n_sel_pad = -(-max(n_sel, tk_all // SEL_BLOCK) // 64) * 64
    tl = max(tq, LANES)
    rows = NSA_GROUP * tq
    at = _imp_to_sel_matrix(n_sel_pad, nbp)
    kern = functools.partial(_nsa_kernel, tq=tq, sel_tk=sel_tk, qpos0=qpos0, wpos0=wpos0, n_sel=n_sel,
                             topk=min(NSA_TOPK, n_sel))
    full = lambda n: pl.BlockSpec((1, 1, n, hd), lambda b, kh, i: (b, kh, 0, 0))
    gw = NSA_GROUP * hd
    return pl.pallas_call(
        kern,
        grid=(bsz, NSA_KV_HEADS, tql // tq),
        in_specs=[pl.BlockSpec((1, 1, NSA_GROUP, tq, hd), lambda b, kh, i: (b, kh, 0, i, 0)),
                  pl.BlockSpec((1, rows, LANES), lambda b, kh, i: (kh, 0, 0)),
                  full(nbp), full(nbp), full(tk_all), full(tk_all), full(tw_all), full(tw_all),
                  pl.BlockSpec((n_sel_pad, nbp), lambda b, kh, i: (0, 0)),
                  pl.BlockSpec((1, tq, LANES), lambda b, kh, i: (b, i, nbg_col)),
                  pl.BlockSpec((3, LANES, gw), lambda b, kh, i: (0, 0, kh))],
        out_specs=pl.BlockSpec((1, tq, gw), lambda b, kh, i: (b, i, kh)),
        out_shape=jax.ShapeDtypeStruct((bsz, tql, NSA_WIDTH), MXU_DT),
        scratch_shapes=[pltpu.VMEM((n_sel_pad, tl), F32),
                        pltpu.VMEM((rows, LANES), F32), pltpu.VMEM((rows, LANES), F32),
                        pltpu.VMEM((rows, hd), F32), pltpu.VMEM((3, tq, gw), F32)],
        compiler_params=_cparams("parallel", "parallel", "arbitrary"),
        name="nsa_attention",
    )(qn, slopes, kc, vc, ks, vs, kw, vw, at, h3, gate_expand)


def _topk_block_mask(p, at_ref, t0, *, tq, tl, n_sel, topk):
    n_sel_pad, nbp = at_ref.shape
    imp = p[0:tq] + p[tq:2 * tq] + p[2 * tq:3 * tq] + p[3 * tq:4 * tq]
    if tl > tq:
        imp = jnp.concatenate([imp, jnp.zeros((tl - tq, nbp), F32)], axis=0)
    imp_sel = sum(_dot_nt(at_ref[...], piece) for piece in _split(imp, 3))
    j = _iota((n_sel_pad, tl), 0)
    tt = t0 + _iota((n_sel_pad, tl), 1)
    forced = (j == 0) | (j == (tt >> 6))
    allowed = (j << 6) <= tt
    score = jnp.where(forced, POS_BIG, jnp.where(allowed, imp_sel, NEG_BIG))
    score = jnp.where(j >= n_sel, LOWEST, score)
    jf = j.astype(F32)
    sel = jnp.zeros((n_sel_pad, tl), F32)
    for _ in range(topk):
        mx = jnp.max(score, axis=0, keepdims=True)
        first = jnp.min(jnp.where(score == mx, jf, 1e9), axis=0, keepdims=True)
        pick = jf == first
        sel = jnp.where(pick, 1.0, sel)
        score = jnp.where(pick, -jnp.inf, score)
    return jnp.where(allowed, sel, 0.0)


def _nsa_prompt_kernel(q_ref, sl_ref, qts_ref, qtw_ref, kc_ref, vc_ref, ks_ref, vs_ref, kw_ref, vw_ref, at_ref,
                       nbg_ref, gx_ref, o_ref, m_ref, acc_ref, br_ref, *, tq, n_sel, topk):
    rows = NSA_GROUP * tq
    hd = NSA_HEAD_DIM
    qi = pl.program_id(2)
    t0 = qi * tq
    q128 = q_ref[0, 0].reshape(rows, LANES)
    slope = sl_ref[0][:, :1]
    nbp = kc_ref.shape[2]

    s = _dot_nt(q128[:, :hd], kc_ref[0, 0])
    tok = t0 + (_iota((rows, nbp), 0) & (tq - 1))
    dist = tok - (_iota((rows, nbp), 1) * CMP_STRIDE + (CMP_BLOCK - 1))
    valid = dist >= 0
    s = jnp.where(valid, s - slope * dist.astype(F32), NEG_BIG)
    p = jnp.where(valid, jnp.exp(s - jnp.max(s, axis=-1, keepdims=True)), 0.0)
    p = p / jnp.maximum(jnp.sum(p, axis=-1, keepdims=True), 1e-30)
    o_cmp = _dot(p.astype(MXU_DT), vc_ref[0, 0])

    def normalised(acc):
        return acc[:, :hd] / jnp.maximum(acc[:, hd:], 1e-30)

    lane = _iota((rows, LANES), 1)
    qw = jnp.where(lane < hd, q128, qtw_ref[0])
    span = WINDOW + tq
    ws = pl.multiple_of(jnp.maximum(t0 - WINDOW, 0), tq)
    sw = _dot_nt(qw, kw_ref[0, 0, pl.ds(ws, span), :])
    d = (_iota((rows, span), 0) & (tq - 1)) - _iota((rows, span), 1) + (t0 - ws)
    sw = jnp.where((d >= 0) & (d < WINDOW), sw, NEG_BIG)
    pw = jnp.exp(sw - jnp.max(sw, axis=-1, keepdims=True))
    o_win = normalised(_dot(pw.astype(MXU_DT), vw_ref[0, 0, pl.ds(ws, span), :]))

    sel = _topk_block_mask(p, at_ref, t0, tq=tq, tl=tq, n_sel=n_sel, topk=topk)
    sel_t = jnp.concatenate([jnp.zeros_like(sel), sel], axis=0).astype(MXU_DT)
    spread = ((_iota((rows, tq), 0) & (tq - 1)) == _iota((rows, tq), 1)).astype(MXU_DT)
    selb = _dot_nt(spread, sel_t)
    neg = jnp.where((lane >= hd) & (selb < 0.5), NEG_BIG, 0.0)
    qa = jnp.concatenate([(q128.astype(F32) + neg).astype(MXU_DT), qts_ref[0]], axis=1)

    tk = PROMPT_SEL_TK
    kd = t0 // tk
    k0 = pl.multiple_of(kd * tk, tk)
    s = _dot_nt(qa, ks_ref[0, 0, pl.ds(k0, tk), :])
    causal = ((_iota((rows, tk), 0) & (tq - 1)) - _iota((rows, tk), 1) + (t0 - k0)) >= 0
    s = jnp.where(causal, s, NEG_BIG)
    m = jnp.max(s, axis=-1, keepdims=True)
    m_ref[...] = jnp.broadcast_to(m, m_ref.shape)
    acc_ref[...] = _dot(jnp.exp(s - m).astype(MXU_DT), vs_ref[0, 0, pl.ds(k0, tk), :])

    def sel_body(kt, carry):
        ka = pl.multiple_of(kt * tk, tk)
        s = _dot_nt(qa, ks_ref[0, 0, pl.ds(ka, tk), :])
        m_prev = m_ref[...]
        m_new = jnp.maximum(m_prev, jnp.max(s, axis=-1, keepdims=True))
        p = jnp.exp(s - jnp.concatenate([m_new] * (tk // LANES), axis=1))
        acc_ref[...] = jnp.exp(m_prev - m_new) * acc_ref[...] + _dot(p.astype(MXU_DT), vs_ref[0, 0, pl.ds(ka, tk), :])
        m_ref[...] = m_new
        return carry

    lax.fori_loop(0, kd, sel_body, 0)
    o_sel = normalised(acc_ref[...])

    _gate_and_store(o_ref, br_ref, nbg_ref, gx_ref, (o_cmp, o_sel, o_win), tq)


def _slope_digits(tq, tile, lane0):
    pieces, r = [], _alibi_slopes()
    for _ in range(3):
        p = r.astype(MXU_DT).astype(np.float32)
        pieces.append(p)
        r = r - p
    pieces = np.stack(pieces, axis=1)
    tab = np.zeros((NSA_HEADS, LANES), np.float32)
    tab[:, lane0:lane0 + 6] = np.concatenate([pieces * float(tile), pieces], axis=1)
    tab = np.broadcast_to(tab.reshape(NSA_KV_HEADS, NSA_GROUP, 1, LANES), (NSA_KV_HEADS, NSA_GROUP, tq, LANES))
    return tab.reshape(NSA_KV_HEADS, NSA_GROUP * tq, LANES).astype(MXU_DT)


def _key_digits(t, tile, lane0, width, onehot):
    r = np.arange(t)
    lane = np.arange(width)[None, :]
    tab = np.zeros((t, width), np.float32)
    if onehot:
        tab = np.where(lane - NSA_HEAD_DIM == (r // SEL_BLOCK)[:, None], 1.0, tab)
    hi = (r // tile).astype(np.float32)[:, None]
    lo = (r % tile).astype(np.float32)[:, None]
    tab = np.where((lane >= lane0) & (lane < lane0 + 3), hi, tab)
    tab = np.where((lane >= lane0 + 3) & (lane < lane0 + 6), lo, tab)
    return tab.astype(MXU_DT)


def nsa_attention_prompt(q128, slopes, kc, vc, ks_aug, vs, kw_aug, vw, h3, nbg_col, gate_expand, *, n_sel):
    bsz, _, _, t, _ = q128.shape
    tq = PROMPT_TQ
    hd = NSA_HEAD_DIM
    nbp = kc.shape[2]
    assert n_sel <= 64 and t % PROMPT_SEL_TK == 0 and t >= WINDOW + tq and PROMPT_SEL_TK % tq == 0
    rows = NSA_GROUP * tq
    at = _imp_to_sel_matrix(64, nbp)
    kern = functools.partial(_nsa_prompt_kernel, tq=tq, n_sel=n_sel, topk=min(NSA_TOPK, n_sel))
    full = lambda n, w: pl.BlockSpec((1, 1, n, w), lambda b, kh, i: (b, kh, 0, 0))
    per_head = pl.BlockSpec((1, rows, LANES), lambda b, kh, i: (kh, 0, 0))
    gw = NSA_GROUP * hd
    return pl.pallas_call(
        kern,
        grid=(bsz, NSA_KV_HEADS, t // tq),
        in_specs=[pl.BlockSpec((1, 1, NSA_GROUP, tq, LANES), lambda b, kh, i: (b, kh, 0, i, 0)),
                  per_head, per_head, per_head,
                  full(nbp, hd), full(nbp, hd), full(t, 2 * LANES), full(t, LANES), full(t, LANES), full(t, LANES),
                  pl.BlockSpec((64, nbp), lambda b, kh, i: (0, 0)),
                  pl.BlockSpec((1, tq, LANES), lambda b, kh, i: (b, i, nbg_col)),
                  pl.BlockSpec((3, LANES, gw), lambda b, kh, i: (0, 0, kh))],
        out_specs=pl.BlockSpec((1, tq, gw), lambda b, kh, i: (b, i, kh)),
        out_shape=jax.ShapeDtypeStruct((bsz, t, NSA_WIDTH), MXU_DT),
        scratch_shapes=[pltpu.VMEM((rows, LANES), F32), pltpu.VMEM((rows, LANES), F32),
                        pltpu.VMEM((3, tq, gw), F32)],
        compiler_params=_cparams("parallel", "parallel", "arbitrary"),
        name="nsa_attention_prompt",
    )(q128, slopes, _slope_digits(tq, SEL_TK, 0), _slope_digits(tq, WIN_TK, hd),
      kc, vc, ks_aug, vs, kw_aug, vw, at, h3, gate_expand)


def _nsa_paged_pre_kernel(q_ref, sl_ref, kc_ref, vc_ref, kw_ref, vw_ref, at_ref,
                          ocmp_ref, owin_ref, sel_ref, *, tq, qpos0, wpos0, n_sel, topk):
    rows = NSA_GROUP * tq
    hd = NSA_HEAD_DIM
    q = q_ref[0, 0].reshape(rows, hd)
    slope = sl_ref[0][:, :1]
    nbp = kc_ref.shape[2]
    tw = kw_ref.shape[2]

    s = _dot_nt(q, kc_ref[0, 0])
    tok = qpos0 + (_iota((rows, nbp), 0) & (tq - 1))
    dist = tok - (_iota((rows, nbp), 1) * CMP_STRIDE + (CMP_BLOCK - 1))
    valid = dist >= 0
    s = jnp.where(valid, s - slope * dist.astype(F32), NEG_BIG)
    p = jnp.where(valid, jnp.exp(s - jnp.max(s, axis=-1, keepdims=True)), 0.0)
    p = p / jnp.maximum(jnp.sum(p, axis=-1, keepdims=True), 1e-30)
    o_cmp = _dot(p.astype(MXU_DT), vc_ref[0, 0])
    sel_ref[0, 0] = _topk_block_mask(p, at_ref, qpos0, tq=tq, tl=sel_ref.shape[3], n_sel=n_sel, topk=topk)

    sw = _dot_nt(q, kw_ref[0, 0])
    d = (qpos0 - wpos0) + (_iota((rows, tw), 0) & (tq - 1)) - _iota((rows, tw), 1)
    valid = (d >= 0) & (d < WINDOW)
    sw = jnp.where(valid, sw - slope * d.astype(F32), NEG_BIG)
    pw = jnp.where(valid, jnp.exp(sw - jnp.max(sw, axis=-1, keepdims=True)), 0.0)
    o_win = _dot(pw.astype(MXU_DT), vw_ref[0, 0]) / jnp.maximum(jnp.sum(pw, axis=-1, keepdims=True), 1e-30)

    for g in range(NSA_GROUP):
        ocmp_ref[0, :, g * hd:(g + 1) * hd] = o_cmp[g * tq:(g + 1) * tq]
        owin_ref[0, :, g * hd:(g + 1) * hd] = o_win[g * tq:(g + 1) * tq]


def _nsa_paged_sel_kernel(*refs, n_in, tq, qpos0, n_tiles):
    hit_ref = refs[1]
    refs = refs[2:]
    q_ref, sl_ref, sel_ref = refs[:3]
    pages = refs[3:3 + n_in]
    (ktn_ref, vtn_ref, gain_ref, ocmp_ref, owin_ref, nbg_ref, gx_ref,
     o_ref, m_ref, l_ref, acc_ref, kt_ref, vt_ref, br_ref) = refs[3 + n_in:]
    rows = NSA_GROUP * tq
    hd = NSA_HEAD_DIM
    tk = n_in * PAGE_SIZE
    nblk = tk // SEL_BLOCK
    i = pl.program_id(1)

    @pl.when(i == 0)
    def _():
        m_ref[...] = jnp.full(m_ref.shape, NEG_BIG, F32)
        l_ref[...] = jnp.zeros(l_ref.shape, F32)
        acc_ref[...] = jnp.zeros(acc_ref.shape, F32)

    @pl.when(i < n_tiles)
    def _():
        for u in range(n_in):
            sl = slice(u * PAGE_SIZE, (u + 1) * PAGE_SIZE)
            kt_ref[:, sl] = _head_rms_t(pages[u][0, 0], gain_ref[...]).astype(kt_ref.dtype)
            vt_ref[:, sl] = pages[u][0, 1].astype(vt_ref.dtype)

    @pl.when(i == n_tiles)
    def _():
        kt_ref[...] = ktn_ref[0]
        vt_ref[...] = vtn_ref[0]

    k0 = i * tk
    d = (qpos0 - k0) + (_iota((rows, tk), 0) & (tq - 1)) - _iota((rows, tk), 1)
    causal = d >= 0
    dist = d.astype(F32)
    spread = ((_iota((rows, LANES), 0) & (tq - 1)) == _iota((rows, LANES), 1)).astype(MXU_DT)
    in_tile = (_iota((nblk, tk), 0) == (_iota((nblk, tk), 1) >> 6)).astype(MXU_DT)
    blk0 = pl.multiple_of(i * nblk, nblk)
    reps = tk // LANES
    def head_update(kvh):
        hs = slice(kvh * hd, (kvh + 1) * hd)
        q = q_ref[0, kvh].reshape(rows, hd)
        s = _dot(q, kt_ref[hs, :])
        mine = _dot_nt(spread, sel_ref[0, kvh, pl.ds(blk0, nblk), :].astype(MXU_DT)).astype(MXU_DT)
        valid = (_dot(mine, in_tile) > 0.5) & causal
        s = jnp.where(valid, s - sl_ref[kvh][:, :1] * dist, NEG_BIG)
        m_prev = m_ref[kvh]
        m_new = jnp.maximum(m_prev, jnp.max(s, axis=-1, keepdims=True))
        alpha = jnp.exp(m_prev - m_new)
        p = jnp.where(valid, jnp.exp(s - jnp.concatenate([m_new] * reps, axis=1)), 0.0)
        l_ref[kvh] = alpha * l_ref[kvh] + jnp.sum(p, axis=-1, keepdims=True)
        acc_ref[kvh] = alpha[:, :hd] * acc_ref[kvh] + _dot_nt(p.astype(MXU_DT), vt_ref[hs, :])
        m_ref[kvh] = m_new

    for kvh in range(NSA_KV_HEADS):
        pl.when(hit_ref[pl.program_id(0), kvh * (n_tiles + 1) + i] > 0)(functools.partial(head_update, kvh))

    @pl.when(i == n_tiles)
    def _():
        br_ref[0] = ocmp_ref[0]
        br_ref[2] = owin_ref[0]
        for kvh in range(NSA_KV_HEADS):
            o_sel = acc_ref[kvh] / jnp.maximum(l_ref[kvh][:, :hd], 1e-30)
            for g in range(NSA_GROUP):
                c0 = (kvh * NSA_GROUP + g) * hd
                br_ref[1, :, c0:c0 + hd] = o_sel[g * tq:(g + 1) * tq]
        pieces = _split(_sigmoid(nbg_ref[0]), 2)
        acc = None
        for br in range(3):
            term = sum(_dot(p, gx_ref[br]) for p in pieces) * br_ref[br]
            acc = term if acc is None else acc + term
        o_ref[0] = acc.astype(o_ref.dtype)


def nsa_attention_paged(qn, slopes, kc, vc, kw, vw, pool_t, pages, kt_new, vt_new, gain_sel, h3, nbg_col,
                        gate_expand, *, tq, t_real, qpos0, wpos0, n_sel):
    bsz, _, _, tql, hd = qn.shape
    assert tql == tq
    nbp, tw = kc.shape[2], kw.shape[2]
    n_pages = pages.shape[1]
    tk = kt_new.shape[2]
    n_in = tk // PAGE_SIZE
    n_tiles = n_pages // n_in
    assert n_pages % n_in == 0 and qpos0 == n_pages * PAGE_SIZE
    n_sel_pad = -(-max(n_sel, (n_tiles + 1) * tk // SEL_BLOCK) // 64) * 64
    rows = NSA_GROUP * tq
    gw = NSA_GROUP * hd
    at = _imp_to_sel_matrix(n_sel_pad, nbp)
    full = lambda n: pl.BlockSpec((1, 1, n, hd), lambda b, kh: (b, kh, 0, 0))
    obr = pl.BlockSpec((1, tq, gw), lambda b, kh: (b, 0, kh))
    o_cmp, o_win, sel = pl.pallas_call(
        functools.partial(_nsa_paged_pre_kernel, tq=tq, qpos0=qpos0, wpos0=wpos0, n_sel=n_sel,
                          topk=min(NSA_TOPK, n_sel)),
        grid=(bsz, NSA_KV_HEADS),
        in_specs=[pl.BlockSpec((1, 1, NSA_GROUP, tq, hd), lambda b, kh: (b, kh, 0, 0, 0)),
                  pl.BlockSpec((1, rows, LANES), lambda b, kh: (kh, 0, 0)),
                  full(nbp), full(nbp), full(tw), full(tw),
                  pl.BlockSpec((n_sel_pad, nbp), lambda b, kh: (0, 0))],
        out_specs=[obr, obr, pl.BlockSpec((1, 1, n_sel_pad, LANES), lambda b, kh: (b, kh, 0, 0))],
        out_shape=[jax.ShapeDtypeStruct((bsz, tq, NSA_WIDTH), F32), jax.ShapeDtypeStruct((bsz, tq, NSA_WIDTH), F32),
                   jax.ShapeDtypeStruct((bsz, NSA_KV_HEADS, n_sel_pad, LANES), F32)],
        compiler_params=_cparams("parallel", "parallel"),
        name="nsa_paged_pre",
    )(qn, slopes, kc, vc, kw, vw, at)

    nblk = tk // SEL_BLOCK
    hit = sel[:, :, :(n_tiles + 1) * nblk, :t_real].reshape(bsz, NSA_KV_HEADS, n_tiles + 1, nblk * t_real)
    hit = (jnp.max(hit, axis=-1) > 0).astype(jnp.int32).reshape(bsz, NSA_KV_HEADS * (n_tiles + 1))

    page_spec = lambda u: pl.BlockSpec(
        (1, 2, NSA_KV_WIDTH, PAGE_SIZE),
        lambda b, i, pt, ht: (pt[b, jnp.minimum(i, n_tiles - 1) * n_in + u], 1, 0, 0))
    const = lambda *shape: pl.BlockSpec(shape, lambda b, i, pt, ht: (0,) * len(shape))
    per_seq = lambda *shape: pl.BlockSpec((1,) + shape, lambda b, i, pt, ht: (b,) + (0,) * len(shape))
    return pl.pallas_call(
        functools.partial(_nsa_paged_sel_kernel, n_in=n_in, tq=tq, qpos0=qpos0, n_tiles=n_tiles),
        grid_spec=pltpu.PrefetchScalarGridSpec(
            num_scalar_prefetch=2,
            grid=(bsz, n_tiles + 1),
            in_specs=[per_seq(NSA_KV_HEADS, NSA_GROUP, tq, hd), const(NSA_KV_HEADS, rows, LANES),
                      per_seq(NSA_KV_HEADS, n_sel_pad, LANES)]
                     + [page_spec(u) for u in range(n_in)]
                     + [per_seq(NSA_KV_WIDTH, tk), per_seq(NSA_KV_WIDTH, tk), const(NSA_KV_WIDTH, LANES),
                        per_seq(tq, NSA_WIDTH), per_seq(tq, NSA_WIDTH),
                        pl.BlockSpec((1, tq, LANES), lambda b, i, pt, ht: (b, 0, nbg_col)),
                        const(3, LANES, NSA_WIDTH)],
            out_specs=per_seq(tq, NSA_WIDTH),
            scratch_shapes=[pltpu.VMEM((NSA_KV_HEADS, rows, LANES), F32), pltpu.VMEM((NSA_KV_HEADS, rows, LANES), F32),
                            pltpu.VMEM((NSA_KV_HEADS, rows, hd), F32),
                            pltpu.VMEM((NSA_KV_WIDTH, tk), MXU_DT), pltpu.VMEM((NSA_KV_WIDTH, tk), MXU_DT),
                            pltpu.VMEM((3, tq, NSA_WIDTH), F32)]),
        out_shape=jax.ShapeDtypeStruct((bsz, tq, NSA_WIDTH), MXU_DT),
        compiler_params=_cparams("parallel", "arbitrary"),
        name="nsa_paged_sel",
    )(pages, hit, qn, slopes, sel, *([pool_t] * n_in), kt_new, vt_new, _gain_t(gain_sel),
      o_cmp, o_win, h3, gate_expand)


def _layout(d_model):
    tn = IN_PROJ_TN
    hw, rw = HG_HEADS * HG_DK, RET_HEADS * RET_DK
    gates = 3 * NSA_HEADS
    lay, off = {}, 0
    for name, width in [("hq", hw), ("hf", hw), ("hi", hw), ("hg", hw), ("nq", NSA_WIDTH),
                        ("nkv", 4 * NSA_KV_WIDTH), ("nwkv", 2 * NSA_KV_WIDTH), ("nbg", gates)]:
        lay[name] = off
        off += width
    lay["post_row"] = off
    lay["post_col"] = off = -(-off // tn) * tn
    for name, width in [("rq", rw), ("rk", rw), ("rv", rw), ("rg", rw), ("mg", 3 * d_model)]:
        lay[name] = off
        off += width
    assert lay["nbg"] % LANES == 0 and off % tn == 0 and lay["post_row"] % 16 == 0
    lay["used"] = lay["post_row"] + off - lay["post_col"]
    lay["total"] = off
    return lay


def _alibi_slopes():
    hh = np.arange(1, NSA_HEADS + 1, dtype=np.float32)
    return np.exp2(np.float32(-8.0) * hh / np.float32(NSA_HEADS)).astype(np.float32)


def _block_sum_matrix(width):
    i = np.arange(width)
    return (i[:, None] // NSA_HEAD_DIM == i[None, :] // NSA_HEAD_DIM).astype(np.float32).astype(MXU_DT)


def _gate_expand_matrix():
    c = np.arange(LANES)[:, None]
    col = np.arange(NSA_WIDTH)[None, :]
    return np.stack([(c == (col // NSA_HEAD_DIM) * 3 + br) for br in range(3)]).astype(np.float32).astype(MXU_DT)


def _slope_rows(tq):
    rows = np.repeat(_alibi_slopes().reshape(NSA_KV_HEADS, NSA_GROUP), tq, axis=1)
    return np.ascontiguousarray(np.broadcast_to(rows[:, :, None], (NSA_KV_HEADS, NSA_GROUP * tq, LANES)))


def _pad_rows(x, n):
    return jnp.pad(x, ((0, 0), (0, n - x.shape[1])) + ((0, 0),) * (x.ndim - 2))


def _trunk_layer(x, lw, lay, consts, nsa_fn, hg_state, ret_state, *, seq_tiles):
    bsz, t, d = x.shape
    tm, tt_h, tt_r, t_valid = seq_tiles
    x2 = x.reshape(bsz * t, d)
    wts, layer = lw["weights"], lw["layer"]
    h = rms_matmul(x2, lw["norm_attn"], wts["w_in"], layer, lay, tm=tm, tn=IN_PROJ_TN)
    np_ = h.shape[1]
    h3 = h.reshape(bsz, t, np_)
    tp = -(-t // tt_h) * tt_h
    h3p = _pad_rows(h3, tp) if tp != t else h3
    oa, hg_new = hgrn_mixer(h3p, lay, lw["lb"], lw["hg_out_norm"], hg_state, tt=tt_h,
                            t_valid=None if tp == t else t_valid)
    tpr = -(-t // tt_r) * tt_r
    h3r = _pad_rows(h3, tpr) if tpr != t else h3
    oc, ret_new = retention_mixer(h3r, lay, consts["lg_tab"], ret_state, tt=tt_r,
                                  n_valid=tt_r if tpr == t else t_valid)
    ob = nsa_fn(h3)
    flat = lambda a: a[:, :t].reshape(bsz * t, a.shape[-1])
    merged = merge_branches(h, lay, flat(oa), flat(ob), flat(oc),
                            wts["w_branch_hg"], wts["w_branch_nsa"], wts["w_branch_ret"], layer, tm=tm, tn=512)
    x2 = matmul_res(merged, wts["w_out"], layer, x2, tm=tm, tn=512)
    hid = ffn_up(x2, lw["norm_ffn"], wts["w_gate"], wts["w_up"], layer, tm=tm, tn=512)
    x2 = matmul_res(hid, wts["w_down"], layer, x2, tm=tm, tn=512)
    return x2.reshape(bsz, t, d), h3, hg_new, ret_new


def kernel(x_prompt, x_sample, cache_nsa, cache_win, state_hgrn, state_ret, page_table,
           norm_attn, w_in, hgrn_lb_logits, hgrn_out_norm, nsa_q_norm, nsa_k_norm, nsa_cmp_w,
           w_branch_hg, w_branch_nsa, w_branch_ret, w_out, norm_ffn, w_gate, w_up, w_down):
    depth = w_in.shape[0]
    bp, tp, d = x_prompt.shape
    bs, ts, _ = x_sample.shape
    n_pool = cache_nsa.shape[1]
    n_pages = page_table.shape[1]
    past = n_pages * PAGE_SIZE
    wbuf = cache_win.shape[2]
    lay = _layout(d)

    sm = jax.nn.softmax(hgrn_lb_logits.astype(F32), axis=0)
    lower_bounds = jnp.clip(jnp.cumsum(sm, axis=0) - sm[0:1], 0.0, 1.0 - 1e-6)

    tt_p = 256 if tp % 256 == 0 else tp
    lg = np.log1p(-np.exp2(np.float32(-5.0) - np.arange(RET_HEADS, dtype=np.float32))).astype(np.float32)
    consts = {
        "gate_expand": _gate_expand_matrix(),
        "lg_tab": np.ascontiguousarray(np.broadcast_to(lg[:, None, None], (RET_HEADS, 1, max(tt_p, LANES)))),
    }
    bsum = _block_sum_matrix(NSA_KV_WIDTH)
    pool_t = jnp.transpose(cache_nsa, (0, 1, 3, 4, 5, 2)).reshape(depth * n_pool, 4, NSA_KV_WIDTH, PAGE_SIZE)
    win3 = cache_win.reshape(depth, bs, wbuf, 2 * NSA_KV_WIDTH)

    ts_pad = 16
    tq_s = 16
    n_sel_p = -(-tp // SEL_BLOCK)
    n_sel_s = past // SEL_BLOCK + -(-ts // SEL_BLOCK)
    tk_tail = 2048 if past % 2048 == 0 else 512
    tq_p = 128 if tp % 128 == 0 else tp

    weights = {
        "w_in": jnp.swapaxes(w_in, 1, 2).astype(MXU_DT).reshape(depth * w_in.shape[2], d),
        "w_branch_hg": w_branch_hg.astype(MXU_DT), "w_branch_nsa": w_branch_nsa.astype(MXU_DT),
        "w_branch_ret": w_branch_ret.astype(MXU_DT), "w_out": w_out.astype(MXU_DT),
        "w_gate": w_gate.astype(MXU_DT), "w_up": w_up.astype(MXU_DT), "w_down": w_down.astype(MXU_DT),
    }
    assert w_in.shape[2] == lay["used"]

    xp, xs = x_prompt, x_sample
    outs = {k: [] for k in ("kv_p", "kv_s", "win_p", "win_s", "hg_p", "hg_s", "ret_p", "ret_s")}
    for l in range(depth):
        lw = {
            "weights": weights, "layer": l,
            "norm_attn": norm_attn[l], "lb": lower_bounds[l].reshape(1, -1),
            "hg_out_norm": hgrn_out_norm[l], "norm_ffn": norm_ffn[l],
        }
        gq, gk, cw = nsa_q_norm[l], nsa_k_norm[l], nsa_cmp_w[l]
        col_cmp = lay["nkv"] // (2 * NSA_KV_WIDTH)
        col_sel = col_cmp + 1
        col_win = lay["nwkv"] // (2 * NSA_KV_WIDTH)
        nbg_col = lay["nbg"] // LANES

        def nsa_prompt(h3):
            tr = 512 if tp % 512 == 0 else tp
            zeros16 = jnp.zeros((bp, CMP_STRIDE, 2 * NSA_KV_WIDTH), F32)
            kc, vc = cmp_prep(h3, col_cmp, zeros16, cw, gk[0], bsum, tr=tr)
            if tp % PROMPT_SEL_TK == 0 and n_sel_p <= 64 and tp >= WINDOW + PROMPT_TQ:
                qn = q_prep(h3, lay, gq, bsum, tr=tr, width=LANES)
                ks, vs = kv_prep(h3, col_sel, gk[1], bsum, tr=tr,
                                 aug=_key_digits(tp, SEL_TK, LANES, 2 * LANES, True))
                kw, vw = kv_prep(h3, col_win, gk[2], bsum, tr=tr,
                                 aug=_key_digits(tp, WIN_TK, NSA_HEAD_DIM, LANES, False))
                return nsa_attention_prompt(qn, _slope_rows(PROMPT_TQ), kc, vc, ks, vs, kw, vw,
                                            h3, nbg_col, consts["gate_expand"], n_sel=n_sel_p)
            qn = q_prep(h3, lay, gq, bsum, tr=tr)
            ks, vs = kv_prep(h3, col_sel, gk[1], bsum, tr=tr)
            kw, vw = kv_prep(h3, col_win, gk[2], bsum, tr=tr)
            return nsa_attention(qn, _slope_rows(tq_p), kc, vc, ks, vs, kw, vw, h3, nbg_col, consts["gate_expand"],
                                 tq=tq_p, sel_tk=SEL_TK, qpos0=0, wpos0=0, n_sel=n_sel_p)

        def nsa_sample(h3):
            pages = page_table + l * n_pool
            h16 = _pad_rows(h3, ts_pad)
            qn = q_prep(h16, lay, gq, bsum, tr=ts_pad)
            w2 = 2 * NSA_KV_WIDTH
            tail16 = h16[:, :, lay["nkv"]:lay["nkv"] + w2].reshape(bs, ts_pad, 2, NSA_KV_WIDTH)
            tail_t = jnp.pad(jnp.transpose(tail16, (0, 2, 3, 1)), ((0, 0), (0, 0), (0, 0), (0, PAGE_SIZE - ts_pad)))
            kc, vc = cmp_prep_paged(pool_t, pages, tail_t, cw, gk[0])
            htail = _pad_rows(h3[:, :, col_sel * w2:(col_sel + 1) * w2], tk_tail)
            ks_tail, vs_tail = kv_prep(htail, 0, gk[1], bsum, tr=tk_tail)
            feature_major = lambda a: jnp.swapaxes(a, 2, 3).reshape(bs, NSA_KV_WIDTH, tk_tail)
            kw_old, vw_old = kv_prep(win3[l], 0, gk[2], bsum, tr=wbuf)
            hw_new = _pad_rows(h3[:, :, col_win * w2:(col_win + 1) * w2], WIN_TK)
            kw_new, vw_new = kv_prep(hw_new, 0, gk[2], bsum, tr=WIN_TK)
            kw = jnp.concatenate([kw_old, kw_new], axis=2)
            vw = jnp.concatenate([vw_old, vw_new], axis=2)
            return nsa_attention_paged(qn, _slope_rows(tq_s), kc, vc, kw, vw, pool_t, pages,
                                       feature_major(ks_tail), feature_major(vs_tail), gk[1], h16, nbg_col,
                                       consts["gate_expand"], tq=tq_s, t_real=ts, qpos0=past, wpos0=past - wbuf,
                                       n_sel=n_sel_s)

        zeros_state = jnp.zeros((bp, HG_HEADS, HG_DK, HG_DV), F32)
        tm_p = 1024 if (bp * tp) % 1024 == 0 else bp * tp
        xp, h3p, hgp, rtp = _trunk_layer(xp, lw, lay, consts, nsa_prompt, zeros_state, zeros_state,
                                         seq_tiles=(tm_p, tt_p, tt_p, tp))
        xs, h3s, hgs, rts = _trunk_layer(xs, lw, lay, consts, nsa_sample, state_hgrn[l], state_ret[l],
                                         seq_tiles=(bs * ts, ts_pad, ts_pad, ts))

        kv_cols = slice(lay["nkv"], lay["nkv"] + 4 * NSA_KV_WIDTH)
        win_cols = slice(lay["nwkv"], lay["nwkv"] + 2 * NSA_KV_WIDTH)
        outs["kv_p"].append(h3p[:, :, kv_cols].reshape(bp, tp, 4, NSA_KV_HEADS, NSA_HEAD_DIM))
        outs["kv_s"].append(h3s[:, :, kv_cols].reshape(bs, ts, 4, NSA_KV_HEADS, NSA_HEAD_DIM))
        wlen = min(WINDOW, tp)
        outs["win_p"].append(h3p[:, tp - wlen:, win_cols].reshape(bp, wlen, 2, NSA_KV_HEADS, NSA_HEAD_DIM))
        ctx = jnp.concatenate([win3[l], h3s[:, :, win_cols]], axis=1)[:, ts:]
        outs["win_s"].append(ctx.reshape(bs, wbuf, 2, NSA_KV_HEADS, NSA_HEAD_DIM))
        outs["hg_p"].append(hgp); outs["hg_s"].append(hgs)
        outs["ret_p"].append(rtp); outs["ret_s"].append(rts)

    st = lambda k: jnp.stack(outs[k])
    return (xp, xs, st("kv_p"), st("kv_s"), st("win_p"), st("win_s"),
            st("hg_p"), st("hg_s"), st("ret_p"), st("ret_s"))
```

```python
import functools

import numpy as np

import jax
import jax.numpy as jnp
from jax import lax
from jax.experimental import pallas as pl
from jax.experimental.pallas import tpu as pltpu

F32 = jnp.float32
MXU_DT = jnp.bfloat16

HG_HEADS, HG_DK, HG_DV = 8, 128, 128
NSA_HEADS, NSA_KV_HEADS, NSA_GROUP, NSA_HEAD_DIM = 16, 4, 4, 64
NSA_WIDTH = NSA_HEADS * NSA_HEAD_DIM
NSA_KV_WIDTH = NSA_KV_HEADS * NSA_HEAD_DIM
CMP_BLOCK, CMP_STRIDE, SEL_BLOCK, NSA_TOPK, WINDOW = 32, 16, 64, 16, 512
RET_HEADS, RET_DK, RET_DV = 8, 128, 128
PAGE_SIZE = 128
EPS = 1e-6
NEG_BIG = -1e30
POS_BIG = 1e30
MIN_F = 1e-20
LOWEST = -3.0e38

LANES = 128
VMEM_LIMIT_BYTES = 56 * 1024 * 1024

HG_SUB_SHIFT = 4
HG_SUB = 1 << HG_SUB_SHIFT
PAGES_PER_STEP = 16
IN_PROJ_TN = 1024
SEL_TK = 256
PROMPT_SEL_TK = 512
PROMPT_TQ = 256
RET_HPS = 2
WIN_TK = 128


def _cparams(*sem):
    return pltpu.CompilerParams(dimension_semantics=sem, vmem_limit_bytes=VMEM_LIMIT_BYTES)


def _dot(a, b):
    return jnp.dot(a, b, preferred_element_type=F32)


def _dot_nt(a, b):
    return lax.dot_general(a, b, (((1,), (1,)), ((), ())), preferred_element_type=F32)


def _split(x, n):
    out = []
    r = x
    for _ in range(n):
        p = r.astype(MXU_DT)
        out.append(p)
        r = r - p.astype(F32)
    return out


def _sigmoid(x):
    return 1.0 / (1.0 + jnp.exp(-x))


def _silu(x):
    return x * _sigmoid(x)


def _iota(shape, dim):
    return lax.broadcasted_iota(jnp.int32, shape, dim)


def _eye(n):
    return (_iota((n, n), 0) == _iota((n, n), 1)).astype(MXU_DT)


def _transpose_exact(x):
    eye = _eye(x.shape[1])
    return sum(_dot_nt(eye, p) for p in _split(x, 3))


def _head_rms(x, bsum, gain):
    ssq = sum(_dot(p, bsum) for p in _split(x * x, 2))
    return x * lax.rsqrt(ssq * (1.0 / NSA_HEAD_DIM) + EPS) * gain


def _rms_matmul_kernel(x_ref, g_ref, w_ref, o_ref, xn_ref):
    @pl.when(pl.program_id(1) == 0)
    def _():
        x = x_ref[...]
        ms = jnp.mean(x * x, axis=-1, keepdims=True)
        xn_ref[...] = (x * lax.rsqrt(ms + EPS) * g_ref[...]).astype(xn_ref.dtype)

    o_ref[...] = _dot_nt(xn_ref[...], w_ref[...])


def rms_matmul(x, gain, wt, layer, lay, *, tm, tn):
    m, d = x.shape
    assert tn == IN_PROJ_TN and lay["used"] % 16 == 0
    nb_pre = lay["post_col"] // tn
    nb_all = lay["total"] // tn
    base = layer * lay["used"]

    def w_row(i, j):
        row = base + jnp.where(j < nb_pre, j * tn, lay["post_row"] + (j - nb_pre) * tn)
        return (pl.multiple_of(row, 16), 0)

    return pl.pallas_call(
        _rms_matmul_kernel,
        grid=(m // tm, nb_all),
        in_specs=[pl.BlockSpec((tm, d), lambda i, j: (i, 0)),
                  pl.BlockSpec((1, d), lambda i, j: (0, 0)),
                  pl.BlockSpec((pl.Element(tn), pl.Element(d)), w_row)],
        out_specs=pl.BlockSpec((tm, tn), lambda i, j: (i, j)),
        out_shape=jax.ShapeDtypeStruct((m, lay["total"]), F32),
        scratch_shapes=[pltpu.VMEM((tm, d), MXU_DT)],
        compiler_params=_cparams("parallel", "arbitrary"),
        name="in_proj",
    )(x, gain.reshape(1, d), wt)


def _ffn_up_kernel(x_ref, g_ref, wg_ref, wu_ref, o_ref, xn_ref):
    @pl.when(pl.program_id(1) == 0)
    def _():
        x = x_ref[...]
        ms = jnp.mean(x * x, axis=-1, keepdims=True)
        xn_ref[...] = (x * lax.rsqrt(ms + EPS) * g_ref[...]).astype(xn_ref.dtype)

    xn = xn_ref[...]
    o_ref[...] = (_silu(_dot(xn, wg_ref[...])) * _dot(xn, wu_ref[...])).astype(o_ref.dtype)


def _layer_cols(k, tn, layer):
    return pl.BlockSpec((None, k, tn), lambda i, j: (layer, 0, j))


def ffn_up(x, gain, wg, wu, layer, *, tm, tn):
    m, d = x.shape
    f = wg.shape[2]
    return pl.pallas_call(
        _ffn_up_kernel,
        grid=(m // tm, f // tn),
        in_specs=[pl.BlockSpec((tm, d), lambda i, j: (i, 0)),
                  pl.BlockSpec((1, d), lambda i, j: (0, 0)),
                  _layer_cols(d, tn, layer), _layer_cols(d, tn, layer)],
        out_specs=pl.BlockSpec((tm, tn), lambda i, j: (i, j)),
        out_shape=jax.ShapeDtypeStruct((m, f), MXU_DT),
        scratch_shapes=[pltpu.VMEM((tm, d), MXU_DT)],
        compiler_params=_cparams("parallel", "arbitrary"),
        name="ffn_up",
    )(x, gain.reshape(1, d), wg, wu)


def _matmul_res_kernel(a_ref, w_ref, r_ref, o_ref):
    o_ref[...] = r_ref[...] + _dot(a_ref[...], w_ref[...])


def matmul_res(a, w, layer, res, *, tm, tn):
    m, k = a.shape
    n = w.shape[2]
    return pl.pallas_call(
        _matmul_res_kernel,
        grid=(m // tm, n // tn),
        in_specs=[pl.BlockSpec((tm, k), lambda i, j: (i, 0)),
                  _layer_cols(k, tn, layer),
                  pl.BlockSpec((tm, tn), lambda i, j: (i, j))],
        out_specs=pl.BlockSpec((tm, tn), lambda i, j: (i, j)),
        out_shape=jax.ShapeDtypeStruct((m, n), F32),
        compiler_params=_cparams("parallel", "arbitrary"),
        name="matmul_res",
    )(a, w, res)


def _merge_kernel(oa_ref, ob_ref, oc_ref, ga_ref, gb_ref, gc_ref, wa_ref, wb_ref, wc_ref, o_ref):
    o_ref[...] = (_sigmoid(ga_ref[...]) * _dot(oa_ref[...], wa_ref[...])
                  + _sigmoid(gb_ref[...]) * _dot(ob_ref[...], wb_ref[...])
                  + _sigmoid(gc_ref[...]) * _dot(oc_ref[...], wc_ref[...])).astype(o_ref.dtype)


def merge_branches(h, lay, oa, ob, oc, wa, wb, wc, layer, *, tm, tn):
    m = h.shape[0]
    d = wa.shape[2]
    wdt = oa.shape[1]
    ca, cb, cc = ((lay["mg"] + i * d) // tn for i in range(3))
    row = lambda i, j: (i, 0)
    return pl.pallas_call(
        _merge_kernel,
        grid=(m // tm, d // tn),
        in_specs=[pl.BlockSpec((tm, wdt), row), pl.BlockSpec((tm, wdt), row), pl.BlockSpec((tm, wdt), row),
                  pl.BlockSpec((tm, tn), lambda i, j: (i, ca + j)),
                  pl.BlockSpec((tm, tn), lambda i, j: (i, cb + j)),
                  pl.BlockSpec((tm, tn), lambda i, j: (i, cc + j)),
                  _layer_cols(wdt, tn, layer), _layer_cols(wdt, tn, layer), _layer_cols(wdt, tn, layer)],
        out_specs=pl.BlockSpec((tm, tn), lambda i, j: (i, j)),
        out_shape=jax.ShapeDtypeStruct((m, d), MXU_DT),
        compiler_params=_cparams("parallel", "arbitrary"),
        name="merge",
    )(oa, ob, oc, h, h, h, wa, wb, wc)


def _hgrn_kernel(hq_ref, hf_ref, hi_ref, hg_ref, lb_ref, gain_ref, s0_ref, o_ref, sout_ref,
                 st_ref, ut_ref, *, tt, t_valid):
    c = HG_SUB
    nj = tt // c
    tb = pl.program_id(2)

    @pl.when(tb == 0)
    def _():
        st_ref[...] = _transpose_exact(s0_ref[0, 0])

    hq = hq_ref[0]
    lb = lb_ref[...]
    q = _silu(hq)
    sig = _sigmoid(hf_ref[0])
    g = jnp.log(jnp.maximum(lb + (1.0 - lb) * sig, MIN_F))
    kk = (1.0 - lb) * (1.0 - sig)
    v = hi_ref[0]
    row = _iota((tt, HG_DK), 0)
    if t_valid is not None:
        live = (tb * tt + row) < t_valid
        g = jnp.where(live, g, 0.0)
        kk = jnp.where(live, kk, 0.0)

    r2 = _iota((tt, tt), 0)
    c2 = _iota((tt, tt), 1)
    same = (r2 >> HG_SUB_SHIFT) == (c2 >> HG_SUB_SHIFT)
    gp = _split(g, 3)
    tri = (same & (c2 <= r2)).astype(MXU_DT)
    blk = same.astype(MXU_DT)
    b = sum(_dot(tri, p) for p in gp)
    dtot = sum(_dot(blk, p) for p in gp)
    qe = q * jnp.exp(b)
    ke = kk * jnp.exp(dtot - b)
    edec = jnp.exp(dtot)

    q3 = q.reshape(nj, c, HG_DK)
    kk3 = kk.reshape(nj, c, HG_DK)
    b3 = b.reshape(nj, c, HG_DK)
    v3 = v.reshape(nj, c, HG_DV)
    half = c // 2
    tpos = _iota((nj, half, HG_DK), 1)
    ones = jnp.ones((HG_DK, HG_DV), MXU_DT)

    def contrib(rows, s, first_visible):
        diff = b3[:, rows, :] - b3[:, s:s + 1, :]
        if first_visible is not None:
            m = tpos >= first_visible
            diff = jnp.where(m, diff, 0.0)
        y = q3[:, rows, :] * kk3[:, s:s + 1, :] * jnp.exp(diff)
        if first_visible is not None:
            y = jnp.where(m, y, 0.0)
        z = _dot(y.reshape(nj * half, HG_DK).astype(MXU_DT), ones).reshape(nj, half, HG_DV)
        return z * v3[:, s:s + 1, :]

    lower, upper = slice(0, half), slice(half, c)
    o_lo = jnp.zeros((nj, half, HG_DV), F32)
    o_up = jnp.zeros((nj, half, HG_DV), F32)
    for s in range(c):
        if s < half:
            o_lo = o_lo + contrib(lower, s, s)
            o_up = o_up + contrib(upper, s, None)
        else:
            o_up = o_up + contrib(upper, s, s - half)
    o = jnp.concatenate([o_lo, o_up], axis=1).reshape(tt, HG_DV)

    v_b = v.astype(MXU_DT)
    ke_b = ke.astype(MXU_DT)
    for j in range(nj):
        ut_ref[j] = lax.dot_general(v_b[j * c:(j + 1) * c], ke_b[j * c:(j + 1) * c],
                                    (((0,), (0,)), ((), ())), preferred_element_type=F32)

    qe_b = qe.astype(MXU_DT)
    outs = []
    for j in range(nj):
        st = st_ref[...]
        outs.append(_dot_nt(qe_b[j * c:(j + 1) * c], st.astype(MXU_DT)))
        st_ref[...] = st * edec[j * c:j * c + 1, :] + ut_ref[j]
    o = o + jnp.concatenate(outs, axis=0)

    on = o * lax.rsqrt(jnp.mean(o * o, axis=-1, keepdims=True) + EPS) * gain_ref[...]
    o_ref[0] = (on * _silu(hg_ref[0])).astype(o_ref.dtype)

    @pl.when(tb == pl.num_programs(2) - 1)
    def _():
        sout_ref[0, 0] = _transpose_exact(st_ref[...])


def hgrn_mixer(h3, lay, lb, out_gain, s0, *, tt, t_valid):
    bsz, t, _ = h3.shape
    cq, cf, ci, cg = (lay[k] // LANES for k in ("hq", "hf", "hi", "hg"))
    col = lambda c0: (lambda b, hd, tb: (b, tb, c0 + hd))
    kern = functools.partial(_hgrn_kernel, tt=tt, t_valid=t_valid)
    return pl.pallas_call(
        kern,
        grid=(bsz, HG_HEADS, t // tt),
        in_specs=[pl.BlockSpec((1, tt, LANES), col(cq)), pl.BlockSpec((1, tt, LANES), col(cf)),
                  pl.BlockSpec((1, tt, LANES), col(ci)), pl.BlockSpec((1, tt, LANES), col(cg)),
                  pl.BlockSpec((1, HG_DK), lambda b, hd, tb: (0, hd)),
                  pl.BlockSpec((1, HG_DV), lambda b, hd, tb: (0, 0)),
                  pl.BlockSpec((1, 1, HG_DK, HG_DV), lambda b, hd, tb: (b, hd, 0, 0))],
        out_specs=[pl.BlockSpec((1, tt, HG_DV), lambda b, hd, tb: (b, tb, hd)),
                   pl.BlockSpec((1, 1, HG_DK, HG_DV), lambda b, hd, tb: (b, hd, 0, 0))],
        out_shape=[jax.ShapeDtypeStruct((bsz, t, HG_HEADS * HG_DV), MXU_DT),
                   jax.ShapeDtypeStruct((bsz, HG_HEADS, HG_DK, HG_DV), F32)],
        scratch_shapes=[pltpu.VMEM((HG_DV, HG_DK), F32),
                        pltpu.VMEM((tt // HG_SUB, HG_DV, HG_DK), F32)],
        compiler_params=_cparams("parallel", "parallel", "arbitrary"),
        name="hgrn",
    )(h3, h3, h3, h3, lb, out_gain.reshape(1, HG_DV), s0)


def _ret_kernel(q_ref, k_ref, v_ref, g_ref, lg_ref, s0_ref, o_ref, sout_ref, s_ref, *, tt, n_valid):
    tb = pl.program_id(2)

    @pl.when(tb == 0)
    def _():
        s_ref[...] = s0_ref[0]

    rel = (_iota((tt, tt), 0) - _iota((tt, tt), 1)).astype(F32)
    spos = _iota((RET_DK, tt), 1)
    for hh in range(RET_HPS):
        cols = slice(hh * RET_DK, (hh + 1) * RET_DK)
        lgl = lg_ref[hh][:, :LANES]
        lgt = lg_ref[hh][:, :tt]
        q = q_ref[0][:, cols] * (RET_DK ** -0.5)
        k = k_ref[0][:, cols]
        vb = v_ref[0][:, cols].astype(MXU_DT)
        dmat = jnp.where(rel >= 0, jnp.exp(jnp.maximum(rel, 0.0) * lgt), 0.0)
        att = _dot_nt(q.astype(MXU_DT), k.astype(MXU_DT)) * dmat
        q_dec = jnp.exp((_iota((tt, RET_DK), 0) + 1).astype(F32) * lgl)
        s = s_ref[hh]
        o = _dot(att.astype(MXU_DT), vb) + _dot((q * q_dec).astype(MXU_DT), s.astype(MXU_DT))

        k_dec = jnp.where(spos < n_valid, jnp.exp(jnp.maximum(n_valid - 1 - spos, 0).astype(F32) * lgt), 0.0)
        kt = _transpose_exact(k)
        s_ref[hh] = jnp.exp(float(n_valid) * lgl) * s + _dot((kt * k_dec).astype(MXU_DT), vb)

        mu = jnp.mean(o, axis=-1, keepdims=True)
        var = jnp.mean(jnp.square(o - mu), axis=-1, keepdims=True)
        o_ref[0, :, cols] = ((o - mu) * lax.rsqrt(var + EPS) * _silu(g_ref[0][:, cols])).astype(o_ref.dtype)

    @pl.when(tb == pl.num_programs(2) - 1)
    def _():
        sout_ref[0] = s_ref[...]


def retention_mixer(h3, lay, lg_tab, s0, *, tt, n_valid):
    bsz, t, _ = h3.shape
    assert n_valid == tt or t == tt
    hps = RET_HPS
    wd = hps * RET_DK
    cq, ck, cv, cg = (lay[k] // wd for k in ("rq", "rk", "rv", "rg"))
    col = lambda c0: (lambda b, hp, tb: (b, tb, c0 + hp))
    kern = functools.partial(_ret_kernel, tt=tt, n_valid=n_valid)
    return pl.pallas_call(
        kern,
        grid=(bsz, RET_HEADS // hps, t // tt),
        in_specs=[pl.BlockSpec((1, tt, wd), col(cq)), pl.BlockSpec((1, tt, wd), col(ck)),
                  pl.BlockSpec((1, tt, wd), col(cv)), pl.BlockSpec((1, tt, wd), col(cg)),
                  pl.BlockSpec((hps, 1, lg_tab.shape[2]), lambda b, hp, tb: (hp, 0, 0)),
                  pl.BlockSpec((1, hps, RET_DK, RET_DV), lambda b, hp, tb: (b, hp, 0, 0))],
        out_specs=[pl.BlockSpec((1, tt, hps * RET_DV), lambda b, hp, tb: (b, tb, hp)),
                   pl.BlockSpec((1, hps, RET_DK, RET_DV), lambda b, hp, tb: (b, hp, 0, 0))],
        out_shape=[jax.ShapeDtypeStruct((bsz, t, RET_HEADS * RET_DV), MXU_DT),
                   jax.ShapeDtypeStruct((bsz, RET_HEADS, RET_DK, RET_DV), F32)],
        scratch_shapes=[pltpu.VMEM((hps, RET_DK, RET_DV), F32)],
        compiler_params=_cparams("parallel", "parallel", "arbitrary"),
        name="retention",
    )(h3, h3, h3, h3, lg_tab, s0)


def _store_heads(ref, x, lead=()):
    for hh in range(NSA_KV_HEADS):
        ref[lead + (hh,)] = x[:, hh * NSA_HEAD_DIM:(hh + 1) * NSA_HEAD_DIM].astype(ref.dtype)


def _q_prep_kernel(q_ref, gain_ref, bsum_ref, o_ref):
    x = q_ref[0]
    hd = NSA_HEAD_DIM
    for kvh in range(NSA_KV_HEADS):
        lo = kvh * NSA_KV_WIDTH
        qn = _head_rms(x[:, lo:lo + NSA_KV_WIDTH], bsum_ref[...], gain_ref[...]) * (hd ** -0.5)
        for g in range(NSA_GROUP):
            o_ref[0, kvh, g, :, 0:hd] = qn[:, g * hd:(g + 1) * hd].astype(o_ref.dtype)
            if o_ref.shape[-1] > hd:
                o_ref[0, kvh, g, :, hd:] = jnp.zeros((x.shape[0], o_ref.shape[-1] - hd), o_ref.dtype)


def q_prep(h3, lay, gain_q, bsum, *, tr, width=NSA_HEAD_DIM):
    bsz, t, _ = h3.shape
    cq = lay["nq"] // NSA_WIDTH
    return pl.pallas_call(
        _q_prep_kernel,
        grid=(bsz, t // tr),
        in_specs=[pl.BlockSpec((1, tr, NSA_WIDTH), lambda b, i: (b, i, cq)),
                  pl.BlockSpec((1, NSA_KV_WIDTH), lambda b, i: (0, 0)),
                  pl.BlockSpec((NSA_KV_WIDTH, NSA_KV_WIDTH), lambda b, i: (0, 0))],
        out_specs=pl.BlockSpec((1, NSA_KV_HEADS, NSA_GROUP, tr, width), lambda b, i: (b, 0, 0, i, 0)),
        out_shape=jax.ShapeDtypeStruct((bsz, NSA_KV_HEADS, NSA_GROUP, t, width), MXU_DT),
        compiler_params=_cparams("parallel", "parallel"),
        name="nsa_q_prep",
    )(h3, jnp.tile(gain_q, NSA_KV_HEADS).reshape(1, NSA_KV_WIDTH), bsum)


def _kv_prep_kernel(x_ref, gain_ref, bsum_ref, *rest):
    ko_ref, vo_ref = rest[-2:]
    hd = NSA_HEAD_DIM
    x = x_ref[0]
    kn = _head_rms(x[:, :NSA_KV_WIDTH], bsum_ref[...], gain_ref[...])
    for hh in range(NSA_KV_HEADS):
        ko_ref[0, hh, :, 0:hd] = kn[:, hh * hd:(hh + 1) * hd].astype(ko_ref.dtype)
        if len(rest) == 3:
            ko_ref[0, hh, :, hd:] = rest[0][:, hd:]
        lo = NSA_KV_WIDTH + hh * hd
        vo_ref[0, hh, :, 0:hd] = x[:, lo:lo + hd].astype(vo_ref.dtype)
        if vo_ref.shape[-1] > hd:
            vo_ref[0, hh, :, hd:] = jnp.ones((x.shape[0], vo_ref.shape[-1] - hd), vo_ref.dtype)


def kv_prep(x3, col, gain, bsum, *, tr, aug=None):
    bsz, t, _ = x3.shape
    w2 = 2 * NSA_KV_WIDTH
    hd = NSA_HEAD_DIM
    kw = hd if aug is None else aug.shape[1]
    vw = hd if aug is None else LANES
    vshape = jax.ShapeDtypeStruct((bsz, NSA_KV_HEADS, t, vw), MXU_DT)
    kshape = jax.ShapeDtypeStruct((bsz, NSA_KV_HEADS, t, kw), MXU_DT)
    spec = lambda w: pl.BlockSpec((1, NSA_KV_HEADS, tr, w), lambda b, i: (b, 0, i, 0))
    in_specs = [pl.BlockSpec((1, tr, w2), lambda b, i: (b, i, col)),
                pl.BlockSpec((1, NSA_KV_WIDTH), lambda b, i: (0, 0)),
                pl.BlockSpec((NSA_KV_WIDTH, NSA_KV_WIDTH), lambda b, i: (0, 0))]
    args = [x3, jnp.tile(gain, NSA_KV_HEADS).reshape(1, NSA_KV_WIDTH), bsum]
    if aug is not None:
        in_specs.append(pl.BlockSpec((tr, kw), lambda b, i: (i, 0)))
        args.append(aug)
    return pl.pallas_call(
        _kv_prep_kernel,
        grid=(bsz, t // tr),
        in_specs=in_specs,
        out_specs=[spec(kw), spec(vw)],
        out_shape=[kshape, vshape],
        compiler_params=_cparams("parallel", "parallel"),
        name="nsa_kv_prep",
    )(*args)


def _head_rms_t(xt, gain_t):
    x3 = xt.reshape(NSA_KV_HEADS, NSA_HEAD_DIM, xt.shape[1])
    ms = jnp.mean(x3 * x3, axis=1, keepdims=True)
    return (x3 * lax.rsqrt(ms + EPS)).reshape(xt.shape) * gain_t


def _gain_t(gain):
    return jnp.broadcast_to(jnp.tile(gain, NSA_KV_HEADS)[:, None], (NSA_KV_WIDTH, LANES))


def _cmp_prep_kernel(*refs, n_in, rows):
    refs = refs[-(n_in + 7):]
    nxt_ref, tail_ref, w_ref, gain_ref, bsum_ref, kc_ref, vc_ref = refs[n_in:]
    cs = CMP_STRIDE
    nb = rows // cs
    w0 = w_ref[0]
    w1 = w_ref[1]
    a0 = []
    a1 = []
    for u in range(n_in):
        x3 = refs[u][0].reshape(nb, cs, 2 * NSA_KV_WIDTH)
        a0.append(jnp.sum(x3 * w0[None], axis=1))
        a1.append(jnp.sum(x3 * w1[None], axis=1))
    a0 = jnp.concatenate(a0, axis=0) if n_in > 1 else a0[0]
    a1 = jnp.concatenate(a1, axis=0) if n_in > 1 else a1[0]
    last = pl.program_id(1) == pl.num_programs(1) - 1
    nx = jnp.where(last, tail_ref[0], nxt_ref[0])
    a1_next = jnp.sum(nx * w1, axis=0, keepdims=True)
    tot = nb * n_in
    a1s = pltpu.roll(a1, tot - 1, 0)
    a1s = jnp.where(_iota(a1s.shape, 0) == tot - 1, a1_next, a1s)
    comp = a0 + a1s
    kc = _head_rms(comp[:, :NSA_KV_WIDTH], bsum_ref[...], gain_ref[...])
    _store_heads(kc_ref, kc, (0,))
    _store_heads(vc_ref, comp[:, NSA_KV_WIDTH:], (0,))


def _cmp_weight_table(cmp_w):
    w = cmp_w.reshape(2, CMP_BLOCK // CMP_STRIDE, CMP_STRIDE)
    w = jnp.transpose(w, (1, 2, 0))
    return jnp.repeat(w, NSA_KV_WIDTH, axis=2)


def cmp_prep(x3, col, tail16, cmp_w, gain, bsum, *, tr):
    bsz, t, _ = x3.shape
    w2 = 2 * NSA_KV_WIDTH
    nbt = tr // CMP_STRIDE
    n_steps = t // tr
    kern = functools.partial(_cmp_prep_kernel, n_in=1, rows=tr)
    oshape = jax.ShapeDtypeStruct((bsz, NSA_KV_HEADS, t // CMP_STRIDE, NSA_HEAD_DIM), MXU_DT)
    ospec = pl.BlockSpec((1, NSA_KV_HEADS, nbt, NSA_HEAD_DIM), lambda b, i: (b, 0, i, 0))
    chunks_per_step = tr // CMP_STRIDE
    return pl.pallas_call(
        kern,
        grid=(bsz, n_steps),
        in_specs=[pl.BlockSpec((1, tr, w2), lambda b, i: (b, i, col)),
                  pl.BlockSpec((1, CMP_STRIDE, w2),
                               lambda b, i: (b, jnp.minimum(i + 1, n_steps - 1) * chunks_per_step, col)),
                  pl.BlockSpec((1, CMP_STRIDE, w2), lambda b, i: (b, 0, 0)),
                  pl.BlockSpec((2, CMP_STRIDE, w2), lambda b, i: (0, 0, 0)),
                  pl.BlockSpec((1, NSA_KV_WIDTH), lambda b, i: (0, 0)),
                  pl.BlockSpec((NSA_KV_WIDTH, NSA_KV_WIDTH), lambda b, i: (0, 0))],
        out_specs=[ospec, ospec],
        out_shape=[oshape, oshape],
        compiler_params=_cparams("parallel", "arbitrary"),
        name="nsa_cmp_prep",
    )(x3, x3, tail16, _cmp_weight_table(cmp_w), jnp.tile(gain, NSA_KV_HEADS).reshape(1, NSA_KV_WIDTH), bsum)


def _cmp_prep_t_kernel(*refs, n_in):
    refs = refs[-(n_in + 6):]
    nxt_ref, tail_ref, w_ref, gain_ref, kc_ref, vc_ref = refs[n_in:]
    nb = w_ref.shape[2] // 4
    kvw = NSA_KV_WIDTH

    last = pl.program_id(1) == pl.num_programs(1) - 1
    nxt = jnp.where(last, tail_ref[0], nxt_ref[0])
    rows = [refs[u][0].reshape(2 * kvw, PAGE_SIZE).astype(MXU_DT) for u in range(n_in)]
    rows.append(nxt.reshape(2 * kvw, PAGE_SIZE).astype(MXU_DT))
    acc = _dot(jnp.concatenate(rows, axis=1), w_ref[...].reshape((n_in + 1) * PAGE_SIZE, 4 * nb))
    comp_k = acc[:kvw, 0:nb] + acc[:kvw, 2 * nb:3 * nb]
    comp_v = acc[kvw:, nb:2 * nb] + acc[kvw:, 3 * nb:4 * nb]
    kct = _head_rms_t(comp_k, gain_ref[...][:, :nb]).astype(MXU_DT)
    eye = _eye(nb)
    _store_heads(kc_ref, _dot_nt(eye, kct), (0,))
    _store_heads(vc_ref, _dot_nt(eye, comp_v.astype(MXU_DT)), (0,))


def _cmp_band_tables(cmp_w, n_pages_step):
    cs = CMP_STRIDE
    w = cmp_w.reshape(2, CMP_BLOCK // cs, cs)
    nb = n_pages_step * PAGE_SIZE // cs
    rho = np.arange(PAGE_SIZE)
    ch = (np.arange(n_pages_step)[:, None] * (PAGE_SIZE // cs) + rho[None, :] // cs)[..., None]
    n = np.arange(nb)[None, None, :]
    first = (ch == n).astype(np.float32)
    second = (ch == n + 1).astype(np.float32)
    nxt_rows = ((rho[:, None] < cs) & (n[0] == nb - 1)).astype(np.float32)
    tabs = []
    for c in range(2):
        w0 = jnp.tile(w[c, 0], PAGE_SIZE // cs)
        w1 = jnp.tile(w[c, 1], PAGE_SIZE // cs)
        full = w0[None, :, None] * first + w1[None, :, None] * second
        nxt = w1[:, None] * nxt_rows
        tabs.append(jnp.concatenate([full, nxt[None]], axis=0))
    hi = [t.astype(MXU_DT) for t in tabs]
    lo = [(t - h.astype(F32)).astype(MXU_DT) for t, h in zip(tabs, hi)]
    return jnp.concatenate(hi + lo, axis=2)


def cmp_prep_paged(pool_t, pages, tail_t, cmp_w, gain):
    bsz, n_pages = pages.shape
    pp = PAGES_PER_STEP
    n_steps = n_pages // pp
    nbt = pp * PAGE_SIZE // CMP_STRIDE
    kern = functools.partial(_cmp_prep_t_kernel, n_in=pp)
    oshape = jax.ShapeDtypeStruct((bsz, NSA_KV_HEADS, n_pages * PAGE_SIZE // CMP_STRIDE, NSA_HEAD_DIM), MXU_DT)
    ospec = pl.BlockSpec((1, NSA_KV_HEADS, nbt, NSA_HEAD_DIM), lambda b, i, pt: (b, 0, i, 0))
    pblock = (1, 2, NSA_KV_WIDTH, PAGE_SIZE)
    page_spec = lambda u: pl.BlockSpec(pblock, lambda b, i, pt: (pt[b, i * pp + u], 0, 0, 0))
    const = lambda *shape: pl.BlockSpec(shape, lambda b, i, pt: (0,) * len(shape))
    return pl.pallas_call(
        kern,
        grid_spec=pltpu.PrefetchScalarGridSpec(
            num_scalar_prefetch=1,
            grid=(bsz, n_steps),
            in_specs=[page_spec(u) for u in range(pp)]
                     + [pl.BlockSpec(pblock, lambda b, i, pt: (pt[b, jnp.minimum(i + 1, n_steps - 1) * pp], 0, 0, 0)),
                        pl.BlockSpec(pblock, lambda b, i, pt: (b, 0, 0, 0)),
                        const(pp + 1, PAGE_SIZE, 4 * nbt), const(NSA_KV_WIDTH, LANES)],
            out_specs=[ospec, ospec]),
        out_shape=[oshape, oshape],
        compiler_params=_cparams("parallel", "arbitrary"),
        name="nsa_cmp_prep_paged",
    )(pages, *([pool_t] * pp), pool_t, tail_t, _cmp_band_tables(cmp_w, pp), _gain_t(gain))


def _gate_and_store(o_ref, br_ref, nbg_ref, gx_ref, branches, tq):
    hd = NSA_HEAD_DIM
    for br, o in enumerate(branches):
        for g in range(NSA_GROUP):
            br_ref[br, :, g * hd:(g + 1) * hd] = o[g * tq:(g + 1) * tq]
    pieces = _split(_sigmoid(nbg_ref[0]), 2)
    acc = None
    for br in range(3):
        term = sum(_dot(p, gx_ref[br]) for p in pieces) * br_ref[br]
        acc = term if acc is None else acc + term
    o_ref[0] = acc.astype(o_ref.dtype)


def _nsa_kernel(q_ref, sl_ref, kc_ref, vc_ref, ks_ref, vs_ref, kw_ref, vw_ref, at_ref, nbg_ref, gx_ref,
                o_ref, sc_ref, m_ref, l_ref, acc_ref, br_ref,
                *, tq, sel_tk, qpos0, wpos0, n_sel, topk):
    rows = NSA_GROUP * tq
    hd = NSA_HEAD_DIM
    t0 = qpos0 + pl.program_id(2) * tq
    q = q_ref[0, 0].reshape(rows, hd)
    slope = sl_ref[0][:, :1]
    nbp = kc_ref.shape[2]
    n_sel_pad, tl = sc_ref.shape

    def tok(shape):
        return t0 + (_iota(shape, 0) & (tq - 1))

    s = _dot_nt(q, kc_ref[0, 0])
    dist = tok((rows, nbp)) - (_iota((rows, nbp), 1) * CMP_STRIDE + (CMP_BLOCK - 1))
    valid = dist >= 0
    s = jnp.where(valid, s - slope * dist.astype(F32), NEG_BIG)
    p = jnp.where(valid, jnp.exp(s - jnp.max(s, axis=-1, keepdims=True)), 0.0)
    p = p / jnp.maximum(jnp.sum(p, axis=-1, keepdims=True), 1e-30)
    o_cmp = _dot(p.astype(MXU_DT), vc_ref[0, 0])

    sel = _topk_block_mask(p, at_ref, t0, tq=tq, tl=tl, n_sel=n_sel, topk=topk)
    spread = ((_iota((rows, tl), 0) & (tq - 1)) == _iota((rows, tl), 1)).astype(MXU_DT)
    nblk = sel_tk // SEL_BLOCK
    if nblk % 8 == 0:
        sc_ref[...] = sel
        in_tile = (_iota((nblk, sel_tk), 0) == (_iota((nblk, sel_tk), 1) >> 6)).astype(MXU_DT)
    else:
        selb = _dot_nt(spread, sel.astype(MXU_DT)).astype(MXU_DT)

    def softmax_pass(k_ref, v_ref, tk, lo, hi, kpos0, mask_fn):
        m_ref[...] = jnp.full(m_ref.shape, NEG_BIG, F32)
        l_ref[...] = jnp.zeros(l_ref.shape, F32)
        acc_ref[...] = jnp.zeros(acc_ref.shape, F32)
        rel = (_iota((rows, tk), 0) & (tq - 1)) - _iota((rows, tk), 1)
        reps = tk // LANES

        def body(kt, carry):
            k0 = pl.multiple_of(kt * tk, tk)
            s = _dot_nt(q, k_ref[0, 0, pl.ds(k0, tk), :])
            d = rel + (t0 - kpos0 - k0)
            valid = mask_fn(d, k0)
            s = jnp.where(valid, s - slope * d.astype(F32), NEG_BIG)
            m_prev = m_ref[...]
            m_new = jnp.maximum(m_prev, jnp.max(s, axis=-1, keepdims=True))
            alpha = jnp.exp(m_prev - m_new)
            mrep = m_new if reps == 1 else jnp.concatenate([m_new] * reps, axis=1)
            p = jnp.where(valid, jnp.exp(s - mrep), 0.0)
            l_ref[...] = alpha * l_ref[...] + jnp.sum(p, axis=-1, keepdims=True)
            acc_ref[...] = alpha[:, :hd] * acc_ref[...] + _dot(p.astype(MXU_DT), v_ref[0, 0, pl.ds(k0, tk), :])
            m_ref[...] = m_new
            return carry

        lax.fori_loop(lo, hi, body, 0)
        return acc_ref[...] / jnp.maximum(l_ref[...][:, :hd], 1e-30)

    n_kt_all = ks_ref.shape[2] // sel_tk
    hi_sel = jnp.minimum(n_kt_all, (t0 + tq - 1) // sel_tk + 1)

    def sel_mask(d, k0):
        if nblk % 8 == 0:
            blk0 = pl.multiple_of(k0 // SEL_BLOCK, nblk)
            mine = _dot_nt(spread, sc_ref[pl.ds(blk0, nblk), :].astype(MXU_DT)).astype(MXU_DT)
            return (_dot(mine, in_tile) > 0.5) & (d >= 0)
        blk_of_key = (k0 + _iota((n_sel_pad, sel_tk), 1)) >> 6
        expand = (_iota((n_sel_pad, sel_tk), 0) == blk_of_key).astype(MXU_DT)
        return (_dot(selb, expand) > 0.5) & (d >= 0)

    o_sel = softmax_pass(ks_ref, vs_ref, sel_tk, 0, hi_sel, 0, sel_mask)

    n_wt_all = kw_ref.shape[2] // WIN_TK
    lo_win = jnp.maximum(t0 - (WINDOW - 1) - wpos0, 0) // WIN_TK
    hi_win = jnp.minimum(n_wt_all, (t0 + tq - 1 - wpos0) // WIN_TK + 1)
    o_win = softmax_pass(kw_ref, vw_ref, WIN_TK, lo_win, hi_win, wpos0,
                         lambda d, k0: (d >= 0) & (d < WINDOW))

    _gate_and_store(o_ref, br_ref, nbg_ref, gx_ref, (o_cmp, o_sel, o_win), tq)


def _imp_to_sel_matrix(n_sel_pad, nbp):
    m = SEL_BLOCK // CMP_STRIDE
    r = CMP_BLOCK // CMP_STRIDE
    jj = np.arange(n_sel_pad)[:, None]
    ii = np.arange(nbp)[None, :]
    cnt = sum((((ii - rr) >= m * jj) & ((ii - rr) < m * (jj + 1))).astype(np.float32) for rr in range(r))
    return cnt.astype(MXU_DT)


def nsa_attention(qn, slopes, kc, vc, ks, vs, kw, vw, h3, nbg_col, gate_expand, *, tq, sel_tk, qpos0, wpos0, n_sel):
    bsz, _, _, tql, hd = qn.shape
    nbp, tk_all, tw_all = kc.shape[2], ks.shape[2], kw.shape[2]
    assert tq & (tq - 1) == 0 and tk_all % sel_tk == 0 and tw_all % WIN_TK == 0
    n_sel_pad = -(-max(n_sel, tk_all // SEL_BLOCK) // 64) * 64
    tl = max(tq, LANES)
    rows = NSA_GROUP * tq
    at = _imp_to_sel_matrix(n_sel_pad, nbp)
    kern = functools.partial(_nsa_kernel, tq=tq, sel_tk=sel_tk, qpos0=qpos0, wpos0=wpos0, n_sel=n_sel,
                             topk=min(NSA_TOPK, n_sel))
    full = lambda n: pl.BlockSpec((1, 1, n, hd), lambda b, kh, i: (b, kh, 0, 0))
    gw = NSA_GROUP * hd
    return pl.pallas_call(
        kern,
        grid=(bsz, NSA_KV_HEADS, tql // tq),
        in_specs=[pl.BlockSpec((1, 1, NSA_GROUP, tq, hd), lambda b, kh, i: (b, kh, 0, i, 0)),
                  pl.BlockSpec((1, rows, LANES), lambda b, kh, i: (kh, 0, 0)),
                  full(nbp), full(nbp), full(tk_all), full(tk_all), full(tw_all), full(tw_all),
                  pl.BlockSpec((n_sel_pad, nbp), lambda b, kh, i: (0, 0)),
                  pl.BlockSpec((1, tq, LANES), lambda b, kh, i: (b, i, nbg_col)),
                  pl.BlockSpec((3, LANES, gw), lambda b, kh, i: (0, 0, kh))],
        out_specs=pl.BlockSpec((1, tq, gw), lambda b, kh, i: (b, i, kh)),
        out_shape=jax.ShapeDtypeStruct((bsz, tql, NSA_WIDTH), MXU_DT),
        scratch_shapes=[pltpu.VMEM((n_sel_pad, tl), F32),
                        pltpu.VMEM((rows, LANES), F32), pltpu.VMEM((rows, LANES), F32),
                        pltpu.VMEM((rows, hd), F32), pltpu.VMEM((3, tq, gw), F32)],
        compiler_params=_cparams("parallel", "parallel", "arbitrary"),
        name="nsa_attention",
    )(qn, slopes, kc, vc, ks, vs, kw, vw, at, h3, gate_expand)


def _topk_block_mask(p, at_ref, t0, *, tq, tl, n_sel, topk):
    n_sel_pad, nbp = at_ref.shape
    imp = p[0:tq] + p[tq:2 * tq] + p[2 * tq:3 * tq] + p[3 * tq:4 * tq]
    if tl > tq:
        imp = jnp.concatenate([imp, jnp.zeros((tl - tq, nbp), F32)], axis=0)
    imp_sel = sum(_dot_nt(at_ref[...], piece) for piece in _split(imp, 3))
    j = _iota((n_sel_pad, tl), 0)
    tt = t0 + _iota((n_sel_pad, tl), 1)
    forced = (j == 0) | (j == (tt >> 6))
    allowed = (j << 6) <= tt
    score = jnp.where(forced, POS_BIG, jnp.where(allowed, imp_sel, NEG_BIG))
    score = jnp.where(j >= n_sel, LOWEST, score)
    jf = j.astype(F32)
    sel = jnp.zeros((n_sel_pad, tl), F32)
    for _ in range(topk):
        mx = jnp.max(score, axis=0, keepdims=True)
        first = jnp.min(jnp.where(score == mx, jf, 1e9), axis=0, keepdims=True)
        pick = jf == first
        sel = jnp.where(pick, 1.0, sel)
        score = jnp.where(pick, -jnp.inf, score)
    return jnp.where(allowed, sel, 0.0)


def _nsa_prompt_kernel(q_ref, sl_ref, qts_ref, qtw_ref, kc_ref, vc_ref, ks_ref, vs_ref, kw_ref, vw_ref, at_ref,
                       nbg_ref, gx_ref, o_ref, m_ref, acc_ref, br_ref, *, tq, n_sel, topk):
    rows = NSA_GROUP * tq
    hd = NSA_HEAD_DIM
    qi = pl.program_id(2)
    t0 = qi * tq
    q128 = q_ref[0, 0].reshape(rows, LANES)
    slope = sl_ref[0][:, :1]
    nbp = kc_ref.shape[2]

    s = _dot_nt(q128[:, :hd], kc_ref[0, 0])
    tok = t0 + (_iota((rows, nbp), 0) & (tq - 1))
    dist = tok - (_iota((rows, nbp), 1) * CMP_STRIDE + (CMP_BLOCK - 1))
    valid = dist >= 0
    s = jnp.where(valid, s - slope * dist.astype(F32), NEG_BIG)
    p = jnp.where(valid, jnp.exp(s - jnp.max(s, axis=-1, keepdims=True)), 0.0)
    p = p / jnp.maximum(jnp.sum(p, axis=-1, keepdims=True), 1e-30)
    o_cmp = _dot(p.astype(MXU_DT), vc_ref[0, 0])

    def normalised(acc):
        return acc[:, :hd] / jnp.maximum(acc[:, hd:], 1e-30)

    lane = _iota((rows, LANES), 1)
    qw = jnp.where(lane < hd, q128, qtw_ref[0])
    span = WINDOW + tq
    ws = pl.multiple_of(jnp.maximum(t0 - WINDOW, 0), tq)
    sw = _dot_nt(qw, kw_ref[0, 0, pl.ds(ws, span), :])
    d = (_iota((rows, span), 0) & (tq - 1)) - _iota((rows, span), 1) + (t0 - ws)
    sw = jnp.where((d >= 0) & (d < WINDOW), sw, NEG_BIG)
    pw = jnp.exp(sw - jnp.max(sw, axis=-1, keepdims=True))
    o_win = normalised(_dot(pw.astype(MXU_DT), vw_ref[0, 0, pl.ds(ws, span), :]))

    sel = _topk_block_mask(p, at_ref, t0, tq=tq, tl=tq, n_sel=n_sel, topk=topk)
    sel_t = jnp.concatenate([jnp.zeros_like(sel), sel], axis=0).astype(MXU_DT)
    spread = ((_iota((rows, tq), 0) & (tq - 1)) == _iota((rows, tq), 1)).astype(MXU_DT)
    selb = _dot_nt(spread, sel_t)
    neg = jnp.where((lane >= hd) & (selb < 0.5), NEG_BIG, 0.0)
    qa = jnp.concatenate([(q128.astype(F32) + neg).astype(MXU_DT), qts_ref[0]], axis=1)

    tk = PROMPT_SEL_TK
    kd = t0 // tk
    k0 = pl.multiple_of(kd * tk, tk)
    s = _dot_nt(qa, ks_ref[0, 0, pl.ds(k0, tk), :])
    causal = ((_iota((rows, tk), 0) & (tq - 1)) - _iota((rows, tk), 1) + (t0 - k0)) >= 0
    s = jnp.where(causal, s, NEG_BIG)
    m = jnp.max(s, axis=-1, keepdims=True)
    m_ref[...] = jnp.broadcast_to(m, m_ref.shape)
    acc_ref[...] = _dot(jnp.exp(s - m).astype(MXU_DT), vs_ref[0, 0, pl.ds(k0, tk), :])

    def sel_body(kt, carry):
        ka = pl.multiple_of(kt * tk, tk)
        s = _dot_nt(qa, ks_ref[0, 0, pl.ds(ka, tk), :])
        m_prev = m_ref[...]
        m_new = jnp.maximum(m_prev, jnp.max(s, axis=-1, keepdims=True))
        p = jnp.exp(s - jnp.concatenate([m_new] * (tk // LANES), axis=1))
        acc_ref[...] = jnp.exp(m_prev - m_new) * acc_ref[...] + _dot(p.astype(MXU_DT), vs_ref[0, 0, pl.ds(ka, tk), :])
        m_ref[...] = m_new
        return carry

    lax.fori_loop(0, kd, sel_body, 0)
    o_sel = normalised(acc_ref[...])

    _gate_and_store(o_ref, br_ref, nbg_ref, gx_ref, (o_cmp, o_sel, o_win), tq)


def _slope_digits(tq, tile, lane0):
    pieces, r = [], _alibi_slopes()
    for _ in range(3):
        p = r.astype(MXU_DT).astype(np.float32)
        pieces.append(p)
        r = r - p
    pieces = np.stack(pieces, axis=1)
    tab = np.zeros((NSA_HEADS, LANES), np.float32)
    tab[:, lane0:lane0 + 6] = np.concatenate([pieces * float(tile), pieces], axis=1)
    tab = np.broadcast_to(tab.reshape(NSA_KV_HEADS, NSA_GROUP, 1, LANES), (NSA_KV_HEADS, NSA_GROUP, tq, LANES))
    return tab.reshape(NSA_KV_HEADS, NSA_GROUP * tq, LANES).astype(MXU_DT)


def _key_digits(t, tile, lane0, width, onehot):
    r = np.arange(t)
    lane = np.arange(width)[None, :]
    tab = np.zeros((t, width), np.float32)
    if onehot:
        tab = np.where(lane - NSA_HEAD_DIM == (r // SEL_BLOCK)[:, None], 1.0, tab)
    hi = (r // tile).astype(np.float32)[:, None]
    lo = (r % tile).astype(np.float32)[:, None]
    tab = np.where((lane >= lane0) & (lane < lane0 + 3), hi, tab)
    tab = np.where((lane >= lane0 + 3) & (lane < lane0 + 6), lo, tab)
    return tab.astype(MXU_DT)


def nsa_attention_prompt(q128, slopes, kc, vc, ks_aug, vs, kw_aug, vw, h3, nbg_col, gate_expand, *, n_sel):
    bsz, _, _, t, _ = q128.shape
    tq = PROMPT_TQ
    hd = NSA_HEAD_DIM
    nbp = kc.shape[2]
    assert n_sel <= 64 and t % PROMPT_SEL_TK == 0 and t >= WINDOW + tq and PROMPT_SEL_TK % tq == 0
    rows = NSA_GROUP * tq
    at = _imp_to_sel_matrix(64, nbp)
    kern = functools.partial(_nsa_prompt_kernel, tq=tq, n_sel=n_sel, topk=min(NSA_TOPK, n_sel))
    full = lambda n, w: pl.BlockSpec((1, 1, n, w), lambda b, kh, i: (b, kh, 0, 0))
    per_head = pl.BlockSpec((1, rows, LANES), lambda b, kh, i: (kh, 0, 0))
    gw = NSA_GROUP * hd
    return pl.pallas_call(
        kern,
        grid=(bsz, NSA_KV_HEADS, t // tq),
        in_specs=[pl.BlockSpec((1, 1, NSA_GROUP, tq, LANES), lambda b, kh, i: (b, kh, 0, i, 0)),
                  per_head, per_head, per_head,
                  full(nbp, hd), full(nbp, hd), full(t, 2 * LANES), full(t, LANES), full(t, LANES), full(t, LANES),
                  pl.BlockSpec((64, nbp), lambda b, kh, i: (0, 0)),
                  pl.BlockSpec((1, tq, LANES), lambda b, kh, i: (b, i, nbg_col)),
                  pl.BlockSpec((3, LANES, gw), lambda b, kh, i: (0, 0, kh))],
        out_specs=pl.BlockSpec((1, tq, gw), lambda b, kh, i: (b, i, kh)),
        out_shape=jax.ShapeDtypeStruct((bsz, t, NSA_WIDTH), MXU_DT),
        scratch_shapes=[pltpu.VMEM((rows, LANES), F32), pltpu.VMEM((rows, LANES), F32),
                        pltpu.VMEM((3, tq, gw), F32)],
        compiler_params=_cparams("parallel", "parallel", "arbitrary"),
        name="nsa_attention_prompt",
    )(q128, slopes, _slope_digits(tq, SEL_TK, 0), _slope_digits(tq, WIN_TK, hd),
      kc, vc, ks_aug, vs, kw_aug, vw, at, h3, gate_expand)


def _nsa_paged_pre_kernel(q_ref, sl_ref, kc_ref, vc_ref, kw_ref, vw_ref, at_ref,
                          ocmp_ref, owin_ref, sel_ref, *, tq, qpos0, wpos0, n_sel, topk):
    rows = NSA_GROUP * tq
    hd = NSA_HEAD_DIM
    q = q_ref[0, 0].reshape(rows, hd)
    slope = sl_ref[0][:, :1]
    nbp = kc_ref.shape[2]
    tw = kw_ref.shape[2]

    s = _dot_nt(q, kc_ref[0, 0])
    tok = qpos0 + (_iota((rows, nbp), 0) & (tq - 1))
    dist = tok - (_iota((rows, nbp), 1) * CMP_STRIDE + (CMP_BLOCK - 1))
    valid = dist >= 0
    s = jnp.where(valid, s - slope * dist.astype(F32), NEG_BIG)
    p = jnp.where(valid, jnp.exp(s - jnp.max(s, axis=-1, keepdims=True)), 0.0)
    p = p / jnp.maximum(jnp.sum(p, axis=-1, keepdims=True), 1e-30)
    o_cmp = _dot(p.astype(MXU_DT), vc_ref[0, 0])
    sel_ref[0, 0] = _topk_block_mask(p, at_ref, qpos0, tq=tq, tl=sel_ref.shape[3], n_sel=n_sel, topk=topk)

    sw = _dot_nt(q, kw_ref[0, 0])
    d = (qpos0 - wpos0) + (_iota((rows, tw), 0) & (tq - 1)) - _iota((rows, tw), 1)
    valid = (d >= 0) & (d < WINDOW)
    sw = jnp.where(valid, sw - slope * d.astype(F32), NEG_BIG)
    pw = jnp.where(valid, jnp.exp(sw - jnp.max(sw, axis=-1, keepdims=True)), 0.0)
    o_win = _dot(pw.astype(MXU_DT), vw_ref[0, 0]) / jnp.maximum(jnp.sum(pw, axis=-1, keepdims=True), 1e-30)

    for g in range(NSA_GROUP):
        ocmp_ref[0, :, g * hd:(g + 1) * hd] = o_cmp[g * tq:(g + 1) * tq]
        owin_ref[0, :, g * hd:(g + 1) * hd] = o_win[g * tq:(g + 1) * tq]


def _nsa_paged_sel_kernel(*refs, n_in, tq, qpos0, n_tiles):
    hit_ref = refs[1]
    refs = refs[2:]
    q_ref, sl_ref, sel_ref = refs[:3]
    pages = refs[3:3 + n_in]
    (ktn_ref, vtn_ref, gain_ref, ocmp_ref, owin_ref, nbg_ref, gx_ref,
     o_ref, m_ref, l_ref, acc_ref, kt_ref, vt_ref, br_ref) = refs[3 + n_in:]
    rows = NSA_GROUP * tq
    hd = NSA_HEAD_DIM
    tk = n_in * PAGE_SIZE
    nblk = tk // SEL_BLOCK
    i = pl.program_id(1)

    @pl.when(i == 0)
    def _():
        m_ref[...] = jnp.full(m_ref.shape, NEG_BIG, F32)
        l_ref[...] = jnp.zeros(l_ref.shape, F32)
        acc_ref[...] = jnp.zeros(acc_ref.shape, F32)

    @pl.when(i < n_tiles)
    def _():
        for u in range(n_in):
            sl = slice(u * PAGE_SIZE, (u + 1) * PAGE_SIZE)
            kt_ref[:, sl] = _head_rms_t(pages[u][0, 0], gain_ref[...]).astype(kt_ref.dtype)
            vt_ref[:, sl] = pages[u][0, 1].astype(vt_ref.dtype)

    @pl.when(i == n_tiles)
    def _():
        kt_ref[...] = ktn_ref[0]
        vt_ref[...] = vtn_ref[0]

    k0 = i * tk
    d = (qpos0 - k0) + (_iota((rows, tk), 0) & (tq - 1)) - _iota((rows, tk), 1)
    causal = d >= 0
    dist = d.astype(F32)
    spread = ((_iota((rows, LANES), 0) & (tq - 1)) == _iota((rows, LANES), 1)).astype(MXU_DT)
    in_tile = (_iota((nblk, tk), 0) == (_iota((nblk, tk), 1) >> 6)).astype(MXU_DT)
    blk0 = pl.multiple_of(i * nblk, nblk)
    reps = tk // LANES
    def head_update(kvh):
        hs = slice(kvh * hd, (kvh + 1) * hd)
        q = q_ref[0, kvh].reshape(rows, hd)
        s = _dot(q, kt_ref[hs, :])
        mine = _dot_nt(spread, sel_ref[0, kvh, pl.ds(blk0, nblk), :].astype(MXU_DT)).astype(MXU_DT)
        valid = (_dot(mine, in_tile) > 0.5) & causal
        s = jnp.where(valid, s - sl_ref[kvh][:, :1] * dist, NEG_BIG)
        m_prev = m_ref[kvh]
        m_new = jnp.maximum(m_prev, jnp.max(s, axis=-1, keepdims=True))
        alpha = jnp.exp(m_prev - m_new)
        p = jnp.where(valid, jnp.exp(s - jnp.concatenate([m_new] * reps, axis=1)), 0.0)
        l_ref[kvh] = alpha * l_ref[kvh] + jnp.sum(p, axis=-1, keepdims=True)
        acc_ref[kvh] = alpha[:, :hd] * acc_ref[kvh] + _dot_nt(p.astype(MXU_DT), vt_ref[hs, :])
        m_ref[kvh] = m_new

    for kvh in range(NSA_KV_HEADS):
        pl.when(hit_ref[pl.program_id(0), kvh * (n_tiles + 1) + i] > 0)(functools.partial(head_update, kvh))

    @pl.when(i == n_tiles)
    def _():
        br_ref[0] = ocmp_ref[0]
        br_ref[2] = owin_ref[0]
        for kvh in range(NSA_KV_HEADS):
            o_sel = acc_ref[kvh] / jnp.maximum(l_ref[kvh][:, :hd], 1e-30)
            for g in range(NSA_GROUP):
                c0 = (kvh * NSA_GROUP + g) * hd
                br_ref[1, :, c0:c0 + hd] = o_sel[g * tq:(g + 1) * tq]
        pieces = _split(_sigmoid(nbg_ref[0]), 2)
        acc = None
        for br in range(3):
            term = sum(_dot(p, gx_ref[br]) for p in pieces) * br_ref[br]
            acc = term if acc is None else acc + term
        o_ref[0] = acc.astype(o_ref.dtype)


def nsa_attention_paged(qn, slopes, kc, vc, kw, vw, pool_t, pages, kt_new, vt_new, gain_sel, h3, nbg_col,
                        gate_expand, *, tq, t_real, qpos0, wpos0, n_sel):
    bsz, _, _, tql, hd = qn.shape
    assert tql == tq
    nbp, tw = kc.shape[2], kw.shape[2]
    n_pages = pages.shape[1]
    tk = kt_new.shape[2]
    n_in = tk // PAGE_SIZE
    n_tiles = n_pages // n_in
    assert n_pages % n_in == 0 and qpos0 == n_pages * PAGE_SIZE
    n_sel_pad = -(-max(n_sel, (n_tiles + 1) * tk // SEL_BLOCK) // 64) * 64
    rows = NSA_GROUP * tq
    gw = NSA_GROUP * hd
    at = _imp_to_sel_matrix(n_sel_pad, nbp)
    full = lambda n: pl.BlockSpec((1, 1, n, hd), lambda b, kh: (b, kh, 0, 0))
    obr = pl.BlockSpec((1, tq, gw), lambda b, kh: (b, 0, kh))
    o_cmp, o_win, sel = pl.pallas_call(
        functools.partial(_nsa_paged_pre_kernel, tq=tq, qpos0=qpos0, wpos0=wpos0, n_sel=n_sel,
                          topk=min(NSA_TOPK, n_sel)),
        grid=(bsz, NSA_KV_HEADS),
        in_specs=[pl.BlockSpec((1, 1, NSA_GROUP, tq, hd), lambda b, kh: (b, kh, 0, 0, 0)),
                  pl.BlockSpec((1, rows, LANES), lambda b, kh: (kh, 0, 0)),
                  full(nbp), full(nbp), full(tw), full(tw),
                  pl.BlockSpec((n_sel_pad, nbp), lambda b, kh: (0, 0))],
        out_specs=[obr, obr, pl.BlockSpec((1, 1, n_sel_pad, LANES), lambda b, kh: (b, kh, 0, 0))],
        out_shape=[jax.ShapeDtypeStruct((bsz, tq, NSA_WIDTH), F32), jax.ShapeDtypeStruct((bsz, tq, NSA_WIDTH), F32),
                   jax.ShapeDtypeStruct((bsz, NSA_KV_HEADS, n_sel_pad, LANES), F32)],
        compiler_params=_cparams("parallel", "parallel"),
        name="nsa_paged_pre",
    )(qn, slopes, kc, vc, kw, vw, at)

    nblk = tk // SEL_BLOCK
    hit = sel[:, :, :(n_tiles + 1) * nblk, :t_real].reshape(bsz, NSA_KV_HEADS, n_tiles + 1, nblk * t_real)
    hit = (jnp.max(hit, axis=-1) > 0).astype(jnp.int32).reshape(bsz, NSA_KV_HEADS * (n_tiles + 1))

    page_spec = lambda u: pl.BlockSpec(
        (1, 2, NSA_KV_WIDTH, PAGE_SIZE),
        lambda b, i, pt, ht: (pt[b, jnp.minimum(i, n_tiles - 1) * n_in + u], 1, 0, 0))
    const = lambda *shape: pl.BlockSpec(shape, lambda b, i, pt, ht: (0,) * len(shape))
    per_seq = lambda *shape: pl.BlockSpec((1,) + shape, lambda b, i, pt, ht: (b,) + (0,) * len(shape))
    return pl.pallas_call(
        functools.partial(_nsa_paged_sel_kernel, n_in=n_in, tq=tq, qpos0=qpos0, n_tiles=n_tiles),
        grid_spec=pltpu.PrefetchScalarGridSpec(
            num_scalar_prefetch=2,
            grid=(bsz, n_tiles + 1),
            in_specs=[per_seq(NSA_KV_HEADS, NSA_GROUP, tq, hd), const(NSA_KV_HEADS, rows, LANES),
                      per_seq(NSA_KV_HEADS, n_sel_pad, LANES)]
                     + [page_spec(u) for u in range(n_in)]
                     + [per_seq(NSA_KV_WIDTH, tk), per_seq(NSA_KV_WIDTH, tk), const(NSA_KV_WIDTH, LANES),
                        per_seq(tq, NSA_WIDTH), per_seq(tq, NSA_WIDTH),
                        pl.BlockSpec((1, tq, LANES), lambda b, i, pt, ht: (b, 0, nbg_col)),
                        const(3, LANES, NSA_WIDTH)],
            out_specs=per_seq(tq, NSA_WIDTH),
            scratch_shapes=[pltpu.VMEM((NSA_KV_HEADS, rows, LANES), F32), pltpu.VMEM((NSA_KV_HEADS, rows, LANES), F32),
                            pltpu.VMEM((NSA_KV_HEADS, rows, hd), F32),
                            pltpu.VMEM((NSA_KV_WIDTH, tk), MXU_DT), pltpu.VMEM((NSA_KV_WIDTH, tk), MXU_DT),
                            pltpu.VMEM((3, tq, NSA_WIDTH), F32)]),
        out_shape=jax.ShapeDtypeStruct((bsz, tq, NSA_WIDTH), MXU_DT),
        compiler_params=_cparams("parallel", "arbitrary"),
        name="nsa_paged_sel",
    )(pages, hit, qn, slopes, sel, *([pool_t] * n_in), kt_new, vt_new, _gain_t(gain_sel),
      o_cmp, o_win, h3, gate_expand)


def _layout(d_model):
    tn = IN_PROJ_TN
    hw, rw = HG_HEADS * HG_DK, RET_HEADS * RET_DK
    gates = 3 * NSA_HEADS
    lay, off = {}, 0
    for name, width in [("hq", hw), ("hf", hw), ("hi", hw), ("hg", hw), ("nq", NSA_WIDTH),
                        ("nkv", 4 * NSA_KV_WIDTH), ("nwkv", 2 * NSA_KV_WIDTH), ("nbg", gates)]:
        lay[name] = off
        off += width
    lay["post_row"] = off
    lay["post_col"] = off = -(-off // tn) * tn
    for name, width in [("rq", rw), ("rk", rw), ("rv", rw), ("rg", rw), ("mg", 3 * d_model)]:
        lay[name] = off
        off += width
    assert lay["nbg"] % LANES == 0 and off % tn == 0 and lay["post_row"] % 16 == 0
    lay["used"] = lay["post_row"] + off - lay["post_col"]
    lay["total"] = off
    return lay


def _alibi_slopes():
    hh = np.arange(1, NSA_HEADS + 1, dtype=np.float32)
    return np.exp2(np.float32(-8.0) * hh / np.float32(NSA_HEADS)).astype(np.float32)


def _block_sum_matrix(width):
    i = np.arange(width)
    return (i[:, None] // NSA_HEAD_DIM == i[None, :] // NSA_HEAD_DIM).astype(np.float32).astype(MXU_DT)


def _gate_expand_matrix():
    c = np.arange(LANES)[:, None]
    col = np.arange(NSA_WIDTH)[None, :]
    return np.stack([(c == (col // NSA_HEAD_DIM) * 3 + br) for br in range(3)]).astype(np.float32).astype(MXU_DT)


def _slope_rows(tq):
    rows = np.repeat(_alibi_slopes().reshape(NSA_KV_HEADS, NSA_GROUP), tq, axis=1)
    return np.ascontiguousarray(np.broadcast_to(rows[:, :, None], (NSA_KV_HEADS, NSA_GROUP * tq, LANES)))


def _pad_rows(x, n):
    return jnp.pad(x, ((0, 0), (0, n - x.shape[1])) + ((0, 0),) * (x.ndim - 2))


def _trunk_layer(x, lw, lay, consts, nsa_fn, hg_state, ret_state, *, seq_tiles):
    bsz, t, d = x.shape
    tm, tt_h, tt_r, t_valid = seq_tiles
    x2 = x.reshape(bsz * t, d)
    wts, layer = lw["weights"], lw["layer"]
    h = rms_matmul(x2, lw["norm_attn"], wts["w_in"], layer, lay, tm=tm, tn=IN_PROJ_TN)
    np_ = h.shape[1]
    h3 = h.reshape(bsz, t, np_)
    tp = -(-t // tt_h) * tt_h
    h3p = _pad_rows(h3, tp) if tp != t else h3
    oa, hg_new = hgrn_mixer(h3p, lay, lw["lb"], lw["hg_out_norm"], hg_state, tt=tt_h,
                            t_valid=None if tp == t else t_valid)
    tpr = -(-t // tt_r) * tt_r
    h3r = _pad_rows(h3, tpr) if tpr != t else h3
    oc, ret_new = retention_mixer(h3r, lay, consts["lg_tab"], ret_state, tt=tt_r,
                                  n_valid=tt_r if tpr == t else t_valid)
    ob = nsa_fn(h3)
    flat = lambda a: a[:, :t].reshape(bsz * t, a.shape[-1])
    merged = merge_branches(h, lay, flat(oa), flat(ob), flat(oc),
                            wts["w_branch_hg"], wts["w_branch_nsa"], wts["w_branch_ret"], layer, tm=tm, tn=512)
    x2 = matmul_res(merged, wts["w_out"], layer, x2, tm=tm, tn=512)
    hid = ffn_up(x2, lw["norm_ffn"], wts["w_gate"], wts["w_up"], layer, tm=tm, tn=512)
    x2 = matmul_res(hid, wts["w_down"], layer, x2, tm=tm, tn=512)
    return x2.reshape(bsz, t, d), h3, hg_new, ret_new


def kernel(x_prompt, x_sample, cache_nsa, cache_win, state_hgrn, state_ret, page_table,
           norm_attn, w_in, hgrn_lb_logits, hgrn_out_norm, nsa_q_norm, nsa_k_norm, nsa_cmp_w,
           w_branch_hg, w_branch_nsa, w_branch_ret, w_out, norm_ffn, w_gate, w_up, w_down):
    depth = w_in.shape[0]
    bp, tp, d = x_prompt.shape
    bs, ts, _ = x_sample.shape
    n_pool = cache_nsa.shape[1]
    n_pages = page_table.shape[1]
    past = n_pages * PAGE_SIZE
    wbuf = cache_win.shape[2]
    lay = _layout(d)

    sm = jax.nn.softmax(hgrn_lb_logits.astype(F32), axis=0)
    lower_bounds = jnp.clip(jnp.cumsum(sm, axis=0) - sm[0:1], 0.0, 1.0 - 1e-6)

    tt_p = 256 if tp % 256 == 0 else tp
    lg = np.log1p(-np.exp2(np.float32(-5.0) - np.arange(RET_HEADS, dtype=np.float32))).astype(np.float32)
    consts = {
        "gate_expand": _gate_expand_matrix(),
        "lg_tab": np.ascontiguousarray(np.broadcast_to(lg[:, None, None], (RET_HEADS, 1, max(tt_p, LANES)))),
    }
    bsum = _block_sum_matrix(NSA_KV_WIDTH)
    pool_t = jnp.transpose(cache_nsa, (0, 1, 3, 4, 5, 2)).reshape(depth * n_pool, 4, NSA_KV_WIDTH, PAGE_SIZE)
    win3 = cache_win.reshape(depth, bs, wbuf, 2 * NSA_KV_WIDTH)

    ts_pad = 16
    tq_s = 16
    n_sel_p = -(-tp // SEL_BLOCK)
    n_sel_s = past // SEL_BLOCK + -(-ts // SEL_BLOCK)
    tk_tail = 2048 if past % 2048 == 0 else 512
    tq_p = 128 if tp % 128 == 0 else tp

    weights = {
        "w_in": jnp.swapaxes(w_in, 1, 2).astype(MXU_DT).reshape(depth * w_in.shape[2], d),
        "w_branch_hg": w_branch_hg.astype(MXU_DT), "w_branch_nsa": w_branch_nsa.astype(MXU_DT),
        "w_branch_ret": w_branch_ret.astype(MXU_DT), "w_out": w_out.astype(MXU_DT),
        "w_gate": w_gate.astype(MXU_DT), "w_up": w_up.astype(MXU_DT), "w_down": w_down.astype(MXU_DT),
    }
    assert w_in.shape[2] == lay["used"]

    xp, xs = x_prompt, x_sample
    outs = {k: [] for k in ("kv_p", "kv_s", "win_p", "win_s", "hg_p", "hg_s", "ret_p", "ret_s")}
    for l in range(depth):
        lw = {
            "weights": weights, "layer": l,
            "norm_attn": norm_attn[l], "lb": lower_bounds[l].reshape(1, -1),
            "hg_out_norm": hgrn_out_norm[l], "norm_ffn": norm_ffn[l],
        }
        gq, gk, cw = nsa_q_norm[l], nsa_k_norm[l], nsa_cmp_w[l]
        col_cmp = lay["nkv"] // (2 * NSA_KV_WIDTH)
        col_sel = col_cmp + 1
        col_win = lay["nwkv"] // (2 * NSA_KV_WIDTH)
        nbg_col = lay["nbg"] // LANES

        def nsa_prompt(h3):
            tr = 512 if tp % 512 == 0 else tp
            zeros16 = jnp.zeros((bp, CMP_STRIDE, 2 * NSA_KV_WIDTH), F32)
            kc, vc = cmp_prep(h3, col_cmp, zeros16, cw, gk[0], bsum, tr=tr)
            if tp % PROMPT_SEL_TK == 0 and n_sel_p <= 64 and tp >= WINDOW + PROMPT_TQ:
                qn = q_prep(h3, lay, gq, bsum, tr=tr, width=LANES)
                ks, vs = kv_prep(h3, col_sel, gk[1], bsum, tr=tr,
                                 aug=_key_digits(tp, SEL_TK, LANES, 2 * LANES, True))
                kw, vw = kv_prep(h3, col_win, gk[2], bsum, tr=tr,
                                 aug=_key_digits(tp, WIN_TK, NSA_HEAD_DIM, LANES, False))
                return nsa_attention_prompt(qn, _slope_rows(PROMPT_TQ), kc, vc, ks, vs, kw, vw,
                                            h3, nbg_col, consts["gate_expand"], n_sel=n_sel_p)
            qn = q_prep(h3, lay, gq, bsum, tr=tr)
            ks, vs = kv_prep(h3, col_sel, gk[1], bsum, tr=tr)
            kw, vw = kv_prep(h3, col_win, gk[2], bsum, tr=tr)
            return nsa_attention(qn, _slope_rows(tq_p), kc, vc, ks, vs, kw, vw, h3, nbg_col, consts["gate_expand"],
                                 tq=tq_p, sel_tk=SEL_TK, qpos0=0, wpos0=0, n_sel=n_sel_p)

        def nsa_sample(h3):
            pages = page_table + l * n_pool
            h16 = _pad_rows(h3, ts_pad)
            qn = q_prep(h16, lay, gq, bsum, tr=ts_pad)
            w2 = 2 * NSA_KV_WIDTH
            tail16 = h16[:, :, lay["nkv"]:lay["nkv"] + w2].reshape(bs, ts_pad, 2, NSA_KV_WIDTH)
            tail_t = jnp.pad(jnp.transpose(tail16, (0, 2, 3, 1)), ((0, 0), (0, 0), (0, 0), (0, PAGE_SIZE - ts_pad)))
            kc, vc = cmp_prep_paged(pool_t, pages, tail_t, cw, gk[0])
            htail = _pad_rows(h3[:, :, col_sel * w2:(col_sel + 1) * w2], tk_tail)
            ks_tail, vs_tail = kv_prep(htail, 0, gk[1], bsum, tr=tk_tail)
            feature_major = lambda a: jnp.swapaxes(a, 2, 3).reshape(bs, NSA_KV_WIDTH, tk_tail)
            kw_old, vw_old = kv_prep(win3[l], 0, gk[2], bsum, tr=wbuf)
            hw_new = _pad_rows(h3[:, :, col_win * w2:(col_win + 1) * w2], WIN_TK)
            kw_new, vw_new = kv_prep(hw_new, 0, gk[2], bsum, tr=WIN_TK)
            kw = jnp.concatenate([kw_old, kw_new], axis=2)
            vw = jnp.concatenate([vw_old, vw_new], axis=2)
            return nsa_attention_paged(qn, _slope_rows(tq_s), kc, vc, kw, vw, pool_t, pages,
                                       feature_major(ks_tail), feature_major(vs_tail), gk[1], h16, nbg_col,
                                       consts["gate_expand"], tq=tq_s, t_real=ts, qpos0=past, wpos0=past - wbuf,
                                       n_sel=n_sel_s)

        zeros_state = jnp.zeros((bp, HG_HEADS, HG_DK, HG_DV), F32)
        tm_p = 1024 if (bp * tp) % 1024 == 0 else bp * tp
        xp, h3p, hgp, rtp = _trunk_layer(xp, lw, lay, consts, nsa_prompt, zeros_state, zeros_state,
                                         seq_tiles=(tm_p, tt_p, tt_p, tp))
        xs, h3s, hgs, rts = _trunk_layer(xs, lw, lay, consts, nsa_sample, state_hgrn[l], state_ret[l],
                                         seq_tiles=(bs * ts, ts_pad, ts_pad, ts))

        kv_cols = slice(lay["nkv"], lay["nkv"] + 4 * NSA_KV_WIDTH)
        win_cols = slice(lay["nwkv"], lay["nwkv"] + 2 * NSA_KV_WIDTH)
        outs["kv_p"].append(h3p[:, :, kv_cols].reshape(bp, tp, 4, NSA_KV_HEADS, NSA_HEAD_DIM))
        outs["kv_s"].append(h3s[:, :, kv_cols].reshape(bs, ts, 4, NSA_KV_HEADS, NSA_HEAD_DIM))
        wlen = min(WINDOW, tp)
        outs["win_p"].append(h3p[:, tp - wlen:, win_cols].reshape(bp, wlen, 2, NSA_KV_HEADS, NSA_HEAD_DIM))
        ctx = jnp.concatenate([win3[l], h3s[:, :, win_cols]], axis=1)[:, ts:]
        outs["win_s"].append(ctx.reshape(bs, wbuf, 2, NSA_KV_HEADS, NSA_HEAD_DIM))
        outs["hg_p"].append(hgp); outs["hg_s"].append(hgs)
        outs["ret_p"].append(rtp); outs["ret_s"].append(rts)

    st = lambda k: jnp.stack(outs[k])
    return (xp, xs, st("kv_p"), st("kv_s"), st("win_p"), st("win_s"),
            st("hg_p"), st("hg_s"), st("ret_p"), st("ret_s"))
```

```python
import functools

import numpy as np

import jax
import jax.numpy as jnp
from jax import lax
from jax.experimental import pallas as pl
from jax.experimental.pallas import tpu as pltpu

F32 = jnp.float32
MXU_DT = jnp.bfloat16

HG_HEADS, HG_DK, HG_DV = 8, 128, 128
NSA_HEADS, NSA_KV_HEADS, NSA_GROUP, NSA_HEAD_DIM = 16, 4, 4, 64
NSA_WIDTH = NSA_HEADS * NSA_HEAD_DIM
NSA_KV_WIDTH = NSA_KV_HEADS * NSA_HEAD_DIM
CMP_BLOCK, CMP_STRIDE, SEL_BLOCK, NSA_TOPK, WINDOW = 32, 16, 64, 16, 512
RET_HEADS, RET_DK, RET_DV = 8, 128, 128
PAGE_SIZE = 128
EPS = 1e-6
NEG_BIG = -1e30
POS_BIG = 1e30
MIN_F = 1e-20
LOWEST = -3.0e38

LANES = 128
VMEM_LIMIT_BYTES = 56 * 1024 * 1024

HG_SUB_SHIFT = 4
HG_SUB = 1 << HG_SUB_SHIFT
PAGES_PER_STEP = 16
IN_PROJ_TN = 1024
SEL_DIGIT = 256
PROMPT_SEL_TK = 512
PROMPT_TQ = 256
RET_HPS = 2
WIN_DIGIT = 128
WIN_PAD = 128


def _cparams(*sem):
    return pltpu.CompilerParams(dimension_semantics=sem, vmem_limit_bytes=VMEM_LIMIT_BYTES)


def _dot(a, b):
    return jnp.dot(a, b, preferred_element_type=F32)


def _dot_nt(a, b):
    return lax.dot_general(a, b, (((1,), (1,)), ((), ())), preferred_element_type=F32)


def _split(x, n):
    out = []
    r = x
    for _ in range(n):
        p = r.astype(MXU_DT)
        out.append(p)
        r = r - p.astype(F32)
    return out


def _sigmoid(x):
    return 1.0 / (1.0 + jnp.exp(-x))


def _silu(x):
    return x * _sigmoid(x)


def _iota(shape, dim):
    return lax.broadcasted_iota(jnp.int32, shape, dim)


def _eye(n):
    return (_iota((n, n), 0) == _iota((n, n), 1)).astype(MXU_DT)


def _transpose_exact(x):
    eye = _eye(x.shape[1])
    return sum(_dot_nt(eye, p) for p in _split(x, 3))


def _head_rms(x, bsum, gain):
    ssq = sum(_dot(p, bsum) for p in _split(x * x, 2))
    return x * lax.rsqrt(ssq * (1.0 / NSA_HEAD_DIM) + EPS) * gain


def _rms_matmul_kernel(x_ref, g_ref, w_ref, o_ref, xn_ref):
    @pl.when(pl.program_id(1) == 0)
    def _():
        x = x_ref[...]
        ms = jnp.mean(x * x, axis=-1, keepdims=True)
        xn_ref[...] = (x * lax.rsqrt(ms + EPS) * g_ref[...]).astype(xn_ref.dtype)

    o_ref[...] = _dot_nt(xn_ref[...], w_ref[...])


def rms_matmul(x, gain, wt, layer, lay, *, tm, tn):
    m, d = x.shape
    assert tn == IN_PROJ_TN and lay["used"] % 16 == 0
    nb_pre = lay["post_col"] // tn
    nb_all = lay["total"] // tn
    base = layer * lay["used"]

    def w_row(i, j):
        row = base + jnp.where(j < nb_pre, j * tn, lay["post_row"] + (j - nb_pre) * tn)
        return (pl.multiple_of(row, 16), 0)

    return pl.pallas_call(
        _rms_matmul_kernel,
        grid=(m // tm, nb_all),
        in_specs=[pl.BlockSpec((tm, d), lambda i, j: (i, 0)),
                  pl.BlockSpec((1, d), lambda i, j: (0, 0)),
                  pl.BlockSpec((pl.Element(tn), pl.Element(d)), w_row)],
        out_specs=pl.BlockSpec((tm, tn), lambda i, j: (i, j)),
        out_shape=jax.ShapeDtypeStruct((m, lay["total"]), F32),
        scratch_shapes=[pltpu.VMEM((tm, d), MXU_DT)],
        compiler_params=_cparams("parallel", "arbitrary"),
        name="in_proj",
    )(x, gain.reshape(1, d), wt)


def _ffn_up_kernel(x_ref, g_ref, wg_ref, wu_ref, o_ref, xn_ref):
    @pl.when(pl.program_id(1) == 0)
    def _():
        x = x_ref[...]
        ms = jnp.mean(x * x, axis=-1, keepdims=True)
        xn_ref[...] = (x * lax.rsqrt(ms + EPS) * g_ref[...]).astype(xn_ref.dtype)

    xn = xn_ref[...]
    o_ref[...] = (_silu(_dot(xn, wg_ref[...])) * _dot(xn, wu_ref[...])).astype(o_ref.dtype)


def _layer_cols(k, tn, layer):
    return pl.BlockSpec((None, k, tn), lambda i, j: (layer, 0, j))


def ffn_up(x, gain, wg, wu, layer, *, tm, tn):
    m, d = x.shape
    f = wg.shape[2]
    return pl.pallas_call(
        _ffn_up_kernel,
        grid=(m // tm, f // tn),
        in_specs=[pl.BlockSpec((tm, d), lambda i, j: (i, 0)),
                  pl.BlockSpec((1, d), lambda i, j: (0, 0)),
                  _layer_cols(d, tn, layer), _layer_cols(d, tn, layer)],
        out_specs=pl.BlockSpec((tm, tn), lambda i, j: (i, j)),
        out_shape=jax.ShapeDtypeStruct((m, f), MXU_DT),
        scratch_shapes=[pltpu.VMEM((tm, d), MXU_DT)],
        compiler_params=_cparams("parallel", "arbitrary"),
        name="ffn_up",
    )(x, gain.reshape(1, d), wg, wu)


def _matmul_res_kernel(a_ref, w_ref, r_ref, o_ref):
    o_ref[...] = r_ref[...] + _dot(a_ref[...], w_ref[...])


def matmul_res(a, w, layer, res, *, tm, tn):
    m, k = a.shape
    n = w.shape[2]
    return pl.pallas_call(
        _matmul_res_kernel,
        grid=(m // tm, n // tn),
        in_specs=[pl.BlockSpec((tm, k), lambda i, j: (i, 0)),
                  _layer_cols(k, tn, layer),
                  pl.BlockSpec((tm, tn), lambda i, j: (i, j))],
        out_specs=pl.BlockSpec((tm, tn), lambda i, j: (i, j)),
        out_shape=jax.ShapeDtypeStruct((m, n), F32),
        compiler_params=_cparams("parallel", "arbitrary"),
        name="matmul_res",
    )(a, w, res)


def _merge_kernel(oa_ref, ob_ref, oc_ref, ga_ref, gb_ref, gc_ref, wa_ref, wb_ref, wc_ref, o_ref):
    o_ref[...] = (_sigmoid(ga_ref[...]) * _dot(oa_ref[...], wa_ref[...])
                  + _sigmoid(gb_ref[...]) * _dot(ob_ref[...], wb_ref[...])
                  + _sigmoid(gc_ref[...]) * _dot(oc_ref[...], wc_ref[...])).astype(o_ref.dtype)


def merge_branches(h, lay, oa, ob, oc, wa, wb, wc, layer, *, tm, tn):
    m = h.shape[0]
    d = wa.shape[2]
    wdt = oa.shape[1]
    ca, cb, cc = ((lay["mg"] + i * d) // tn for i in range(3))
    row = lambda i, j: (i, 0)
    return pl.pallas_call(
        _merge_kernel,
        grid=(m // tm, d // tn),
        in_specs=[pl.BlockSpec((tm, wdt), row), pl.BlockSpec((tm, wdt), row), pl.BlockSpec((tm, wdt), row),
                  pl.BlockSpec((tm, tn), lambda i, j: (i, ca + j)),
                  pl.BlockSpec((tm, tn), lambda i, j: (i, cb + j)),
                  pl.BlockSpec((tm, tn), lambda i, j: (i, cc + j)),
                  _layer_cols(wdt, tn, layer), _layer_cols(wdt, tn, layer), _layer_cols(wdt, tn, layer)],
        out_specs=pl.BlockSpec((tm, tn), lambda i, j: (i, j)),
        out_shape=jax.ShapeDtypeStruct((m, d), MXU_DT),
        compiler_params=_cparams("parallel", "arbitrary"),
        name="merge",
    )(oa, ob, oc, h, h, h, wa, wb, wc)


def _hgrn_kernel(hq_ref, hf_ref, hi_ref, hg_ref, lb_ref, gain_ref, s0_ref, o_ref, sout_ref,
                 st_ref, ut_ref, *, tt, t_valid):
    c = HG_SUB
    nj = tt // c
    tb = pl.program_id(2)

    @pl.when(tb == 0)
    def _():
        st_ref[...] = _transpose_exact(s0_ref[0, 0])

    hq = hq_ref[0]
    lb = lb_ref[...]
    q = _silu(hq)
    sig = _sigmoid(hf_ref[0])
    g = jnp.log(jnp.maximum(lb + (1.0 - lb) * sig, MIN_F))
    kk = (1.0 - lb) * (1.0 - sig)
    v = hi_ref[0]
    row = _iota((tt, HG_DK), 0)
    if t_valid is not None:
        live = (tb * tt + row) < t_valid
        g = jnp.where(live, g, 0.0)
        kk = jnp.where(live, kk, 0.0)

    r2 = _iota((tt, tt), 0)
    c2 = _iota((tt, tt), 1)
    same = (r2 >> HG_SUB_SHIFT) == (c2 >> HG_SUB_SHIFT)
    gp = _split(g, 3)
    tri = (same & (c2 <= r2)).astype(MXU_DT)
    blk = same.astype(MXU_DT)
    b = sum(_dot(tri, p) for p in gp)
    dtot = sum(_dot(blk, p) for p in gp)
    qe = q * jnp.exp(b)
    ke = kk * jnp.exp(dtot - b)
    edec = jnp.exp(dtot)

    q3 = q.reshape(nj, c, HG_DK)
    kk3 = kk.reshape(nj, c, HG_DK)
    b3 = b.reshape(nj, c, HG_DK)
    v3 = v.reshape(nj, c, HG_DV)
    half = c // 2
    tpos = _iota((nj, half, HG_DK), 1)
    ones = jnp.ones((HG_DK, HG_DV), MXU_DT)

    def contrib(rows, s, first_visible):
        diff = b3[:, rows, :] - b3[:, s:s + 1, :]
        if first_visible is not None:
            m = tpos >= first_visible
            diff = jnp.where(m, diff, 0.0)
        y = q3[:, rows, :] * kk3[:, s:s + 1, :] * jnp.exp(diff)
        if first_visible is not None:
            y = jnp.where(m, y, 0.0)
        z = _dot(y.reshape(nj * half, HG_DK).astype(MXU_DT), ones).reshape(nj, half, HG_DV)
        return z * v3[:, s:s + 1, :]

    lower, upper = slice(0, half), slice(half, c)
    o_lo = jnp.zeros((nj, half, HG_DV), F32)
    o_up = jnp.zeros((nj, half, HG_DV), F32)
    for s in range(c):
        if s < half:
            o_lo = o_lo + contrib(lower, s, s)
            o_up = o_up + contrib(upper, s, None)
        else:
            o_up = o_up + contrib(upper, s, s - half)
    o = jnp.concatenate([o_lo, o_up], axis=1).reshape(tt, HG_DV)

    v_b = v.astype(MXU_DT)
    ke_b = ke.astype(MXU_DT)
    for j in range(nj):
        ut_ref[j] = lax.dot_general(v_b[j * c:(j + 1) * c], ke_b[j * c:(j + 1) * c],
                                    (((0,), (0,)), ((), ())), preferred_element_type=F32)

    qe_b = qe.astype(MXU_DT)
    outs = []
    for j in range(nj):
        st = st_ref[...]
        outs.append(_dot_nt(qe_b[j * c:(j + 1) * c], st.astype(MXU_DT)))
        st_ref[...] = st * edec[j * c:j * c + 1, :] + ut_ref[j]
    o = o + jnp.concatenate(outs, axis=0)

    on = o * lax.rsqrt(jnp.mean(o * o, axis=-1, keepdims=True) + EPS) * gain_ref[...]
    o_ref[0] = (on * _silu(hg_ref[0])).astype(o_ref.dtype)

    @pl.when(tb == pl.num_programs(2) - 1)
    def _():
        sout_ref[0, 0] = _transpose_exact(st_ref[...])


def hgrn_mixer(h3, lay, lb, out_gain, s0, *, tt, t_valid):
    bsz, t, _ = h3.shape
    cq, cf, ci, cg = (lay[k] // LANES for k in ("hq", "hf", "hi", "hg"))
    col = lambda c0: (lambda b, hd, tb: (b, tb, c0 + hd))
    kern = functools.partial(_hgrn_kernel, tt=tt, t_valid=t_valid)
    return pl.pallas_call(
        kern,
        grid=(bsz, HG_HEADS, t // tt),
        in_specs=[pl.BlockSpec((1, tt, LANES), col(cq)), pl.BlockSpec((1, tt, LANES), col(cf)),
                  pl.BlockSpec((1, tt, LANES), col(ci)), pl.BlockSpec((1, tt, LANES), col(cg)),
                  pl.BlockSpec((1, HG_DK), lambda b, hd, tb: (0, hd)),
                  pl.BlockSpec((1, HG_DV), lambda b, hd, tb: (0, 0)),
                  pl.BlockSpec((1, 1, HG_DK, HG_DV), lambda b, hd, tb: (b, hd, 0, 0))],
        out_specs=[pl.BlockSpec((1, tt, HG_DV), lambda b, hd, tb: (b, tb, hd)),
                   pl.BlockSpec((1, 1, HG_DK, HG_DV), lambda b, hd, tb: (b, hd, 0, 0))],
        out_shape=[jax.ShapeDtypeStruct((bsz, t, HG_HEADS * HG_DV), MXU_DT),
                   jax.ShapeDtypeStruct((bsz, HG_HEADS, HG_DK, HG_DV), F32)],
        scratch_shapes=[pltpu.VMEM((HG_DV, HG_DK), F32),
                        pltpu.VMEM((tt // HG_SUB, HG_DV, HG_DK), F32)],
        compiler_params=_cparams("parallel", "parallel", "arbitrary"),
        name="hgrn",
    )(h3, h3, h3, h3, lb, out_gain.reshape(1, HG_DV), s0)


def _ret_kernel(q_ref, k_ref, v_ref, g_ref, lg_ref, s0_ref, o_ref, sout_ref, s_ref, *, tt, n_valid):
    tb = pl.program_id(2)

    @pl.when(tb == 0)
    def _():
        s_ref[...] = s0_ref[0]

    rel = (_iota((tt, tt), 0) - _iota((tt, tt), 1)).astype(F32)
    spos = _iota((RET_DK, tt), 1)
    for hh in range(RET_HPS):
        cols = slice(hh * RET_DK, (hh + 1) * RET_DK)
        lgl = lg_ref[hh][:, :LANES]
        lgt = lg_ref[hh][:, :tt]
        q = q_ref[0][:, cols] * (RET_DK ** -0.5)
        k = k_ref[0][:, cols]
        vb = v_ref[0][:, cols].astype(MXU_DT)
        dmat = jnp.where(rel >= 0, jnp.exp(jnp.maximum(rel, 0.0) * lgt), 0.0)
        att = _dot_nt(q.astype(MXU_DT), k.astype(MXU_DT)) * dmat
        q_dec = jnp.exp((_iota((tt, RET_DK), 0) + 1).astype(F32) * lgl)
        s = s_ref[hh]
        o = _dot(att.astype(MXU_DT), vb) + _dot((q * q_dec).astype(MXU_DT), s.astype(MXU_DT))

        k_dec = jnp.where(spos < n_valid, jnp.exp(jnp.maximum(n_valid - 1 - spos, 0).astype(F32) * lgt), 0.0)
        kt = _transpose_exact(k)
        s_ref[hh] = jnp.exp(float(n_valid) * lgl) * s + _dot((kt * k_dec).astype(MXU_DT), vb)

        mu = jnp.mean(o, axis=-1, keepdims=True)
        var = jnp.mean(jnp.square(o - mu), axis=-1, keepdims=True)
        o_ref[0, :, cols] = ((o - mu) * lax.rsqrt(var + EPS) * _silu(g_ref[0][:, cols])).astype(o_ref.dtype)

    @pl.when(tb == pl.num_programs(2) - 1)
    def _():
        sout_ref[0] = s_ref[...]


def retention_mixer(h3, lay, lg_tab, s0, *, tt, n_valid):
    bsz, t, _ = h3.shape
    assert n_valid == tt or t == tt
    hps = RET_HPS
    wd = hps * RET_DK
    cq, ck, cv, cg = (lay[k] // wd for k in ("rq", "rk", "rv", "rg"))
    col = lambda c0: (lambda b, hp, tb: (b, tb, c0 + hp))
    kern = functools.partial(_ret_kernel, tt=tt, n_valid=n_valid)
    return pl.pallas_call(
        kern,
        grid=(bsz, RET_HEADS // hps, t // tt),
        in_specs=[pl.BlockSpec((1, tt, wd), col(cq)), pl.BlockSpec((1, tt, wd), col(ck)),
                  pl.BlockSpec((1, tt, wd), col(cv)), pl.BlockSpec((1, tt, wd), col(cg)),
                  pl.BlockSpec((hps, 1, lg_tab.shape[2]), lambda b, hp, tb: (hp, 0, 0)),
                  pl.BlockSpec((1, hps, RET_DK, RET_DV), lambda b, hp, tb: (b, hp, 0, 0))],
        out_specs=[pl.BlockSpec((1, tt, hps * RET_DV), lambda b, hp, tb: (b, tb, hp)),
                   pl.BlockSpec((1, hps, RET_DK, RET_DV), lambda b, hp, tb: (b, hp, 0, 0))],
        out_shape=[jax.ShapeDtypeStruct((bsz, t, RET_HEADS * RET_DV), MXU_DT),
                   jax.ShapeDtypeStruct((bsz, RET_HEADS, RET_DK, RET_DV), F32)],
        scratch_shapes=[pltpu.VMEM((hps, RET_DK, RET_DV), F32)],
        compiler_params=_cparams("parallel", "parallel", "arbitrary"),
        name="retention",
    )(h3, h3, h3, h3, lg_tab, s0)


def _store_heads(ref, x, lead=()):
    for hh in range(NSA_KV_HEADS):
        ref[lead + (hh,)] = x[:, hh * NSA_HEAD_DIM:(hh + 1) * NSA_HEAD_DIM].astype(ref.dtype)


def _q_prep_kernel(q_ref, gain_ref, bsum_ref, o_ref):
    x = q_ref[0]
    hd = NSA_HEAD_DIM
    for kvh in range(NSA_KV_HEADS):
        lo = kvh * NSA_KV_WIDTH
        qn = _head_rms(x[:, lo:lo + NSA_KV_WIDTH], bsum_ref[...], gain_ref[...]) * (hd ** -0.5)
        for g in range(NSA_GROUP):
            o_ref[0, kvh, g, :, 0:hd] = qn[:, g * hd:(g + 1) * hd].astype(o_ref.dtype)
            if o_ref.shape[-1] > hd:
                o_ref[0, kvh, g, :, hd:] = jnp.zeros((x.shape[0], o_ref.shape[-1] - hd), o_ref.dtype)


def q_prep(h3, lay, gain_q, bsum, *, tr, width=NSA_HEAD_DIM):
    bsz, t, _ = h3.shape
    cq = lay["nq"] // NSA_WIDTH
    return pl.pallas_call(
        _q_prep_kernel,
        grid=(bsz, t // tr),
        in_specs=[pl.BlockSpec((1, tr, NSA_WIDTH), lambda b, i: (b, i, cq)),
                  pl.BlockSpec((1, NSA_KV_WIDTH), lambda b, i: (0, 0)),
                  pl.BlockSpec((NSA_KV_WIDTH, NSA_KV_WIDTH), lambda b, i: (0, 0))],
        out_specs=pl.BlockSpec((1, NSA_KV_HEADS, NSA_GROUP, tr, width), lambda b, i: (b, 0, 0, i, 0)),
        out_shape=jax.ShapeDtypeStruct((bsz, NSA_KV_HEADS, NSA_GROUP, t, width), MXU_DT),
        compiler_params=_cparams("parallel", "parallel"),
        name="nsa_q_prep",
    )(h3, jnp.tile(gain_q, NSA_KV_HEADS).reshape(1, NSA_KV_WIDTH), bsum)


def _kv_prep_kernel(x_ref, gain_ref, bsum_ref, *rest):
    ko_ref, vo_ref = rest[-2:]
    hd = NSA_HEAD_DIM
    x = x_ref[0]
    kn = _head_rms(x[:, :NSA_KV_WIDTH], bsum_ref[...], gain_ref[...])
    for hh in range(NSA_KV_HEADS):
        ko_ref[0, hh, :, 0:hd] = kn[:, hh * hd:(hh + 1) * hd].astype(ko_ref.dtype)
        if len(rest) == 3:
            ko_ref[0, hh, :, hd:] = rest[0][:, hd:]
        lo = NSA_KV_WIDTH + hh * hd
        vo_ref[0, hh, :, 0:hd] = x[:, lo:lo + hd].astype(vo_ref.dtype)
        if vo_ref.shape[-1] > hd:
            vo_ref[0, hh, :, hd:] = jnp.ones((x.shape[0], vo_ref.shape[-1] - hd), vo_ref.dtype)


def kv_prep(x3, col, gain, bsum, *, tr, aug=None):
    bsz, t, _ = x3.shape
    w2 = 2 * NSA_KV_WIDTH
    hd = NSA_HEAD_DIM
    kw = hd if aug is None else aug.shape[1]
    vw = hd if aug is None else LANES
    vshape = jax.ShapeDtypeStruct((bsz, NSA_KV_HEADS, t, vw), MXU_DT)
    kshape = jax.ShapeDtypeStruct((bsz, NSA_KV_HEADS, t, kw), MXU_DT)
    spec = lambda w: pl.BlockSpec((1, NSA_KV_HEADS, tr, w), lambda b, i: (b, 0, i, 0))
    in_specs = [pl.BlockSpec((1, tr, w2), lambda b, i: (b, i, col)),
                pl.BlockSpec((1, NSA_KV_WIDTH), lambda b, i: (0, 0)),
                pl.BlockSpec((NSA_KV_WIDTH, NSA_KV_WIDTH), lambda b, i: (0, 0))]
    args = [x3, jnp.tile(gain, NSA_KV_HEADS).reshape(1, NSA_KV_WIDTH), bsum]
    if aug is not None:
        in_specs.append(pl.BlockSpec((tr, kw), lambda b, i: (i, 0)))
        args.append(aug)
    return pl.pallas_call(
        _kv_prep_kernel,
        grid=(bsz, t // tr),
        in_specs=in_specs,
        out_specs=[spec(kw), spec(vw)],
        out_shape=[kshape, vshape],
        compiler_params=_cparams("parallel", "parallel"),
        name="nsa_kv_prep",
    )(*args)


def _head_rms_t(xt, gain_t):
    x3 = xt.reshape(NSA_KV_HEADS, NSA_HEAD_DIM, xt.shape[1])
    ms = jnp.mean(x3 * x3, axis=1, keepdims=True)
    return (x3 * lax.rsqrt(ms + EPS)).reshape(xt.shape) * gain_t


def _gain_t(gain):
    return jnp.broadcast_to(jnp.tile(gain, NSA_KV_HEADS)[:, None], (NSA_KV_WIDTH, LANES))


def _cmp_prep_kernel(x_ref, nxt_ref, tail_ref, w_ref, gain_ref, bsum_ref, kc_ref, vc_ref, *, rows):
    cs = CMP_STRIDE
    nb = rows // cs
    w0 = w_ref[0]
    w1 = w_ref[1]
    x3 = x_ref[0].reshape(nb, cs, 2 * NSA_KV_WIDTH)
    a0 = jnp.sum(x3 * w0[None], axis=1)
    a1 = jnp.sum(x3 * w1[None], axis=1)
    last = pl.program_id(1) == pl.num_programs(1) - 1
    nx = jnp.where(last, tail_ref[0], nxt_ref[0])
    a1_next = jnp.sum(nx * w1, axis=0, keepdims=True)
    a1s = pltpu.roll(a1, nb - 1, 0)
    a1s = jnp.where(_iota(a1s.shape, 0) == nb - 1, a1_next, a1s)
    comp = a0 + a1s
    kc = _head_rms(comp[:, :NSA_KV_WIDTH], bsum_ref[...], gain_ref[...])
    _store_heads(kc_ref, kc, (0,))
    _store_heads(vc_ref, comp[:, NSA_KV_WIDTH:], (0,))


def _cmp_weight_table(cmp_w):
    w = cmp_w.reshape(2, CMP_BLOCK // CMP_STRIDE, CMP_STRIDE)
    w = jnp.transpose(w, (1, 2, 0))
    return jnp.repeat(w, NSA_KV_WIDTH, axis=2)


def cmp_prep(x3, col, tail16, cmp_w, gain, bsum, *, tr):
    bsz, t, _ = x3.shape
    w2 = 2 * NSA_KV_WIDTH
    nbt = tr // CMP_STRIDE
    n_steps = t // tr
    kern = functools.partial(_cmp_prep_kernel, rows=tr)
    oshape = jax.ShapeDtypeStruct((bsz, NSA_KV_HEADS, t // CMP_STRIDE, NSA_HEAD_DIM), MXU_DT)
    ospec = pl.BlockSpec((1, NSA_KV_HEADS, nbt, NSA_HEAD_DIM), lambda b, i: (b, 0, i, 0))
    chunks_per_step = tr // CMP_STRIDE
    return pl.pallas_call(
        kern,
        grid=(bsz, n_steps),
        in_specs=[pl.BlockSpec((1, tr, w2), lambda b, i: (b, i, col)),
                  pl.BlockSpec((1, CMP_STRIDE, w2),
                               lambda b, i: (b, jnp.minimum(i + 1, n_steps - 1) * chunks_per_step, col)),
                  pl.BlockSpec((1, CMP_STRIDE, w2), lambda b, i: (b, 0, 0)),
                  pl.BlockSpec((2, CMP_STRIDE, w2), lambda b, i: (0, 0, 0)),
                  pl.BlockSpec((1, NSA_KV_WIDTH), lambda b, i: (0, 0)),
                  pl.BlockSpec((NSA_KV_WIDTH, NSA_KV_WIDTH), lambda b, i: (0, 0))],
        out_specs=[ospec, ospec],
        out_shape=[oshape, oshape],
        compiler_params=_cparams("parallel", "arbitrary"),
        name="nsa_cmp_prep",
    )(x3, x3, tail16, _cmp_weight_table(cmp_w), jnp.tile(gain, NSA_KV_HEADS).reshape(1, NSA_KV_WIDTH), bsum)


def _cmp_prep_t_kernel(*refs, n_in):
    refs = refs[-(n_in + 6):]
    nxt_ref, tail_ref, w_ref, gain_ref, kc_ref, vc_ref = refs[n_in:]
    nb = w_ref.shape[2] // 4
    kvw = NSA_KV_WIDTH

    last = pl.program_id(1) == pl.num_programs(1) - 1
    nxt = jnp.where(last, tail_ref[0], nxt_ref[0])
    rows = [refs[u][0].reshape(2 * kvw, PAGE_SIZE).astype(MXU_DT) for u in range(n_in)]
    rows.append(nxt.reshape(2 * kvw, PAGE_SIZE).astype(MXU_DT))
    acc = _dot(jnp.concatenate(rows, axis=1), w_ref[...].reshape((n_in + 1) * PAGE_SIZE, 4 * nb))
    comp_k = acc[:kvw, 0:nb] + acc[:kvw, 2 * nb:3 * nb]
    comp_v = acc[kvw:, nb:2 * nb] + acc[kvw:, 3 * nb:4 * nb]
    kct = _head_rms_t(comp_k, gain_ref[...][:, :nb]).astype(MXU_DT)
    eye = _eye(nb)
    _store_heads(kc_ref, _dot_nt(eye, kct), (0,))
    _store_heads(vc_ref, _dot_nt(eye, comp_v.astype(MXU_DT)), (0,))


def _cmp_band_tables(cmp_w, n_pages_step):
    cs = CMP_STRIDE
    w = cmp_w.reshape(2, CMP_BLOCK // cs, cs)
    nb = n_pages_step * PAGE_SIZE // cs
    rho = np.arange(PAGE_SIZE)
    ch = (np.arange(n_pages_step)[:, None] * (PAGE_SIZE // cs) + rho[None, :] // cs)[..., None]
    n = np.arange(nb)[None, None, :]
    first = (ch == n).astype(np.float32)
    second = (ch == n + 1).astype(np.float32)
    nxt_rows = ((rho[:, None] < cs) & (n[0] == nb - 1)).astype(np.float32)
    tabs = []
    for c in range(2):
        w0 = jnp.tile(w[c, 0], PAGE_SIZE // cs)
        w1 = jnp.tile(w[c, 1], PAGE_SIZE // cs)
        full = w0[None, :, None] * first + w1[None, :, None] * second
        nxt = w1[:, None] * nxt_rows
        tabs.append(jnp.concatenate([full, nxt[None]], axis=0))
    hi = [t.astype(MXU_DT) for t in tabs]
    lo = [(t - h.astype(F32)).astype(MXU_DT) for t, h in zip(tabs, hi)]
    return jnp.concatenate(hi + lo, axis=2)


def cmp_prep_paged(pool_t, pages, tail_t, cmp_w, gain):
    bsz, n_pages = pages.shape
    pp = PAGES_PER_STEP
    n_steps = n_pages // pp
    nbt = pp * PAGE_SIZE // CMP_STRIDE
    kern = functools.partial(_cmp_prep_t_kernel, n_in=pp)
    oshape = jax.ShapeDtypeStruct((bsz, NSA_KV_HEADS, n_pages * PAGE_SIZE // CMP_STRIDE, NSA_HEAD_DIM), MXU_DT)
    ospec = pl.BlockSpec((1, NSA_KV_HEADS, nbt, NSA_HEAD_DIM), lambda b, i, pt: (b, 0, i, 0))
    pblock = (1, 2, NSA_KV_WIDTH, PAGE_SIZE)
    page_spec = lambda u: pl.BlockSpec(pblock, lambda b, i, pt: (pt[b, i * pp + u], 0, 0, 0))
    const = lambda *shape: pl.BlockSpec(shape, lambda b, i, pt: (0,) * len(shape))
    return pl.pallas_call(
        kern,
        grid_spec=pltpu.PrefetchScalarGridSpec(
            num_scalar_prefetch=1,
            grid=(bsz, n_steps),
            in_specs=[page_spec(u) for u in range(pp)]
                     + [pl.BlockSpec(pblock, lambda b, i, pt: (pt[b, jnp.minimum(i + 1, n_steps - 1) * pp], 0, 0, 0)),
                        pl.BlockSpec(pblock, lambda b, i, pt: (b, 0, 0, 0)),
                        const(pp + 1, PAGE_SIZE, 4 * nbt), const(NSA_KV_WIDTH, LANES)],
            out_specs=[ospec, ospec]),
        out_shape=[oshape, oshape],
        compiler_params=_cparams("parallel", "arbitrary"),
        name="nsa_cmp_prep_paged",
    )(pages, *([pool_t] * pp), pool_t, tail_t, _cmp_band_tables(cmp_w, pp), _gain_t(gain))


def _gate_and_store(o_ref, br_ref, nbg_ref, gx_ref, branches, tq):
    hd = NSA_HEAD_DIM
    for br, o in enumerate(branches):
        for g in range(NSA_GROUP):
            br_ref[br, :, g * hd:(g + 1) * hd] = o[g * tq:(g + 1) * tq]
    pieces = _split(_sigmoid(nbg_ref[0]), 2)
    acc = None
    for br in range(3):
        term = sum(_dot(p, gx_ref[br]) for p in pieces) * br_ref[br]
        acc = term if acc is None else acc + term
    o_ref[0] = acc.astype(o_ref.dtype)


def _imp_to_sel_matrix(n_sel_pad, nbp):
    m = SEL_BLOCK // CMP_STRIDE
    r = CMP_BLOCK // CMP_STRIDE
    jj = np.arange(n_sel_pad)[:, None]
    ii = np.arange(nbp)[None, :]
    cnt = sum((((ii - rr) >= m * jj) & ((ii - rr) < m * (jj + 1))).astype(np.float32) for rr in range(r))
    return cnt.astype(MXU_DT)


def _topk_block_mask(p, at_ref, t0, *, tq, tl, n_sel, topk):
    n_sel_pad, nbp = at_ref.shape
    imp = p[0:tq] + p[tq:2 * tq] + p[2 * tq:3 * tq] + p[3 * tq:4 * tq]
    if tl > tq:
        imp = jnp.concatenate([imp, jnp.zeros((tl - tq, nbp), F32)], axis=0)
    imp_sel = sum(_dot_nt(at_ref[...], piece) for piece in _split(imp, 3))
    j = _iota((n_sel_pad, tl), 0)
    tt = t0 + _iota((n_sel_pad, tl), 1)
    forced = (j == 0) | (j == (tt >> 6))
    allowed = (j << 6) <= tt
    score = jnp.where(forced, POS_BIG, jnp.where(allowed, imp_sel, NEG_BIG))
    score = jnp.where(j >= n_sel, LOWEST, score)
    jf = j.astype(F32)
    sel = jnp.zeros((n_sel_pad, tl), F32)
    for _ in range(topk):
        mx = jnp.max(score, axis=0, keepdims=True)
        first = jnp.min(jnp.where(score == mx, jf, 1e9), axis=0, keepdims=True)
        pick = jf == first
        sel = jnp.where(pick, 1.0, sel)
        score = jnp.where(pick, -jnp.inf, score)
    return jnp.where(allowed, sel, 0.0)


def _nsa_prompt_kernel(q_ref, sl_ref, qts_ref, qtw_ref, kc_ref, vc_ref, ks_ref, vs_ref, kw_ref, vw_ref, at_ref,
                       nbg_ref, gx_ref, o_ref, m_ref, acc_ref, br_ref, *, tq, n_sel, topk):
    rows = NSA_GROUP * tq
    hd = NSA_HEAD_DIM
    qi = pl.program_id(2)
    t0 = qi * tq
    q128 = q_ref[0, 0].reshape(rows, LANES)
    slope = sl_ref[0][:, :1]
    nbp = kc_ref.shape[2]

    s = _dot_nt(q128[:, :hd], kc_ref[0, 0])
    tok = t0 + (_iota((rows, nbp), 0) & (tq - 1))
    dist = tok - (_iota((rows, nbp), 1) * CMP_STRIDE + (CMP_BLOCK - 1))
    valid = dist >= 0
    s = jnp.where(valid, s - slope * dist.astype(F32), NEG_BIG)
    p = jnp.where(valid, jnp.exp(s - jnp.max(s, axis=-1, keepdims=True)), 0.0)
    p = p / jnp.maximum(jnp.sum(p, axis=-1, keepdims=True), 1e-30)
    o_cmp = _dot(p.astype(MXU_DT), vc_ref[0, 0])

    def normalised(acc):
        return acc[:, :hd] / jnp.maximum(acc[:, hd:], 1e-30)

    lane = _iota((rows, LANES), 1)
    qw = jnp.where(lane < hd, q128, qtw_ref[0])
    span = WINDOW + tq
    ws = pl.multiple_of(jnp.maximum(t0 - WINDOW, 0), tq)
    sw = _dot_nt(qw, kw_ref[0, 0, pl.ds(ws, span), :])
    d = (_iota((rows, span), 0) & (tq - 1)) - _iota((rows, span), 1) + (t0 - ws)
    sw = jnp.where((d >= 0) & (d < WINDOW), sw, NEG_BIG)
    pw = jnp.exp(sw - jnp.max(sw, axis=-1, keepdims=True))
    o_win = normalised(_dot(pw.astype(MXU_DT), vw_ref[0, 0, pl.ds(ws, span), :]))

    sel = _topk_block_mask(p, at_ref, t0, tq=tq, tl=tq, n_sel=n_sel, topk=topk)
    sel_t = jnp.concatenate([jnp.zeros_like(sel), sel], axis=0).astype(MXU_DT)
    spread = ((_iota((rows, tq), 0) & (tq - 1)) == _iota((rows, tq), 1)).astype(MXU_DT)
    selb = _dot_nt(spread, sel_t)
    neg = jnp.where((lane >= hd) & (selb < 0.5), NEG_BIG, 0.0)
    qa = jnp.concatenate([(q128.astype(F32) + neg).astype(MXU_DT), qts_ref[0]], axis=1)

    tk = PROMPT_SEL_TK
    kd = t0 // tk
    k0 = pl.multiple_of(kd * tk, tk)
    s = _dot_nt(qa, ks_ref[0, 0, pl.ds(k0, tk), :])
    causal = ((_iota((rows, tk), 0) & (tq - 1)) - _iota((rows, tk), 1) + (t0 - k0)) >= 0
    s = jnp.where(causal, s, NEG_BIG)
    m = jnp.max(s, axis=-1, keepdims=True)
    m_ref[...] = jnp.broadcast_to(m, m_ref.shape)
    acc_ref[...] = _dot(jnp.exp(s - m).astype(MXU_DT), vs_ref[0, 0, pl.ds(k0, tk), :])

    def sel_body(kt, carry):
        ka = pl.multiple_of(kt * tk, tk)
        s = _dot_nt(qa, ks_ref[0, 0, pl.ds(ka, tk), :])
        m_prev = m_ref[...]
        m_new = jnp.maximum(m_prev, jnp.max(s, axis=-1, keepdims=True))
        p = jnp.exp(s - jnp.concatenate([m_new] * (tk // LANES), axis=1))
        acc_ref[...] = jnp.exp(m_prev - m_new) * acc_ref[...] + _dot(p.astype(MXU_DT), vs_ref[0, 0, pl.ds(ka, tk), :])
        m_ref[...] = m_new
        return carry

    lax.fori_loop(0, kd, sel_body, 0)
    o_sel = normalised(acc_ref[...])

    _gate_and_store(o_ref, br_ref, nbg_ref, gx_ref, (o_cmp, o_sel, o_win), tq)


def _slope_digits(tq, tile, lane0):
    pieces, r = [], _alibi_slopes()
    for _ in range(3):
        p = r.astype(MXU_DT).astype(np.float32)
        pieces.append(p)
        r = r - p
    pieces = np.stack(pieces, axis=1)
    tab = np.zeros((NSA_HEADS, LANES), np.float32)
    tab[:, lane0:lane0 + 6] = np.concatenate([pieces * float(tile), pieces], axis=1)
    tab = np.broadcast_to(tab.reshape(NSA_KV_HEADS, NSA_GROUP, 1, LANES), (NSA_KV_HEADS, NSA_GROUP, tq, LANES))
    return tab.reshape(NSA_KV_HEADS, NSA_GROUP * tq, LANES).astype(MXU_DT)


def _key_digits(t, tile, lane0, width, onehot):
    r = np.arange(t)
    lane = np.arange(width)[None, :]
    tab = np.zeros((t, width), np.float32)
    if onehot:
        tab = np.where(lane - NSA_HEAD_DIM == (r // SEL_BLOCK)[:, None], 1.0, tab)
    hi = (r // tile).astype(np.float32)[:, None]
    lo = (r % tile).astype(np.float32)[:, None]
    tab = np.where((lane >= lane0) & (lane < lane0 + 3), hi, tab)
    tab = np.where((lane >= lane0 + 3) & (lane < lane0 + 6), lo, tab)
    return tab.astype(MXU_DT)


def nsa_attention_prompt(q128, slopes, kc, vc, ks_aug, vs, kw_aug, vw, h3, nbg_col, gate_expand, *, n_sel):
    bsz, _, _, t, _ = q128.shape
    tq = PROMPT_TQ
    hd = NSA_HEAD_DIM
    nbp = kc.shape[2]
    assert n_sel <= 64 and t % PROMPT_SEL_TK == 0 and t >= WINDOW + tq and PROMPT_SEL_TK % tq == 0
    rows = NSA_GROUP * tq
    at = _imp_to_sel_matrix(64, nbp)
    kern = functools.partial(_nsa_prompt_kernel, tq=tq, n_sel=n_sel, topk=min(NSA_TOPK, n_sel))
    full = lambda n, w: pl.BlockSpec((1, 1, n, w), lambda b, kh, i: (b, kh, 0, 0))
    per_head = pl.BlockSpec((1, rows, LANES), lambda b, kh, i: (kh, 0, 0))
    gw = NSA_GROUP * hd
    return pl.pallas_call(
        kern,
        grid=(bsz, NSA_KV_HEADS, t // tq),
        in_specs=[pl.BlockSpec((1, 1, NSA_GROUP, tq, LANES), lambda b, kh, i: (b, kh, 0, i, 0)),
                  per_head, per_head, per_head,
                  full(nbp, hd), full(nbp, hd), full(t, 2 * LANES), full(t, LANES), full(t, LANES), full(t, LANES),
                  pl.BlockSpec((64, nbp), lambda b, kh, i: (0, 0)),
                  pl.BlockSpec((1, tq, LANES), lambda b, kh, i: (b, i, nbg_col)),
                  pl.BlockSpec((3, LANES, gw), lambda b, kh, i: (0, 0, kh))],
        out_specs=pl.BlockSpec((1, tq, gw), lambda b, kh, i: (b, i, kh)),
        out_shape=jax.ShapeDtypeStruct((bsz, t, NSA_WIDTH), MXU_DT),
        scratch_shapes=[pltpu.VMEM((rows, LANES), F32), pltpu.VMEM((rows, LANES), F32),
                        pltpu.VMEM((3, tq, gw), F32)],
        compiler_params=_cparams("parallel", "parallel", "arbitrary"),
        name="nsa_attention_prompt",
    )(q128, slopes, _slope_digits(tq, SEL_DIGIT, 0), _slope_digits(tq, WIN_DIGIT, hd),
      kc, vc, ks_aug, vs, kw_aug, vw, at, h3, gate_expand)


def _nsa_paged_pre_kernel(q_ref, sl_ref, kc_ref, vc_ref, kw_ref, vw_ref, at_ref,
                          ocmp_ref, owin_ref, sel_ref, *, tq, qpos0, wpos0, n_sel, topk):
    rows = NSA_GROUP * tq
    hd = NSA_HEAD_DIM
    q = q_ref[0, 0].reshape(rows, hd)
    slope = sl_ref[0][:, :1]
    nbp = kc_ref.shape[2]
    tw = kw_ref.shape[2]

    s = _dot_nt(q, kc_ref[0, 0])
    tok = qpos0 + (_iota((rows, nbp), 0) & (tq - 1))
    dist = tok - (_iota((rows, nbp), 1) * CMP_STRIDE + (CMP_BLOCK - 1))
    valid = dist >= 0
    s = jnp.where(valid, s - slope * dist.astype(F32), NEG_BIG)
    p = jnp.where(valid, jnp.exp(s - jnp.max(s, axis=-1, keepdims=True)), 0.0)
    p = p / jnp.maximum(jnp.sum(p, axis=-1, keepdims=True), 1e-30)
    o_cmp = _dot(p.astype(MXU_DT), vc_ref[0, 0])
    sel_ref[0, 0] = _topk_block_mask(p, at_ref, qpos0, tq=tq, tl=sel_ref.shape[3], n_sel=n_sel, topk=topk)

    sw = _dot_nt(q, kw_ref[0, 0])
    d = (qpos0 - wpos0) + (_iota((rows, tw), 0) & (tq - 1)) - _iota((rows, tw), 1)
    valid = (d >= 0) & (d < WINDOW)
    sw = jnp.where(valid, sw - slope * d.astype(F32), NEG_BIG)
    pw = jnp.where(valid, jnp.exp(sw - jnp.max(sw, axis=-1, keepdims=True)), 0.0)
    o_win = _dot(pw.astype(MXU_DT), vw_ref[0, 0]) / jnp.maximum(jnp.sum(pw, axis=-1, keepdims=True), 1e-30)

    for g in range(NSA_GROUP):
        ocmp_ref[0, :, g * hd:(g + 1) * hd] = o_cmp[g * tq:(g + 1) * tq]
        owin_ref[0, :, g * hd:(g + 1) * hd] = o_win[g * tq:(g + 1) * tq]


def _nsa_paged_sel_kernel(*refs, n_in, tq, qpos0, n_tiles):
    hit_ref = refs[1]
    refs = refs[2:]
    q_ref, sl_ref, sel_ref = refs[:3]
    pages = refs[3:3 + n_in]
    (ktn_ref, vtn_ref, gain_ref, ocmp_ref, owin_ref, nbg_ref, gx_ref,
     o_ref, m_ref, l_ref, acc_ref, kt_ref, vt_ref, br_ref) = refs[3 + n_in:]
    rows = NSA_GROUP * tq
    hd = NSA_HEAD_DIM
    tk = n_in * PAGE_SIZE
    nblk = tk // SEL_BLOCK
    i = pl.program_id(1)

    @pl.when(i == 0)
    def _():
        m_ref[...] = jnp.full(m_ref.shape, NEG_BIG, F32)
        l_ref[...] = jnp.zeros(l_ref.shape, F32)
        acc_ref[...] = jnp.zeros(acc_ref.shape, F32)

    @pl.when(i < n_tiles)
    def _():
        for u in range(n_in):
            sl = slice(u * PAGE_SIZE, (u + 1) * PAGE_SIZE)
            kt_ref[:, sl] = _head_rms_t(pages[u][0, 0], gain_ref[...]).astype(kt_ref.dtype)
            vt_ref[:, sl] = pages[u][0, 1].astype(vt_ref.dtype)

    @pl.when(i == n_tiles)
    def _():
        kt_ref[...] = ktn_ref[0]
        vt_ref[...] = vtn_ref[0]

    k0 = i * tk
    d = (qpos0 - k0) + (_iota((rows, tk), 0) & (tq - 1)) - _iota((rows, tk), 1)
    causal = d >= 0
    dist = d.astype(F32)
    spread = ((_iota((rows, LANES), 0) & (tq - 1)) == _iota((rows, LANES), 1)).astype(MXU_DT)
    in_tile = (_iota((nblk, tk), 0) == (_iota((nblk, tk), 1) >> 6)).astype(MXU_DT)
    blk0 = pl.multiple_of(i * nblk, nblk)
    reps = tk // LANES
    def head_update(kvh):
        hs = slice(kvh * hd, (kvh + 1) * hd)
        q = q_ref[0, kvh].reshape(rows, hd)
        s = _dot(q, kt_ref[hs, :])
        mine = _dot_nt(spread, sel_ref[0, kvh, pl.ds(blk0, nblk), :].astype(MXU_DT)).astype(MXU_DT)
        valid = (_dot(mine, in_tile) > 0.5) & causal
        s = jnp.where(valid, s - sl_ref[kvh][:, :1] * dist, NEG_BIG)
        m_prev = m_ref[kvh]
        m_new = jnp.maximum(m_prev, jnp.max(s, axis=-1, keepdims=True))
        alpha = jnp.exp(m_prev - m_new)
        p = jnp.where(valid, jnp.exp(s - jnp.concatenate([m_new] * reps, axis=1)), 0.0)
        l_ref[kvh] = alpha * l_ref[kvh] + jnp.sum(p, axis=-1, keepdims=True)
        acc_ref[kvh] = alpha[:, :hd] * acc_ref[kvh] + _dot_nt(p.astype(MXU_DT), vt_ref[hs, :])
        m_ref[kvh] = m_new

    for kvh in range(NSA_KV_HEADS):
        pl.when(hit_ref[pl.program_id(0), kvh * (n_tiles + 1) + i] > 0)(functools.partial(head_update, kvh))

    @pl.when(i == n_tiles)
    def _():
        br_ref[0] = ocmp_ref[0]
        br_ref[2] = owin_ref[0]
        for kvh in range(NSA_KV_HEADS):
            o_sel = acc_ref[kvh] / jnp.maximum(l_ref[kvh][:, :hd], 1e-30)
            for g in range(NSA_GROUP):
                c0 = (kvh * NSA_GROUP + g) * hd
                br_ref[1, :, c0:c0 + hd] = o_sel[g * tq:(g + 1) * tq]
        pieces = _split(_sigmoid(nbg_ref[0]), 2)
        acc = None
        for br in range(3):
            term = sum(_dot(p, gx_ref[br]) for p in pieces) * br_ref[br]
            acc = term if acc is None else acc + term
        o_ref[0] = acc.astype(o_ref.dtype)


def nsa_attention_paged(qn, slopes, kc, vc, kw, vw, pool_t, pages, kt_new, vt_new, gain_sel, h3, nbg_col,
                        gate_expand, *, tq, t_real, qpos0, wpos0, n_sel):
    bsz, _, _, tql, hd = qn.shape
    assert tql == tq
    nbp, tw = kc.shape[2], kw.shape[2]
    n_pages = pages.shape[1]
    tk = kt_new.shape[2]
    n_in = tk // PAGE_SIZE
    n_tiles = n_pages // n_in
    assert n_pages % n_in == 0 and qpos0 == n_pages * PAGE_SIZE
    n_sel_pad = -(-max(n_sel, (n_tiles + 1) * tk // SEL_BLOCK) // 64) * 64
    rows = NSA_GROUP * tq
    gw = NSA_GROUP * hd
    at = _imp_to_sel_matrix(n_sel_pad, nbp)
    full = lambda n: pl.BlockSpec((1, 1, n, hd), lambda b, kh: (b, kh, 0, 0))
    obr = pl.BlockSpec((1, tq, gw), lambda b, kh: (b, 0, kh))
    o_cmp, o_win, sel = pl.pallas_call(
        functools.partial(_nsa_paged_pre_kernel, tq=tq, qpos0=qpos0, wpos0=wpos0, n_sel=n_sel,
                          topk=min(NSA_TOPK, n_sel)),
        grid=(bsz, NSA_KV_HEADS),
        in_specs=[pl.BlockSpec((1, 1, NSA_GROUP, tq, hd), lambda b, kh: (b, kh, 0, 0, 0)),
                  pl.BlockSpec((1, rows, LANES), lambda b, kh: (kh, 0, 0)),
                  full(nbp), full(nbp), full(tw), full(tw),
                  pl.BlockSpec((n_sel_pad, nbp), lambda b, kh: (0, 0))],
        out_specs=[obr, obr, pl.BlockSpec((1, 1, n_sel_pad, LANES), lambda b, kh: (b, kh, 0, 0))],
        out_shape=[jax.ShapeDtypeStruct((bsz, tq, NSA_WIDTH), F32), jax.ShapeDtypeStruct((bsz, tq, NSA_WIDTH), F32),
                   jax.ShapeDtypeStruct((bsz, NSA_KV_HEADS, n_sel_pad, LANES), F32)],
        compiler_params=_cparams("parallel", "parallel"),
        name="nsa_paged_pre",
    )(qn, slopes, kc, vc, kw, vw, at)

    nblk = tk // SEL_BLOCK
    hit = sel[:, :, :(n_tiles + 1) * nblk, :t_real].reshape(bsz, NSA_KV_HEADS, n_tiles + 1, nblk * t_real)
    hit = (jnp.max(hit, axis=-1) > 0).astype(jnp.int32).reshape(bsz, NSA_KV_HEADS * (n_tiles + 1))

    page_spec = lambda u: pl.BlockSpec(
        (1, 2, NSA_KV_WIDTH, PAGE_SIZE),
        lambda b, i, pt, ht: (pt[b, jnp.minimum(i, n_tiles - 1) * n_in + u], 1, 0, 0))
    const = lambda *shape: pl.BlockSpec(shape, lambda b, i, pt, ht: (0,) * len(shape))
    per_seq = lambda *shape: pl.BlockSpec((1,) + shape, lambda b, i, pt, ht: (b,) + (0,) * len(shape))
    return pl.pallas_call(
        functools.partial(_nsa_paged_sel_kernel, n_in=n_in, tq=tq, qpos0=qpos0, n_tiles=n_tiles),
        grid_spec=pltpu.PrefetchScalarGridSpec(
            num_scalar_prefetch=2,
            grid=(bsz, n_tiles + 1),
            in_specs=[per_seq(NSA_KV_HEADS, NSA_GROUP, tq, hd), const(NSA_KV_HEADS, rows, LANES),
                      per_seq(NSA_KV_HEADS, n_sel_pad, LANES)]
                     + [page_spec(u) for u in range(n_in)]
                     + [per_seq(NSA_KV_WIDTH, tk), per_seq(NSA_KV_WIDTH, tk), const(NSA_KV_WIDTH, LANES),
                        per_seq(tq, NSA_WIDTH), per_seq(tq, NSA_WIDTH),
                        pl.BlockSpec((1, tq, LANES), lambda b, i, pt, ht: (b, 0, nbg_col)),
                        const(3, LANES, NSA_WIDTH)],
            out_specs=per_seq(tq, NSA_WIDTH),
            scratch_shapes=[pltpu.VMEM((NSA_KV_HEADS, rows, LANES), F32), pltpu.VMEM((NSA_KV_HEADS, rows, LANES), F32),
                            pltpu.VMEM((NSA_KV_HEADS, rows, hd), F32),
                            pltpu.VMEM((NSA_KV_WIDTH, tk), MXU_DT), pltpu.VMEM((NSA_KV_WIDTH, tk), MXU_DT),
                            pltpu.VMEM((3, tq, NSA_WIDTH), F32)]),
        out_shape=jax.ShapeDtypeStruct((bsz, tq, NSA_WIDTH), MXU_DT),
        compiler_params=_cparams("parallel", "arbitrary"),
        name="nsa_paged_sel",
    )(pages, hit, qn, slopes, sel, *([pool_t] * n_in), kt_new, vt_new, _gain_t(gain_sel),
      o_cmp, o_win, h3, gate_expand)


def _layout(d_model):
    tn = IN_PROJ_TN
    hw, rw = HG_HEADS * HG_DK, RET_HEADS * RET_DK
    gates = 3 * NSA_HEADS
    lay, off = {}, 0
    for name, width in [("hq", hw), ("hf", hw), ("hi", hw), ("hg", hw), ("nq", NSA_WIDTH),
                        ("nkv", 4 * NSA_KV_WIDTH), ("nwkv", 2 * NSA_KV_WIDTH), ("nbg", gates)]:
        lay[name] = off
        off += width
    lay["post_row"] = off
    lay["post_col"] = off = -(-off // tn) * tn
    for name, width in [("rq", rw), ("rk", rw), ("rv", rw), ("rg", rw), ("mg", 3 * d_model)]:
        lay[name] = off
        off += width
    assert lay["nbg"] % LANES == 0 and off % tn == 0 and lay["post_row"] % 16 == 0
    lay["used"] = lay["post_row"] + off - lay["post_col"]
    lay["total"] = off
    return lay


def _alibi_slopes():
    hh = np.arange(1, NSA_HEADS + 1, dtype=np.float32)
    return np.exp2(np.float32(-8.0) * hh / np.float32(NSA_HEADS)).astype(np.float32)


def _block_sum_matrix(width):
    i = np.arange(width)
    return (i[:, None] // NSA_HEAD_DIM == i[None, :] // NSA_HEAD_DIM).astype(np.float32).astype(MXU_DT)


def _gate_expand_matrix():
    c = np.arange(LANES)[:, None]
    col = np.arange(NSA_WIDTH)[None, :]
    return np.stack([(c == (col // NSA_HEAD_DIM) * 3 + br) for br in range(3)]).astype(np.float32).astype(MXU_DT)


def _slope_rows(tq):
    rows = np.repeat(_alibi_slopes().reshape(NSA_KV_HEADS, NSA_GROUP), tq, axis=1)
    return np.ascontiguousarray(np.broadcast_to(rows[:, :, None], (NSA_KV_HEADS, NSA_GROUP * tq, LANES)))


def _pad_rows(x, n):
    return jnp.pad(x, ((0, 0), (0, n - x.shape[1])) + ((0, 0),) * (x.ndim - 2))


def _trunk_layer(x, lw, lay, consts, nsa_fn, hg_state, ret_state, *, seq_tiles):
    bsz, t, d = x.shape
    tm, tt_h, tt_r, t_valid = seq_tiles
    x2 = x.reshape(bsz * t, d)
    wts, layer = lw["weights"], lw["layer"]
    h = rms_matmul(x2, lw["norm_attn"], wts["w_in"], layer, lay, tm=tm, tn=IN_PROJ_TN)
    np_ = h.shape[1]
    h3 = h.reshape(bsz, t, np_)
    tp = -(-t // tt_h) * tt_h
    h3p = _pad_rows(h3, tp) if tp != t else h3
    oa, hg_new = hgrn_mixer(h3p, lay, lw["lb"], lw["hg_out_norm"], hg_state, tt=tt_h,
                            t_valid=None if tp == t else t_valid)
    tpr = -(-t // tt_r) * tt_r
    h3r = _pad_rows(h3, tpr) if tpr != t else h3
    oc, ret_new = retention_mixer(h3r, lay, consts["lg_tab"], ret_state, tt=tt_r,
                                  n_valid=tt_r if tpr == t else t_valid)
    ob = nsa_fn(h3)
    flat = lambda a: a[:, :t].reshape(bsz * t, a.shape[-1])
    merged = merge_branches(h, lay, flat(oa), flat(ob), flat(oc),
                            wts["w_branch_hg"], wts["w_branch_nsa"], wts["w_branch_ret"], layer, tm=tm, tn=512)
    x2 = matmul_res(merged, wts["w_out"], layer, x2, tm=tm, tn=512)
    hid = ffn_up(x2, lw["norm_ffn"], wts["w_gate"], wts["w_up"], layer, tm=tm, tn=512)
    x2 = matmul_res(hid, wts["w_down"], layer, x2, tm=tm, tn=512)
    return x2.reshape(bsz, t, d), h3, hg_new, ret_new


def kernel(x_prompt, x_sample, cache_nsa, cache_win, state_hgrn, state_ret, page_table,
           norm_attn, w_in, hgrn_lb_logits, hgrn_out_norm, nsa_q_norm, nsa_k_norm, nsa_cmp_w,
           w_branch_hg, w_branch_nsa, w_branch_ret, w_out, norm_ffn, w_gate, w_up, w_down):
    depth = w_in.shape[0]
    bp, tp, d = x_prompt.shape
    bs, ts, _ = x_sample.shape
    n_pool = cache_nsa.shape[1]
    n_pages = page_table.shape[1]
    past = n_pages * PAGE_SIZE
    wbuf = cache_win.shape[2]
    lay = _layout(d)

    sm = jax.nn.softmax(hgrn_lb_logits.astype(F32), axis=0)
    lower_bounds = jnp.clip(jnp.cumsum(sm, axis=0) - sm[0:1], 0.0, 1.0 - 1e-6)

    tt_p = 256 if tp % 256 == 0 else tp
    lg = np.log1p(-np.exp2(np.float32(-5.0) - np.arange(RET_HEADS, dtype=np.float32))).astype(np.float32)
    consts = {
        "gate_expand": _gate_expand_matrix(),
        "lg_tab": np.ascontiguousarray(np.broadcast_to(lg[:, None, None], (RET_HEADS, 1, max(tt_p, LANES)))),
    }
    bsum = _block_sum_matrix(NSA_KV_WIDTH)
    pool_t = jnp.transpose(cache_nsa, (0, 1, 3, 4, 5, 2)).reshape(depth * n_pool, 4, NSA_KV_WIDTH, PAGE_SIZE)
    win3 = cache_win.reshape(depth, bs, wbuf, 2 * NSA_KV_WIDTH)

    ts_pad = 16
    tq_s = 16
    n_sel_p = -(-tp // SEL_BLOCK)
    n_sel_s = past // SEL_BLOCK + -(-ts // SEL_BLOCK)
    tk_tail = 2048 if past % 2048 == 0 else 512

    weights = {
        "w_in": jnp.swapaxes(w_in, 1, 2).astype(MXU_DT).reshape(depth * w_in.shape[2], d),
        "w_branch_hg": w_branch_hg.astype(MXU_DT), "w_branch_nsa": w_branch_nsa.astype(MXU_DT),
        "w_branch_ret": w_branch_ret.astype(MXU_DT), "w_out": w_out.astype(MXU_DT),
        "w_gate": w_gate.astype(MXU_DT), "w_up": w_up.astype(MXU_DT), "w_down": w_down.astype(MXU_DT),
    }
    assert w_in.shape[2] == lay["used"]

    xp, xs = x_prompt, x_sample
    outs = {k: [] for k in ("kv_p", "kv_s", "win_p", "win_s", "hg_p", "hg_s", "ret_p", "ret_s")}
    for l in range(depth):
        lw = {
            "weights": weights, "layer": l,
            "norm_attn": norm_attn[l], "lb": lower_bounds[l].reshape(1, -1),
            "hg_out_norm": hgrn_out_norm[l], "norm_ffn": norm_ffn[l],
        }
        gq, gk, cw = nsa_q_norm[l], nsa_k_norm[l], nsa_cmp_w[l]
        col_cmp = lay["nkv"] // (2 * NSA_KV_WIDTH)
        col_sel = col_cmp + 1
        col_win = lay["nwkv"] // (2 * NSA_KV_WIDTH)
        nbg_col = lay["nbg"] // LANES

        def nsa_prompt(h3):
            tr = 512
            assert tp % tr == 0
            zeros16 = jnp.zeros((bp, CMP_STRIDE, 2 * NSA_KV_WIDTH), F32)
            kc, vc = cmp_prep(h3, col_cmp, zeros16, cw, gk[0], bsum, tr=tr)
            qn = q_prep(h3, lay, gq, bsum, tr=tr, width=LANES)
            ks, vs = kv_prep(h3, col_sel, gk[1], bsum, tr=tr,
                             aug=_key_digits(tp, SEL_DIGIT, LANES, 2 * LANES, True))
            kw, vw = kv_prep(h3, col_win, gk[2], bsum, tr=tr,
                             aug=_key_digits(tp, WIN_DIGIT, NSA_HEAD_DIM, LANES, False))
            return nsa_attention_prompt(qn, _slope_rows(PROMPT_TQ), kc, vc, ks, vs, kw, vw,
                                        h3, nbg_col, consts["gate_expand"], n_sel=n_sel_p)

        def nsa_sample(h3):
            pages = page_table + l * n_pool
            h16 = _pad_rows(h3, ts_pad)
            qn = q_prep(h16, lay, gq, bsum, tr=ts_pad)
            w2 = 2 * NSA_KV_WIDTH
            tail16 = h16[:, :, lay["nkv"]:lay["nkv"] + w2].reshape(bs, ts_pad, 2, NSA_KV_WIDTH)
            tail_t = jnp.pad(jnp.transpose(tail16, (0, 2, 3, 1)), ((0, 0), (0, 0), (0, 0), (0, PAGE_SIZE - ts_pad)))
            kc, vc = cmp_prep_paged(pool_t, pages, tail_t, cw, gk[0])
            htail = _pad_rows(h3[:, :, col_sel * w2:(col_sel + 1) * w2], tk_tail)
            ks_tail, vs_tail = kv_prep(htail, 0, gk[1], bsum, tr=tk_tail)
            feature_major = lambda a: jnp.swapaxes(a, 2, 3).reshape(bs, NSA_KV_WIDTH, tk_tail)
            kw_old, vw_old = kv_prep(win3[l], 0, gk[2], bsum, tr=wbuf)
            hw_new = _pad_rows(h3[:, :, col_win * w2:(col_win + 1) * w2], WIN_PAD)
            kw_new, vw_new = kv_prep(hw_new, 0, gk[2], bsum, tr=WIN_PAD)
            kw = jnp.concatenate([kw_old, kw_new], axis=2)
            vw = jnp.concatenate([vw_old, vw_new], axis=2)
            return nsa_attention_paged(qn, _slope_rows(tq_s), kc, vc, kw, vw, pool_t, pages,
                                       feature_major(ks_tail), feature_major(vs_tail), gk[1], h16, nbg_col,
                                       consts["gate_expand"], tq=tq_s, t_real=ts, qpos0=past, wpos0=past - wbuf,
                                       n_sel=n_sel_s)

        zeros_state = jnp.zeros((bp, HG_HEADS, HG_DK, HG_DV), F32)
        tm_p = 1024 if (bp * tp) % 1024 == 0 else bp * tp
        xp, h3p, hgp, rtp = _trunk_layer(xp, lw, lay, consts, nsa_prompt, zeros_state, zeros_state,
                                         seq_tiles=(tm_p, tt_p, tt_p, tp))
        xs, h3s, hgs, rts = _trunk_layer(xs, lw, lay, consts, nsa_sample, state_hgrn[l], state_ret[l],
                                         seq_tiles=(bs * ts, ts_pad, ts_pad, ts))

        kv_cols = slice(lay["nkv"], lay["nkv"] + 4 * NSA_KV_WIDTH)
        win_cols = slice(lay["nwkv"], lay["nwkv"] + 2 * NSA_KV_WIDTH)
        outs["kv_p"].append(h3p[:, :, kv_cols].reshape(bp, tp, 4, NSA_KV_HEADS, NSA_HEAD_DIM))
        outs["kv_s"].append(h3s[:, :, kv_cols].reshape(bs, ts, 4, NSA_KV_HEADS, NSA_HEAD_DIM))
        wlen = min(WINDOW, tp)
        outs["win_p"].append(h3p[:, tp - wlen:, win_cols].reshape(bp, wlen, 2, NSA_KV_HEADS, NSA_HEAD_DIM))
        ctx = jnp.concatenate([win3[l], h3s[:, :, win_cols]], axis=1)[:, ts:]
        outs["win_s"].append(ctx.reshape(bs, wbuf, 2, NSA_KV_HEADS, NSA_HEAD_DIM))
        outs["hg_p"].append(hgp); outs["hg_s"].append(hgs)
        outs["ret_p"].append(rtp); outs["ret_s"].append(rts)

    st = lambda k: jnp.stack(outs[k])
    return (xp, xs, st("kv_p"), st("kv_s"), st("win_p"), st("win_s"),
            st("hg_p"), st("hg_s"), st("ret_p"), st("ret_s"))
```

```python
import functools

import numpy as np

import jax
import jax.numpy as jnp
from jax import lax
from jax.experimental import pallas as pl
from jax.experimental.pallas import tpu as pltpu

F32 = jnp.float32
MXU_DT = jnp.bfloat16

HG_HEADS, HG_DK, HG_DV = 8, 128, 128
NSA_HEADS, NSA_KV_HEADS, NSA_GROUP, NSA_HEAD_DIM = 16, 4, 4, 64
NSA_WIDTH = NSA_HEADS * NSA_HEAD_DIM
NSA_KV_WIDTH = NSA_KV_HEADS * NSA_HEAD_DIM
CMP_BLOCK, CMP_STRIDE, SEL_BLOCK, NSA_TOPK, WINDOW = 32, 16, 64, 16, 512
RET_HEADS, RET_DK, RET_DV = 8, 128, 128
PAGE_SIZE = 128
EPS = 1e-6
NEG_BIG = -1e30
POS_BIG = 1e30
MIN_F = 1e-20
LOWEST = -3.0e38

LANES = 128
VMEM_LIMIT_BYTES = 56 * 1024 * 1024

HG_SUB_SHIFT = 4
HG_SUB = 1 << HG_SUB_SHIFT
PAGES_PER_STEP = 16
IN_PROJ_TN = 1024
SEL_DIGIT = 256
PROMPT_SEL_TK = 512
PROMPT_TQ = 256
RET_HPS = 4
WIN_DIGIT = 128
WIN_PAD = 128


def _cparams(*sem):
    return pltpu.CompilerParams(dimension_semantics=sem, vmem_limit_bytes=VMEM_LIMIT_BYTES)


def _dot(a, b):
    return jnp.dot(a, b, preferred_element_type=F32)


def _dot_nt(a, b):
    return lax.dot_general(a, b, (((1,), (1,)), ((), ())), preferred_element_type=F32)


def _split(x, n):
    out = []
    r = x
    for _ in range(n):
        p = r.astype(MXU_DT)
        out.append(p)
        r = r - p.astype(F32)
    return out


def _sigmoid(x):
    return 1.0 / (1.0 + jnp.exp(-x))


def _silu(x):
    return x * _sigmoid(x)


def _iota(shape, dim):
    return lax.broadcasted_iota(jnp.int32, shape, dim)


def _eye(n):
    return (_iota((n, n), 0) == _iota((n, n), 1)).astype(MXU_DT)


def _transpose_exact(x):
    eye = _eye(x.shape[1])
    return sum(_dot_nt(eye, p) for p in _split(x, 3))


def _head_rms(x, bsum, gain):
    ssq = sum(_dot(p, bsum) for p in _split(x * x, 2))
    return x * lax.rsqrt(ssq * (1.0 / NSA_HEAD_DIM) + EPS) * gain


def _rms_matmul_kernel(x_ref, g_ref, w_ref, o_ref, xn_ref):
    @pl.when(pl.program_id(1) == 0)
    def _():
        x = x_ref[...]
        ms = jnp.mean(x * x, axis=-1, keepdims=True)
        xn_ref[...] = (x * lax.rsqrt(ms + EPS) * g_ref[...]).astype(xn_ref.dtype)

    o_ref[...] = _dot_nt(xn_ref[...], w_ref[...])


def rms_matmul(x, gain, wt, layer, lay, *, tm, tn):
    m, d = x.shape
    assert tn == IN_PROJ_TN and lay["used"] % 16 == 0
    nb_pre = lay["post_col"] // tn
    nb_all = lay["total"] // tn
    base = layer * lay["used"]

    def w_row(i, j):
        row = base + jnp.where(j < nb_pre, j * tn, lay["post_row"] + (j - nb_pre) * tn)
        return (pl.multiple_of(row, 16), 0)

    return pl.pallas_call(
        _rms_matmul_kernel,
        grid=(m // tm, nb_all),
        in_specs=[pl.BlockSpec((tm, d), lambda i, j: (i, 0)),
                  pl.BlockSpec((1, d), lambda i, j: (0, 0)),
                  pl.BlockSpec((pl.Element(tn), pl.Element(d)), w_row)],
        out_specs=pl.BlockSpec((tm, tn), lambda i, j: (i, j)),
        out_shape=jax.ShapeDtypeStruct((m, lay["total"]), F32),
        scratch_shapes=[pltpu.VMEM((tm, d), MXU_DT)],
        compiler_params=_cparams("parallel", "arbitrary"),
        name="in_proj",
    )(x, gain.reshape(1, d), wt)


def _ffn_up_kernel(x_ref, g_ref, wg_ref, wu_ref, o_ref, xn_ref):
    @pl.when(pl.program_id(1) == 0)
    def _():
        x = x_ref[...]
        ms = jnp.mean(x * x, axis=-1, keepdims=True)
        xn_ref[...] = (x * lax.rsqrt(ms + EPS) * g_ref[...]).astype(xn_ref.dtype)

    xn = xn_ref[...]
    o_ref[...] = (_silu(_dot(xn, wg_ref[...])) * _dot(xn, wu_ref[...])).astype(o_ref.dtype)


def _layer_cols(k, tn, layer):
    return pl.BlockSpec((None, k, tn), lambda i, j: (layer, 0, j))


def ffn_up(x, gain, wg, wu, layer, *, tm, tn):
    m, d = x.shape
    f = wg.shape[2]
    return pl.pallas_call(
        _ffn_up_kernel,
        grid=(m // tm, f // tn),
        in_specs=[pl.BlockSpec((tm, d), lambda i, j: (i, 0)),
                  pl.BlockSpec((1, d), lambda i, j: (0, 0)),
                  _layer_cols(d, tn, layer), _layer_cols(d, tn, layer)],
        out_specs=pl.BlockSpec((tm, tn), lambda i, j: (i, j)),
        out_shape=jax.ShapeDtypeStruct((m, f), MXU_DT),
        scratch_shapes=[pltpu.VMEM((tm, d), MXU_DT)],
        compiler_params=_cparams("parallel", "arbitrary"),
        name="ffn_up",
    )(x, gain.reshape(1, d), wg, wu)


def _matmul_res_kernel(a_ref, w_ref, r_ref, o_ref):
    o_ref[...] = r_ref[...] + _dot(a_ref[...], w_ref[...])


def matmul_res(a, w, layer, res, *, tm, tn):
    m, k = a.shape
    n = w.shape[2]
    return pl.pallas_call(
        _matmul_res_kernel,
        grid=(m // tm, n // tn),
        in_specs=[pl.BlockSpec((tm, k), lambda i, j: (i, 0)),
                  _layer_cols(k, tn, layer),
                  pl.BlockSpec((tm, tn), lambda i, j: (i, j))],
        out_specs=pl.BlockSpec((tm, tn), lambda i, j: (i, j)),
        out_shape=jax.ShapeDtypeStruct((m, n), F32),
        compiler_params=_cparams("parallel", "arbitrary"),
        name="matmul_res",
    )(a, w, res)


def _merge_kernel(oa_ref, ob_ref, oc_ref, ga_ref, gb_ref, gc_ref, wa_ref, wb_ref, wc_ref, o_ref):
    o_ref[...] = (_sigmoid(ga_ref[...]) * _dot(oa_ref[...], wa_ref[...])
                  + _sigmoid(gb_ref[...]) * _dot(ob_ref[...], wb_ref[...])
                  + _sigmoid(gc_ref[...]) * _dot(oc_ref[...], wc_ref[...])).astype(o_ref.dtype)


def merge_branches(h, lay, oa, ob, oc, wa, wb, wc, layer, *, tm, tn):
    m = h.shape[0]
    d = wa.shape[2]
    wdt = oa.shape[1]
    ca, cb, cc = ((lay["mg"] + i * d) // tn for i in range(3))
    row = lambda i, j: (i, 0)
    return pl.pallas_call(
        _merge_kernel,
        grid=(m // tm, d // tn),
        in_specs=[pl.BlockSpec((tm, wdt), row), pl.BlockSpec((tm, wdt), row), pl.BlockSpec((tm, wdt), row),
                  pl.BlockSpec((tm, tn), lambda i, j: (i, ca + j)),
                  pl.BlockSpec((tm, tn), lambda i, j: (i, cb + j)),
                  pl.BlockSpec((tm, tn), lambda i, j: (i, cc + j)),
                  _layer_cols(wdt, tn, layer), _layer_cols(wdt, tn, layer), _layer_cols(wdt, tn, layer)],
        out_specs=pl.BlockSpec((tm, tn), lambda i, j: (i, j)),
        out_shape=jax.ShapeDtypeStruct((m, d), MXU_DT),
        compiler_params=_cparams("parallel", "arbitrary"),
        name="merge",
    )(oa, ob, oc, h, h, h, wa, wb, wc)


def _hgrn_kernel(hq_ref, hf_ref, hi_ref, hg_ref, lb_ref, gain_ref, s0_ref, o_ref, sout_ref,
                 st_ref, ut_ref, *, tt, t_valid):
    c = HG_SUB
    nj = tt // c
    tb = pl.program_id(2)

    @pl.when(tb == 0)
    def _():
        st_ref[...] = _transpose_exact(s0_ref[0, 0])

    hq = hq_ref[0]
    lb = lb_ref[...]
    q = _silu(hq)
    sig = _sigmoid(hf_ref[0])
    g = jnp.log(jnp.maximum(lb + (1.0 - lb) * sig, MIN_F))
    kk = (1.0 - lb) * (1.0 - sig)
    v = hi_ref[0]
    row = _iota((tt, HG_DK), 0)
    if t_valid is not None:
        live = (tb * tt + row) < t_valid
        g = jnp.where(live, g, 0.0)
        kk = jnp.where(live, kk, 0.0)

    r2 = _iota((tt, tt), 0)
    c2 = _iota((tt, tt), 1)
    same = (r2 >> HG_SUB_SHIFT) == (c2 >> HG_SUB_SHIFT)
    gp = _split(g, 3)
    tri = (same & (c2 <= r2)).astype(MXU_DT)
    blk = same.astype(MXU_DT)
    b = sum(_dot(tri, p) for p in gp)
    dtot = sum(_dot(blk, p) for p in gp)
    qe = q * jnp.exp(b)
    ke = kk * jnp.exp(dtot - b)
    edec = jnp.exp(dtot)

    q3 = q.reshape(nj, c, HG_DK)
    kk3 = kk.reshape(nj, c, HG_DK)
    b3 = b.reshape(nj, c, HG_DK)
    v3 = v.reshape(nj, c, HG_DV)
    half = c // 2
    tpos = _iota((nj, half, HG_DK), 1)
    ones = jnp.ones((HG_DK, HG_DV), MXU_DT)

    def contrib(rows, s, first_visible):
        diff = b3[:, rows, :] - b3[:, s:s + 1, :]
        if first_visible is not None:
            m = tpos >= first_visible
            diff = jnp.where(m, diff, 0.0)
        y = q3[:, rows, :] * kk3[:, s:s + 1, :] * jnp.exp(diff)
        if first_visible is not None:
            y = jnp.where(m, y, 0.0)
        z = _dot(y.reshape(nj * half, HG_DK).astype(MXU_DT), ones).reshape(nj, half, HG_DV)
        return z * v3[:, s:s + 1, :]

    lower, upper = slice(0, half), slice(half, c)
    o_lo = jnp.zeros((nj, half, HG_DV), F32)
    o_up = jnp.zeros((nj, half, HG_DV), F32)
    for s in range(c):
        if s < half:
            o_lo = o_lo + contrib(lower, s, s)
            o_up = o_up + contrib(upper, s, None)
        else:
            o_up = o_up + contrib(upper, s, s - half)
    o = jnp.concatenate([o_lo, o_up], axis=1).reshape(tt, HG_DV)

    v_b = v.astype(MXU_DT)
    ke_b = ke.astype(MXU_DT)
    for j in range(nj):
        ut_ref[j] = lax.dot_general(v_b[j * c:(j + 1) * c], ke_b[j * c:(j + 1) * c],
                                    (((0,), (0,)), ((), ())), preferred_element_type=F32)

    qe_b = qe.astype(MXU_DT)
    outs = []
    for j in range(nj):
        st = st_ref[...]
        outs.append(_dot_nt(qe_b[j * c:(j + 1) * c], st.astype(MXU_DT)))
        st_ref[...] = st * edec[j * c:j * c + 1, :] + ut_ref[j]
    o = o + jnp.concatenate(outs, axis=0)

    on = o * lax.rsqrt(jnp.mean(o * o, axis=-1, keepdims=True) + EPS) * gain_ref[...]
    o_ref[0] = (on * _silu(hg_ref[0])).astype(o_ref.dtype)

    @pl.when(tb == pl.num_programs(2) - 1)
    def _():
        sout_ref[0, 0] = _transpose_exact(st_ref[...])


def hgrn_mixer(h3, lay, lb, out_gain, s0, *, tt, t_valid):
    bsz, t, _ = h3.shape
    cq, cf, ci, cg = (lay[k] // LANES for k in ("hq", "hf", "hi", "hg"))
    col = lambda c0: (lambda b, hd, tb: (b, tb, c0 + hd))
    kern = functools.partial(_hgrn_kernel, tt=tt, t_valid=t_valid)
    return pl.pallas_call(
        kern,
        grid=(bsz, HG_HEADS, t // tt),
        in_specs=[pl.BlockSpec((1, tt, LANES), col(cq)), pl.BlockSpec((1, tt, LANES), col(cf)),
                  pl.BlockSpec((1, tt, LANES), col(ci)), pl.BlockSpec((1, tt, LANES), col(cg)),
                  pl.BlockSpec((1, HG_DK), lambda b, hd, tb: (0, hd)),
                  pl.BlockSpec((1, HG_DV), lambda b, hd, tb: (0, 0)),
                  pl.BlockSpec((1, 1, HG_DK, HG_DV), lambda b, hd, tb: (b, hd, 0, 0))],
        out_specs=[pl.BlockSpec((1, tt, HG_DV), lambda b, hd, tb: (b, tb, hd)),
                   pl.BlockSpec((1, 1, HG_DK, HG_DV), lambda b, hd, tb: (b, hd, 0, 0))],
        out_shape=[jax.ShapeDtypeStruct((bsz, t, HG_HEADS * HG_DV), MXU_DT),
                   jax.ShapeDtypeStruct((bsz, HG_HEADS, HG_DK, HG_DV), F32)],
        scratch_shapes=[pltpu.VMEM((HG_DV, HG_DK), F32),
                        pltpu.VMEM((tt // HG_SUB, HG_DV, HG_DK), F32)],
        compiler_params=_cparams("parallel", "parallel", "arbitrary"),
        name="hgrn",
    )(h3, h3, h3, h3, lb, out_gain.reshape(1, HG_DV), s0)


def _ret_kernel(q_ref, k_ref, v_ref, g_ref, lg_ref, s0_ref, o_ref, sout_ref, s_ref, *, tt, n_valid):
    tb = pl.program_id(2)

    @pl.when(tb == 0)
    def _():
        s_ref[...] = s0_ref[0]

    rel = (_iota((tt, tt), 0) - _iota((tt, tt), 1)).astype(F32)
    spos = _iota((RET_DK, tt), 1)
    for hh in range(RET_HPS):
        cols = slice(hh * RET_DK, (hh + 1) * RET_DK)
        lgl = lg_ref[hh][:, :LANES]
        lgt = lg_ref[hh][:, :tt]
        q = q_ref[0][:, cols] * (RET_DK ** -0.5)
        k = k_ref[0][:, cols]
        vb = v_ref[0][:, cols].astype(MXU_DT)
        dmat = jnp.where(rel >= 0, jnp.exp(jnp.maximum(rel, 0.0) * lgt), 0.0)
        att = _dot_nt(q.astype(MXU_DT), k.astype(MXU_DT)) * dmat
        q_dec = jnp.exp((_iota((tt, RET_DK), 0) + 1).astype(F32) * lgl)
        s = s_ref[hh]
        o = _dot(att.astype(MXU_DT), vb) + _dot((q * q_dec).astype(MXU_DT), s.astype(MXU_DT))

        k_dec = jnp.where(spos < n_valid, jnp.exp(jnp.maximum(n_valid - 1 - spos, 0).astype(F32) * lgt), 0.0)
        kt = _transpose_exact(k)
        s_ref[hh] = jnp.exp(float(n_valid) * lgl) * s + _dot((kt * k_dec).astype(MXU_DT), vb)

        mu = jnp.mean(o, axis=-1, keepdims=True)
        var = jnp.mean(jnp.square(o - mu), axis=-1, keepdims=True)
        o_ref[0, :, cols] = ((o - mu) * lax.rsqrt(var + EPS) * _silu(g_ref[0][:, cols])).astype(o_ref.dtype)

    @pl.when(tb == pl.num_programs(2) - 1)
    def _():
        sout_ref[0] = s_ref[...]


def retention_mixer(h3, lay, lg_tab, s0, *, tt, n_valid):
    bsz, t, _ = h3.shape
    assert n_valid == tt or t == tt
    hps = RET_HPS
    wd = hps * RET_DK
    cq, ck, cv, cg = (lay[k] // wd for k in ("rq", "rk", "rv", "rg"))
    col = lambda c0: (lambda b, hp, tb: (b, tb, c0 + hp))
    kern = functools.partial(_ret_kernel, tt=tt, n_valid=n_valid)
    return pl.pallas_call(
        kern,
        grid=(bsz, RET_HEADS // hps, t // tt),
        in_specs=[pl.BlockSpec((1, tt, wd), col(cq)), pl.BlockSpec((1, tt, wd), col(ck)),
                  pl.BlockSpec((1, tt, wd), col(cv)), pl.BlockSpec((1, tt, wd), col(cg)),
                  pl.BlockSpec((hps, 1, lg_tab.shape[2]), lambda b, hp, tb: (hp, 0, 0)),
                  pl.BlockSpec((1, hps, RET_DK, RET_DV), lambda b, hp, tb: (b, hp, 0, 0))],
        out_specs=[pl.BlockSpec((1, tt, hps * RET_DV), lambda b, hp, tb: (b, tb, hp)),
                   pl.BlockSpec((1, hps, RET_DK, RET_DV), lambda b, hp, tb: (b, hp, 0, 0))],
        out_shape=[jax.ShapeDtypeStruct((bsz, t, RET_HEADS * RET_DV), MXU_DT),
                   jax.ShapeDtypeStruct((bsz, RET_HEADS, RET_DK, RET_DV), F32)],
        scratch_shapes=[pltpu.VMEM((hps, RET_DK, RET_DV), F32)],
        compiler_params=_cparams("parallel", "parallel", "arbitrary"),
        name="retention",
    )(h3, h3, h3, h3, lg_tab, s0)


def _store_heads(ref, x, lead=()):
    for hh in range(NSA_KV_HEADS):
        ref[lead + (hh,)] = x[:, hh * NSA_HEAD_DIM:(hh + 1) * NSA_HEAD_DIM].astype(ref.dtype)


def _q_prep_kernel(q_ref, gain_ref, bsum_ref, o_ref):
    x = q_ref[0]
    hd = NSA_HEAD_DIM
    for kvh in range(NSA_KV_HEADS):
        lo = kvh * NSA_KV_WIDTH
        qn = _head_rms(x[:, lo:lo + NSA_KV_WIDTH], bsum_ref[...], gain_ref[...]) * (hd ** -0.5)
        for g in range(NSA_GROUP):
            o_ref[0, kvh, g, :, 0:hd] = qn[:, g * hd:(g + 1) * hd].astype(o_ref.dtype)
            if o_ref.shape[-1] > hd:
                o_ref[0, kvh, g, :, hd:] = jnp.zeros((x.shape[0], o_ref.shape[-1] - hd), o_ref.dtype)


def q_prep(h3, lay, gain_q, bsum, *, tr, width=NSA_HEAD_DIM):
    bsz, t, _ = h3.shape
    cq = lay["nq"] // NSA_WIDTH
    return pl.pallas_call(
        _q_prep_kernel,
        grid=(bsz, t // tr),
        in_specs=[pl.BlockSpec((1, tr, NSA_WIDTH), lambda b, i: (b, i, cq)),
                  pl.BlockSpec((1, NSA_KV_WIDTH), lambda b, i: (0, 0)),
                  pl.BlockSpec((NSA_KV_WIDTH, NSA_KV_WIDTH), lambda b, i: (0, 0))],
        out_specs=pl.BlockSpec((1, NSA_KV_HEADS, NSA_GROUP, tr, width), lambda b, i: (b, 0, 0, i, 0)),
        out_shape=jax.ShapeDtypeStruct((bsz, NSA_KV_HEADS, NSA_GROUP, t, width), MXU_DT),
        compiler_params=_cparams("parallel", "parallel"),
        name="nsa_q_prep",
    )(h3, jnp.tile(gain_q, NSA_KV_HEADS).reshape(1, NSA_KV_WIDTH), bsum)


def _kv_prep_kernel(x_ref, gain_ref, bsum_ref, *rest):
    ko_ref, vo_ref = rest[-2:]
    hd = NSA_HEAD_DIM
    x = x_ref[0]
    kn = _head_rms(x[:, :NSA_KV_WIDTH], bsum_ref[...], gain_ref[...])
    for hh in range(NSA_KV_HEADS):
        ko_ref[0, hh, :, 0:hd] = kn[:, hh * hd:(hh + 1) * hd].astype(ko_ref.dtype)
        if len(rest) == 3:
            ko_ref[0, hh, :, hd:] = rest[0][:, hd:]
        lo = NSA_KV_WIDTH + hh * hd
        vo_ref[0, hh, :, 0:hd] = x[:, lo:lo + hd].astype(vo_ref.dtype)
        if vo_ref.shape[-1] > hd:
            vo_ref[0, hh, :, hd:] = jnp.ones((x.shape[0], vo_ref.shape[-1] - hd), vo_ref.dtype)


def kv_prep(x3, col, gain, bsum, *, tr, aug=None):
    bsz, t, _ = x3.shape
    w2 = 2 * NSA_KV_WIDTH
    hd = NSA_HEAD_DIM
    kw = hd if aug is None else aug.shape[1]
    vw = hd if aug is None else LANES
    vshape = jax.ShapeDtypeStruct((bsz, NSA_KV_HEADS, t, vw), MXU_DT)
    kshape = jax.ShapeDtypeStruct((bsz, NSA_KV_HEADS, t, kw), MXU_DT)
    spec = lambda w: pl.BlockSpec((1, NSA_KV_HEADS, tr, w), lambda b, i: (b, 0, i, 0))
    in_specs = [pl.BlockSpec((1, tr, w2), lambda b, i: (b, i, col)),
                pl.BlockSpec((1, NSA_KV_WIDTH), lambda b, i: (0, 0)),
                pl.BlockSpec((NSA_KV_WIDTH, NSA_KV_WIDTH), lambda b, i: (0, 0))]
    args = [x3, jnp.tile(gain, NSA_KV_HEADS).reshape(1, NSA_KV_WIDTH), bsum]
    if aug is not None:
        in_specs.append(pl.BlockSpec((tr, kw), lambda b, i: (i, 0)))
        args.append(aug)
    return pl.pallas_call(
        _kv_prep_kernel,
        grid=(bsz, t // tr),
        in_specs=in_specs,
        out_specs=[spec(kw), spec(vw)],
        out_shape=[kshape, vshape],
        compiler_params=_cparams("parallel", "parallel"),
        name="nsa_kv_prep",
    )(*args)


def _head_rms_t(xt, gain_t):
    x3 = xt.reshape(NSA_KV_HEADS, NSA_HEAD_DIM, xt.shape[1])
    ms = jnp.mean(x3 * x3, axis=1, keepdims=True)
    return (x3 * lax.rsqrt(ms + EPS)).reshape(xt.shape) * gain_t


def _gain_t(gain):
    return jnp.broadcast_to(jnp.tile(gain, NSA_KV_HEADS)[:, None], (NSA_KV_WIDTH, LANES))


def _cmp_prep_kernel(x_ref, nxt_ref, tail_ref, w_ref, gain_ref, bsum_ref, kc_ref, vc_ref, *, rows):
    cs = CMP_STRIDE
    nb = rows // cs
    w0 = w_ref[0]
    w1 = w_ref[1]
    x3 = x_ref[0].reshape(nb, cs, 2 * NSA_KV_WIDTH)
    a0 = jnp.sum(x3 * w0[None], axis=1)
    a1 = jnp.sum(x3 * w1[None], axis=1)
    last = pl.program_id(1) == pl.num_programs(1) - 1
    nx = jnp.where(last, tail_ref[0], nxt_ref[0])
    a1_next = jnp.sum(nx * w1, axis=0, keepdims=True)
    a1s = pltpu.roll(a1, nb - 1, 0)
    a1s = jnp.where(_iota(a1s.shape, 0) == nb - 1, a1_next, a1s)
    comp = a0 + a1s
    kc = _head_rms(comp[:, :NSA_KV_WIDTH], bsum_ref[...], gain_ref[...])
    _store_heads(kc_ref, kc, (0,))
    _store_heads(vc_ref, comp[:, NSA_KV_WIDTH:], (0,))


def _cmp_weight_table(cmp_w):
    w = cmp_w.reshape(2, CMP_BLOCK // CMP_STRIDE, CMP_STRIDE)
    w = jnp.transpose(w, (1, 2, 0))
    return jnp.repeat(w, NSA_KV_WIDTH, axis=2)


def cmp_prep(x3, col, tail16, cmp_w, gain, bsum, *, tr):
    bsz, t, _ = x3.shape
    w2 = 2 * NSA_KV_WIDTH
    nbt = tr // CMP_STRIDE
    n_steps = t // tr
    kern = functools.partial(_cmp_prep_kernel, rows=tr)
    oshape = jax.ShapeDtypeStruct((bsz, NSA_KV_HEADS, t // CMP_STRIDE, NSA_HEAD_DIM), MXU_DT)
    ospec = pl.BlockSpec((1, NSA_KV_HEADS, nbt, NSA_HEAD_DIM), lambda b, i: (b, 0, i, 0))
    chunks_per_step = tr // CMP_STRIDE
    return pl.pallas_call(
        kern,
        grid=(bsz, n_steps),
        in_specs=[pl.BlockSpec((1, tr, w2), lambda b, i: (b, i, col)),
                  pl.BlockSpec((1, CMP_STRIDE, w2),
                               lambda b, i: (b, jnp.minimum(i + 1, n_steps - 1) * chunks_per_step, col)),
                  pl.BlockSpec((1, CMP_STRIDE, w2), lambda b, i: (b, 0, 0)),
                  pl.BlockSpec((2, CMP_STRIDE, w2), lambda b, i: (0, 0, 0)),
                  pl.BlockSpec((1, NSA_KV_WIDTH), lambda b, i: (0, 0)),
                  pl.BlockSpec((NSA_KV_WIDTH, NSA_KV_WIDTH), lambda b, i: (0, 0))],
        out_specs=[ospec, ospec],
        out_shape=[oshape, oshape],
        compiler_params=_cparams("parallel", "arbitrary"),
        name="nsa_cmp_prep",
    )(x3, x3, tail16, _cmp_weight_table(cmp_w), jnp.tile(gain, NSA_KV_HEADS).reshape(1, NSA_KV_WIDTH), bsum)


def _cmp_prep_t_kernel(*refs, n_in):
    refs = refs[-(n_in + 6):]
    nxt_ref, tail_ref, w_ref, gain_ref, kc_ref, vc_ref = refs[n_in:]
    nb = w_ref.shape[2] // 4
    kvw = NSA_KV_WIDTH

    last = pl.program_id(1) == pl.num_programs(1) - 1
    nxt = jnp.where(last, tail_ref[0], nxt_ref[0])
    rows = [refs[u][0].reshape(2 * kvw, PAGE_SIZE).astype(MXU_DT) for u in range(n_in)]
    rows.append(nxt.reshape(2 * kvw, PAGE_SIZE).astype(MXU_DT))
    acc = _dot(jnp.concatenate(rows, axis=1), w_ref[...].reshape((n_in + 1) * PAGE_SIZE, 4 * nb))
    comp_k = acc[:kvw, 0:nb] + acc[:kvw, 2 * nb:3 * nb]
    comp_v = acc[kvw:, nb:2 * nb] + acc[kvw:, 3 * nb:4 * nb]
    kct = _head_rms_t(comp_k, gain_ref[...][:, :nb]).astype(MXU_DT)
    eye = _eye(nb)
    _store_heads(kc_ref, _dot_nt(eye, kct), (0,))
    _store_heads(vc_ref, _dot_nt(eye, comp_v.astype(MXU_DT)), (0,))


def _cmp_band_tables(cmp_w, n_pages_step):
    cs = CMP_STRIDE
    w = cmp_w.reshape(2, CMP_BLOCK // cs, cs)
    nb = n_pages_step * PAGE_SIZE // cs
    rho = np.arange(PAGE_SIZE)
    ch = (np.arange(n_pages_step)[:, None] * (PAGE_SIZE // cs) + rho[None, :] // cs)[..., None]
    n = np.arange(nb)[None, None, :]
    first = (ch == n).astype(np.float32)
    second = (ch == n + 1).astype(np.float32)
    nxt_rows = ((rho[:, None] < cs) & (n[0] == nb - 1)).astype(np.float32)
    tabs = []
    for c in range(2):
        w0 = jnp.tile(w[c, 0], PAGE_SIZE // cs)
        w1 = jnp.tile(w[c, 1], PAGE_SIZE // cs)
        full = w0[None, :, None] * first + w1[None, :, None] * second
        nxt = w1[:, None] * nxt_rows
        tabs.append(jnp.concatenate([full, nxt[None]], axis=0))
    hi = [t.astype(MXU_DT) for t in tabs]
    lo = [(t - h.astype(F32)).astype(MXU_DT) for t, h in zip(tabs, hi)]
    return jnp.concatenate(hi + lo, axis=2)


def cmp_prep_paged(pool_t, pages, tail_t, cmp_w, gain):
    bsz, n_pages = pages.shape
    pp = PAGES_PER_STEP
    n_steps = n_pages // pp
    nbt = pp * PAGE_SIZE // CMP_STRIDE
    kern = functools.partial(_cmp_prep_t_kernel, n_in=pp)
    oshape = jax.ShapeDtypeStruct((bsz, NSA_KV_HEADS, n_pages * PAGE_SIZE // CMP_STRIDE, NSA_HEAD_DIM), MXU_DT)
    ospec = pl.BlockSpec((1, NSA_KV_HEADS, nbt, NSA_HEAD_DIM), lambda b, i, pt: (b, 0, i, 0))
    pblock = (1, 2, NSA_KV_WIDTH, PAGE_SIZE)
    page_spec = lambda u: pl.BlockSpec(pblock, lambda b, i, pt: (pt[b, i * pp + u], 0, 0, 0))
    const = lambda *shape: pl.BlockSpec(shape, lambda b, i, pt: (0,) * len(shape))
    return pl.pallas_call(
        kern,
        grid_spec=pltpu.PrefetchScalarGridSpec(
            num_scalar_prefetch=1,
            grid=(bsz, n_steps),
            in_specs=[page_spec(u) for u in range(pp)]
                     + [pl.BlockSpec(pblock, lambda b, i, pt: (pt[b, jnp.minimum(i + 1, n_steps - 1) * pp], 0, 0, 0)),
                        pl.BlockSpec(pblock, lambda b, i, pt: (b, 0, 0, 0)),
                        const(pp + 1, PAGE_SIZE, 4 * nbt), const(NSA_KV_WIDTH, LANES)],
            out_specs=[ospec, ospec]),
        out_shape=[oshape, oshape],
        compiler_params=_cparams("parallel", "arbitrary"),
        name="nsa_cmp_prep_paged",
    )(pages, *([pool_t] * pp), pool_t, tail_t, _cmp_band_tables(cmp_w, pp), _gain_t(gain))


def _gate_and_store(o_ref, br_ref, nbg_ref, gx_ref, branches, tq):
    hd = NSA_HEAD_DIM
    for br, o in enumerate(branches):
        for g in range(NSA_GROUP):
            br_ref[br, :, g * hd:(g + 1) * hd] = o[g * tq:(g + 1) * tq]
    pieces = _split(_sigmoid(nbg_ref[0]), 2)
    acc = None
    for br in range(3):
        term = sum(_dot(p, gx_ref[br]) for p in pieces) * br_ref[br]
        acc = term if acc is None else acc + term
    o_ref[0] = acc.astype(o_ref.dtype)


def _imp_to_sel_matrix(n_sel_pad, nbp):
    m = SEL_BLOCK // CMP_STRIDE
    r = CMP_BLOCK // CMP_STRIDE
    jj = np.arange(n_sel_pad)[:, None]
    ii = np.arange(nbp)[None, :]
    cnt = sum((((ii - rr) >= m * jj) & ((ii - rr) < m * (jj + 1))).astype(np.float32) for rr in range(r))
    return cnt.astype(MXU_DT)


def _topk_block_mask(p, at_ref, t0, *, tq, tl, n_sel, topk):
    n_sel_pad, nbp = at_ref.shape
    imp = p[0:tq] + p[tq:2 * tq] + p[2 * tq:3 * tq] + p[3 * tq:4 * tq]
    if tl > tq:
        imp = jnp.concatenate([imp, jnp.zeros((tl - tq, nbp), F32)], axis=0)
    imp_sel = sum(_dot_nt(at_ref[...], piece) for piece in _split(imp, 3))
    j = _iota((n_sel_pad, tl), 0)
    tt = t0 + _iota((n_sel_pad, tl), 1)
    forced = (j == 0) | (j == (tt >> 6))
    allowed = (j << 6) <= tt
    score = jnp.where(forced, POS_BIG, jnp.where(allowed, imp_sel, NEG_BIG))
    score = jnp.where(j >= n_sel, LOWEST, score)
    jf = j.astype(F32)
    sel = jnp.zeros((n_sel_pad, tl), F32)
    for _ in range(topk):
        mx = jnp.max(score, axis=0, keepdims=True)
        first = jnp.min(jnp.where(score == mx, jf, 1e9), axis=0, keepdims=True)
        pick = jf == first
        sel = jnp.where(pick, 1.0, sel)
        score = jnp.where(pick, -jnp.inf, score)
    return jnp.where(allowed, sel, 0.0)


def _nsa_prompt_kernel(q_ref, sl_ref, qts_ref, qtw_ref, kc_ref, vc_ref, ks_ref, vs_ref, kw_ref, vw_ref, at_ref,
                       nbg_ref, gx_ref, o_ref, m_ref, acc_ref, br_ref, *, tq, n_sel, topk):
    rows = NSA_GROUP * tq
    hd = NSA_HEAD_DIM
    qi = pl.program_id(2)
    t0 = qi * tq
    q128 = q_ref[0, 0].reshape(rows, LANES)
    slope = sl_ref[0][:, :1]
    nbp = kc_ref.shape[2]

    s = _dot_nt(q128[:, :hd], kc_ref[0, 0])
    tok = t0 + (_iota((rows, nbp), 0) & (tq - 1))
    dist = tok - (_iota((rows, nbp), 1) * CMP_STRIDE + (CMP_BLOCK - 1))
    valid = dist >= 0
    s = jnp.where(valid, s - slope * dist.astype(F32), NEG_BIG)
    p = jnp.where(valid, jnp.exp(s - jnp.max(s, axis=-1, keepdims=True)), 0.0)
    p = p / jnp.maximum(jnp.sum(p, axis=-1, keepdims=True), 1e-30)
    o_cmp = _dot(p.astype(MXU_DT), vc_ref[0, 0])

    def normalised(acc):
        return acc[:, :hd] / jnp.maximum(acc[:, hd:], 1e-30)

    lane = _iota((rows, LANES), 1)
    qw = jnp.where(lane < hd, q128, qtw_ref[0])
    span = WINDOW + tq
    ws = pl.multiple_of(jnp.maximum(t0 - WINDOW, 0), tq)
    sw = _dot_nt(qw, kw_ref[0, 0, pl.ds(ws, span), :])
    d = (_iota((rows, span), 0) & (tq - 1)) - _iota((rows, span), 1) + (t0 - ws)
    sw = jnp.where((d >= 0) & (d < WINDOW), sw, NEG_BIG)
    pw = jnp.exp(sw - jnp.max(sw, axis=-1, keepdims=True))
    o_win = normalised(_dot(pw.astype(MXU_DT), vw_ref[0, 0, pl.ds(ws, span), :]))

    sel = _topk_block_mask(p, at_ref, t0, tq=tq, tl=tq, n_sel=n_sel, topk=topk)
    sel_t = jnp.concatenate([jnp.zeros_like(sel), sel], axis=0).astype(MXU_DT)
    spread = ((_iota((rows, tq), 0) & (tq - 1)) == _iota((rows, tq), 1)).astype(MXU_DT)
    selb = _dot_nt(spread, sel_t)
    neg = jnp.where((lane >= hd) & (selb < 0.5), NEG_BIG, 0.0)
    qa = jnp.concatenate([(q128.astype(F32) + neg).astype(MXU_DT), qts_ref[0]], axis=1)

    tk = PROMPT_SEL_TK
    kd = t0 // tk
    k0 = pl.multiple_of(kd * tk, tk)
    s = _dot_nt(qa, ks_ref[0, 0, pl.ds(k0, tk), :])
    causal = ((_iota((rows, tk), 0) & (tq - 1)) - _iota((rows, tk), 1) + (t0 - k0)) >= 0
    s = jnp.where(causal, s, NEG_BIG)
    m = jnp.max(s, axis=-1, keepdims=True)
    m_ref[...] = jnp.broadcast_to(m, m_ref.shape)
    acc_ref[...] = _dot(jnp.exp(s - m).astype(MXU_DT), vs_ref[0, 0, pl.ds(k0, tk), :])

    def sel_body(kt, carry):
        ka = pl.multiple_of(kt * tk, tk)
        s = _dot_nt(qa, ks_ref[0, 0, pl.ds(ka, tk), :])
        m_prev = m_ref[...]
        m_new = jnp.maximum(m_prev, jnp.max(s, axis=-1, keepdims=True))
        p = jnp.exp(s - jnp.concatenate([m_new] * (tk // LANES), axis=1))
        acc_ref[...] = jnp.exp(m_prev - m_new) * acc_ref[...] + _dot(p.astype(MXU_DT), vs_ref[0, 0, pl.ds(ka, tk), :])
        m_ref[...] = m_new
        return carry

    lax.fori_loop(0, kd, sel_body, 0)
    o_sel = normalised(acc_ref[...])

    _gate_and_store(o_ref, br_ref, nbg_ref, gx_ref, (o_cmp, o_sel, o_win), tq)


def _slope_digits(tq, tile, lane0):
    pieces, r = [], _alibi_slopes()
    for _ in range(3):
        p = r.astype(MXU_DT).astype(np.float32)
        pieces.append(p)
        r = r - p
    pieces = np.stack(pieces, axis=1)
    tab = np.zeros((NSA_HEADS, LANES), np.float32)
    tab[:, lane0:lane0 + 6] = np.concatenate([pieces * float(tile), pieces], axis=1)
    tab = np.broadcast_to(tab.reshape(NSA_KV_HEADS, NSA_GROUP, 1, LANES), (NSA_KV_HEADS, NSA_GROUP, tq, LANES))
    return tab.reshape(NSA_KV_HEADS, NSA_GROUP * tq, LANES).astype(MXU_DT)


def _key_digits(t, tile, lane0, width, onehot):
    r = np.arange(t)
    lane = np.arange(width)[None, :]
    tab = np.zeros((t, width), np.float32)
    if onehot:
        tab = np.where(lane - NSA_HEAD_DIM == (r // SEL_BLOCK)[:, None], 1.0, tab)
    hi = (r // tile).astype(np.float32)[:, None]
    lo = (r % tile).astype(np.float32)[:, None]
    tab = np.where((lane >= lane0) & (lane < lane0 + 3), hi, tab)
    tab = np.where((lane >= lane0 + 3) & (lane < lane0 + 6), lo, tab)
    return tab.astype(MXU_DT)


def nsa_attention_prompt(q128, slopes, kc, vc, ks_aug, vs, kw_aug, vw, h3, nbg_col, gate_expand, *, n_sel):
    bsz, _, _, t, _ = q128.shape
    tq = PROMPT_TQ
    hd = NSA_HEAD_DIM
    nbp = kc.shape[2]
    assert n_sel <= 64 and t % PROMPT_SEL_TK == 0 and t >= WINDOW + tq and PROMPT_SEL_TK % tq == 0
    rows = NSA_GROUP * tq
    at = _imp_to_sel_matrix(64, nbp)
    kern = functools.partial(_nsa_prompt_kernel, tq=tq, n_sel=n_sel, topk=min(NSA_TOPK, n_sel))
    full = lambda n, w: pl.BlockSpec((1, 1, n, w), lambda b, kh, i: (b, kh, 0, 0))
    per_head = pl.BlockSpec((1, rows, LANES), lambda b, kh, i: (kh, 0, 0))
    gw = NSA_GROUP * hd
    return pl.pallas_call(
        kern,
        grid=(bsz, NSA_KV_HEADS, t // tq),
        in_specs=[pl.BlockSpec((1, 1, NSA_GROUP, tq, LANES), lambda b, kh, i: (b, kh, 0, i, 0)),
                  per_head, per_head, per_head,
                  full(nbp, hd), full(nbp, hd), full(t, 2 * LANES), full(t, LANES), full(t, LANES), full(t, LANES),
                  pl.BlockSpec((64, nbp), lambda b, kh, i: (0, 0)),
                  pl.BlockSpec((1, tq, LANES), lambda b, kh, i: (b, i, nbg_col)),
                  pl.BlockSpec((3, LANES, gw), lambda b, kh, i: (0, 0, kh))],
        out_specs=pl.BlockSpec((1, tq, gw), lambda b, kh, i: (b, i, kh)),
        out_shape=jax.ShapeDtypeStruct((bsz, t, NSA_WIDTH), MXU_DT),
        scratch_shapes=[pltpu.VMEM((rows, LANES), F32), pltpu.VMEM((rows, LANES), F32),
                        pltpu.VMEM((3, tq, gw), F32)],
        compiler_params=_cparams("parallel", "parallel", "arbitrary"),
        name="nsa_attention_prompt",
    )(q128, slopes, _slope_digits(tq, SEL_DIGIT, 0), _slope_digits(tq, WIN_DIGIT, hd),
      kc, vc, ks_aug, vs, kw_aug, vw, at, h3, gate_expand)


def _nsa_paged_pre_kernel(q_ref, sl_ref, kc_ref, vc_ref, kw_ref, vw_ref, at_ref,
                          ocmp_ref, owin_ref, sel_ref, *, tq, qpos0, wpos0, n_sel, topk):
    rows = NSA_GROUP * tq
    hd = NSA_HEAD_DIM
    q = q_ref[0, 0].reshape(rows, hd)
    slope = sl_ref[0][:, :1]
    nbp = kc_ref.shape[2]
    tw = kw_ref.shape[2]

    s = _dot_nt(q, kc_ref[0, 0])
    tok = qpos0 + (_iota((rows, nbp), 0) & (tq - 1))
    dist = tok - (_iota((rows, nbp), 1) * CMP_STRIDE + (CMP_BLOCK - 1))
    valid = dist >= 0
    s = jnp.where(valid, s - slope * dist.astype(F32), NEG_BIG)
    p = jnp.where(valid, jnp.exp(s - jnp.max(s, axis=-1, keepdims=True)), 0.0)
    p = p / jnp.maximum(jnp.sum(p, axis=-1, keepdims=True), 1e-30)
    o_cmp = _dot(p.astype(MXU_DT), vc_ref[0, 0])
    sel_ref[0, 0] = _topk_block_mask(p, at_ref, qpos0, tq=tq, tl=sel_ref.shape[3], n_sel=n_sel, topk=topk)

    sw = _dot_nt(q, kw_ref[0, 0])
    d = (qpos0 - wpos0) + (_iota((rows, tw), 0) & (tq - 1)) - _iota((rows, tw), 1)
    valid = (d >= 0) & (d < WINDOW)
    sw = jnp.where(valid, sw - slope * d.astype(F32), NEG_BIG)
    pw = jnp.where(valid, jnp.exp(sw - jnp.max(sw, axis=-1, keepdims=True)), 0.0)
    o_win = _dot(pw.astype(MXU_DT), vw_ref[0, 0]) / jnp.maximum(jnp.sum(pw, axis=-1, keepdims=True), 1e-30)

    for g in range(NSA_GROUP):
        ocmp_ref[0, :, g * hd:(g + 1) * hd] = o_cmp[g * tq:(g + 1) * tq]
        owin_ref[0, :, g * hd:(g + 1) * hd] = o_win[g * tq:(g + 1) * tq]


def _nsa_paged_sel_kernel(*refs, n_in, tq, qpos0, n_tiles):
    hit_ref = refs[1]
    refs = refs[2:]
    q_ref, sl_ref, sel_ref = refs[:3]
    pages = refs[3:3 + n_in]
    (ktn_ref, vtn_ref, gain_ref, ocmp_ref, owin_ref, nbg_ref, gx_ref,
     o_ref, m_ref, l_ref, acc_ref, kt_ref, vt_ref, br_ref) = refs[3 + n_in:]
    rows = NSA_GROUP * tq
    hd = NSA_HEAD_DIM
    tk = n_in * PAGE_SIZE
    nblk = tk // SEL_BLOCK
    i = pl.program_id(1)

    @pl.when(i == 0)
    def _():
        m_ref[...] = jnp.full(m_ref.shape, NEG_BIG, F32)
        l_ref[...] = jnp.zeros(l_ref.shape, F32)
        acc_ref[...] = jnp.zeros(acc_ref.shape, F32)

    @pl.when(i < n_tiles)
    def _():
        for u in range(n_in):
            sl = slice(u * PAGE_SIZE, (u + 1) * PAGE_SIZE)
            kt_ref[:, sl] = _head_rms_t(pages[u][0, 0], gain_ref[...]).astype(kt_ref.dtype)
            vt_ref[:, sl] = pages[u][0, 1].astype(vt_ref.dtype)

    @pl.when(i == n_tiles)
    def _():
        kt_ref[...] = ktn_ref[0]
        vt_ref[...] = vtn_ref[0]

    k0 = i * tk
    d = (qpos0 - k0) + (_iota((rows, tk), 0) & (tq - 1)) - _iota((rows, tk), 1)
    causal = d >= 0
    dist = d.astype(F32)
    spread = ((_iota((rows, LANES), 0) & (tq - 1)) == _iota((rows, LANES), 1)).astype(MXU_DT)
    in_tile = (_iota((nblk, tk), 0) == (_iota((nblk, tk), 1) >> 6)).astype(MXU_DT)
    blk0 = pl.multiple_of(i * nblk, nblk)
    reps = tk // LANES
    def head_update(kvh):
        hs = slice(kvh * hd, (kvh + 1) * hd)
        q = q_ref[0, kvh].reshape(rows, hd)
        s = _dot(q, kt_ref[hs, :])
        mine = _dot_nt(spread, sel_ref[0, kvh, pl.ds(blk0, nblk), :].astype(MXU_DT)).astype(MXU_DT)
        valid = (_dot(mine, in_tile) > 0.5) & causal
        s = jnp.where(valid, s - sl_ref[kvh][:, :1] * dist, NEG_BIG)
        m_prev = m_ref[kvh]
        m_new = jnp.maximum(m_prev, jnp.max(s, axis=-1, keepdims=True))
        alpha = jnp.exp(m_prev - m_new)
        p = jnp.where(valid, jnp.exp(s - jnp.concatenate([m_new] * reps, axis=1)), 0.0)
        l_ref[kvh] = alpha * l_ref[kvh] + jnp.sum(p, axis=-1, keepdims=True)
        acc_ref[kvh] = alpha[:, :hd] * acc_ref[kvh] + _dot_nt(p.astype(MXU_DT), vt_ref[hs, :])
        m_ref[kvh] = m_new

    for kvh in range(NSA_KV_HEADS):
        pl.when(hit_ref[pl.program_id(0), kvh * (n_tiles + 1) + i] > 0)(functools.partial(head_update, kvh))

    @pl.when(i == n_tiles)
    def _():
        br_ref[0] = ocmp_ref[0]
        br_ref[2] = owin_ref[0]
        for kvh in range(NSA_KV_HEADS):
            o_sel = acc_ref[kvh] / jnp.maximum(l_ref[kvh][:, :hd], 1e-30)
            for g in range(NSA_GROUP):
                c0 = (kvh * NSA_GROUP + g) * hd
                br_ref[1, :, c0:c0 + hd] = o_sel[g * tq:(g + 1) * tq]
        pieces = _split(_sigmoid(nbg_ref[0]), 2)
        acc = None
        for br in range(3):
            term = sum(_dot(p, gx_ref[br]) for p in pieces) * br_ref[br]
            acc = term if acc is None else acc + term
        o_ref[0] = acc.astype(o_ref.dtype)


def nsa_attention_paged(qn, slopes, kc, vc, kw, vw, pool_t, pages, kt_new, vt_new, gain_sel, h3, nbg_col,
                        gate_expand, *, tq, t_real, qpos0, wpos0, n_sel):
    bsz, _, _, tql, hd = qn.shape
    assert tql == tq
    nbp, tw = kc.shape[2], kw.shape[2]
    n_pages = pages.shape[1]
    tk = kt_new.shape[2]
    n_in = tk // PAGE_SIZE
    n_tiles = n_pages // n_in
    assert n_pages % n_in == 0 and qpos0 == n_pages * PAGE_SIZE
    n_sel_pad = -(-max(n_sel, (n_tiles + 1) * tk // SEL_BLOCK) // 64) * 64
    rows = NSA_GROUP * tq
    gw = NSA_GROUP * hd
    at = _imp_to_sel_matrix(n_sel_pad, nbp)
    full = lambda n: pl.BlockSpec((1, 1, n, hd), lambda b, kh: (b, kh, 0, 0))
    obr = pl.BlockSpec((1, tq, gw), lambda b, kh: (b, 0, kh))
    o_cmp, o_win, sel = pl.pallas_call(
        functools.partial(_nsa_paged_pre_kernel, tq=tq, qpos0=qpos0, wpos0=wpos0, n_sel=n_sel,
                          topk=min(NSA_TOPK, n_sel)),
        grid=(bsz, NSA_KV_HEADS),
        in_specs=[pl.BlockSpec((1, 1, NSA_GROUP, tq, hd), lambda b, kh: (b, kh, 0, 0, 0)),
                  pl.BlockSpec((1, rows, LANES), lambda b, kh: (kh, 0, 0)),
                  full(nbp), full(nbp), full(tw), full(tw),
                  pl.BlockSpec((n_sel_pad, nbp), lambda b, kh: (0, 0))],
        out_specs=[obr, obr, pl.BlockSpec((1, 1, n_sel_pad, LANES), lambda b, kh: (b, kh, 0, 0))],
        out_shape=[jax.ShapeDtypeStruct((bsz, tq, NSA_WIDTH), F32), jax.ShapeDtypeStruct((bsz, tq, NSA_WIDTH), F32),
                   jax.ShapeDtypeStruct((bsz, NSA_KV_HEADS, n_sel_pad, LANES), F32)],
        compiler_params=_cparams("parallel", "parallel"),
        name="nsa_paged_pre",
    )(qn, slopes, kc, vc, kw, vw, at)

    nblk = tk // SEL_BLOCK
    hit = sel[:, :, :(n_tiles + 1) * nblk, :t_real].reshape(bsz, NSA_KV_HEADS, n_tiles + 1, nblk * t_real)
    hit = (jnp.max(hit, axis=-1) > 0).astype(jnp.int32).reshape(bsz, NSA_KV_HEADS * (n_tiles + 1))

    page_spec = lambda u: pl.BlockSpec(
        (1, 2, NSA_KV_WIDTH, PAGE_SIZE),
        lambda b, i, pt, ht: (pt[b, jnp.minimum(i, n_tiles - 1) * n_in + u], 1, 0, 0))
    const = lambda *shape: pl.BlockSpec(shape, lambda b, i, pt, ht: (0,) * len(shape))
    per_seq = lambda *shape: pl.BlockSpec((1,) + shape, lambda b, i, pt, ht: (b,) + (0,) * len(shape))
    return pl.pallas_call(
        functools.partial(_nsa_paged_sel_kernel, n_in=n_in, tq=tq, qpos0=qpos0, n_tiles=n_tiles),
        grid_spec=pltpu.PrefetchScalarGridSpec(
            num_scalar_prefetch=2,
            grid=(bsz, n_tiles + 1),
            in_specs=[per_seq(NSA_KV_HEADS, NSA_GROUP, tq, hd), const(NSA_KV_HEADS, rows, LANES),
                      per_seq(NSA_KV_HEADS, n_sel_pad, LANES)]
                     + [page_spec(u) for u in range(n_in)]
                     + [per_seq(NSA_KV_WIDTH, tk), per_seq(NSA_KV_WIDTH, tk), const(NSA_KV_WIDTH, LANES),
                        per_seq(tq, NSA_WIDTH), per_seq(tq, NSA_WIDTH),
                        pl.BlockSpec((1, tq, LANES), lambda b, i, pt, ht: (b, 0, nbg_col)),
                        const(3, LANES, NSA_WIDTH)],
            out_specs=per_seq(tq, NSA_WIDTH),
            scratch_shapes=[pltpu.VMEM((NSA_KV_HEADS, rows, LANES), F32), pltpu.VMEM((NSA_KV_HEADS, rows, LANES), F32),
                            pltpu.VMEM((NSA_KV_HEADS, rows, hd), F32),
                            pltpu.VMEM((NSA_KV_WIDTH, tk), MXU_DT), pltpu.VMEM((NSA_KV_WIDTH, tk), MXU_DT),
                            pltpu.VMEM((3, tq, NSA_WIDTH), F32)]),
        out_shape=jax.ShapeDtypeStruct((bsz, tq, NSA_WIDTH), MXU_DT),
        compiler_params=_cparams("parallel", "arbitrary"),
        name="nsa_paged_sel",
    )(pages, hit, qn, slopes, sel, *([pool_t] * n_in), kt_new, vt_new, _gain_t(gain_sel),
      o_cmp, o_win, h3, gate_expand)


def _layout(d_model):
    tn = IN_PROJ_TN
    hw, rw = HG_HEADS * HG_DK, RET_HEADS * RET_DK
    gates = 3 * NSA_HEADS
    lay, off = {}, 0
    for name, width in [("hq", hw), ("hf", hw), ("hi", hw), ("hg", hw), ("nq", NSA_WIDTH),
                        ("nkv", 4 * NSA_KV_WIDTH), ("nwkv", 2 * NSA_KV_WIDTH), ("nbg", gates)]:
        lay[name] = off
        off += width
    lay["post_row"] = off
    lay["post_col"] = off = -(-off // tn) * tn
    for name, width in [("rq", rw), ("rk", rw), ("rv", rw), ("rg", rw), ("mg", 3 * d_model)]:
        lay[name] = off
        off += width
    assert lay["nbg"] % LANES == 0 and off % tn == 0 and lay["post_row"] % 16 == 0
    lay["used"] = lay["post_row"] + off - lay["post_col"]
    lay["total"] = off
    return lay


def _alibi_slopes():
    hh = np.arange(1, NSA_HEADS + 1, dtype=np.float32)
    return np.exp2(np.float32(-8.0) * hh / np.float32(NSA_HEADS)).astype(np.float32)


def _block_sum_matrix(width):
    i = np.arange(width)
    return (i[:, None] // NSA_HEAD_DIM == i[None, :] // NSA_HEAD_DIM).astype(np.float32).astype(MXU_DT)


def _gate_expand_matrix():
    c = np.arange(LANES)[:, None]
    col = np.arange(NSA_WIDTH)[None, :]
    return np.stack([(c == (col // NSA_HEAD_DIM) * 3 + br) for br in range(3)]).astype(np.float32).astype(MXU_DT)


def _slope_rows(tq):
    rows = np.repeat(_alibi_slopes().reshape(NSA_KV_HEADS, NSA_GROUP), tq, axis=1)
    return np.ascontiguousarray(np.broadcast_to(rows[:, :, None], (NSA_KV_HEADS, NSA_GROUP * tq, LANES)))


def _pad_rows(x, n):
    return jnp.pad(x, ((0, 0), (0, n - x.shape[1])) + ((0, 0),) * (x.ndim - 2))


def _trunk_layer(x, lw, lay, consts, nsa_fn, hg_state, ret_state, *, seq_tiles):
    bsz, t, d = x.shape
    tm, tt_h, tt_r, t_valid = seq_tiles
    x2 = x.reshape(bsz * t, d)
    wts, layer = lw["weights"], lw["layer"]
    h = rms_matmul(x2, lw["norm_attn"], wts["w_in"], layer, lay, tm=tm, tn=IN_PROJ_TN)
    np_ = h.shape[1]
    h3 = h.reshape(bsz, t, np_)
    tp = -(-t // tt_h) * tt_h
    h3p = _pad_rows(h3, tp) if tp != t else h3
    oa, hg_new = hgrn_mixer(h3p, lay, lw["lb"], lw["hg_out_norm"], hg_state, tt=tt_h,
                            t_valid=None if tp == t else t_valid)
    tpr = -(-t // tt_r) * tt_r
    h3r = _pad_rows(h3, tpr) if tpr != t else h3
    oc, ret_new = retention_mixer(h3r, lay, consts["lg_tab"], ret_state, tt=tt_r,
                                  n_valid=tt_r if tpr == t else t_valid)
    ob = nsa_fn(h3)
    flat = lambda a: a[:, :t].reshape(bsz * t, a.shape[-1])
    merged = merge_branches(h, lay, flat(oa), flat(ob), flat(oc),
                            wts["w_branch_hg"], wts["w_branch_nsa"], wts["w_branch_ret"], layer, tm=tm, tn=512)
    x2 = matmul_res(merged, wts["w_out"], layer, x2, tm=tm, tn=512)
    hid = ffn_up(x2, lw["norm_ffn"], wts["w_gate"], wts["w_up"], layer, tm=tm, tn=512)
    x2 = matmul_res(hid, wts["w_down"], layer, x2, tm=tm, tn=512)
    return x2.reshape(bsz, t, d), h3, hg_new, ret_new


def kernel(x_prompt, x_sample, cache_nsa, cache_win, state_hgrn, state_ret, page_table,
           norm_attn, w_in, hgrn_lb_logits, hgrn_out_norm, nsa_q_norm, nsa_k_norm, nsa_cmp_w,
           w_branch_hg, w_branch_nsa, w_branch_ret, w_out, norm_ffn, w_gate, w_up, w_down):
    depth = w_in.shape[0]
    bp, tp, d = x_prompt.shape
    bs, ts, _ = x_sample.shape
    n_pool = cache_nsa.shape[1]
    n_pages = page_table.shape[1]
    past = n_pages * PAGE_SIZE
    wbuf = cache_win.shape[2]
    lay = _layout(d)

    sm = jax.nn.softmax(hgrn_lb_logits.astype(F32), axis=0)
    lower_bounds = jnp.clip(jnp.cumsum(sm, axis=0) - sm[0:1], 0.0, 1.0 - 1e-6)

    tt_p = 256 if tp % 256 == 0 else tp
    lg = np.log1p(-np.exp2(np.float32(-5.0) - np.arange(RET_HEADS, dtype=np.float32))).astype(np.float32)
    consts = {
        "gate_expand": _gate_expand_matrix(),
        "lg_tab": np.ascontiguousarray(np.broadcast_to(lg[:, None, None], (RET_HEADS, 1, max(tt_p, LANES)))),
    }
    bsum = _block_sum_matrix(NSA_KV_WIDTH)
    pool_t = jnp.transpose(cache_nsa, (0, 1, 3, 4, 5, 2)).reshape(depth * n_pool, 4, NSA_KV_WIDTH, PAGE_SIZE)
    win3 = cache_win.reshape(depth, bs, wbuf, 2 * NSA_KV_WIDTH)

    ts_pad = 16
    tq_s = 16
    n_sel_p = -(-tp // SEL_BLOCK)
    n_sel_s = past // SEL_BLOCK + -(-ts // SEL_BLOCK)
    tk_tail = 2048 if past % 2048 == 0 else 512

    weights = {
        "w_in": jnp.swapaxes(w_in, 1, 2).astype(MXU_DT).reshape(depth * w_in.shape[2], d),
        "w_branch_hg": w_branch_hg.astype(MXU_DT), "w_branch_nsa": w_branch_nsa.astype(MXU_DT),
        "w_branch_ret": w_branch_ret.astype(MXU_DT), "w_out": w_out.astype(MXU_DT),
        "w_gate": w_gate.astype(MXU_DT), "w_up": w_up.astype(MXU_DT), "w_down": w_down.astype(MXU_DT),
    }
    assert w_in.shape[2] == lay["used"]

    xp, xs = x_prompt, x_sample
    outs = {k: [] for k in ("kv_p", "kv_s", "win_p", "win_s", "hg_p", "hg_s", "ret_p", "ret_s")}
    for l in range(depth):
        lw = {
            "weights": weights, "layer": l,
            "norm_attn": norm_attn[l], "lb": lower_bounds[l].reshape(1, -1),
            "hg_out_norm": hgrn_out_norm[l], "norm_ffn": norm_ffn[l],
        }
        gq, gk, cw = nsa_q_norm[l], nsa_k_norm[l], nsa_cmp_w[l]
        col_cmp = lay["nkv"] // (2 * NSA_KV_WIDTH)
        col_sel = col_cmp + 1
        col_win = lay["nwkv"] // (2 * NSA_KV_WIDTH)
        nbg_col = lay["nbg"] // LANES

        def nsa_prompt(h3):
            tr = 1024
            assert tp % tr == 0
            zeros16 = jnp.zeros((bp, CMP_STRIDE, 2 * NSA_KV_WIDTH), F32)
            kc, vc = cmp_prep(h3, col_cmp, zeros16, cw, gk[0], bsum, tr=tr)
            qn = q_prep(h3, lay, gq, bsum, tr=tr, width=LANES)
            ks, vs = kv_prep(h3, col_sel, gk[1], bsum, tr=tr,
                             aug=_key_digits(tp, SEL_DIGIT, LANES, 2 * LANES, True))
            kw, vw = kv_prep(h3, col_win, gk[2], bsum, tr=tr,
                             aug=_key_digits(tp, WIN_DIGIT, NSA_HEAD_DIM, LANES, False))
            return nsa_attention_prompt(qn, _slope_rows(PROMPT_TQ), kc, vc, ks, vs, kw, vw,
                                        h3, nbg_col, consts["gate_expand"], n_sel=n_sel_p)

        def nsa_sample(h3):
            pages = page_table + l * n_pool
            h16 = _pad_rows(h3, ts_pad)
            qn = q_prep(h16, lay, gq, bsum, tr=ts_pad)
            w2 = 2 * NSA_KV_WIDTH
            tail16 = h16[:, :, lay["nkv"]:lay["nkv"] + w2].reshape(bs, ts_pad, 2, NSA_KV_WIDTH)
            tail_t = jnp.pad(jnp.transpose(tail16, (0, 2, 3, 1)), ((0, 0), (0, 0), (0, 0), (0, PAGE_SIZE - ts_pad)))
            kc, vc = cmp_prep_paged(pool_t, pages, tail_t, cw, gk[0])
            htail = _pad_rows(h3[:, :, col_sel * w2:(col_sel + 1) * w2], tk_tail)
            ks_tail, vs_tail = kv_prep(htail, 0, gk[1], bsum, tr=tk_tail)
            feature_major = lambda a: jnp.swapaxes(a, 2, 3).reshape(bs, NSA_KV_WIDTH, tk_tail)
            kw_old, vw_old = kv_prep(win3[l], 0, gk[2], bsum, tr=wbuf)
            hw_new = _pad_rows(h3[:, :, col_win * w2:(col_win + 1) * w2], WIN_PAD)
            kw_new, vw_new = kv_prep(hw_new, 0, gk[2], bsum, tr=WIN_PAD)
            kw = jnp.concatenate([kw_old, kw_new], axis=2)
            vw = jnp.concatenate([vw_old, vw_new], axis=2)
            return nsa_attention_paged(qn, _slope_rows(tq_s), kc, vc, kw, vw, pool_t, pages,
                                       feature_major(ks_tail), feature_major(vs_tail), gk[1], h16, nbg_col,
                                       consts["gate_expand"], tq=tq_s, t_real=ts, qpos0=past, wpos0=past - wbuf,
                                       n_sel=n_sel_s)

        zeros_state = jnp.zeros((bp, HG_HEADS, HG_DK, HG_DV), F32)
        tm_p = 1024 if (bp * tp) % 1024 == 0 else bp * tp
        xp, h3p, hgp, rtp = _trunk_layer(xp, lw, lay, consts, nsa_prompt, zeros_state, zeros_state,
                                         seq_tiles=(tm_p, tt_p, tt_p, tp))
        xs, h3s, hgs, rts = _trunk_layer(xs, lw, lay, consts, nsa_sample, state_hgrn[l], state_ret[l],
                                         seq_tiles=(bs * ts, ts_pad, ts_pad, ts))

        kv_cols = slice(lay["nkv"], lay["nkv"] + 4 * NSA_KV_WIDTH)
        win_cols = slice(lay["nwkv"], lay["nwkv"] + 2 * NSA_KV_WIDTH)
        outs["kv_p"].append(h3p[:, :, kv_cols].reshape(bp, tp, 4, NSA_KV_HEADS, NSA_HEAD_DIM))
        outs["kv_s"].append(h3s[:, :, kv_cols].reshape(bs, ts, 4, NSA_KV_HEADS, NSA_HEAD_DIM))
        wlen = min(WINDOW, tp)
        outs["win_p"].append(h3p[:, tp - wlen:, win_cols].reshape(bp, wlen, 2, NSA_KV_HEADS, NSA_HEAD_DIM))
        ctx = jnp.concatenate([win3[l], h3s[:, :, win_cols]], axis=1)[:, ts:]
        outs["win_s"].append(ctx.reshape(bs, wbuf, 2, NSA_KV_HEADS, NSA_HEAD_DIM))
        outs["hg_p"].append(hgp); outs["hg_s"].append(hgs)
        outs["ret_p"].append(rtp); outs["ret_s"].append(rts)

    st = lambda k: jnp.stack(outs[k])
    return (xp, xs, st("kv_p"), st("kv_s"), st("win_p"), st("win_s"),
            st("hg_p"), st("hg_s"), st("ret_p"), st("ret_s"))
```
